```python
import math, functools
import jax, jax.numpy as jnp
from jax import lax
import numpy as np

D_MODEL = 1024
BATCH = 8
SEQ = 2048
DEPTH = 1
DEC_BATCH = 128
DEC_SEQ = 8
PAST_LEN = 2048
PAGE_SIZE = 128

HEAD_DIM = 64
D_MIX = D_MODEL
D_ATTN = D_MIX // 2
D_SSD = D_MIX - D_ATTN
N_ATTN_HEADS = D_ATTN // HEAD_DIM
N_SSD_HEADS = D_SSD // HEAD_DIM
SSD_GROUPS = 2
HEADS_PER_GROUP = N_SSD_HEADS // SSD_GROUPS
D_STATE = 128
CONV_W = 4
CONV_DIM = D_SSD + 2 * SSD_GROUPS * D_STATE
SSD_CHUNK = 128
DILATED_PAIRS = ((128, 1), (512, 4), (2048, 16))
MAX_WINDOW = 2048
D_FF = 4 * D_MODEL
D_PLE = 256
LN_EPS = 1e-5
DEEPNORM_ALPHA = (2 * DEPTH) ** 0.25
DEEPNORM_BETA = (8 * DEPTH) ** -0.25
D_IN_PROJ = 3 * D_ATTN + D_SSD + CONV_DIM + N_SSD_HEADS
IN_SPLITS = [D_ATTN, 2 * D_ATTN, 3 * D_ATTN, 3 * D_ATTN + D_SSD, 3 * D_ATTN + D_SSD + CONV_DIM]

kernel_name = "hymba_dilated_ssd_deepnorm_step"


def alibi_slopes(n_heads):
    return np.array([2.0 ** (-8.0 * (h + 1) / n_heads) for h in range(n_heads)], dtype=np.float32)


def layer_norm(x, g, b):
    xf = x.astype(jnp.float32)
    mu = jnp.mean(xf, -1, keepdims=True)
    var = jnp.mean(jnp.square(xf - mu), -1, keepdims=True)
    return ((xf - mu) * lax.rsqrt(var + LN_EPS) * g + b).astype(x.dtype)


def dilated_branch_prompt(q, k, v, window, dil, slopes):
    bsz, seq, nh, hd = q.shape
    nw = window // dil
    qb = nw
    sub = seq // dil
    nb = -(-sub // qb)
    lp = nb * qb

    def strided(t):
        t = t.reshape(bsz, sub, dil, nh, hd).transpose(0, 2, 1, 3, 4)
        return jnp.pad(t, ((0, 0), (0, 0), (0, lp - sub), (0, 0), (0, 0)))

    def band(t):
        prev = jnp.pad(t, ((0, 0), (0, 0), (qb, 0), (0, 0), (0, 0)))[:, :, :lp]
        return jnp.concatenate([prev.reshape(bsz, dil, nb, qb, nh, hd),
                                t.reshape(bsz, dil, nb, qb, nh, hd)], axis=3)

    qs, ks, vs = strided(q), strided(k), strided(v)
    qblk = qs.reshape(bsz, dil, nb, qb, nh, hd)
    kblk, vblk = band(ks), band(vs)
    scores = jnp.einsum('brnqhd,brnkhd->brnhqk', qblk, kblk).astype(jnp.float32) * (hd ** -0.5)
    dist = np.arange(qb)[:, None] - np.arange(2 * qb)[None, :] + qb
    key_sub = np.arange(nb)[:, None] * qb - qb + np.arange(2 * qb)[None, :]
    valid = ((dist >= 0) & (dist <= nw))[None] & (key_sub >= 0)[:, None, :]
    bias = -slopes[:, None, None] * (dist * dil).astype(np.float32)[None]
    scores = jnp.where(valid[None, None, :, None], scores + bias[None, None, None], -jnp.inf)
    m = jnp.max(scores, -1)
    e = jnp.exp(scores - m[..., None])
    s = jnp.sum(e, -1)
    acc = jnp.einsum('brnhqk,brnkhd->brnqhd', e, vblk.astype(jnp.float32))

    def unstride(t):
        t = t.reshape((bsz, dil, lp) + t.shape[4:])[:, :, :sub]
        t = jnp.moveaxis(t, 1, 2)
        return t.reshape((bsz, seq) + t.shape[3:])

    return unstride(acc), unstride(jnp.swapaxes(m, -1, -2)), unstride(jnp.swapaxes(s, -1, -2))


def dilated_branch_sample(q, kc, vc, window, dil, slopes, w_buf):
    t_new, hd = q.shape[1], q.shape[3]
    nw = window // dil
    i = np.arange(t_new)[:, None]
    back = np.arange(nw + 1)[None, :] * dil
    idx = w_buf + i - back
    valid = (PAST_LEN + i - back >= 0) & (idx >= 0)
    idx = np.clip(idx, 0, None)
    kg = kc[:, idx]
    vg = vc[:, idx]
    scores = jnp.einsum('bthd,btkhd->bhtk', q, kg).astype(jnp.float32) * (hd ** -0.5)
    bias = -slopes[:, None, None] * back.astype(np.float32)[None]
    scores = jnp.where(valid[None, None], scores + bias[None], -jnp.inf)
    m = jnp.max(scores, -1)
    e = jnp.exp(scores - m[..., None])
    s = jnp.sum(e, -1)
    acc = jnp.einsum('bhtk,btkhd->bthd', e, vg.astype(jnp.float32))
    return acc, jnp.swapaxes(m, 1, 2), jnp.swapaxes(s, 1, 2)


def combine_dilated(branches):
    m_max = functools.reduce(jnp.maximum, [m for _, m, _ in branches])
    scaled = [s * jnp.exp(m - m_max) for _, m, s in branches]
    total = functools.reduce(jnp.add, scaled)
    outs = [(sd / total)[..., None] * (acc / s[..., None])
            for (acc, _, s), sd in zip(branches, scaled)]
    return functools.reduce(jnp.add, outs)


def dilated_attention_prompt(q, k, v, slopes):
    return combine_dilated([dilated_branch_prompt(q, k, v, w, r, slopes) for (w, r) in DILATED_PAIRS])


def dilated_attention_sample(q, k, v, cache_k_l, cache_v_l, slopes):
    w_buf = cache_k_l.shape[1]
    kc = jnp.concatenate([cache_k_l.astype(k.dtype), k], axis=1)
    vc = jnp.concatenate([cache_v_l.astype(v.dtype), v], axis=1)
    return combine_dilated([dilated_branch_sample(q, kc, vc, w, r, slopes, w_buf) for (w, r) in DILATED_PAIRS])


def causal_dwconv(xpad, w, b):
    out = lax.conv_general_dilated(xpad, w[:, None, :], window_strides=(1,), padding='VALID',
                                   dimension_numbers=('NWC', 'WIO', 'NWC'),
                                   feature_group_count=xpad.shape[-1])
    return out + b


def segsum(a):
    t = a.shape[-1]
    x = jnp.broadcast_to(a[..., :, None], a.shape + (t,))
    x = jnp.where(np.tril(np.ones((t, t), dtype=bool), -1), x, 0.0)
    xs = jnp.cumsum(x, axis=-2)
    return jnp.where(np.tril(np.ones((t, t), dtype=bool)), xs, -jnp.inf)


def ssd_scan(xdt, adt, b, c, h0):
    n, seq, nh, hp = xdt.shape
    cl = min(SSD_CHUNK, seq)
    nc = -(-seq // cl)
    pad = nc * cl - seq
    padl = lambda t: jnp.pad(t, ((0, 0), (0, pad)) + ((0, 0),) * (t.ndim - 2))
    x = padl(xdt).reshape(n, nc, cl, nh, hp)
    a = padl(adt).reshape(n, nc, cl, nh).transpose(0, 3, 1, 2)
    bb = padl(b).reshape(n, nc, cl, nh, -1)
    cc = padl(c).reshape(n, nc, cl, nh, -1)
    a_cs = jnp.cumsum(a, axis=-1)
    lmat = jnp.exp(segsum(a))
    y_diag = jnp.einsum('bclhn,bcshn,bhcls,bcshp->bclhp', cc, bb, lmat, x)
    decay_states = jnp.exp(a_cs[..., -1:] - a_cs)
    states = jnp.einsum('bclhn,bhcl,bclhp->bchpn', bb, decay_states, x)
    states = jnp.concatenate([h0[:, None], states], axis=1)
    decay_chunk = jnp.exp(segsum(jnp.pad(a_cs[..., -1], ((0, 0), (0, 0), (1, 0)))))
    new_states = jnp.einsum('bhzc,bchpn->bzhpn', decay_chunk, states)
    states, h_final = new_states[:, :-1], new_states[:, -1]
    y_off = jnp.einsum('bclhn,bchpn,bhcl->bclhp', cc, states, jnp.exp(a_cs))
    y = (y_diag + y_off).reshape(n, nc * cl, nh, hp)[:, :seq]
    return y, h_final


def ssd_mixer(z, xbc, dt, conv_prev, h0, conv_w, conv_b, dt_bias, a_log, d_skip, norm_w):
    n, seq, _ = xbc.shape
    xpad = jnp.concatenate([conv_prev.astype(xbc.dtype), xbc], axis=1)
    conv_new = xpad[:, -(CONV_W - 1):]
    xbc_c = jax.nn.silu(causal_dwconv(xpad, conv_w, conv_b)).astype(jnp.float32)
    xs = xbc_c[..., :D_SSD].reshape(n, seq, N_SSD_HEADS, HEAD_DIM)
    bs = xbc_c[..., D_SSD:D_SSD + SSD_GROUPS * D_STATE].reshape(n, seq, SSD_GROUPS, D_STATE)
    cs = xbc_c[..., D_SSD + SSD_GROUPS * D_STATE:].reshape(n, seq, SSD_GROUPS, D_STATE)
    bs = jnp.repeat(bs, HEADS_PER_GROUP, axis=2)
    cs = jnp.repeat(cs, HEADS_PER_GROUP, axis=2)
    dt = jax.nn.softplus(dt.astype(jnp.float32) + dt_bias)
    a = -jnp.exp(a_log.astype(jnp.float32))
    y, h_new = ssd_scan(xs * dt[..., None], dt * a, bs, cs, h0.astype(jnp.float32))
    y = y + d_skip[:, None] * xs
    y = y.reshape(n, seq, D_SSD) * jax.nn.silu(z.astype(jnp.float32))
    yg = y.reshape(n, seq, SSD_GROUPS, D_SSD // SSD_GROUPS)
    yg = yg * lax.rsqrt(jnp.mean(jnp.square(yg), -1, keepdims=True) + LN_EPS)
    y = yg.reshape(n, seq, D_SSD) * norm_w
    return y.astype(z.dtype), conv_new, h_new.astype(h0.dtype)


def trunk_layer(x, pe, attend, conv_prev, ssm_prev, lw):
    (w_in, conv_w, conv_b, dt_bias, a_log, d_skip, ssd_norm_w, w_out,
     ln1_g, ln1_b, w_up, w_down, ln2_g, ln2_b, w_gate, w_ple, ln3_g, ln3_b) = lw
    n, seq, _ = x.shape
    proj = x @ w_in
    q, k, v, z, xbc, dt = jnp.split(proj, IN_SPLITS, axis=-1)
    heads = lambda t: t.reshape(n, seq, N_ATTN_HEADS, HEAD_DIM)
    q, k, v = heads(q), heads(k), heads(v)
    attn = attend(q, k, v).reshape(n, seq, D_ATTN).astype(x.dtype)
    ssd, conv_new, ssm_new = ssd_mixer(z, xbc, dt, conv_prev, ssm_prev, conv_w, conv_b,
                                       dt_bias, a_log, d_skip, ssd_norm_w)
    mix = jnp.concatenate([attn, ssd], axis=-1) @ w_out
    h = layer_norm(DEEPNORM_ALPHA * x + mix, ln1_g, ln1_b)
    u = jnp.square(jax.nn.relu(h @ w_up)) @ w_down
    h = layer_norm(DEEPNORM_ALPHA * h + u, ln2_g, ln2_b)
    g = jax.nn.sigmoid(h @ w_gate) * (pe @ w_ple)
    h = layer_norm(DEEPNORM_ALPHA * h + g, ln3_g, ln3_b)
    return h, k, v, conv_new, ssm_new


def setup_inputs(seed: int = 0) -> dict:
    key = jax.random.key(seed)
    ks = jax.random.split(key, 32)
    f32 = jnp.float32
    nrm = lambda kk, shape, scale: scale * jax.random.normal(kk, shape, f32)
    w_buf = min(MAX_WINDOW, PAST_LEN)
    dt0 = jnp.exp(jax.random.uniform(ks[8], (DEPTH, N_SSD_HEADS), f32, math.log(1e-3), math.log(1e-1)))
    return {
        "x_prompt": nrm(ks[0], (BATCH, SEQ, D_MODEL), 1.0),
        "x_sample": nrm(ks[1], (DEC_BATCH, DEC_SEQ, D_MODEL), 1.0),
        "cache_k": nrm(ks[2], (DEPTH, DEC_BATCH, w_buf, N_ATTN_HEADS, HEAD_DIM), 1.0),
        "cache_v": nrm(ks[3], (DEPTH, DEC_BATCH, w_buf, N_ATTN_HEADS, HEAD_DIM), 1.0),
        "state_conv": nrm(ks[4], (DEPTH, DEC_BATCH, CONV_W - 1, CONV_DIM), 1.0),
        "state_ssm": nrm(ks[5], (DEPTH, DEC_BATCH, N_SSD_HEADS, HEAD_DIM, D_STATE), 0.2),
        "p_prompt": nrm(ks[6], (DEPTH, BATCH, SEQ, D_PLE), 1.0),
        "p_sample": nrm(ks[7], (DEPTH, DEC_BATCH, DEC_SEQ, D_PLE), 1.0),
        "ln_in_g": 1.0 + nrm(ks[9], (D_MODEL,), 0.1),
        "ln_in_b": nrm(ks[10], (D_MODEL,), 0.02),
        "w_in": nrm(ks[11], (DEPTH, D_MODEL, D_IN_PROJ), D_MODEL ** -0.5),
        "conv_w": nrm(ks[12], (DEPTH, CONV_W, CONV_DIM), CONV_W ** -0.5),
        "conv_b": nrm(ks[13], (DEPTH, CONV_DIM), 0.02),
        "dt_bias": dt0 + jnp.log(-jnp.expm1(-dt0)),
        "a_log": jnp.log(jax.random.uniform(ks[14], (DEPTH, N_SSD_HEADS), f32, 1.0, 16.0)),
        "d_skip": 1.0 + nrm(ks[15], (DEPTH, N_SSD_HEADS), 0.1),
        "ssd_norm_w": 1.0 + nrm(ks[16], (DEPTH, D_SSD), 0.1),
        "w_out": nrm(ks[17], (DEPTH, D_MIX, D_MODEL), D_MIX ** -0.5 * DEEPNORM_BETA),
        "ln1_g": 1.0 + nrm(ks[18], (DEPTH, D_MODEL), 0.1),
        "ln1_b": nrm(ks[19], (DEPTH, D_MODEL), 0.02),
        "w_up": nrm(ks[20], (DEPTH, D_MODEL, D_FF), D_MODEL ** -0.5),
        "w_down": nrm(ks[21], (DEPTH, D_FF, D_MODEL), D_FF ** -0.5 * DEEPNORM_BETA),
        "ln2_g": 1.0 + nrm(ks[22], (DEPTH, D_MODEL), 0.1),
        "ln2_b": nrm(ks[23], (DEPTH, D_MODEL), 0.02),
        "w_gate": nrm(ks[24], (DEPTH, D_MODEL, D_MODEL), D_MODEL ** -0.5),
        "w_ple": nrm(ks[25], (DEPTH, D_PLE, D_MODEL), D_PLE ** -0.5 * DEEPNORM_BETA),
        "ln3_g": 1.0 + nrm(ks[26], (DEPTH, D_MODEL), 0.1),
        "ln3_b": nrm(ks[27], (DEPTH, D_MODEL), 0.02),
    }


def reference(x_prompt, x_sample, cache_k, cache_v, state_conv, state_ssm, p_prompt, p_sample,
              ln_in_g, ln_in_b, w_in, conv_w, conv_b, dt_bias, a_log, d_skip, ssd_norm_w, w_out,
              ln1_g, ln1_b, w_up, w_down, ln2_g, ln2_b, w_gate, w_ple, ln3_g, ln3_b):
    slopes = alibi_slopes(N_ATTN_HEADS)
    hp = layer_norm(x_prompt, ln_in_g, ln_in_b)
    hs = layer_norm(x_sample, ln_in_g, ln_in_b)
    kp_l, vp_l, ks_l, vs_l, cp_l, cs_l, sp_l, ss_l = [], [], [], [], [], [], [], []
    attend_prompt = functools.partial(dilated_attention_prompt, slopes=slopes)
    for i in range(DEPTH):
        lw = (w_in[i], conv_w[i], conv_b[i], dt_bias[i], a_log[i], d_skip[i], ssd_norm_w[i], w_out[i],
              ln1_g[i], ln1_b[i], w_up[i], w_down[i], ln2_g[i], ln2_b[i], w_gate[i], w_ple[i],
              ln3_g[i], ln3_b[i])
        nbp = hp.shape[0]
        conv0 = jnp.zeros((nbp, CONV_W - 1, CONV_DIM), hp.dtype)
        ssm0 = jnp.zeros((nbp, N_SSD_HEADS, HEAD_DIM, D_STATE), hp.dtype)
        hp, kp, vp, cp, sp = trunk_layer(hp, p_prompt[i], attend_prompt, conv0, ssm0, lw)
        attend_sample = functools.partial(dilated_attention_sample, cache_k_l=cache_k[i],
                                          cache_v_l=cache_v[i], slopes=slopes)
        hs, kn, vn, cn, sn = trunk_layer(hs, p_sample[i], attend_sample, state_conv[i], state_ssm[i], lw)
        keep = min(MAX_WINDOW, kp.shape[1])
        kp_l.append(kp[:, -keep:])
        vp_l.append(vp[:, -keep:])
        ks_l.append(kn)
        vs_l.append(vn)
        cp_l.append(cp)
        cs_l.append(cn)
        sp_l.append(sp)
        ss_l.append(sn)
    new_k_prompt = jnp.stack(kp_l)
    new_v_prompt = jnp.stack(vp_l)
    new_k_sample = jnp.stack(ks_l)
    new_v_sample = jnp.stack(vs_l)
    new_conv_prompt = jnp.stack(cp_l)
    new_conv_sample = jnp.stack(cs_l)
    new_ssm_prompt = jnp.stack(sp_l)
    new_ssm_sample = jnp.stack(ss_l)
    return (hp, hs, new_k_prompt, new_v_prompt, new_k_sample, new_v_sample,
            new_conv_prompt, new_conv_sample, new_ssm_prompt, new_ssm_sample)
```

```python
import functools

import numpy as np
import jax
import jax.numpy as jnp
from jax import lax
from jax.experimental import pallas as pl
from jax.experimental.pallas import tpu as pltpu

F32 = jnp.float32
BF16 = jnp.bfloat16

D_MODEL = 1024
HEAD_DIM = 64
D_ATTN = 512
D_SSD = 512
N_HEADS = 8
SSD_GROUPS = 2
GROUP_W = D_SSD // SSD_GROUPS
D_STATE = 128
CONV_W = 4
CONV_DIM = D_SSD + 2 * SSD_GROUPS * D_STATE
CHUNK = 128
D_FF = 4096
D_PLE = 256
LN_EPS = 1e-5
WINDOW_STEPS = 128
DILATIONS = (1, 4, 16)
PHASES = 16
CACHE_LEN = 2048
DT_PAD = 128
D_PROJ = 3 * D_ATTN + D_SSD + CONV_DIM + DT_PAD
SUBLANES = 8
LANES = 128
VMEM_LIMIT = 52 * 1024 * 1024
NEG_INF = float("-inf")


def _slopes():
    return np.array([2.0 ** (-8.0 * (h + 1) / N_HEADS) for h in range(N_HEADS)], dtype=np.float64)


def _layer_norm(x, g, b):
    mu = jnp.mean(x, -1, keepdims=True)
    xc = x - mu
    var = jnp.mean(xc * xc, -1, keepdims=True)
    return xc * lax.rsqrt(var + LN_EPS) * g + b


def _silu(x):
    return x * (1.0 / (1.0 + jnp.exp(-x)))


def _dot(a, b):
    return jnp.dot(a, b, preferred_element_type=F32)


def _dot_nt(a, b):
    return lax.dot_general(a, b, (((1,), (1,)), ((), ())), preferred_element_type=F32)


def _const_spec(shape):
    nd = len(shape)
    return pl.BlockSpec(shape, lambda *_: (0,) * nd, pipeline_mode=pl.Buffered(1))


def _inproj_body(x_ref, g_ref, b_ref, w_ref, q_ref, k_ref, v_ref, z_ref, xbc_ref, dt_ref):
    h = _layer_norm(x_ref[...], g_ref[...], b_ref[...]).astype(BF16)
    lo = 0
    for ref, width in ((q_ref, D_ATTN), (k_ref, D_ATTN), (v_ref, D_ATTN), (z_ref, D_SSD),
                       (xbc_ref, CONV_DIM), (dt_ref, DT_PAD)):
        ref[...] = _dot(h, w_ref[:, lo:lo + width]).astype(ref.dtype)
        lo += width


def _in_proj(x, g, b, w, *, q_dtype, tm):
    n = x.shape[0]
    widths = (D_ATTN, D_ATTN, D_ATTN, D_SSD, CONV_DIM, DT_PAD)
    dtypes = (q_dtype, F32, F32, F32, F32, F32)
    row = lambda width: pl.BlockSpec((tm, width), lambda i: (i, 0))
    return pl.pallas_call(
        _inproj_body,
        grid=(n // tm,),
        in_specs=[row(D_MODEL), _const_spec((1, D_MODEL)), _const_spec((1, D_MODEL)),
                  _const_spec((D_MODEL, D_PROJ))],
        out_specs=[row(wd) for wd in widths],
        out_shape=[jax.ShapeDtypeStruct((n, wd), dt) for wd, dt in zip(widths, dtypes)],
        compiler_params=pltpu.CompilerParams(dimension_semantics=("parallel",),
                                             vmem_limit_bytes=VMEM_LIMIT),
        name="in_proj",
    )(x, g, b, w)


FF_CHUNK = 1024


def _post_body(alpha, x_ref, attn_ref, ssd_ref, pe_ref, gin_ref, bin_ref, wout_ref, g1_ref, b1_ref,
               wup_ref, wdown_ref, g2_ref, b2_ref, wgate_ref, wple_ref, g3_ref, b3_ref, y_ref):
    xn = _layer_norm(x_ref[...], gin_ref[...], bin_ref[...])
    mixed = jnp.concatenate([attn_ref[...], ssd_ref[...]], axis=-1).astype(BF16)
    h = _layer_norm(alpha * xn + _dot(mixed, wout_ref[...]), g1_ref[...], b1_ref[...])
    hb = h.astype(BF16)
    u = None
    for c in range(D_FF // FF_CHUNK):
        a = jnp.maximum(_dot(hb, wup_ref[:, c * FF_CHUNK:(c + 1) * FF_CHUNK]), 0.0)
        part = _dot((a * a).astype(BF16), wdown_ref[c * FF_CHUNK:(c + 1) * FF_CHUNK, :])
        u = part if u is None else u + part
    h = _layer_norm(alpha * h + u, g2_ref[...], b2_ref[...])
    gate = 1.0 / (1.0 + jnp.exp(-_dot(h.astype(BF16), wgate_ref[...])))
    g = gate * _dot(pe_ref[...].astype(BF16), wple_ref[...])
    y_ref[...] = _layer_norm(alpha * h + g, g3_ref[...], b3_ref[...])


def _post(x, attn, ssd, pe, gin, bin_, wout, g1, b1, wup, wdown, g2, b2, wgate, wple, g3, b3, *,
          alpha, tm):
    n = x.shape[0]
    row = lambda width: pl.BlockSpec((tm, width), lambda i: (i, 0))
    vec = _const_spec((1, D_MODEL))
    return pl.pallas_call(
        functools.partial(_post_body, alpha),
        grid=(n // tm,),
        in_specs=[row(D_MODEL), row(D_ATTN), row(D_SSD), row(D_PLE), vec, vec,
                  _const_spec((D_MODEL, D_MODEL)), vec, vec,
                  _const_spec((D_MODEL, D_FF)), _const_spec((D_FF, D_MODEL)), vec, vec,
                  _const_spec((D_MODEL, D_MODEL)), _const_spec((D_PLE, D_MODEL)), vec, vec],
        out_specs=row(D_MODEL),
        out_shape=jax.ShapeDtypeStruct((n, D_MODEL), F32),
        compiler_params=pltpu.CompilerParams(dimension_semantics=("parallel",),
                                             vmem_limit_bytes=VMEM_LIMIT),
        name="post",
    )(x, attn, ssd, pe, gin, bin_, wout, g1, b1, wup, wdown, g2, b2, wgate, wple, g3, b3)


HEAD_PAIRS = N_HEADS // 2
U_ROWS = CACHE_LEN // PHASES
TAB3_LO, TAB2_LO, TAB1_LO, TAB_W = 0, 128, 384, 640


def _prompt_bias_tables():
    slopes = _slopes()
    u = np.arange(U_ROWS)
    d3 = (u[:, None] - u[None, :]).astype(np.float64)
    k4, ul4 = np.meshgrid(np.arange(4), np.arange(32), indexing="ij")
    j2 = (4 * ul4 + k4).reshape(-1)
    d2 = j2[:, None] - np.concatenate([j2 - WINDOW_STEPS, j2])[None, :]
    p16, ul16 = np.meshgrid(np.arange(16), np.arange(8), indexing="ij")
    t1 = (16 * ul16 + p16).reshape(-1)
    d1 = t1[:, None] - np.concatenate([t1 - WINDOW_STEPS, t1])[None, :]
    tabs = []
    for dist, dil in ((d3, 16), (d2, 4), (d1, 1)):
        valid = (dist >= 0) & (dist <= WINDOW_STEPS)
        per_head = [np.where(valid, -slopes[h] * dist * dil, NEG_INF) for h in range(N_HEADS)]
        tabs.append(np.stack(per_head))
    tab = np.concatenate(tabs, axis=-1)
    return jnp.asarray(tab.reshape(HEAD_PAIRS, 2 * U_ROWS, TAB_W), dtype=F32)


def _attn_block(q, k, v, bias, first_head):
    q2 = jnp.concatenate([jnp.where(first_head, q, 0.0), jnp.where(first_head, 0.0, q)], axis=0)
    s = _dot_nt(q2.astype(BF16), k.astype(BF16)) + bias
    m = jnp.max(s, -1, keepdims=True)
    e = jnp.exp(s - m)
    den = jnp.sum(e, -1, keepdims=True)
    pv = _dot(e.astype(BF16), v.astype(BF16))
    half = q.shape[0]
    o = jnp.where(first_head, pv[:half], pv[half:])
    m2 = jnp.where(first_head, m[:half], m[half:])
    den2 = jnp.where(first_head, den[:half], den[half:])
    return o / den2, m2 + jnp.log(den2)


def _attn_prompt_body(q_ref, k_ref, v_ref, tab_ref, o_ref, qs, ks, vs, o_scr, l_scr):
    c_id = pl.program_id(1)
    for c in range(HEAD_PAIRS):
        @pl.when(c_id == c)
        def _stage():
            for p in range(PHASES):
                lo = p * D_ATTN + c * LANES
                qs[p] = q_ref[:, lo:lo + LANES].astype(F32) * (HEAD_DIM ** -0.5)
                ks[p] = k_ref[:, lo:lo + LANES]
                vs[p] = v_ref[:, lo:lo + LANES]

    lane = lax.broadcasted_iota(jnp.int32, (U_ROWS, LANES), 1)
    first_head = lane < HEAD_DIM

    def branch3(p, carry):
        o, l = _attn_block(qs[p], ks[p], vs[p], tab_ref[:, TAB3_LO:TAB3_LO + 128], first_head)
        o_scr[0, p] = o
        l_scr[0, p] = l
        return carry
    lax.fori_loop(0, PHASES, branch3, 0)

    def branch2(r, carry):
        for n in range(4):
            rows = pl.ds(32 * n, 32)
            gather = lambda src, rr: jnp.concatenate([src[4 * kk + r, rr, :] for kk in range(4)], 0)
            q = gather(qs, rows)
            kc, vc = gather(ks, rows), gather(vs, rows)
            if n == 0:
                k, v, bias = kc, vc, tab_ref[:, TAB2_LO + 128:TAB2_LO + 256]
            else:
                prev = pl.ds(32 * (n - 1), 32)
                k = jnp.concatenate([gather(ks, prev), kc], 0)
                v = jnp.concatenate([gather(vs, prev), vc], 0)
                bias = tab_ref[:, TAB2_LO:TAB2_LO + 256]
            o, l = _attn_block(q, k, v, bias, first_head)
            for kk in range(4):
                o_scr[1, 4 * kk + r, rows, :] = o[32 * kk:32 * kk + 32]
                l_scr[1, 4 * kk + r, rows, :] = l[32 * kk:32 * kk + 32]
        return carry
    lax.fori_loop(0, 4, branch2, 0)

    def branch1_block(n, with_prev):
        rows = pl.ds(pl.multiple_of(n * SUBLANES, SUBLANES), SUBLANES)
        gather = lambda src, rr: jnp.concatenate([src[p, rr, :] for p in range(PHASES)], 0)
        q = gather(qs, rows)
        kc, vc = gather(ks, rows), gather(vs, rows)
        if with_prev:
            prev = pl.ds(pl.multiple_of((n - 1) * SUBLANES, SUBLANES), SUBLANES)
            k = jnp.concatenate([gather(ks, prev), kc], 0)
            v = jnp.concatenate([gather(vs, prev), vc], 0)
            bias = tab_ref[:, TAB1_LO:TAB1_LO + 256]
        else:
            k, v, bias = kc, vc, tab_ref[:, TAB1_LO + 128:TAB1_LO + 256]
        o, l = _attn_block(q, k, v, bias, first_head)
        for p in range(PHASES):
            o_scr[2, p, rows, :] = o[SUBLANES * p:SUBLANES * (p + 1)]
            l_scr[2, p, rows, :] = l[SUBLANES * p:SUBLANES * (p + 1)]

    branch1_block(0, False)

    def branch1(n, carry):
        branch1_block(n, True)
        return carry
    lax.fori_loop(1, U_ROWS // SUBLANES, branch1, 0)

    def merge(p, carry):
        l3, l2, l1 = l_scr[0, p], l_scr[1, p], l_scr[2, p]
        top = jnp.maximum(jnp.maximum(l3, l2), l1)
        e3, e2, e1 = jnp.exp(l3 - top), jnp.exp(l2 - top), jnp.exp(l1 - top)
        total = e3 + e2 + e1
        o_scr[0, p] = (e3 * o_scr[0, p] + e2 * o_scr[1, p] + e1 * o_scr[2, p]) / total
        return carry
    lax.fori_loop(0, PHASES, merge, 0)

    for c in range(HEAD_PAIRS):
        @pl.when(c_id == c)
        def _emit():
            for p in range(PHASES):
                lo = p * D_ATTN + c * LANES
                o_ref[:, lo:lo + LANES] = o_scr[0, p]


def _attn_prompt(q, k, v):
    bsz, seq, _ = k.shape
    assert seq == CACHE_LEN, "prompt attention is laid out for a 2048-token prompt"
    wide = PHASES * D_ATTN
    view = lambda t: t.reshape(bsz, U_ROWS, wide)
    blk = pl.BlockSpec((None, U_ROWS, wide), lambda b, c: (b, 0, 0))
    tile = (PHASES, U_ROWS, LANES)
    out = pl.pallas_call(
        _attn_prompt_body,
        grid=(bsz, HEAD_PAIRS),
        in_specs=[blk, blk, blk, pl.BlockSpec((None, 2 * U_ROWS, TAB_W), lambda b, c: (c, 0, 0))],
        out_specs=blk,
        out_shape=jax.ShapeDtypeStruct((bsz, U_ROWS, wide), F32),
        scratch_shapes=[pltpu.VMEM(tile, F32), pltpu.VMEM(tile, F32), pltpu.VMEM(tile, F32),
                        pltpu.VMEM((3,) + tile, F32), pltpu.VMEM((3,) + tile, F32)],
        compiler_params=pltpu.CompilerParams(dimension_semantics=("parallel", "arbitrary"),
                                             vmem_limit_bytes=VMEM_LIMIT),
        name="attn_prompt",
    )(view(q), view(k), view(v), _prompt_bias_tables())
    return out.reshape(bsz, seq, D_ATTN)


DEC_T = 8
QH = DEC_T * N_HEADS
B2_SPAN, B1_SPAN = 512, 128
SAMPLE_TAB_W = CACHE_LEN + B2_SPAN + B1_SPAN


def _sample_bias_tables():
    slopes = _slopes()
    i = np.repeat(np.arange(DEC_T), N_HEADS)[:, None]
    sl = np.tile(slopes, DEC_T)[:, None]

    def cache_bias(span, dil):
        t = CACHE_LEN - span + np.arange(span)[None, :]
        dist = CACHE_LEN + i - t
        valid = (dist % dil == 0) & (dist <= WINDOW_STEPS * dil)
        return np.where(valid, -sl * dist, NEG_INF)

    cache_tab = np.concatenate([cache_bias(CACHE_LEN, 16), cache_bias(B2_SPAN, 4),
                                cache_bias(B1_SPAN, 1)], axis=1)
    c = np.arange(CHUNK)[None, :]
    dn = i - c
    is_new = c < DEC_T
    n1 = np.where(is_new & (dn >= 0), -sl * dn, NEG_INF)
    n2 = np.where(is_new & ((dn == 0) | (dn == 4)), -sl * dn, NEG_INF)
    n3 = np.where(is_new & (dn == 0), 0.0, NEG_INF)
    return jnp.asarray(cache_tab, dtype=F32), jnp.asarray(np.stack([n1, n2, n3]), dtype=F32)


def _head_selector():
    h_row = np.tile(np.arange(N_HEADS), DEC_T)[:, None]
    h_col = (np.arange(D_ATTN) // HEAD_DIM)[None, :]
    return jnp.asarray((h_row == h_col).astype(np.float32))


def _attn_sample_body(q_ref, kn_ref, vn_ref, kt_ref, vt_ref, tab_ref, tabn_ref, sel_ref, o_ref):
    sel = sel_ref[...]
    q = q_ref[...] * (HEAD_DIM ** -0.5)
    q_rows = jnp.concatenate([jnp.broadcast_to(q[i:i + 1, :], (N_HEADS, D_ATTN))
                              for i in range(DEC_T)], 0)
    qall = (q_rows * sel).astype(BF16)
    pad = jnp.zeros((CHUNK - DEC_T, D_ATTN), F32)
    kn = jnp.concatenate([kn_ref[...], pad], 0).astype(BF16)
    vn = jnp.concatenate([vn_ref[...], pad], 0).astype(BF16)

    s_all = _dot(qall, kt_ref[...].astype(BF16))
    s_new = _dot_nt(qall, kn)
    lo2, lo1 = CACHE_LEN - B2_SPAN, CACHE_LEN - B1_SPAN

    def branch(s_cache, s_fresh, v_t):
        m = jnp.maximum(jnp.max(s_cache, -1, keepdims=True), jnp.max(s_fresh, -1, keepdims=True))
        e_c, e_f = jnp.exp(s_cache - m), jnp.exp(s_fresh - m)
        den = jnp.sum(e_c, -1, keepdims=True) + jnp.sum(e_f, -1, keepdims=True)
        acc = _dot_nt(e_c.astype(BF16), v_t.astype(BF16)) + _dot(e_f.astype(BF16), vn)
        return m, den, acc

    m3, den3, acc3 = branch(s_all + tab_ref[:, :CACHE_LEN], s_new + tabn_ref[2], vt_ref[...])
    m2, den2, acc2 = branch(s_all[:, lo2:] + tab_ref[:, CACHE_LEN:CACHE_LEN + B2_SPAN],
                            s_new + tabn_ref[1], vt_ref[:, lo2:])
    m1, den1, acc1 = branch(s_all[:, lo1:] + tab_ref[:, CACHE_LEN + B2_SPAN:],
                            s_new + tabn_ref[0], vt_ref[:, lo1:])

    top = jnp.maximum(jnp.maximum(m1, m2), m3)
    sc1, sc2, sc3 = den1 * jnp.exp(m1 - top), den2 * jnp.exp(m2 - top), den3 * jnp.exp(m3 - top)
    total = sc1 + sc2 + sc3
    mixed = ((sc1 / total) * (acc1 / den1) + (sc2 / total) * (acc2 / den2)
             + (sc3 / total) * (acc3 / den3)) * sel
    for i in range(DEC_T):
        o_ref[i:i + 1, :] = jnp.sum(mixed[N_HEADS * i:N_HEADS * (i + 1)], axis=0, keepdims=True)


def _attn_sample(q, k_new, v_new, cache_kt, cache_vt):
    n = q.shape[0]
    assert cache_kt.shape[1:] == (D_ATTN, CACHE_LEN) and q.shape[1] == DEC_T
    tok = pl.BlockSpec((None, DEC_T, D_ATTN), lambda i: (i, 0, 0))
    cache = pl.BlockSpec((None, D_ATTN, CACHE_LEN), lambda i: (i, 0, 0))
    cache_tab, new_tab = _sample_bias_tables()
    return pl.pallas_call(
        _attn_sample_body,
        grid=(n,),
        in_specs=[tok, tok, tok, cache, cache, _const_spec((QH, SAMPLE_TAB_W)),
                  _const_spec((3, QH, CHUNK)), _const_spec((QH, D_ATTN))],
        out_specs=tok,
        out_shape=jax.ShapeDtypeStruct((n, DEC_T, D_ATTN), F32),
        compiler_params=pltpu.CompilerParams(dimension_semantics=("parallel",),
                                             vmem_limit_bytes=VMEM_LIMIT),
        name="attn_sample",
    )(q, k_new, v_new, cache_kt, cache_vt, cache_tab, new_tab, _head_selector())


HEADS_PER_GROUP = N_HEADS // SSD_GROUPS
PAD_ROWS = SUBLANES


def _ssd_body(rows, z_ref, xbc_ref, dt_ref, cp_ref, h0_ref, cw_ref, cb_ref, dtb_ref, alog_ref,
              dsk_ref, nw_ref, y_ref, cn_ref, hn_ref, xpad, hst):
    c_id = pl.program_id(1)

    @pl.when(c_id == 0)
    def _init():
        xpad[0:PAD_ROWS, :] = cp_ref[...]
        hst[...] = h0_ref[...]

    xpad[PAD_ROWS:PAD_ROWS + rows, :] = xbc_ref[...]
    if rows < CHUNK:
        xpad[PAD_ROWS + rows:, :] = jnp.zeros((CHUNK - rows, CONV_DIM), F32)

    conv = cb_ref[...]
    for back in range(CONV_W):
        tap = CONV_W - 1 - back
        conv = conv + xpad[pl.ds(PAD_ROWS - back, CHUNK), :] * cw_ref[tap:tap + 1, :]
    xc = _silu(conv)
    xs = xc[:, :D_SSD]

    lane = lax.broadcasted_iota(jnp.int32, (rows, DT_PAD), 1)
    dt_raw = dt_ref[...] + dtb_ref[...]
    dt = jnp.maximum(dt_raw, 0.0) + jnp.log1p(jnp.exp(-jnp.abs(dt_raw)))
    dt = jnp.where(lane < N_HEADS, dt, 0.0)
    if rows < CHUNK:
        dt = jnp.concatenate([dt, jnp.zeros((CHUNK - rows, DT_PAD), F32)], 0)
    adt = dt * (-jnp.exp(alog_ref[...]))

    ri = lax.broadcasted_iota(jnp.int32, (CHUNK, CHUNK), 0)
    ci = lax.broadcasted_iota(jnp.int32, (CHUNK, CHUNK), 1)
    causal = ri >= ci
    tril = jnp.where(causal, 1.0, 0.0).astype(F32)
    cs = jnp.dot(tril, adt, precision=lax.Precision.HIGHEST, preferred_element_type=F32)
    cs_t = cs.T

    head_of_lane = lax.broadcasted_iota(jnp.int32, (CHUNK, GROUP_W), 1) // HEAD_DIM

    def per_head(cols):
        out = cols[HEADS_PER_GROUP - 1]
        for hl in range(HEADS_PER_GROUP - 2, -1, -1):
            out = jnp.where(head_of_lane == hl, cols[hl], out)
        return out

    y_groups = []
    for g in range(SSD_GROUPS):
        heads = range(g * HEADS_PER_GROUP, (g + 1) * HEADS_PER_GROUP)
        b_g = xc[:, D_SSD + g * D_STATE:D_SSD + (g + 1) * D_STATE].astype(BF16)
        c_g = xc[:, D_SSD + (SSD_GROUPS + g) * D_STATE:
                 D_SSD + (SSD_GROUPS + g + 1) * D_STATE].astype(BF16)
        xs_g = xs[:, g * GROUP_W:(g + 1) * GROUP_W]
        cs_cols = [cs[:, h:h + 1] for h in heads]
        xdt_g = xs_g * per_head([dt[:, h:h + 1] for h in heads])
        xdt_b = xdt_g.astype(BF16)
        gram = _dot_nt(c_g, b_g)

        y_diag = jnp.zeros((CHUNK, GROUP_W), F32)
        for hl, h in enumerate(heads):
            seg = jnp.where(causal, cs_cols[hl] - cs_t[h:h + 1, :], NEG_INF)
            weights = (gram * jnp.exp(seg)).astype(BF16)
            y_diag = y_diag + jnp.where(head_of_lane == hl, _dot(weights, xdt_b), 0.0)

        h_prev = hst[g * GROUP_W:(g + 1) * GROUP_W, :]
        y_off = _dot_nt(c_g, h_prev.astype(BF16)) * per_head([jnp.exp(col) for col in cs_cols])
        y_groups.append(y_diag + y_off)

        last = [cs[CHUNK - 1:CHUNK, h:h + 1] for h in heads]
        to_end = per_head([jnp.exp(last[hl] - cs_cols[hl]) for hl in range(HEADS_PER_GROUP)])
        new_states = _dot((xdt_g * to_end).T.astype(BF16), b_g)
        keep = jnp.concatenate([jnp.broadcast_to(jnp.exp(last[hl]), (HEAD_DIM, D_STATE))
                                for hl in range(HEADS_PER_GROUP)], 0)
        hst[g * GROUP_W:(g + 1) * GROUP_W, :] = h_prev * keep + new_states

    y = jnp.concatenate(y_groups, axis=-1) + dsk_ref[...] * xs
    y = y[:rows] * _silu(z_ref[...])
    normed = []
    for g in range(SSD_GROUPS):
        yg = y[:, g * GROUP_W:(g + 1) * GROUP_W]
        normed.append(yg * lax.rsqrt(jnp.mean(yg * yg, -1, keepdims=True) + LN_EPS))
    y_ref[...] = jnp.concatenate(normed, axis=-1) * nw_ref[...]

    cn_ref[...] = xpad[rows:rows + PAD_ROWS, :]
    hn_ref[...] = hst[...]
    if rows == CHUNK:
        xpad[0:PAD_ROWS, :] = xpad[CHUNK:CHUNK + PAD_ROWS, :]


def _ssd(z, xbc, dt, conv_prev, h0, cw, cb, dtb, alog, dsk, nw):
    n, seq, _ = z.shape
    rows = min(CHUNK, seq)
    assert seq % rows == 0
    nchunks = seq // rows
    seq_blk = lambda width: pl.BlockSpec((None, rows, width), lambda s, c: (s, c, 0))
    per_seq = lambda r, width: pl.BlockSpec((None, r, width), lambda s, c: (s, 0, 0))
    return pl.pallas_call(
        functools.partial(_ssd_body, rows),
        grid=(n, nchunks),
        in_specs=[seq_blk(D_SSD), seq_blk(CONV_DIM), seq_blk(DT_PAD),
                  per_seq(PAD_ROWS, CONV_DIM), per_seq(D_SSD, D_STATE),
                  _const_spec((CONV_W, CONV_DIM)), _const_spec((1, CONV_DIM)),
                  _const_spec((1, DT_PAD)), _const_spec((1, DT_PAD)),
                  _const_spec((1, D_SSD)), _const_spec((1, D_SSD))],
        out_specs=[seq_blk(D_SSD), per_seq(PAD_ROWS, CONV_DIM), per_seq(D_SSD, D_STATE)],
        out_shape=[jax.ShapeDtypeStruct((n, seq, D_SSD), F32),
                   jax.ShapeDtypeStruct((n, PAD_ROWS, CONV_DIM), F32),
                   jax.ShapeDtypeStruct((n, D_SSD, D_STATE), F32)],
        scratch_shapes=[pltpu.VMEM((PAD_ROWS + CHUNK, CONV_DIM), F32),
                        pltpu.VMEM((D_SSD, D_STATE), F32)],
        compiler_params=pltpu.CompilerParams(dimension_semantics=("parallel", "arbitrary"),
                                             vmem_limit_bytes=VMEM_LIMIT),
        name="ssd",
    )(z, xbc, dt, conv_prev, h0, cw, cb, dtb, alog, dsk, nw)


def _row(v, width=None):
    v = v.reshape(1, -1).astype(F32)
    if width is not None and v.shape[1] < width:
        v = jnp.pad(v, ((0, 0), (0, width - v.shape[1])))
    return v


def kernel(x_prompt, x_sample, cache_k, cache_v, state_conv, state_ssm, p_prompt, p_sample,
           ln_in_g, ln_in_b, w_in, conv_w, conv_b, dt_bias, a_log, d_skip, ssd_norm_w, w_out,
           ln1_g, ln1_b, w_up, w_down, ln2_g, ln2_b, w_gate, w_ple, ln3_g, ln3_b):
    depth = w_in.shape[0]
    assert depth == 1, "single-layer step"
    alpha = (2 * depth) ** 0.25
    bsz, seq, _ = x_prompt.shape
    nd, dec_t, _ = x_sample.shape
    lyr = 0

    w_proj = jnp.pad(w_in[lyr], ((0, 0), (0, D_PROJ - w_in.shape[2]))).astype(BF16)
    gin, bin_ = _row(ln_in_g), _row(ln_in_b)
    ssd_params = (conv_w[lyr].astype(F32), _row(conv_b[lyr]), _row(dt_bias[lyr], DT_PAD),
                  _row(a_log[lyr], DT_PAD), _row(jnp.repeat(d_skip[lyr], HEAD_DIM)),
                  _row(ssd_norm_w[lyr]))
    post_params = (gin, bin_, w_out[lyr].astype(BF16), _row(ln1_g[lyr]), _row(ln1_b[lyr]),
                   w_up[lyr].astype(BF16), w_down[lyr].astype(BF16), _row(ln2_g[lyr]),
                   _row(ln2_b[lyr]), w_gate[lyr].astype(BF16), w_ple[lyr].astype(BF16),
                   _row(ln3_g[lyr]), _row(ln3_b[lyr]))

    xp = x_prompt.reshape(bsz * seq, D_MODEL)
    q, k, v, z, xbc, dt = _in_proj(xp, gin, bin_, w_proj, q_dtype=BF16, tm=512)
    seqs = lambda t: t.reshape(bsz, seq, t.shape[-1])
    attn = _attn_prompt(seqs(q), seqs(k), seqs(v))
    ssd_y, conv_p, ssm_p = _ssd(seqs(z), seqs(xbc), seqs(dt),
                                jnp.zeros((bsz, PAD_ROWS, CONV_DIM), F32),
                                jnp.zeros((bsz, D_SSD, D_STATE), F32), *ssd_params)
    y_prompt = _post(xp, attn.reshape(bsz * seq, D_ATTN), ssd_y.reshape(bsz * seq, D_SSD),
                     p_prompt[lyr].reshape(bsz * seq, D_PLE), *post_params, alpha=alpha, tm=256)

    xs = x_sample.reshape(nd * dec_t, D_MODEL)
    qs, ks, vs, zs, xbcs, dts = _in_proj(xs, gin, bin_, w_proj, q_dtype=F32, tm=512)
    toks = lambda t: t.reshape(nd, dec_t, t.shape[-1])
    transposed = lambda c: jnp.transpose(c, (0, 2, 3, 1)).reshape(nd, D_ATTN, CACHE_LEN)
    attn_s = _attn_sample(toks(qs), toks(ks), toks(vs), transposed(cache_k[lyr]),
                          transposed(cache_v[lyr]))
    conv_prev = jnp.pad(state_conv[lyr].astype(F32), ((0, 0), (PAD_ROWS - (CONV_W - 1), 0), (0, 0)))
    ssd_s, conv_s, ssm_s = _ssd(toks(zs), toks(xbcs), toks(dts), conv_prev,
                                state_ssm[lyr].reshape(nd, D_SSD, D_STATE).astype(F32), *ssd_params)
    y_sample = _post(xs, attn_s.reshape(nd * dec_t, D_ATTN), ssd_s.reshape(nd * dec_t, D_SSD),
                     p_sample[lyr].reshape(nd * dec_t, D_PLE), *post_params, alpha=alpha, tm=256)

    heads = lambda t, n, length: t.reshape(1, n, length, N_HEADS, HEAD_DIM)
    tail = lambda t: t[None, :, PAD_ROWS - (CONV_W - 1):, :]
    state = lambda t, n: t.reshape(1, n, N_HEADS, HEAD_DIM, D_STATE)
    return (y_prompt.reshape(bsz, seq, D_MODEL), y_sample.reshape(nd, dec_t, D_MODEL),
            heads(k, bsz, seq), heads(v, bsz, seq), heads(ks, nd, dec_t), heads(vs, nd, dec_t),
            tail(conv_p), tail(conv_s), state(ssm_p, bsz), state(ssm_s, nd))
```

```python
import functools

import numpy as np
import jax
import jax.numpy as jnp
from jax import lax
from jax.experimental import pallas as pl
from jax.experimental.pallas import tpu as pltpu

F32 = jnp.float32
BF16 = jnp.bfloat16

D_MODEL = 1024
HEAD_DIM = 64
D_ATTN = 512
D_SSD = 512
N_HEADS = 8
SSD_GROUPS = 2
GROUP_W = D_SSD // SSD_GROUPS
D_STATE = 128
CONV_W = 4
CONV_DIM = D_SSD + 2 * SSD_GROUPS * D_STATE
CHUNK = 128
D_FF = 4096
D_PLE = 256
LN_EPS = 1e-5
WINDOW_STEPS = 128
DILATIONS = (1, 4, 16)
PHASES = 16
CACHE_LEN = 2048
DT_PAD = 128
D_PROJ = 3 * D_ATTN + D_SSD + CONV_DIM + DT_PAD
SUBLANES = 8
LANES = 128
VMEM_LIMIT = 52 * 1024 * 1024
NEG_INF = float("-inf")


def _slopes():
    return np.array([2.0 ** (-8.0 * (h + 1) / N_HEADS) for h in range(N_HEADS)], dtype=np.float64)


def _layer_norm(x, g, b):
    mu = jnp.mean(x, -1, keepdims=True)
    xc = x - mu
    var = jnp.mean(xc * xc, -1, keepdims=True)
    return xc * lax.rsqrt(var + LN_EPS) * g + b


def _silu(x):
    return x * (1.0 / (1.0 + jnp.exp(-x)))


def _dot(a, b):
    return jnp.dot(a, b, preferred_element_type=F32)


def _dot_nt(a, b):
    return lax.dot_general(a, b, (((1,), (1,)), ((), ())), preferred_element_type=F32)


def _const_spec(shape):
    nd = len(shape)
    return pl.BlockSpec(shape, lambda *_: (0,) * nd, pipeline_mode=pl.Buffered(1))


HEAD_PAIRS = N_HEADS // 2


def _inproj_body(head_major, x_ref, g_ref, b_ref, w_ref, *out_refs):
    h = _layer_norm(x_ref[...], g_ref[...], b_ref[...]).astype(BF16)
    proj = lambda lo, width: _dot(h, w_ref[:, lo:lo + width])
    if head_major:
        q_ref, k_ref, v_ref, kt_ref, vt_ref, z_ref, xbc_ref, dt_ref = out_refs
        for idx, (ref, t_ref) in enumerate(((q_ref, None), (k_ref, kt_ref), (v_ref, vt_ref))):
            res = proj(idx * D_ATTN, D_ATTN)
            for c in range(HEAD_PAIRS):
                ref[c] = res[:, c * LANES:(c + 1) * LANES]
            if t_ref is not None:
                t_ref[...] = res.T
    else:
        q_ref, k_ref, v_ref, z_ref, xbc_ref, dt_ref = out_refs
        for idx, ref in enumerate((q_ref, k_ref, v_ref)):
            ref[...] = proj(idx * D_ATTN, D_ATTN)
    lo = 3 * D_ATTN
    for ref, width in ((z_ref, D_SSD), (xbc_ref, CONV_DIM), (dt_ref, DT_PAD)):
        ref[...] = proj(lo, width)
        lo += width


def _in_proj(x, g, b, w, *, head_major, tm):
    bsz, seq, _ = x.shape
    row = lambda width: pl.BlockSpec((None, tm, width), lambda i, j: (i, j, 0))
    row_shape = lambda width: jax.ShapeDtypeStruct((bsz, seq, width), F32)
    if head_major:
        slab = pl.BlockSpec((None, HEAD_PAIRS, tm, LANES), lambda i, j: (i, 0, j, 0))
        slab_shape = jax.ShapeDtypeStruct((bsz, HEAD_PAIRS, seq, LANES), F32)
        tr = pl.BlockSpec((None, D_ATTN, tm), lambda i, j: (i, 0, j))
        tr_shape = jax.ShapeDtypeStruct((bsz, D_ATTN, seq), F32)
        qkv_specs, qkv_shapes = [slab, slab, slab, tr, tr], [slab_shape] * 3 + [tr_shape] * 2
    else:
        qkv_specs, qkv_shapes = [row(D_ATTN)] * 3, [row_shape(D_ATTN)] * 3
    rest = (D_SSD, CONV_DIM, DT_PAD)
    return pl.pallas_call(
        functools.partial(_inproj_body, head_major),
        grid=(bsz, seq // tm),
        in_specs=[row(D_MODEL), _const_spec((1, D_MODEL)), _const_spec((1, D_MODEL)),
                  _const_spec((D_MODEL, D_PROJ))],
        out_specs=qkv_specs + [row(wd) for wd in rest],
        out_shape=qkv_shapes + [row_shape(wd) for wd in rest],
        compiler_params=pltpu.CompilerParams(dimension_semantics=("parallel", "parallel"),
                                             vmem_limit_bytes=VMEM_LIMIT),
        name="in_proj",
    )(x, g, b, w)


FF_CHUNK = 1024


def _post_body(alpha, x_ref, attn_ref, ssd_ref, pe_ref, gin_ref, bin_ref, wout_ref, g1_ref, b1_ref,
               wup_ref, wdown_ref, g2_ref, b2_ref, wgate_ref, wple_ref, g3_ref, b3_ref, y_ref):
    xn = _layer_norm(x_ref[...], gin_ref[...], bin_ref[...])
    mixed = jnp.concatenate([attn_ref[c] for c in range(HEAD_PAIRS)] + [ssd_ref[...]],
                            axis=-1).astype(BF16)
    h = _layer_norm(alpha * xn + _dot(mixed, wout_ref[...]), g1_ref[...], b1_ref[...])
    hb = h.astype(BF16)
    u = None
    for c in range(D_FF // FF_CHUNK):
        a = jnp.maximum(_dot(hb, wup_ref[:, c * FF_CHUNK:(c + 1) * FF_CHUNK]), 0.0)
        part = _dot((a * a).astype(BF16), wdown_ref[c * FF_CHUNK:(c + 1) * FF_CHUNK, :])
        u = part if u is None else u + part
    h = _layer_norm(alpha * h + u, g2_ref[...], b2_ref[...])
    gate = 1.0 / (1.0 + jnp.exp(-_dot(h.astype(BF16), wgate_ref[...])))
    g = gate * _dot(pe_ref[...].astype(BF16), wple_ref[...])
    y_ref[...] = _layer_norm(alpha * h + g, g3_ref[...], b3_ref[...])


def _post(x, attn, ssd, pe, gin, bin_, wout, g1, b1, wup, wdown, g2, b2, wgate, wple, g3, b3, *,
          alpha, tm):
    bsz, seq, _ = x.shape
    row = lambda width: pl.BlockSpec((None, tm, width), lambda i, j: (i, j, 0))
    slab = pl.BlockSpec((None, HEAD_PAIRS, tm, LANES), lambda i, j: (i, 0, j, 0))
    vec = _const_spec((1, D_MODEL))
    return pl.pallas_call(
        functools.partial(_post_body, alpha),
        grid=(bsz, seq // tm),
        in_specs=[row(D_MODEL), slab, row(D_SSD), row(D_PLE), vec, vec,
                  _const_spec((D_MODEL, D_MODEL)), vec, vec,
                  _const_spec((D_MODEL, D_FF)), _const_spec((D_FF, D_MODEL)), vec, vec,
                  _const_spec((D_MODEL, D_MODEL)), _const_spec((D_PLE, D_MODEL)), vec, vec],
        out_specs=row(D_MODEL),
        out_shape=jax.ShapeDtypeStruct((bsz, seq, D_MODEL), F32),
        compiler_params=pltpu.CompilerParams(dimension_semantics=("parallel", "parallel"),
                                             vmem_limit_bytes=VMEM_LIMIT),
        name="post",
    )(x, attn, ssd, pe, gin, bin_, wout, g1, b1, wup, wdown, g2, b2, wgate, wple, g3, b3)


U_ROWS = CACHE_LEN // PHASES
TAB3_LO, TAB2_LO, TAB1_LO, TAB_W = 0, 128, 384, 640
BLOCKS_IN_FLIGHT = 4


def _prompt_bias_tables():
    slopes = _slopes()
    u = np.arange(U_ROWS)
    d3 = (u[:, None] - u[None, :]).astype(np.float64)
    k4, ul4 = np.meshgrid(np.arange(4), np.arange(32), indexing="ij")
    j2 = (4 * ul4 + k4).reshape(-1)
    d2 = j2[:, None] - np.concatenate([j2 - WINDOW_STEPS, j2])[None, :]
    p16, ul16 = np.meshgrid(np.arange(16), np.arange(8), indexing="ij")
    t1 = (16 * ul16 + p16).reshape(-1)
    d1 = t1[:, None] - np.concatenate([t1 - WINDOW_STEPS, t1])[None, :]
    tabs = []
    for dist, dil in ((d3, 16), (d2, 4), (d1, 1)):
        valid = (dist >= 0) & (dist <= WINDOW_STEPS)
        per_head = [np.where(valid, -slopes[h] * dist * dil, NEG_INF) for h in range(N_HEADS)]
        tabs.append(np.stack(per_head))
    tab = np.concatenate(tabs, axis=-1)
    return jnp.asarray(tab.reshape(HEAD_PAIRS, 2 * U_ROWS, TAB_W), dtype=F32)


def _attn_block(q, k, v, bias, first_head):
    q2 = jnp.concatenate([jnp.where(first_head, q, 0.0), jnp.where(first_head, 0.0, q)], axis=0)
    s = _dot_nt(q2.astype(BF16), k.astype(BF16)) + bias
    m = jnp.max(s, -1, keepdims=True)
    e = jnp.exp(s - m)
    den = jnp.sum(e, -1, keepdims=True)
    pv = _dot(e.astype(BF16), v.astype(BF16))
    half = q.shape[0]
    o = jnp.where(first_head, pv[:half], pv[half:])
    m2 = jnp.where(first_head, m[:half], m[half:])
    den2 = jnp.where(first_head, den[:half], den[half:])
    return o / den2, m2 + jnp.log(den2)


def _attn_prompt_body(q_ref, k_ref, v_ref, tab_ref, o_ref, qs, ks, vs, o_scr, l_scr):
    def phase_rows(p):
        return pl.ds(p, U_ROWS, stride=PHASES)

    def stage(p, carry):
        qs[p] = q_ref[phase_rows(p), :] * (HEAD_DIM ** -0.5)
        ks[p] = k_ref[phase_rows(p), :]
        vs[p] = v_ref[phase_rows(p), :]
        return carry
    lax.fori_loop(0, PHASES, stage, 0)

    lane = lax.broadcasted_iota(jnp.int32, (U_ROWS, LANES), 1)
    first_head = lane < HEAD_DIM

    def branch3(p, carry):
        o, l = _attn_block(qs[p], ks[p], vs[p], tab_ref[:, TAB3_LO:TAB3_LO + 128], first_head)
        o_scr[0, p] = o
        l_scr[0, p] = l
        return carry
    lax.fori_loop(0, PHASES, branch3, 0, unroll=BLOCKS_IN_FLIGHT)

    def branch2(r, carry):
        for n in range(4):
            rows = pl.ds(32 * n, 32)
            gather = lambda src, rr: jnp.concatenate([src[4 * kk + r, rr, :] for kk in range(4)], 0)
            q = gather(qs, rows)
            kc, vc = gather(ks, rows), gather(vs, rows)
            if n == 0:
                k, v, bias = kc, vc, tab_ref[:, TAB2_LO + 128:TAB2_LO + 256]
            else:
                prev = pl.ds(32 * (n - 1), 32)
                k = jnp.concatenate([gather(ks, prev), kc], 0)
                v = jnp.concatenate([gather(vs, prev), vc], 0)
                bias = tab_ref[:, TAB2_LO:TAB2_LO + 256]
            o, l = _attn_block(q, k, v, bias, first_head)
            for kk in range(4):
                o_scr[1, 4 * kk + r, rows, :] = o[32 * kk:32 * kk + 32]
                l_scr[1, 4 * kk + r, rows, :] = l[32 * kk:32 * kk + 32]
        return carry
    lax.fori_loop(0, 4, branch2, 0)

    def branch1_block(n, with_prev):
        rows = pl.ds(pl.multiple_of(n * SUBLANES, SUBLANES), SUBLANES)
        gather = lambda src, rr: jnp.concatenate([src[p, rr, :] for p in range(PHASES)], 0)
        q = gather(qs, rows)
        kc, vc = gather(ks, rows), gather(vs, rows)
        if with_prev:
            prev = pl.ds(pl.multiple_of((n - 1) * SUBLANES, SUBLANES), SUBLANES)
            k = jnp.concatenate([gather(ks, prev), kc], 0)
            v = jnp.concatenate([gather(vs, prev), vc], 0)
            bias = tab_ref[:, TAB1_LO:TAB1_LO + 256]
        else:
            k, v, bias = kc, vc, tab_ref[:, TAB1_LO + 128:TAB1_LO + 256]
        o, l = _attn_block(q, k, v, bias, first_head)
        for p in range(PHASES):
            o_scr[2, p, rows, :] = o[SUBLANES * p:SUBLANES * (p + 1)]
            l_scr[2, p, rows, :] = l[SUBLANES * p:SUBLANES * (p + 1)]

    branch1_block(0, False)

    def branch1(n, carry):
        branch1_block(n, True)
        return carry
    lax.fori_loop(1, U_ROWS // SUBLANES, branch1, 0, unroll=3)

    def merge(p, carry):
        l3, l2, l1 = l_scr[0, p], l_scr[1, p], l_scr[2, p]
        top = jnp.maximum(jnp.maximum(l3, l2), l1)
        e3, e2, e1 = jnp.exp(l3 - top), jnp.exp(l2 - top), jnp.exp(l1 - top)
        total = e3 + e2 + e1
        o_ref[phase_rows(p), :] = (e3 * o_scr[0, p] + e2 * o_scr[1, p] + e1 * o_scr[2, p]) / total
        return carry
    lax.fori_loop(0, PHASES, merge, 0)


def _attn_prompt(q, k, v):
    bsz, _, seq, _ = k.shape
    assert seq == CACHE_LEN, "prompt attention is laid out for a 2048-token prompt"
    slab = pl.BlockSpec((None, None, seq, LANES), lambda b, c: (b, c, 0, 0))
    tile = (PHASES, U_ROWS, LANES)
    return pl.pallas_call(
        _attn_prompt_body,
        grid=(bsz, HEAD_PAIRS),
        in_specs=[slab, slab, slab,
                  pl.BlockSpec((None, 2 * U_ROWS, TAB_W), lambda b, c: (c, 0, 0))],
        out_specs=slab,
        out_shape=jax.ShapeDtypeStruct((bsz, HEAD_PAIRS, seq, LANES), F32),
        scratch_shapes=[pltpu.VMEM(tile, F32), pltpu.VMEM(tile, F32), pltpu.VMEM(tile, F32),
                        pltpu.VMEM((3,) + tile, F32), pltpu.VMEM((3,) + tile, F32)],
        compiler_params=pltpu.CompilerParams(dimension_semantics=("parallel", "parallel"),
                                             vmem_limit_bytes=VMEM_LIMIT),
        name="attn_prompt",
    )(q, k, v, _prompt_bias_tables())


DEC_T = 8
QH = DEC_T * N_HEADS
B2_SPAN, B1_SPAN = 512, 128
SAMPLE_TAB_W = CACHE_LEN + B2_SPAN + B1_SPAN


def _sample_bias_tables():
    slopes = _slopes()
    i = np.repeat(np.arange(DEC_T), N_HEADS)[:, None]
    sl = np.tile(slopes, DEC_T)[:, None]

    def cache_bias(span, dil):
        t = CACHE_LEN - span + np.arange(span)[None, :]
        dist = CACHE_LEN + i - t
        valid = (dist % dil == 0) & (dist <= WINDOW_STEPS * dil)
        return np.where(valid, -sl * dist, NEG_INF)

    cache_tab = np.concatenate([cache_bias(CACHE_LEN, 16), cache_bias(B2_SPAN, 4),
                                cache_bias(B1_SPAN, 1)], axis=1)
    c = np.arange(CHUNK)[None, :]
    dn = i - c
    is_new = c < DEC_T
    n1 = np.where(is_new & (dn >= 0), -sl * dn, NEG_INF)
    n2 = np.where(is_new & ((dn == 0) | (dn == 4)), -sl * dn, NEG_INF)
    n3 = np.where(is_new & (dn == 0), 0.0, NEG_INF)
    return jnp.asarray(cache_tab, dtype=F32), jnp.asarray(np.stack([n1, n2, n3]), dtype=F32)


def _head_selector():
    h_row = np.tile(np.arange(N_HEADS), DEC_T)[:, None]
    h_col = (np.arange(D_ATTN) // HEAD_DIM)[None, :]
    return jnp.asarray((h_row == h_col).astype(np.float32))


def _attn_sample_body(q_ref, kn_ref, vn_ref, kt_ref, vt_ref, tab_ref, tabn_ref, sel_ref, o_ref):
    sel = sel_ref[...]
    q = q_ref[...] * (HEAD_DIM ** -0.5)
    q_rows = jnp.concatenate([jnp.broadcast_to(q[i:i + 1, :], (N_HEADS, D_ATTN))
                              for i in range(DEC_T)], 0)
    qall = (q_rows * sel).astype(BF16)
    pad = jnp.zeros((CHUNK - DEC_T, D_ATTN), F32)
    kn = jnp.concatenate([kn_ref[...], pad], 0).astype(BF16)
    vn = jnp.concatenate([vn_ref[...], pad], 0).astype(BF16)

    s_all = _dot(qall, kt_ref[...].astype(BF16))
    s_new = _dot_nt(qall, kn)
    lo2, lo1 = CACHE_LEN - B2_SPAN, CACHE_LEN - B1_SPAN

    def branch(s_cache, s_fresh, v_t):
        m = jnp.maximum(jnp.max(s_cache, -1, keepdims=True), jnp.max(s_fresh, -1, keepdims=True))
        e_c, e_f = jnp.exp(s_cache - m), jnp.exp(s_fresh - m)
        den = jnp.sum(e_c, -1, keepdims=True) + jnp.sum(e_f, -1, keepdims=True)
        acc = _dot_nt(e_c.astype(BF16), v_t.astype(BF16)) + _dot(e_f.astype(BF16), vn)
        return m, den, acc

    m3, den3, acc3 = branch(s_all + tab_ref[:, :CACHE_LEN], s_new + tabn_ref[2], vt_ref[...])
    m2, den2, acc2 = branch(s_all[:, lo2:] + tab_ref[:, CACHE_LEN:CACHE_LEN + B2_SPAN],
                            s_new + tabn_ref[1], vt_ref[:, lo2:])
    m1, den1, acc1 = branch(s_all[:, lo1:] + tab_ref[:, CACHE_LEN + B2_SPAN:],
                            s_new + tabn_ref[0], vt_ref[:, lo1:])

    top = jnp.maximum(jnp.maximum(m1, m2), m3)
    sc1, sc2, sc3 = den1 * jnp.exp(m1 - top), den2 * jnp.exp(m2 - top), den3 * jnp.exp(m3 - top)
    total = sc1 + sc2 + sc3
    mixed = ((sc1 / total) * (acc1 / den1) + (sc2 / total) * (acc2 / den2)
             + (sc3 / total) * (acc3 / den3)) * sel
    for i in range(DEC_T):
        row = jnp.sum(mixed[N_HEADS * i:N_HEADS * (i + 1)], axis=0, keepdims=True)
        for c in range(HEAD_PAIRS):
            o_ref[c, i:i + 1, :] = row[:, c * LANES:(c + 1) * LANES]


def _attn_sample(q, k_new, v_new, cache_kt, cache_vt):
    n = q.shape[0]
    assert cache_kt.shape[1:] == (D_ATTN, CACHE_LEN) and q.shape[1] == DEC_T
    tok = pl.BlockSpec((None, DEC_T, D_ATTN), lambda i: (i, 0, 0))
    cache = pl.BlockSpec((None, D_ATTN, CACHE_LEN), lambda i: (i, 0, 0))
    cache_tab, new_tab = _sample_bias_tables()
    return pl.pallas_call(
        _attn_sample_body,
        grid=(n,),
        in_specs=[tok, tok, tok, cache, cache, _const_spec((QH, SAMPLE_TAB_W)),
                  _const_spec((3, QH, CHUNK)), _const_spec((QH, D_ATTN))],
        out_specs=pl.BlockSpec((None, HEAD_PAIRS, DEC_T, LANES), lambda i: (0, 0, i, 0)),
        out_shape=jax.ShapeDtypeStruct((1, HEAD_PAIRS, n * DEC_T, LANES), F32),
        compiler_params=pltpu.CompilerParams(dimension_semantics=("parallel",),
                                             vmem_limit_bytes=VMEM_LIMIT),
        name="attn_sample",
    )(q, k_new, v_new, cache_kt, cache_vt, cache_tab, new_tab, _head_selector())


HEADS_PER_GROUP = N_HEADS // SSD_GROUPS
PAD_ROWS = SUBLANES


def _ssd_body(rows, z_ref, xbc_ref, dt_ref, cp_ref, h0_ref, cw_ref, cb_ref, dtb_ref, alog_ref,
              dsk_ref, nw_ref, y_ref, cn_ref, hn_ref, xpad, hst):
    c_id = pl.program_id(1)

    @pl.when(c_id == 0)
    def _init():
        xpad[0:PAD_ROWS, :] = cp_ref[...]
        hst[...] = h0_ref[...]

    xpad[PAD_ROWS:PAD_ROWS + rows, :] = xbc_ref[...]
    if rows < CHUNK:
        xpad[PAD_ROWS + rows:, :] = jnp.zeros((CHUNK - rows, CONV_DIM), F32)

    conv = cb_ref[...]
    for back in range(CONV_W):
        tap = CONV_W - 1 - back
        conv = conv + xpad[pl.ds(PAD_ROWS - back, CHUNK), :] * cw_ref[tap:tap + 1, :]
    xc = _silu(conv)
    xs = xc[:, :D_SSD]

    lane = lax.broadcasted_iota(jnp.int32, (rows, DT_PAD), 1)
    dt_raw = dt_ref[...] + dtb_ref[...]
    dt = jnp.maximum(dt_raw, 0.0) + jnp.log1p(jnp.exp(-jnp.abs(dt_raw)))
    dt = jnp.where(lane < N_HEADS, dt, 0.0)
    if rows < CHUNK:
        dt = jnp.concatenate([dt, jnp.zeros((CHUNK - rows, DT_PAD), F32)], 0)
    adt = dt * (-jnp.exp(alog_ref[...]))

    ri = lax.broadcasted_iota(jnp.int32, (CHUNK, CHUNK), 0)
    ci = lax.broadcasted_iota(jnp.int32, (CHUNK, CHUNK), 1)
    causal = ri >= ci
    tril = jnp.where(causal, 1.0, 0.0).astype(F32)
    cs = jnp.dot(tril, adt, precision=lax.Precision.HIGHEST, preferred_element_type=F32)
    cs_t = cs.T

    head_of_lane = lax.broadcasted_iota(jnp.int32, (CHUNK, GROUP_W), 1) // HEAD_DIM

    def per_head(cols):
        out = cols[HEADS_PER_GROUP - 1]
        for hl in range(HEADS_PER_GROUP - 2, -1, -1):
            out = jnp.where(head_of_lane == hl, cols[hl], out)
        return out

    y_groups = []
    for g in range(SSD_GROUPS):
        heads = range(g * HEADS_PER_GROUP, (g + 1) * HEADS_PER_GROUP)
        b_g = xc[:, D_SSD + g * D_STATE:D_SSD + (g + 1) * D_STATE].astype(BF16)
        c_g = xc[:, D_SSD + (SSD_GROUPS + g) * D_STATE:
                 D_SSD + (SSD_GROUPS + g + 1) * D_STATE].astype(BF16)
        xs_g = xs[:, g * GROUP_W:(g + 1) * GROUP_W]
        cs_cols = [cs[:, h:h + 1] for h in heads]
        xdt_g = xs_g * per_head([dt[:, h:h + 1] for h in heads])
        xdt_b = xdt_g.astype(BF16)
        gram = _dot_nt(c_g, b_g)

        y_diag = jnp.zeros((CHUNK, GROUP_W), F32)
        for hl, h in enumerate(heads):
            seg = jnp.where(causal, cs_cols[hl] - cs_t[h:h + 1, :], NEG_INF)
            weights = (gram * jnp.exp(seg)).astype(BF16)
            y_diag = y_diag + jnp.where(head_of_lane == hl, _dot(weights, xdt_b), 0.0)

        h_prev = hst[g * GROUP_W:(g + 1) * GROUP_W, :]
        y_off = _dot_nt(c_g, h_prev.astype(BF16)) * per_head([jnp.exp(col) for col in cs_cols])
        y_groups.append(y_diag + y_off)

        last = [cs[CHUNK - 1:CHUNK, h:h + 1] for h in heads]
        to_end = per_head([jnp.exp(last[hl] - cs_cols[hl]) for hl in range(HEADS_PER_GROUP)])
        new_states = _dot((xdt_g * to_end).T.astype(BF16), b_g)
        keep = jnp.concatenate([jnp.broadcast_to(jnp.exp(last[hl]), (HEAD_DIM, D_STATE))
                                for hl in range(HEADS_PER_GROUP)], 0)
        hst[g * GROUP_W:(g + 1) * GROUP_W, :] = h_prev * keep + new_states

    y = jnp.concatenate(y_groups, axis=-1) + dsk_ref[...] * xs
    y = y[:rows] * _silu(z_ref[...])
    normed = []
    for g in range(SSD_GROUPS):
        yg = y[:, g * GROUP_W:(g + 1) * GROUP_W]
        normed.append(yg * lax.rsqrt(jnp.mean(yg * yg, -1, keepdims=True) + LN_EPS))
    y_ref[...] = jnp.concatenate(normed, axis=-1) * nw_ref[...]

    cn_ref[...] = xpad[rows:rows + PAD_ROWS, :]
    hn_ref[...] = hst[...]
    if rows == CHUNK:
        xpad[0:PAD_ROWS, :] = xpad[CHUNK:CHUNK + PAD_ROWS, :]


def _ssd(z, xbc, dt, conv_prev, h0, cw, cb, dtb, alog, dsk, nw):
    n, seq, _ = z.shape
    rows = min(CHUNK, seq)
    assert seq % rows == 0
    nchunks = seq // rows
    seq_blk = lambda width: pl.BlockSpec((None, rows, width), lambda s, c: (s, c, 0))
    per_seq = lambda r, width: pl.BlockSpec((None, r, width), lambda s, c: (s, 0, 0))
    return pl.pallas_call(
        functools.partial(_ssd_body, rows),
        grid=(n, nchunks),
        in_specs=[seq_blk(D_SSD), seq_blk(CONV_DIM), seq_blk(DT_PAD),
                  per_seq(PAD_ROWS, CONV_DIM), per_seq(D_SSD, D_STATE),
                  _const_spec((CONV_W, CONV_DIM)), _const_spec((1, CONV_DIM)),
                  _const_spec((1, DT_PAD)), _const_spec((1, DT_PAD)),
                  _const_spec((1, D_SSD)), _const_spec((1, D_SSD))],
        out_specs=[seq_blk(D_SSD), per_seq(PAD_ROWS, CONV_DIM), per_seq(D_SSD, D_STATE)],
        out_shape=[jax.ShapeDtypeStruct((n, seq, D_SSD), F32),
                   jax.ShapeDtypeStruct((n, PAD_ROWS, CONV_DIM), F32),
                   jax.ShapeDtypeStruct((n, D_SSD, D_STATE), F32)],
        scratch_shapes=[pltpu.VMEM((PAD_ROWS + CHUNK, CONV_DIM), F32),
                        pltpu.VMEM((D_SSD, D_STATE), F32)],
        compiler_params=pltpu.CompilerParams(dimension_semantics=("parallel", "arbitrary"),
                                             vmem_limit_bytes=VMEM_LIMIT),
        name="ssd",
    )(z, xbc, dt, conv_prev, h0, cw, cb, dtb, alog, dsk, nw)


def _row(v, width=None):
    v = v.reshape(1, -1).astype(F32)
    if width is not None and v.shape[1] < width:
        v = jnp.pad(v, ((0, 0), (0, width - v.shape[1])))
    return v


def kernel(x_prompt, x_sample, cache_k, cache_v, state_conv, state_ssm, p_prompt, p_sample,
           ln_in_g, ln_in_b, w_in, conv_w, conv_b, dt_bias, a_log, d_skip, ssd_norm_w, w_out,
           ln1_g, ln1_b, w_up, w_down, ln2_g, ln2_b, w_gate, w_ple, ln3_g, ln3_b):
    depth = w_in.shape[0]
    assert depth == 1, "single-layer step"
    alpha = (2 * depth) ** 0.25
    bsz, seq, _ = x_prompt.shape
    nd, dec_t, _ = x_sample.shape
    lyr = 0

    w_proj = jnp.pad(w_in[lyr], ((0, 0), (0, D_PROJ - w_in.shape[2]))).astype(BF16)
    gin, bin_ = _row(ln_in_g), _row(ln_in_b)
    ssd_params = (conv_w[lyr].astype(F32), _row(conv_b[lyr]), _row(dt_bias[lyr], DT_PAD),
                  _row(a_log[lyr], DT_PAD), _row(jnp.repeat(d_skip[lyr], HEAD_DIM)),
                  _row(ssd_norm_w[lyr]))
    post_params = (gin, bin_, w_out[lyr].astype(BF16), _row(ln1_g[lyr]), _row(ln1_b[lyr]),
                   w_up[lyr].astype(BF16), w_down[lyr].astype(BF16), _row(ln2_g[lyr]),
                   _row(ln2_b[lyr]), w_gate[lyr].astype(BF16), w_ple[lyr].astype(BF16),
                   _row(ln3_g[lyr]), _row(ln3_b[lyr]))

    q, k, v, k_t, v_t, z, xbc, dt = _in_proj(x_prompt, gin, bin_, w_proj, head_major=True, tm=512)
    attn = _attn_prompt(q, k, v)
    ssd_y, conv_p, ssm_p = _ssd(z, xbc, dt, jnp.zeros((bsz, PAD_ROWS, CONV_DIM), F32),
                                jnp.zeros((bsz, D_SSD, D_STATE), F32), *ssd_params)
    y_prompt = _post(x_prompt, attn, ssd_y, p_prompt[lyr], *post_params, alpha=alpha, tm=256)

    n_tok = nd * dec_t
    flat = lambda t: t.reshape(1, n_tok, t.shape[-1])
    toks = lambda t: t.reshape(nd, dec_t, t.shape[-1])
    qs, ks, vs, zs, xbcs, dts = _in_proj(flat(x_sample), gin, bin_, w_proj, head_major=False, tm=512)
    transposed = lambda c: jnp.transpose(c, (0, 2, 3, 1)).reshape(nd, D_ATTN, CACHE_LEN)
    attn_s = _attn_sample(toks(qs), toks(ks), toks(vs), transposed(cache_k[lyr]),
                          transposed(cache_v[lyr]))
    conv_prev = jnp.pad(state_conv[lyr].astype(F32), ((0, 0), (PAD_ROWS - (CONV_W - 1), 0), (0, 0)))
    ssd_s, conv_s, ssm_s = _ssd(toks(zs), toks(xbcs), toks(dts), conv_prev,
                                state_ssm[lyr].reshape(nd, D_SSD, D_STATE).astype(F32), *ssd_params)
    y_sample = _post(flat(x_sample), attn_s, flat(ssd_s), flat(p_sample[lyr]), *post_params,
                     alpha=alpha, tm=256)

    from_t = lambda t: jnp.transpose(t.reshape(1, bsz, N_HEADS, HEAD_DIM, seq), (0, 1, 4, 2, 3))
    heads = lambda t: t.reshape(1, nd, dec_t, N_HEADS, HEAD_DIM)
    tail = lambda t: t[None, :, PAD_ROWS - (CONV_W - 1):, :]
    state = lambda t, n: t.reshape(1, n, N_HEADS, HEAD_DIM, D_STATE)
    return (y_prompt, y_sample.reshape(nd, dec_t, D_MODEL), from_t(k_t), from_t(v_t),
            heads(ks), heads(vs), tail(conv_p), tail(conv_s), state(ssm_p, bsz), state(ssm_s, nd))
```

```python
import functools

import numpy as np
import jax
import jax.numpy as jnp
from jax import lax
from jax.experimental import pallas as pl
from jax.experimental.pallas import tpu as pltpu

F32 = jnp.float32
BF16 = jnp.bfloat16

D_MODEL = 1024
HEAD_DIM = 64
D_ATTN = 512
D_SSD = 512
N_HEADS = 8
SSD_GROUPS = 2
GROUP_W = D_SSD // SSD_GROUPS
D_STATE = 128
CONV_W = 4
CONV_DIM = D_SSD + 2 * SSD_GROUPS * D_STATE
CHUNK = 128
D_FF = 4096
D_PLE = 256
LN_EPS = 1e-5
WINDOW_STEPS = 128
DILATIONS = (1, 4, 16)
PHASES = 16
CACHE_LEN = 2048
DT_PAD = 128
D_PROJ = 3 * D_ATTN + D_SSD + CONV_DIM + DT_PAD
SUBLANES = 8
LANES = 128
VMEM_LIMIT = 52 * 1024 * 1024
NEG_INF = float("-inf")


def _slopes():
    return np.array([2.0 ** (-8.0 * (h + 1) / N_HEADS) for h in range(N_HEADS)], dtype=np.float64)


def _layer_norm(x, g, b):
    mu = jnp.mean(x, -1, keepdims=True)
    xc = x - mu
    var = jnp.mean(xc * xc, -1, keepdims=True)
    return xc * lax.rsqrt(var + LN_EPS) * g + b


def _silu(x):
    return x * (1.0 / (1.0 + jnp.exp(-x)))


def _dot(a, b):
    return jnp.dot(a, b, preferred_element_type=F32)


def _dot_nt(a, b):
    return lax.dot_general(a, b, (((1,), (1,)), ((), ())), preferred_element_type=F32)


def _const_spec(shape):
    nd = len(shape)
    return pl.BlockSpec(shape, lambda *_: (0,) * nd, pipeline_mode=pl.Buffered(1))


HEAD_PAIRS = N_HEADS // 2


def _inproj_body(head_major, x_ref, g_ref, b_ref, w_ref, *out_refs):
    h = _layer_norm(x_ref[...], g_ref[...], b_ref[...]).astype(BF16)
    proj = lambda lo, width: _dot(h, w_ref[:, lo:lo + width])
    if head_major:
        q_ref, k_ref, v_ref, kt_ref, vt_ref, z_ref, xbc_ref, dt_ref = out_refs
        for idx, (ref, t_ref) in enumerate(((q_ref, None), (k_ref, kt_ref), (v_ref, vt_ref))):
            res = proj(idx * D_ATTN, D_ATTN)
            for c in range(HEAD_PAIRS):
                ref[c] = res[:, c * LANES:(c + 1) * LANES]
            if t_ref is not None:
                t_ref[...] = res.T
    else:
        q_ref, k_ref, v_ref, z_ref, xbc_ref, dt_ref = out_refs
        for idx, ref in enumerate((q_ref, k_ref, v_ref)):
            ref[...] = proj(idx * D_ATTN, D_ATTN)
    lo = 3 * D_ATTN
    for ref, width in ((z_ref, D_SSD), (xbc_ref, CONV_DIM), (dt_ref, DT_PAD)):
        ref[...] = proj(lo, width)
        lo += width


def _in_proj(x, g, b, w, *, head_major, tm):
    bsz, seq, _ = x.shape
    row = lambda width: pl.BlockSpec((None, tm, width), lambda i, j: (i, j, 0))
    row_shape = lambda width: jax.ShapeDtypeStruct((bsz, seq, width), F32)
    if head_major:
        slab = pl.BlockSpec((None, HEAD_PAIRS, tm, LANES), lambda i, j: (i, 0, j, 0))
        slab_shape = jax.ShapeDtypeStruct((bsz, HEAD_PAIRS, seq, LANES), F32)
        tr = pl.BlockSpec((None, D_ATTN, tm), lambda i, j: (i, 0, j))
        tr_shape = jax.ShapeDtypeStruct((bsz, D_ATTN, seq), F32)
        qkv_specs, qkv_shapes = [slab, slab, slab, tr, tr], [slab_shape] * 3 + [tr_shape] * 2
    else:
        qkv_specs, qkv_shapes = [row(D_ATTN)] * 3, [row_shape(D_ATTN)] * 3
    rest = (D_SSD, CONV_DIM, DT_PAD)
    return pl.pallas_call(
        functools.partial(_inproj_body, head_major),
        grid=(bsz, seq // tm),
        in_specs=[row(D_MODEL), _const_spec((1, D_MODEL)), _const_spec((1, D_MODEL)),
                  _const_spec((D_MODEL, D_PROJ))],
        out_specs=qkv_specs + [row(wd) for wd in rest],
        out_shape=qkv_shapes + [row_shape(wd) for wd in rest],
        compiler_params=pltpu.CompilerParams(dimension_semantics=("parallel", "parallel"),
                                             vmem_limit_bytes=VMEM_LIMIT),
        name="in_proj",
    )(x, g, b, w)


FF_CHUNK = 1024


def _post_body(alpha, x_ref, attn_ref, ssd_ref, pe_ref, gin_ref, bin_ref, wout_ref, g1_ref, b1_ref,
               wup_ref, wdown_ref, g2_ref, b2_ref, wgate_ref, wple_ref, g3_ref, b3_ref, y_ref):
    xn = _layer_norm(x_ref[...], gin_ref[...], bin_ref[...])
    mixed = jnp.concatenate([attn_ref[c] for c in range(HEAD_PAIRS)] + [ssd_ref[...]],
                            axis=-1).astype(BF16)
    h = _layer_norm(alpha * xn + _dot(mixed, wout_ref[...]), g1_ref[...], b1_ref[...])
    hb = h.astype(BF16)
    u = None
    for c in range(D_FF // FF_CHUNK):
        a = jnp.maximum(_dot(hb, wup_ref[:, c * FF_CHUNK:(c + 1) * FF_CHUNK]), 0.0)
        part = _dot((a * a).astype(BF16), wdown_ref[c * FF_CHUNK:(c + 1) * FF_CHUNK, :])
        u = part if u is None else u + part
    h = _layer_norm(alpha * h + u, g2_ref[...], b2_ref[...])
    gate = 1.0 / (1.0 + jnp.exp(-_dot(h.astype(BF16), wgate_ref[...])))
    g = gate * _dot(pe_ref[...].astype(BF16), wple_ref[...])
    y_ref[...] = _layer_norm(alpha * h + g, g3_ref[...], b3_ref[...])


def _post(x, attn, ssd, pe, gin, bin_, wout, g1, b1, wup, wdown, g2, b2, wgate, wple, g3, b3, *,
          alpha, tm):
    bsz, seq, _ = x.shape
    row = lambda width: pl.BlockSpec((None, tm, width), lambda i, j: (i, j, 0))
    slab = pl.BlockSpec((None, HEAD_PAIRS, tm, LANES), lambda i, j: (i, 0, j, 0))
    vec = _const_spec((1, D_MODEL))
    return pl.pallas_call(
        functools.partial(_post_body, alpha),
        grid=(bsz, seq // tm),
        in_specs=[row(D_MODEL), slab, row(D_SSD), row(D_PLE), vec, vec,
                  _const_spec((D_MODEL, D_MODEL)), vec, vec,
                  _const_spec((D_MODEL, D_FF)), _const_spec((D_FF, D_MODEL)), vec, vec,
                  _const_spec((D_MODEL, D_MODEL)), _const_spec((D_PLE, D_MODEL)), vec, vec],
        out_specs=row(D_MODEL),
        out_shape=jax.ShapeDtypeStruct((bsz, seq, D_MODEL), F32),
        compiler_params=pltpu.CompilerParams(dimension_semantics=("parallel", "parallel"),
                                             vmem_limit_bytes=VMEM_LIMIT),
        name="post",
    )(x, attn, ssd, pe, gin, bin_, wout, g1, b1, wup, wdown, g2, b2, wgate, wple, g3, b3)


U_ROWS = CACHE_LEN // PHASES
TAB3_LO, TAB2_LO, TAB1_LO, TAB_W = 0, 128, 384, 640
BLOCKS_IN_FLIGHT = 8
B1_IN_FLIGHT = 5


def _prompt_bias_tables():
    slopes = _slopes()
    u = np.arange(U_ROWS)
    d3 = (u[:, None] - u[None, :]).astype(np.float64)
    k4, ul4 = np.meshgrid(np.arange(4), np.arange(32), indexing="ij")
    j2 = (4 * ul4 + k4).reshape(-1)
    d2 = j2[:, None] - np.concatenate([j2 - WINDOW_STEPS, j2])[None, :]
    p16, ul16 = np.meshgrid(np.arange(16), np.arange(8), indexing="ij")
    t1 = (16 * ul16 + p16).reshape(-1)
    d1 = t1[:, None] - (np.arange(2 * WINDOW_STEPS) - WINDOW_STEPS)[None, :]
    tabs = []
    for dist, dil in ((d3, 16), (d2, 4), (d1, 1)):
        valid = (dist >= 0) & (dist <= WINDOW_STEPS)
        per_head = [np.where(valid, -slopes[h] * dist * dil, NEG_INF) for h in range(N_HEADS)]
        tabs.append(np.stack(per_head))
    tab = np.concatenate(tabs, axis=-1)
    return jnp.asarray(tab.reshape(HEAD_PAIRS, 2 * U_ROWS, TAB_W), dtype=F32)


def _attn_blocks(blocks, first_head):
    scores = [_dot_nt(q2, k) + bias for q2, k, _, bias in blocks]
    tops = [jnp.max(s, -1, keepdims=True) for s in scores]
    exps = [jnp.exp(s - m) for s, m in zip(scores, tops)]
    dens = [jnp.sum(e, -1, keepdims=True) for e in exps]
    pvs = [_dot(e.astype(BF16), blk[2]) for e, blk in zip(exps, blocks)]
    half = U_ROWS
    outs = []
    for pv, m, den in zip(pvs, tops, dens):
        o = jnp.where(first_head, pv[:half], pv[half:])
        m2 = jnp.where(first_head, m[:half], m[half:])
        den2 = jnp.where(first_head, den[:half], den[half:])
        outs.append((o / den2, m2 + jnp.log(den2)))
    return outs


def _attn_prompt_body(q_ref, k_ref, v_ref, tab_ref, o_ref, qs, q0p, q1p, kp, vp, kn, vn, quarter,
                      o_scr, l_scr):
    lane = lax.broadcasted_iota(jnp.int32, (U_ROWS, LANES), 1)
    first_head = lane < HEAD_DIM
    scale = HEAD_DIM ** -0.5

    def natural(i, carry):
        rows = pl.ds(pl.multiple_of(i * U_ROWS, U_ROWS), U_ROWS)
        kn[rows, :] = k_ref[rows, :].astype(BF16)
        vn[rows, :] = v_ref[rows, :].astype(BF16)
        return carry
    lax.fori_loop(0, PHASES, natural, 0)

    def split(src_ref, emit):
        for r in range(4):
            quarter[r] = src_ref[pl.ds(r, CACHE_LEN // 4, stride=4), :]
        for r in range(4):
            for kk in range(4):
                emit(4 * kk + r, quarter[r, pl.ds(kk, U_ROWS, stride=4), :])

    def emit_q(p, tile):
        tile = tile * scale
        qs[p] = tile
        q0p[p] = jnp.where(first_head, tile, 0.0).astype(BF16)
        q1p[p] = jnp.where(first_head, 0.0, tile).astype(BF16)

    def emit_to(dst):
        def emit(p, tile):
            dst[p] = tile.astype(BF16)
        return emit

    split(q_ref, emit_q)
    split(k_ref, emit_to(kp))
    split(v_ref, emit_to(vp))

    def branch3(g, carry):
        phases = [g * BLOCKS_IN_FLIGHT + j for j in range(BLOCKS_IN_FLIGHT)]
        bias = tab_ref[:, TAB3_LO:TAB3_LO + 128]
        outs = _attn_blocks([(jnp.concatenate([q0p[p], q1p[p]], 0), kp[p], vp[p], bias)
                             for p in phases], first_head)
        for p, (o, l) in zip(phases, outs):
            o_scr[0, p] = o
            l_scr[0, p] = l
        return carry
    lax.fori_loop(0, PHASES // BLOCKS_IN_FLIGHT, branch3, 0)

    def branch2(g, carry):
        blocks, where = [], []
        for r in (2 * g, 2 * g + 1):
            gather = lambda src, rr, r=r: [src[4 * kk + r, rr, :] for kk in range(4)]
            for n in range(4):
                rows = pl.ds(32 * n, 32)
                q2 = jnp.concatenate(gather(q0p, rows) + gather(q1p, rows), 0)
                if n == 0:
                    k = jnp.concatenate(gather(kp, rows), 0)
                    v = jnp.concatenate(gather(vp, rows), 0)
                    bias = tab_ref[:, TAB2_LO + 128:TAB2_LO + 256]
                else:
                    prev = pl.ds(32 * (n - 1), 32)
                    k = jnp.concatenate(gather(kp, prev) + gather(kp, rows), 0)
                    v = jnp.concatenate(gather(vp, prev) + gather(vp, rows), 0)
                    bias = tab_ref[:, TAB2_LO:TAB2_LO + 256]
                blocks.append((q2, k, v, bias))
                where.append((r, rows))
        for (r, rows), (o, l) in zip(where, _attn_blocks(blocks, first_head)):
            for kk in range(4):
                o_scr[1, 4 * kk + r, rows, :] = o[32 * kk:32 * kk + 32]
                l_scr[1, 4 * kk + r, rows, :] = l[32 * kk:32 * kk + 32]
        return carry
    lax.fori_loop(0, 2, branch2, 0)

    def branch1_blocks(ns, with_prev):
        blocks = []
        for n in ns:
            rows = pl.ds(pl.multiple_of(n * SUBLANES, SUBLANES), SUBLANES)
            q = jnp.concatenate([qs[p, rows, :] for p in range(PHASES)], 0)
            q2 = jnp.concatenate([jnp.where(first_head, q, 0.0), jnp.where(first_head, 0.0, q)],
                                 0).astype(BF16)
            if with_prev:
                keys = pl.ds(pl.multiple_of((n - 1) * U_ROWS, U_ROWS), 2 * U_ROWS)
                bias = tab_ref[:, TAB1_LO:TAB1_LO + 256]
            else:
                keys = pl.ds(0, U_ROWS)
                bias = tab_ref[:, TAB1_LO + 128:TAB1_LO + 256]
            blocks.append((q2, kn[keys, :], vn[keys, :], bias))
        for n, (o, l) in zip(ns, _attn_blocks(blocks, first_head)):
            rows = pl.ds(pl.multiple_of(n * SUBLANES, SUBLANES), SUBLANES)
            for p in range(PHASES):
                o_scr[2, p, rows, :] = o[SUBLANES * p:SUBLANES * (p + 1)]
                l_scr[2, p, rows, :] = l[SUBLANES * p:SUBLANES * (p + 1)]

    branch1_blocks([0], False)

    def branch1(g, carry):
        branch1_blocks([1 + g * B1_IN_FLIGHT + j for j in range(B1_IN_FLIGHT)], True)
        return carry
    lax.fori_loop(0, (U_ROWS // SUBLANES - 1) // B1_IN_FLIGHT, branch1, 0)

    def phase_rows(p):
        return pl.ds(p, U_ROWS, stride=PHASES)

    def merge(p, carry):
        l3, l2, l1 = l_scr[0, p], l_scr[1, p], l_scr[2, p]
        top = jnp.maximum(jnp.maximum(l3, l2), l1)
        e3, e2, e1 = jnp.exp(l3 - top), jnp.exp(l2 - top), jnp.exp(l1 - top)
        total = e3 + e2 + e1
        o_ref[phase_rows(p), :] = (e3 * o_scr[0, p] + e2 * o_scr[1, p] + e1 * o_scr[2, p]) / total
        return carry
    lax.fori_loop(0, PHASES, merge, 0)


def _attn_prompt(q, k, v):
    bsz, _, seq, _ = k.shape
    assert seq == CACHE_LEN, "prompt attention is laid out for a 2048-token prompt"
    slab = pl.BlockSpec((None, None, seq, LANES), lambda b, c: (b, c, 0, 0))
    tile = (PHASES, U_ROWS, LANES)
    return pl.pallas_call(
        _attn_prompt_body,
        grid=(bsz, HEAD_PAIRS),
        in_specs=[slab, slab, slab,
                  pl.BlockSpec((None, 2 * U_ROWS, TAB_W), lambda b, c: (c, 0, 0))],
        out_specs=slab,
        out_shape=jax.ShapeDtypeStruct((bsz, HEAD_PAIRS, seq, LANES), F32),
        scratch_shapes=[pltpu.VMEM(tile, F32)] + [pltpu.VMEM(tile, BF16)] * 4
                       + [pltpu.VMEM((seq, LANES), BF16)] * 2
                       + [pltpu.VMEM((4, seq // 4, LANES), F32),
                          pltpu.VMEM((3,) + tile, F32), pltpu.VMEM((3,) + tile, F32)],
        compiler_params=pltpu.CompilerParams(dimension_semantics=("parallel", "parallel"),
                                             vmem_limit_bytes=VMEM_LIMIT),
        name="attn_prompt",
    )(q, k, v, _prompt_bias_tables())


DEC_T = 8
QH = DEC_T * N_HEADS
B2_SPAN, B1_SPAN = 512, 128
SAMPLE_TAB_W = CACHE_LEN + B2_SPAN + B1_SPAN


def _sample_bias_tables():
    slopes = _slopes()
    i = np.repeat(np.arange(DEC_T), N_HEADS)[:, None]
    sl = np.tile(slopes, DEC_T)[:, None]

    def cache_bias(span, dil):
        t = CACHE_LEN - span + np.arange(span)[None, :]
        dist = CACHE_LEN + i - t
        valid = (dist % dil == 0) & (dist <= WINDOW_STEPS * dil)
        return np.where(valid, -sl * dist, NEG_INF)

    cache_tab = np.concatenate([cache_bias(CACHE_LEN, 16), cache_bias(B2_SPAN, 4),
                                cache_bias(B1_SPAN, 1)], axis=1)
    c = np.arange(CHUNK)[None, :]
    dn = i - c
    is_new = c < DEC_T
    n1 = np.where(is_new & (dn >= 0), -sl * dn, NEG_INF)
    n2 = np.where(is_new & ((dn == 0) | (dn == 4)), -sl * dn, NEG_INF)
    n3 = np.where(is_new & (dn == 0), 0.0, NEG_INF)
    return jnp.asarray(cache_tab, dtype=F32), jnp.asarray(np.stack([n1, n2, n3]), dtype=F32)


def _head_selector():
    h_row = np.tile(np.arange(N_HEADS), DEC_T)[:, None]
    h_col = (np.arange(D_ATTN) // HEAD_DIM)[None, :]
    return jnp.asarray((h_row == h_col).astype(np.float32))


def _attn_sample_body(q_ref, kn_ref, vn_ref, kt_ref, vt_ref, tab_ref, tabn_ref, sel_ref, o_ref):
    sel = sel_ref[...]
    q = q_ref[...] * (HEAD_DIM ** -0.5)
    q_rows = jnp.concatenate([jnp.broadcast_to(q[i:i + 1, :], (N_HEADS, D_ATTN))
                              for i in range(DEC_T)], 0)
    qall = (q_rows * sel).astype(BF16)
    pad = jnp.zeros((CHUNK - DEC_T, D_ATTN), F32)
    kn = jnp.concatenate([kn_ref[...], pad], 0).astype(BF16)
    vn = jnp.concatenate([vn_ref[...], pad], 0).astype(BF16)

    s_all = _dot(qall, kt_ref[...].astype(BF16))
    s_new = _dot_nt(qall, kn)
    lo2, lo1 = CACHE_LEN - B2_SPAN, CACHE_LEN - B1_SPAN

    def branch(s_cache, s_fresh, v_t):
        m = jnp.maximum(jnp.max(s_cache, -1, keepdims=True), jnp.max(s_fresh, -1, keepdims=True))
        e_c, e_f = jnp.exp(s_cache - m), jnp.exp(s_fresh - m)
        den = jnp.sum(e_c, -1, keepdims=True) + jnp.sum(e_f, -1, keepdims=True)
        acc = _dot_nt(e_c.astype(BF16), v_t.astype(BF16)) + _dot(e_f.astype(BF16), vn)
        return m, den, acc

    m3, den3, acc3 = branch(s_all + tab_ref[:, :CACHE_LEN], s_new + tabn_ref[2], vt_ref[...])
    m2, den2, acc2 = branch(s_all[:, lo2:] + tab_ref[:, CACHE_LEN:CACHE_LEN + B2_SPAN],
                            s_new + tabn_ref[1], vt_ref[:, lo2:])
    m1, den1, acc1 = branch(s_all[:, lo1:] + tab_ref[:, CACHE_LEN + B2_SPAN:],
                            s_new + tabn_ref[0], vt_ref[:, lo1:])

    top = jnp.maximum(jnp.maximum(m1, m2), m3)
    sc1, sc2, sc3 = den1 * jnp.exp(m1 - top), den2 * jnp.exp(m2 - top), den3 * jnp.exp(m3 - top)
    total = sc1 + sc2 + sc3
    mixed = ((sc1 / total) * (acc1 / den1) + (sc2 / total) * (acc2 / den2)
             + (sc3 / total) * (acc3 / den3)) * sel
    for i in range(DEC_T):
        row = jnp.sum(mixed[N_HEADS * i:N_HEADS * (i + 1)], axis=0, keepdims=True)
        for c in range(HEAD_PAIRS):
            o_ref[c, i:i + 1, :] = row[:, c * LANES:(c + 1) * LANES]


def _attn_sample(q, k_new, v_new, cache_kt, cache_vt):
    n = q.shape[0]
    assert cache_kt.shape[1:] == (D_ATTN, CACHE_LEN) and q.shape[1] == DEC_T
    tok = pl.BlockSpec((None, DEC_T, D_ATTN), lambda i: (i, 0, 0))
    cache = pl.BlockSpec((None, D_ATTN, CACHE_LEN), lambda i: (i, 0, 0))
    cache_tab, new_tab = _sample_bias_tables()
    return pl.pallas_call(
        _attn_sample_body,
        grid=(n,),
        in_specs=[tok, tok, tok, cache, cache, _const_spec((QH, SAMPLE_TAB_W)),
                  _const_spec((3, QH, CHUNK)), _const_spec((QH, D_ATTN))],
        out_specs=pl.BlockSpec((None, HEAD_PAIRS, DEC_T, LANES), lambda i: (0, 0, i, 0)),
        out_shape=jax.ShapeDtypeStruct((1, HEAD_PAIRS, n * DEC_T, LANES), F32),
        compiler_params=pltpu.CompilerParams(dimension_semantics=("parallel",),
                                             vmem_limit_bytes=VMEM_LIMIT),
        name="attn_sample",
    )(q, k_new, v_new, cache_kt, cache_vt, cache_tab, new_tab, _head_selector())


HEADS_PER_GROUP = N_HEADS // SSD_GROUPS
PAD_ROWS = SUBLANES
SHORT_SEQ = SUBLANES
SEQS_PER_TILE = CHUNK // SHORT_SEQ


def _ssd_body(packed, z_ref, xbc_ref, dt_ref, cp_ref, h0_ref, cw_ref, cb_ref, dtb_ref, alog_ref,
              dsk_ref, nw_ref, y_ref, hn_ref, xpad, aux):
    if packed:
        aux[0:CHUNK, :] = cp_ref[...]
        aux[CHUNK:, :] = jnp.zeros((PAD_ROWS, CONV_DIM), F32)
        xpad[0:PAD_ROWS, :] = jnp.zeros((PAD_ROWS, CONV_DIM), F32)
    else:
        @pl.when(pl.program_id(1) == 0)
        def _init():
            xpad[0:PAD_ROWS, :] = cp_ref[...]
            aux[...] = h0_ref[...]
    xpad[PAD_ROWS:, :] = xbc_ref[...]

    step = lax.broadcasted_iota(jnp.int32, (CHUNK, CONV_DIM), 0) % SHORT_SEQ
    conv = cb_ref[...]
    for back in range(CONV_W):
        tap = CONV_W - 1 - back
        rows_back = xpad[pl.ds(PAD_ROWS - back, CHUNK), :]
        if packed and back:
            rows_back = jnp.where(step < back, aux[pl.ds(PAD_ROWS - back, CHUNK), :], rows_back)
        conv = conv + rows_back * cw_ref[tap:tap + 1, :]
    xc = _silu(conv)
    xs = xc[:, :D_SSD]

    lane = lax.broadcasted_iota(jnp.int32, (CHUNK, DT_PAD), 1)
    dt_raw = dt_ref[...] + dtb_ref[...]
    dt = jnp.maximum(dt_raw, 0.0) + jnp.log1p(jnp.exp(-jnp.abs(dt_raw)))
    dt = jnp.where(lane < N_HEADS, dt, 0.0)
    adt = dt * (-jnp.exp(alog_ref[...]))

    ri = lax.broadcasted_iota(jnp.int32, (CHUNK, CHUNK), 0)
    ci = lax.broadcasted_iota(jnp.int32, (CHUNK, CHUNK), 1)
    causal = ri >= ci
    if packed:
        causal = causal & (ri // SHORT_SEQ == ci // SHORT_SEQ)
    exact_dot = functools.partial(jnp.dot, precision=lax.Precision.HIGHEST,
                                  preferred_element_type=F32)
    cs = exact_dot(jnp.where(causal, 1.0, 0.0).astype(F32), adt)
    cs_t = cs.T
    if packed:
        pick_last = ci == (ri // SHORT_SEQ) * SHORT_SEQ + (SHORT_SEQ - 1)
        cs_end = exact_dot(jnp.where(pick_last, 1.0, 0.0).astype(F32), cs)
    else:
        cs_end = cs[CHUNK - 1:CHUNK, :]

    head_of_lane = lax.broadcasted_iota(jnp.int32, (CHUNK, GROUP_W), 1) // HEAD_DIM

    def per_head(cols):
        out = cols[HEADS_PER_GROUP - 1]
        for hl in range(HEADS_PER_GROUP - 2, -1, -1):
            out = jnp.where(head_of_lane == hl, cols[hl], out)
        return out

    y_groups = []
    for g in range(SSD_GROUPS):
        heads = range(g * HEADS_PER_GROUP, (g + 1) * HEADS_PER_GROUP)
        grp = slice(g * GROUP_W, (g + 1) * GROUP_W)
        b_g = xc[:, D_SSD + g * D_STATE:D_SSD + (g + 1) * D_STATE].astype(BF16)
        c_g = xc[:, D_SSD + (SSD_GROUPS + g) * D_STATE:
                 D_SSD + (SSD_GROUPS + g + 1) * D_STATE].astype(BF16)
        cs_cols = [cs[:, h:h + 1] for h in heads]
        xdt_g = xs[:, grp] * per_head([dt[:, h:h + 1] for h in heads])
        xdt_b = xdt_g.astype(BF16)
        gram = _dot_nt(c_g, b_g)

        y_diag = jnp.zeros((CHUNK, GROUP_W), F32)
        for hl, h in enumerate(heads):
            seg = jnp.where(causal, cs_cols[hl] - cs_t[h:h + 1, :], NEG_INF)
            weights = (gram * jnp.exp(seg)).astype(BF16)
            y_diag = y_diag + jnp.where(head_of_lane == hl, _dot(weights, xdt_b), 0.0)

        to_end = per_head([jnp.exp(cs_end[:, h:h + 1] - cs_cols[hl]) for hl, h in enumerate(heads)])
        decayed_t = (xdt_g * to_end).T
        carried = per_head([jnp.exp(col) for col in cs_cols])
        if packed:
            h_prev = h0_ref[:, grp, :]
            wide = _dot_nt(c_g, h_prev.reshape(SEQS_PER_TILE * GROUP_W, D_STATE).astype(BF16))
            y_off = jnp.concatenate(
                [wide[s * SHORT_SEQ:(s + 1) * SHORT_SEQ, s * GROUP_W:(s + 1) * GROUP_W]
                 for s in range(SEQS_PER_TILE)], 0) * carried
            seq_of_step = lax.broadcasted_iota(jnp.int32, (GROUP_W, CHUNK), 1) // SHORT_SEQ
            per_seq = jnp.concatenate([jnp.where(seq_of_step == s, decayed_t, 0.0)
                                       for s in range(SEQS_PER_TILE)], 0).astype(BF16)
            new_states = _dot(per_seq, b_g).reshape(SEQS_PER_TILE, GROUP_W, D_STATE)
            for s in range(SEQS_PER_TILE):
                row = s * SHORT_SEQ
                keep = jnp.concatenate(
                    [jnp.broadcast_to(jnp.exp(cs_end[row:row + 1, h:h + 1]), (HEAD_DIM, D_STATE))
                     for h in heads], 0)
                hn_ref[s, grp, :] = h_prev[s] * keep + new_states[s]
        else:
            h_prev = aux[grp, :]
            y_off = _dot_nt(c_g, h_prev.astype(BF16)) * carried
            keep = jnp.concatenate(
                [jnp.broadcast_to(jnp.exp(cs_end[:, h:h + 1]), (HEAD_DIM, D_STATE)) for h in heads], 0)
            aux[grp, :] = h_prev * keep + _dot(decayed_t.astype(BF16), b_g)
        y_groups.append(y_diag + y_off)

    y = jnp.concatenate(y_groups, axis=-1) + dsk_ref[...] * xs
    y = y * _silu(z_ref[...])
    normed = []
    for g in range(SSD_GROUPS):
        yg = y[:, g * GROUP_W:(g + 1) * GROUP_W]
        normed.append(yg * lax.rsqrt(jnp.mean(yg * yg, -1, keepdims=True) + LN_EPS))
    y_ref[...] = jnp.concatenate(normed, axis=-1) * nw_ref[...]

    if not packed:
        hn_ref[...] = aux[...]
        xpad[0:PAD_ROWS, :] = xpad[CHUNK:, :]


def _ssd(z, xbc, dt, conv_prev, h0, cw, cb, dtb, alog, dsk, nw):
    n, seq, _ = z.shape
    packed = seq == SHORT_SEQ
    if packed:
        assert n % SEQS_PER_TILE == 0
        tiles, nchunks, per_tile = n // SEQS_PER_TILE, 1, SEQS_PER_TILE
        fold = lambda t: t.reshape(tiles, CHUNK, t.shape[-1])
        z, xbc, dt, conv_prev = fold(z), fold(xbc), fold(dt), fold(conv_prev)
        history = pl.BlockSpec((None, CHUNK, CONV_DIM), lambda s, c: (s, 0, 0))
        state = pl.BlockSpec((per_tile, D_SSD, D_STATE), lambda s, c: (s, 0, 0))
        aux = pltpu.VMEM((CHUNK + PAD_ROWS, CONV_DIM), F32)
    else:
        assert seq % CHUNK == 0
        tiles, nchunks = n, seq // CHUNK
        history = pl.BlockSpec((None, PAD_ROWS, CONV_DIM), lambda s, c: (s, 0, 0))
        state = pl.BlockSpec((None, D_SSD, D_STATE), lambda s, c: (s, 0, 0))
        aux = pltpu.VMEM((D_SSD, D_STATE), F32)
    tile = lambda width: pl.BlockSpec((None, CHUNK, width), lambda s, c: (s, c, 0))
    y, h_new = pl.pallas_call(
        functools.partial(_ssd_body, packed),
        grid=(tiles, nchunks),
        in_specs=[tile(D_SSD), tile(CONV_DIM), tile(DT_PAD), history, state,
                  _const_spec((CONV_W, CONV_DIM)), _const_spec((1, CONV_DIM)),
                  _const_spec((1, DT_PAD)), _const_spec((1, DT_PAD)),
                  _const_spec((1, D_SSD)), _const_spec((1, D_SSD))],
        out_specs=[tile(D_SSD), state],
        out_shape=[jax.ShapeDtypeStruct(z.shape, F32),
                   jax.ShapeDtypeStruct((n, D_SSD, D_STATE), F32)],
        scratch_shapes=[pltpu.VMEM((PAD_ROWS + CHUNK, CONV_DIM), F32), aux],
        compiler_params=pltpu.CompilerParams(dimension_semantics=("parallel", "arbitrary"),
                                             vmem_limit_bytes=VMEM_LIMIT),
        name="ssd",
    )(z, xbc, dt, conv_prev, h0, cw, cb, dtb, alog, dsk, nw)
    return y.reshape(n, seq, D_SSD), h_new


def _row(v, width=None):
    v = v.reshape(1, -1).astype(F32)
    if width is not None and v.shape[1] < width:
        v = jnp.pad(v, ((0, 0), (0, width - v.shape[1])))
    return v


def kernel(x_prompt, x_sample, cache_k, cache_v, state_conv, state_ssm, p_prompt, p_sample,
           ln_in_g, ln_in_b, w_in, conv_w, conv_b, dt_bias, a_log, d_skip, ssd_norm_w, w_out,
           ln1_g, ln1_b, w_up, w_down, ln2_g, ln2_b, w_gate, w_ple, ln3_g, ln3_b):
    depth = w_in.shape[0]
    assert depth == 1, "single-layer step"
    alpha = (2 * depth) ** 0.25
    bsz, seq, _ = x_prompt.shape
    nd, dec_t, _ = x_sample.shape
    lyr = 0

    w_proj = jnp.pad(w_in[lyr], ((0, 0), (0, D_PROJ - w_in.shape[2]))).astype(BF16)
    gin, bin_ = _row(ln_in_g), _row(ln_in_b)
    ssd_params = (conv_w[lyr].astype(F32), _row(conv_b[lyr]), _row(dt_bias[lyr], DT_PAD),
                  _row(a_log[lyr], DT_PAD), _row(jnp.repeat(d_skip[lyr], HEAD_DIM)),
                  _row(ssd_norm_w[lyr]))
    post_params = (gin, bin_, w_out[lyr].astype(BF16), _row(ln1_g[lyr]), _row(ln1_b[lyr]),
                   w_up[lyr].astype(BF16), w_down[lyr].astype(BF16), _row(ln2_g[lyr]),
                   _row(ln2_b[lyr]), w_gate[lyr].astype(BF16), w_ple[lyr].astype(BF16),
                   _row(ln3_g[lyr]), _row(ln3_b[lyr]))

    q, k, v, k_t, v_t, z, xbc, dt = _in_proj(x_prompt, gin, bin_, w_proj, head_major=True, tm=512)
    attn = _attn_prompt(q, k, v)
    ssd_y, ssm_p = _ssd(z, xbc, dt, jnp.zeros((bsz, PAD_ROWS, CONV_DIM), F32),
                        jnp.zeros((bsz, D_SSD, D_STATE), F32), *ssd_params)
    y_prompt = _post(x_prompt, attn, ssd_y, p_prompt[lyr], *post_params, alpha=alpha, tm=512)

    n_tok = nd * dec_t
    flat = lambda t: t.reshape(1, n_tok, t.shape[-1])
    toks = lambda t: t.reshape(nd, dec_t, t.shape[-1])
    qs, ks, vs, zs, xbcs, dts = _in_proj(flat(x_sample), gin, bin_, w_proj, head_major=False, tm=512)
    transposed = lambda c: jnp.transpose(c, (0, 2, 3, 1)).reshape(nd, D_ATTN, CACHE_LEN)
    attn_s = _attn_sample(toks(qs), toks(ks), toks(vs), transposed(cache_k[lyr]),
                          transposed(cache_v[lyr]))
    conv_prev = jnp.pad(state_conv[lyr].astype(F32), ((0, 0), (PAD_ROWS - (CONV_W - 1), 0), (0, 0)))
    ssd_s, ssm_s = _ssd(toks(zs), toks(xbcs), toks(dts), conv_prev,
                        state_ssm[lyr].reshape(nd, D_SSD, D_STATE).astype(F32), *ssd_params)
    y_sample = _post(flat(x_sample), attn_s, flat(ssd_s), flat(p_sample[lyr]), *post_params,
                     alpha=alpha, tm=256)

    from_t = lambda t: jnp.transpose(t.reshape(1, bsz, N_HEADS, HEAD_DIM, seq), (0, 1, 4, 2, 3))
    heads = lambda t: t.reshape(1, nd, dec_t, N_HEADS, HEAD_DIM)
    tail = lambda t: t[None, :, -(CONV_W - 1):, :]
    state = lambda t, n: t.reshape(1, n, N_HEADS, HEAD_DIM, D_STATE)
    return (y_prompt, y_sample.reshape(nd, dec_t, D_MODEL), from_t(k_t), from_t(v_t),
            heads(ks), heads(vs), tail(xbc), tail(toks(xbcs)), state(ssm_p, bsz), state(ssm_s, nd))
```

```python
import functools

import numpy as np
import jax
import jax.numpy as jnp
from jax import lax
from jax.experimental import pallas as pl
from jax.experimental.pallas import tpu as pltpu

F32 = jnp.float32
BF16 = jnp.bfloat16

D_MODEL = 1024
HEAD_DIM = 64
D_ATTN = 512
D_SSD = 512
N_HEADS = 8
SSD_GROUPS = 2
GROUP_W = D_SSD // SSD_GROUPS
D_STATE = 128
CONV_W = 4
CONV_DIM = D_SSD + 2 * SSD_GROUPS * D_STATE
CHUNK = 128
D_FF = 4096
D_PLE = 256
LN_EPS = 1e-5
WINDOW_STEPS = 128
DILATIONS = (1, 4, 16)
PHASES = 16
CACHE_LEN = 2048
DT_PAD = 128
D_PROJ = 3 * D_ATTN + D_SSD + CONV_DIM + DT_PAD
SUBLANES = 8
LANES = 128
VMEM_LIMIT = 52 * 1024 * 1024
NEG_INF = float("-inf")


def _slopes():
    return np.array([2.0 ** (-8.0 * (h + 1) / N_HEADS) for h in range(N_HEADS)], dtype=np.float64)


def _layer_norm(x, g, b):
    mu = jnp.mean(x, -1, keepdims=True)
    xc = x - mu
    var = jnp.mean(xc * xc, -1, keepdims=True)
    return xc * lax.rsqrt(var + LN_EPS) * g + b


def _silu(x):
    return x * (1.0 / (1.0 + jnp.exp(-x)))


def _dot(a, b):
    return jnp.dot(a, b, preferred_element_type=F32)


def _dot_nt(a, b):
    return lax.dot_general(a, b, (((1,), (1,)), ((), ())), preferred_element_type=F32)


def _const_spec(shape):
    nd = len(shape)
    return pl.BlockSpec(shape, lambda *_: (0,) * nd, pipeline_mode=pl.Buffered(1))


HEAD_PAIRS = N_HEADS // 2


def _inproj_body(head_major, x_ref, g_ref, b_ref, w_ref, *out_refs):
    h = _layer_norm(x_ref[...], g_ref[...], b_ref[...]).astype(BF16)
    proj = lambda lo, width: _dot(h, w_ref[:, lo:lo + width])
    if head_major:
        q_ref, k_ref, v_ref, kt_ref, vt_ref, z_ref, xbc_ref, dt_ref = out_refs
        for idx, (ref, t_ref) in enumerate(((q_ref, None), (k_ref, kt_ref), (v_ref, vt_ref))):
            res = proj(idx * D_ATTN, D_ATTN)
            for c in range(HEAD_PAIRS):
                ref[c] = res[:, c * LANES:(c + 1) * LANES]
            if t_ref is not None:
                t_ref[...] = res.T
    else:
        q_ref, k_ref, v_ref, z_ref, xbc_ref, dt_ref = out_refs
        for idx, ref in enumerate((q_ref, k_ref, v_ref)):
            ref[...] = proj(idx * D_ATTN, D_ATTN)
    lo = 3 * D_ATTN
    for ref, width in ((z_ref, D_SSD), (xbc_ref, CONV_DIM), (dt_ref, DT_PAD)):
        ref[...] = proj(lo, width)
        lo += width


def _in_proj(x, g, b, w, *, head_major, tm):
    bsz, seq, _ = x.shape
    row = lambda width: pl.BlockSpec((None, tm, width), lambda i, j: (i, j, 0))
    row_shape = lambda width: jax.ShapeDtypeStruct((bsz, seq, width), F32)
    if head_major:
        slab = pl.BlockSpec((None, HEAD_PAIRS, tm, LANES), lambda i, j: (i, 0, j, 0))
        slab_shape = jax.ShapeDtypeStruct((bsz, HEAD_PAIRS, seq, LANES), F32)
        tr = pl.BlockSpec((None, D_ATTN, tm), lambda i, j: (i, 0, j))
        tr_shape = jax.ShapeDtypeStruct((bsz, D_ATTN, seq), F32)
        qkv_specs, qkv_shapes = [slab, slab, slab, tr, tr], [slab_shape] * 3 + [tr_shape] * 2
    else:
        qkv_specs, qkv_shapes = [row(D_ATTN)] * 3, [row_shape(D_ATTN)] * 3
    rest = (D_SSD, CONV_DIM, DT_PAD)
    return pl.pallas_call(
        functools.partial(_inproj_body, head_major),
        grid=(bsz, seq // tm),
        in_specs=[row(D_MODEL), _const_spec((1, D_MODEL)), _const_spec((1, D_MODEL)),
                  _const_spec((D_MODEL, D_PROJ))],
        out_specs=qkv_specs + [row(wd) for wd in rest],
        out_shape=qkv_shapes + [row_shape(wd) for wd in rest],
        compiler_params=pltpu.CompilerParams(dimension_semantics=("parallel", "parallel"),
                                             vmem_limit_bytes=VMEM_LIMIT),
        name="in_proj",
    )(x, g, b, w)


FF_CHUNK = 1024


def _post_body(alpha, x_ref, attn_ref, ssd_ref, pe_ref, gin_ref, bin_ref, wout_ref, g1_ref, b1_ref,
               wup_ref, wdown_ref, g2_ref, b2_ref, wgate_ref, wple_ref, g3_ref, b3_ref, y_ref):
    xn = _layer_norm(x_ref[...], gin_ref[...], bin_ref[...])
    mixed = jnp.concatenate([attn_ref[c] for c in range(HEAD_PAIRS)] + [ssd_ref[...]],
                            axis=-1).astype(BF16)
    h = _layer_norm(alpha * xn + _dot(mixed, wout_ref[...]), g1_ref[...], b1_ref[...])
    hb = h.astype(BF16)
    u = None
    for c in range(D_FF // FF_CHUNK):
        a = jnp.maximum(_dot(hb, wup_ref[:, c * FF_CHUNK:(c + 1) * FF_CHUNK]), 0.0)
        part = _dot((a * a).astype(BF16), wdown_ref[c * FF_CHUNK:(c + 1) * FF_CHUNK, :])
        u = part if u is None else u + part
    h = _layer_norm(alpha * h + u, g2_ref[...], b2_ref[...])
    gate = 1.0 / (1.0 + jnp.exp(-_dot(h.astype(BF16), wgate_ref[...])))
    g = gate * _dot(pe_ref[...].astype(BF16), wple_ref[...])
    y_ref[...] = _layer_norm(alpha * h + g, g3_ref[...], b3_ref[...])


def _post(x, attn, ssd, pe, gin, bin_, wout, g1, b1, wup, wdown, g2, b2, wgate, wple, g3, b3, *,
          alpha, tm):
    bsz, seq, _ = x.shape
    row = lambda width: pl.BlockSpec((None, tm, width), lambda i, j: (i, j, 0))
    slab = pl.BlockSpec((None, HEAD_PAIRS, tm, LANES), lambda i, j: (i, 0, j, 0))
    vec = _const_spec((1, D_MODEL))
    return pl.pallas_call(
        functools.partial(_post_body, alpha),
        grid=(bsz, seq // tm),
        in_specs=[row(D_MODEL), slab, row(D_SSD), row(D_PLE), vec, vec,
                  _const_spec((D_MODEL, D_MODEL)), vec, vec,
                  _const_spec((D_MODEL, D_FF)), _const_spec((D_FF, D_MODEL)), vec, vec,
                  _const_spec((D_MODEL, D_MODEL)), _const_spec((D_PLE, D_MODEL)), vec, vec],
        out_specs=row(D_MODEL),
        out_shape=jax.ShapeDtypeStruct((bsz, seq, D_MODEL), F32),
        compiler_params=pltpu.CompilerParams(dimension_semantics=("parallel", "parallel"),
                                             vmem_limit_bytes=VMEM_LIMIT),
        name="post",
    )(x, attn, ssd, pe, gin, bin_, wout, g1, b1, wup, wdown, g2, b2, wgate, wple, g3, b3)


U_ROWS = CACHE_LEN // PHASES
TAB3_LO, TAB2_LO, TAB1_LO, TAB_W = 0, 128, 384, 640
BLOCKS_IN_FLIGHT = 8
B1_IN_FLIGHT = 5


def _prompt_bias_tables():
    slopes = _slopes()
    u = np.arange(U_ROWS)
    d3 = (u[:, None] - u[None, :]).astype(np.float64)
    k4, ul4 = np.meshgrid(np.arange(4), np.arange(32), indexing="ij")
    j2 = (4 * ul4 + k4).reshape(-1)
    d2 = j2[:, None] - np.concatenate([j2 - WINDOW_STEPS, j2])[None, :]
    p16, ul16 = np.meshgrid(np.arange(16), np.arange(8), indexing="ij")
    t1 = (16 * ul16 + p16).reshape(-1)
    d1 = t1[:, None] - (np.arange(2 * WINDOW_STEPS) - WINDOW_STEPS)[None, :]
    tabs = []
    for dist, dil in ((d3, 16), (d2, 4), (d1, 1)):
        valid = (dist >= 0) & (dist <= WINDOW_STEPS)
        per_head = [np.where(valid, -slopes[h] * dist * dil, NEG_INF) for h in range(N_HEADS)]
        tabs.append(np.stack(per_head))
    tab = np.concatenate(tabs, axis=-1)
    return jnp.asarray(tab.reshape(HEAD_PAIRS, 2 * U_ROWS, TAB_W), dtype=F32)


def _attn_blocks(blocks, first_head):
    scores = [_dot_nt(q2, k) + bias for q2, k, _, bias in blocks]
    tops = [jnp.max(s, -1, keepdims=True) for s in scores]
    exps = [jnp.exp(s - m) for s, m in zip(scores, tops)]
    dens = [jnp.sum(e, -1, keepdims=True) for e in exps]
    pvs = [_dot(e.astype(BF16), blk[2]) for e, blk in zip(exps, blocks)]
    half = U_ROWS
    outs = []
    for pv, m, den in zip(pvs, tops, dens):
        o = jnp.where(first_head, pv[:half], pv[half:])
        m2 = jnp.where(first_head, m[:half], m[half:])
        den2 = jnp.where(first_head, den[:half], den[half:])
        outs.append((o / den2, m2 + jnp.log(den2)))
    return outs


def _attn_prompt_body(q_ref, k_ref, v_ref, tab_ref, o_ref, qs, q0p, q1p, kp, vp, kn, vn, quarter,
                      o_scr, l_scr):
    lane = lax.broadcasted_iota(jnp.int32, (U_ROWS, LANES), 1)
    first_head = lane < HEAD_DIM
    scale = HEAD_DIM ** -0.5

    def natural(i, carry):
        rows = pl.ds(pl.multiple_of(i * U_ROWS, U_ROWS), U_ROWS)
        kn[rows, :] = k_ref[rows, :].astype(BF16)
        vn[rows, :] = v_ref[rows, :].astype(BF16)
        return carry
    lax.fori_loop(0, PHASES, natural, 0)

    def split(src_ref, emit):
        for r in range(4):
            quarter[r] = src_ref[pl.ds(r, CACHE_LEN // 4, stride=4), :]
        for r in range(4):
            for kk in range(4):
                emit(4 * kk + r, quarter[r, pl.ds(kk, U_ROWS, stride=4), :])

    def emit_q(p, tile):
        tile = tile * scale
        qs[p] = tile
        q0p[p] = jnp.where(first_head, tile, 0.0).astype(BF16)
        q1p[p] = jnp.where(first_head, 0.0, tile).astype(BF16)

    def emit_to(dst):
        def emit(p, tile):
            dst[p] = tile.astype(BF16)
        return emit

    split(q_ref, emit_q)
    split(k_ref, emit_to(kp))
    split(v_ref, emit_to(vp))

    def branch3(g, carry):
        phases = [g * BLOCKS_IN_FLIGHT + j for j in range(BLOCKS_IN_FLIGHT)]
        bias = tab_ref[:, TAB3_LO:TAB3_LO + 128]
        outs = _attn_blocks([(jnp.concatenate([q0p[p], q1p[p]], 0), kp[p], vp[p], bias)
                             for p in phases], first_head)
        for p, (o, l) in zip(phases, outs):
            o_scr[0, p] = o
            l_scr[0, p] = l
        return carry
    lax.fori_loop(0, PHASES // BLOCKS_IN_FLIGHT, branch3, 0)

    def branch2(g, carry):
        blocks, where = [], []
        for r in (2 * g, 2 * g + 1):
            gather = lambda src, rr, r=r: [src[4 * kk + r, rr, :] for kk in range(4)]
            for n in range(4):
                rows = pl.ds(32 * n, 32)
                q2 = jnp.concatenate(gather(q0p, rows) + gather(q1p, rows), 0)
                if n == 0:
                    k = jnp.concatenate(gather(kp, rows), 0)
                    v = jnp.concatenate(gather(vp, rows), 0)
                    bias = tab_ref[:, TAB2_LO + 128:TAB2_LO + 256]
                else:
                    prev = pl.ds(32 * (n - 1), 32)
                    k = jnp.concatenate(gather(kp, prev) + gather(kp, rows), 0)
                    v = jnp.concatenate(gather(vp, prev) + gather(vp, rows), 0)
                    bias = tab_ref[:, TAB2_LO:TAB2_LO + 256]
                blocks.append((q2, k, v, bias))
                where.append((r, rows))
        for (r, rows), (o, l) in zip(where, _attn_blocks(blocks, first_head)):
            for kk in range(4):
                o_scr[1, 4 * kk + r, rows, :] = o[32 * kk:32 * kk + 32]
                l_scr[1, 4 * kk + r, rows, :] = l[32 * kk:32 * kk + 32]
        return carry
    lax.fori_loop(0, 2, branch2, 0)

    def branch1_blocks(ns, with_prev):
        blocks = []
        for n in ns:
            rows = pl.ds(pl.multiple_of(n * SUBLANES, SUBLANES), SUBLANES)
            q = jnp.concatenate([qs[p, rows, :] for p in range(PHASES)], 0)
            q2 = jnp.concatenate([jnp.where(first_head, q, 0.0), jnp.where(first_head, 0.0, q)],
                                 0).astype(BF16)
            if with_prev:
                keys = pl.ds(pl.multiple_of((n - 1) * U_ROWS, U_ROWS), 2 * U_ROWS)
                bias = tab_ref[:, TAB1_LO:TAB1_LO + 256]
            else:
                keys = pl.ds(0, U_ROWS)
                bias = tab_ref[:, TAB1_LO + 128:TAB1_LO + 256]
            blocks.append((q2, kn[keys, :], vn[keys, :], bias))
        for n, (o, l) in zip(ns, _attn_blocks(blocks, first_head)):
            rows = pl.ds(pl.multiple_of(n * SUBLANES, SUBLANES), SUBLANES)
            for p in range(PHASES):
                o_scr[2, p, rows, :] = o[SUBLANES * p:SUBLANES * (p + 1)]
                l_scr[2, p, rows, :] = l[SUBLANES * p:SUBLANES * (p + 1)]

    branch1_blocks([0], False)

    def branch1(g, carry):
        branch1_blocks([1 + g * B1_IN_FLIGHT + j for j in range(B1_IN_FLIGHT)], True)
        return carry
    lax.fori_loop(0, (U_ROWS // SUBLANES - 1) // B1_IN_FLIGHT, branch1, 0)

    def phase_rows(p):
        return pl.ds(p, U_ROWS, stride=PHASES)

    def merge(p, carry):
        l3, l2, l1 = l_scr[0, p], l_scr[1, p], l_scr[2, p]
        top = jnp.maximum(jnp.maximum(l3, l2), l1)
        e3, e2, e1 = jnp.exp(l3 - top), jnp.exp(l2 - top), jnp.exp(l1 - top)
        total = e3 + e2 + e1
        o_ref[phase_rows(p), :] = (e3 * o_scr[0, p] + e2 * o_scr[1, p] + e1 * o_scr[2, p]) / total
        return carry
    lax.fori_loop(0, PHASES, merge, 0)


def _attn_prompt(q, k, v):
    bsz, _, seq, _ = k.shape
    assert seq == CACHE_LEN, "prompt attention is laid out for a 2048-token prompt"
    slab = pl.BlockSpec((None, None, seq, LANES), lambda b, c: (b, c, 0, 0))
    tile = (PHASES, U_ROWS, LANES)
    return pl.pallas_call(
        _attn_prompt_body,
        grid=(bsz, HEAD_PAIRS),
        in_specs=[slab, slab, slab,
                  pl.BlockSpec((None, 2 * U_ROWS, TAB_W), lambda b, c: (c, 0, 0))],
        out_specs=slab,
        out_shape=jax.ShapeDtypeStruct((bsz, HEAD_PAIRS, seq, LANES), F32),
        scratch_shapes=[pltpu.VMEM(tile, F32)] + [pltpu.VMEM(tile, BF16)] * 4
                       + [pltpu.VMEM((seq, LANES), BF16)] * 2
                       + [pltpu.VMEM((4, seq // 4, LANES), F32),
                          pltpu.VMEM((3,) + tile, F32), pltpu.VMEM((3,) + tile, F32)],
        compiler_params=pltpu.CompilerParams(dimension_semantics=("parallel", "parallel"),
                                             vmem_limit_bytes=VMEM_LIMIT),
        name="attn_prompt",
    )(q, k, v, _prompt_bias_tables())


DEC_T = 8
QH = DEC_T * N_HEADS
B2_SPAN, B1_SPAN = 512, 128
SAMPLE_TAB_W = CACHE_LEN + B2_SPAN + B1_SPAN


def _sample_bias_tables():
    slopes = _slopes()
    i = np.repeat(np.arange(DEC_T), N_HEADS)[:, None]
    sl = np.tile(slopes, DEC_T)[:, None]

    def cache_bias(span, dil):
        t = CACHE_LEN - span + np.arange(span)[None, :]
        dist = CACHE_LEN + i - t
        valid = (dist % dil == 0) & (dist <= WINDOW_STEPS * dil)
        return np.where(valid, -sl * dist, NEG_INF)

    cache_tab = np.concatenate([cache_bias(CACHE_LEN, 16), cache_bias(B2_SPAN, 4),
                                cache_bias(B1_SPAN, 1)], axis=1)
    c = np.arange(CHUNK)[None, :]
    dn = i - c
    is_new = c < DEC_T
    n1 = np.where(is_new & (dn >= 0), -sl * dn, NEG_INF)
    n2 = np.where(is_new & ((dn == 0) | (dn == 4)), -sl * dn, NEG_INF)
    n3 = np.where(is_new & (dn == 0), 0.0, NEG_INF)
    return jnp.asarray(cache_tab, dtype=F32), jnp.asarray(np.stack([n1, n2, n3]), dtype=F32)


def _head_selector():
    h_row = np.tile(np.arange(N_HEADS), DEC_T)[:, None]
    h_col = (np.arange(D_ATTN) // HEAD_DIM)[None, :]
    return jnp.asarray((h_row == h_col).astype(np.float32))


def _attn_sample_body(q_ref, kn_ref, vn_ref, kt_ref, vt_ref, tab_ref, tabn_ref, sel_ref, o_ref):
    sel = sel_ref[...]
    q = q_ref[...] * (HEAD_DIM ** -0.5)
    q_rows = jnp.concatenate([jnp.broadcast_to(q[i:i + 1, :], (N_HEADS, D_ATTN))
                              for i in range(DEC_T)], 0)
    qall = (q_rows * sel).astype(BF16)
    pad = jnp.zeros((CHUNK - DEC_T, D_ATTN), F32)
    kn = jnp.concatenate([kn_ref[...], pad], 0).astype(BF16)
    vn = jnp.concatenate([vn_ref[...], pad], 0).astype(BF16)

    s_all = _dot(qall, kt_ref[...].astype(BF16))
    s_new = _dot_nt(qall, kn)
    lo2, lo1 = CACHE_LEN - B2_SPAN, CACHE_LEN - B1_SPAN

    def branch(s_cache, s_fresh, v_t):
        m = jnp.maximum(jnp.max(s_cache, -1, keepdims=True), jnp.max(s_fresh, -1, keepdims=True))
        e_c, e_f = jnp.exp(s_cache - m), jnp.exp(s_fresh - m)
        den = jnp.sum(e_c, -1, keepdims=True) + jnp.sum(e_f, -1, keepdims=True)
        acc = _dot_nt(e_c.astype(BF16), v_t.astype(BF16)) + _dot(e_f.astype(BF16), vn)
        return m, den, acc

    m3, den3, acc3 = branch(s_all + tab_ref[:, :CACHE_LEN], s_new + tabn_ref[2], vt_ref[...])
    m2, den2, acc2 = branch(s_all[:, lo2:] + tab_ref[:, CACHE_LEN:CACHE_LEN + B2_SPAN],
                            s_new + tabn_ref[1], vt_ref[:, lo2:])
    m1, den1, acc1 = branch(s_all[:, lo1:] + tab_ref[:, CACHE_LEN + B2_SPAN:],
                            s_new + tabn_ref[0], vt_ref[:, lo1:])

    top = jnp.maximum(jnp.maximum(m1, m2), m3)
    sc1, sc2, sc3 = den1 * jnp.exp(m1 - top), den2 * jnp.exp(m2 - top), den3 * jnp.exp(m3 - top)
    total = sc1 + sc2 + sc3
    mixed = ((sc1 / total) * (acc1 / den1) + (sc2 / total) * (acc2 / den2)
             + (sc3 / total) * (acc3 / den3)) * sel
    for i in range(DEC_T):
        row = jnp.sum(mixed[N_HEADS * i:N_HEADS * (i + 1)], axis=0, keepdims=True)
        for c in range(HEAD_PAIRS):
            o_ref[c, i:i + 1, :] = row[:, c * LANES:(c + 1) * LANES]


def _attn_sample_call(q, k_new, v_new, cache_kt, cache_vt):
    n = q.shape[0]
    assert cache_kt.shape[1:] == (D_ATTN, CACHE_LEN) and q.shape[1] == DEC_T
    tok = pl.BlockSpec((None, DEC_T, D_ATTN), lambda i: (i, 0, 0))
    cache = pl.BlockSpec((None, D_ATTN, CACHE_LEN), lambda i: (i, 0, 0))
    cache_tab, new_tab = _sample_bias_tables()
    operands = (q, k_new, v_new, cache_kt, cache_vt, cache_tab, new_tab, _head_selector())
    in_specs = [tok, tok, tok, cache, cache, _const_spec((QH, SAMPLE_TAB_W)),
                _const_spec((3, QH, CHUNK)), _const_spec((QH, D_ATTN))]
    out_spec = pl.BlockSpec((None, HEAD_PAIRS, DEC_T, LANES), lambda i: (0, 0, i, 0))
    out_shape = jax.ShapeDtypeStruct((1, HEAD_PAIRS, n * DEC_T, LANES), F32)
    return operands, in_specs, out_spec, out_shape


def _attn_sample(q, k_new, v_new, cache_kt, cache_vt):
    operands, in_specs, out_spec, out_shape = _attn_sample_call(q, k_new, v_new, cache_kt, cache_vt)
    return pl.pallas_call(
        _attn_sample_body,
        grid=(q.shape[0],),
        in_specs=in_specs,
        out_specs=out_spec,
        out_shape=out_shape,
        compiler_params=pltpu.CompilerParams(dimension_semantics=("parallel",),
                                             vmem_limit_bytes=VMEM_LIMIT),
        name="attn_sample",
    )(*operands)


HEADS_PER_GROUP = N_HEADS // SSD_GROUPS
PAD_ROWS = SUBLANES
SHORT_SEQ = SUBLANES
SEQS_PER_TILE = CHUNK // SHORT_SEQ


def _ssd_body(packed, first_chunk, z_ref, xbc_ref, dt_ref, cp_ref, h0_ref, cw_ref, cb_ref, dtb_ref,
              alog_ref, dsk_ref, nw_ref, y_ref, hn_ref, xpad, aux):
    if packed:
        aux[0:CHUNK, :] = cp_ref[...]
        aux[CHUNK:, :] = jnp.zeros((PAD_ROWS, CONV_DIM), F32)
        xpad[0:PAD_ROWS, :] = jnp.zeros((PAD_ROWS, CONV_DIM), F32)
    else:
        @pl.when(first_chunk())
        def _init():
            xpad[0:PAD_ROWS, :] = cp_ref[...]
            aux[...] = h0_ref[...]
    xpad[PAD_ROWS:, :] = xbc_ref[...]

    step = lax.broadcasted_iota(jnp.int32, (CHUNK, CONV_DIM), 0) % SHORT_SEQ
    conv = cb_ref[...]
    for back in range(CONV_W):
        tap = CONV_W - 1 - back
        rows_back = xpad[pl.ds(PAD_ROWS - back, CHUNK), :]
        if packed and back:
            rows_back = jnp.where(step < back, aux[pl.ds(PAD_ROWS - back, CHUNK), :], rows_back)
        conv = conv + rows_back * cw_ref[tap:tap + 1, :]
    xc = _silu(conv)
    xs = xc[:, :D_SSD]

    lane = lax.broadcasted_iota(jnp.int32, (CHUNK, DT_PAD), 1)
    dt_raw = dt_ref[...] + dtb_ref[...]
    dt = jnp.maximum(dt_raw, 0.0) + jnp.log1p(jnp.exp(-jnp.abs(dt_raw)))
    dt = jnp.where(lane < N_HEADS, dt, 0.0)
    adt = dt * (-jnp.exp(alog_ref[...]))

    ri = lax.broadcasted_iota(jnp.int32, (CHUNK, CHUNK), 0)
    ci = lax.broadcasted_iota(jnp.int32, (CHUNK, CHUNK), 1)
    causal = ri >= ci
    if packed:
        causal = causal & (ri // SHORT_SEQ == ci // SHORT_SEQ)
    exact_dot = functools.partial(jnp.dot, precision=lax.Precision.HIGHEST,
                                  preferred_element_type=F32)
    cs = exact_dot(jnp.where(causal, 1.0, 0.0).astype(F32), adt)
    cs_t = cs.T
    if packed:
        pick_last = ci == (ri // SHORT_SEQ) * SHORT_SEQ + (SHORT_SEQ - 1)
        cs_end = exact_dot(jnp.where(pick_last, 1.0, 0.0).astype(F32), cs)
    else:
        cs_end = cs[CHUNK - 1:CHUNK, :]

    head_of_lane = lax.broadcasted_iota(jnp.int32, (CHUNK, GROUP_W), 1) // HEAD_DIM

    def per_head(cols):
        out = cols[HEADS_PER_GROUP - 1]
        for hl in range(HEADS_PER_GROUP - 2, -1, -1):
            out = jnp.where(head_of_lane == hl, cols[hl], out)
        return out

    y_groups = []
    for g in range(SSD_GROUPS):
        heads = range(g * HEADS_PER_GROUP, (g + 1) * HEADS_PER_GROUP)
        grp = slice(g * GROUP_W, (g + 1) * GROUP_W)
        b_g = xc[:, D_SSD + g * D_STATE:D_SSD + (g + 1) * D_STATE].astype(BF16)
        c_g = xc[:, D_SSD + (SSD_GROUPS + g) * D_STATE:
                 D_SSD + (SSD_GROUPS + g + 1) * D_STATE].astype(BF16)
        cs_cols = [cs[:, h:h + 1] for h in heads]
        xdt_g = xs[:, grp] * per_head([dt[:, h:h + 1] for h in heads])
        xdt_b = xdt_g.astype(BF16)
        gram = _dot_nt(c_g, b_g)

        y_diag = jnp.zeros((CHUNK, GROUP_W), F32)
        for hl, h in enumerate(heads):
            seg = jnp.where(causal, cs_cols[hl] - cs_t[h:h + 1, :], NEG_INF)
            weights = (gram * jnp.exp(seg)).astype(BF16)
            y_diag = y_diag + jnp.where(head_of_lane == hl, _dot(weights, xdt_b), 0.0)

        to_end = per_head([jnp.exp(cs_end[:, h:h + 1] - cs_cols[hl]) for hl, h in enumerate(heads)])
        decayed_t = (xdt_g * to_end).T
        carried = per_head([jnp.exp(col) for col in cs_cols])
        if packed:
            h_prev = h0_ref[:, grp, :]
            wide = _dot_nt(c_g, h_prev.reshape(SEQS_PER_TILE * GROUP_W, D_STATE).astype(BF16))
            y_off = jnp.concatenate(
                [wide[s * SHORT_SEQ:(s + 1) * SHORT_SEQ, s * GROUP_W:(s + 1) * GROUP_W]
                 for s in range(SEQS_PER_TILE)], 0) * carried
            seq_of_step = lax.broadcasted_iota(jnp.int32, (GROUP_W, CHUNK), 1) // SHORT_SEQ
            per_seq = jnp.concatenate([jnp.where(seq_of_step == s, decayed_t, 0.0)
                                       for s in range(SEQS_PER_TILE)], 0).astype(BF16)
            new_states = _dot(per_seq, b_g).reshape(SEQS_PER_TILE, GROUP_W, D_STATE)
            for s in range(SEQS_PER_TILE):
                row = s * SHORT_SEQ
                keep = jnp.concatenate(
                    [jnp.broadcast_to(jnp.exp(cs_end[row:row + 1, h:h + 1]), (HEAD_DIM, D_STATE))
                     for h in heads], 0)
                hn_ref[s, grp, :] = h_prev[s] * keep + new_states[s]
        else:
            h_prev = aux[grp, :]
            y_off = _dot_nt(c_g, h_prev.astype(BF16)) * carried
            keep = jnp.concatenate(
                [jnp.broadcast_to(jnp.exp(cs_end[:, h:h + 1]), (HEAD_DIM, D_STATE)) for h in heads], 0)
            aux[grp, :] = h_prev * keep + _dot(decayed_t.astype(BF16), b_g)
        y_groups.append(y_diag + y_off)

    y = jnp.concatenate(y_groups, axis=-1) + dsk_ref[...] * xs
    y = y * _silu(z_ref[...])
    normed = []
    for g in range(SSD_GROUPS):
        yg = y[:, g * GROUP_W:(g + 1) * GROUP_W]
        normed.append(yg * lax.rsqrt(jnp.mean(yg * yg, -1, keepdims=True) + LN_EPS))
    y_ref[...] = jnp.concatenate(normed, axis=-1) * nw_ref[...]

    if not packed:
        hn_ref[...] = aux[...]
        xpad[0:PAD_ROWS, :] = xpad[CHUNK:, :]


N_SSD_INPUTS = 11


def _ssd_call(z, xbc, dt, conv_prev, h0, cw, cb, dtb, alog, dsk, nw, where=None):
    n, seq, _ = z.shape
    packed = seq == SHORT_SEQ
    if packed:
        assert n % SEQS_PER_TILE == 0 and where is None
        grid = (n // SEQS_PER_TILE, 1)
        fold = lambda t: t.reshape(grid[0], CHUNK, t.shape[-1])
        z, xbc, dt, conv_prev = fold(z), fold(xbc), fold(dt), fold(conv_prev)
        history_shape, state_shape = (None, CHUNK, CONV_DIM), (SEQS_PER_TILE, D_SSD, D_STATE)
        aux = pltpu.VMEM((CHUNK + PAD_ROWS, CONV_DIM), F32)
    else:
        assert seq % CHUNK == 0
        grid = (n, seq // CHUNK)
        history_shape, state_shape = (None, PAD_ROWS, CONV_DIM), (None, D_SSD, D_STATE)
        aux = pltpu.VMEM((D_SSD, D_STATE), F32)
    where = where or (lambda s, c: (s, c))
    tile = lambda width: pl.BlockSpec((None, CHUNK, width), lambda *g: where(*g) + (0,))
    per_seq = lambda shape: pl.BlockSpec(shape, lambda *g: (where(*g)[0], 0, 0))
    history, state = per_seq(history_shape), per_seq(state_shape)
    operands = (z, xbc, dt, conv_prev, h0, cw, cb, dtb, alog, dsk, nw)
    in_specs = [tile(D_SSD), tile(CONV_DIM), tile(DT_PAD), history, state,
                _const_spec((CONV_W, CONV_DIM)), _const_spec((1, CONV_DIM)),
                _const_spec((1, DT_PAD)), _const_spec((1, DT_PAD)),
                _const_spec((1, D_SSD)), _const_spec((1, D_SSD))]
    out_specs = [tile(D_SSD), state]
    out_shape = [jax.ShapeDtypeStruct(z.shape, F32), jax.ShapeDtypeStruct((n, D_SSD, D_STATE), F32)]
    scratch = [pltpu.VMEM((PAD_ROWS + CHUNK, CONV_DIM), F32), aux]
    return packed, grid, operands, in_specs, out_specs, out_shape, scratch


def _ssd(z, *rest):
    packed, grid, operands, in_specs, out_specs, out_shape, scratch = _ssd_call(z, *rest)
    y, h_new = pl.pallas_call(
        functools.partial(_ssd_body, packed, lambda: pl.program_id(1) == 0),
        grid=grid,
        in_specs=in_specs,
        out_specs=out_specs,
        out_shape=out_shape,
        scratch_shapes=scratch,
        compiler_params=pltpu.CompilerParams(dimension_semantics=("parallel", "arbitrary"),
                                             vmem_limit_bytes=VMEM_LIMIT),
        name="ssd",
    )(*operands)
    return y.reshape(z.shape), h_new


def _ssd_with_sample_attn(ssd_args, attn_args):
    z = ssd_args[0]
    nchunks = z.shape[1] // CHUNK
    steps = z.shape[0] * nchunks
    assert steps == attn_args[0].shape[0]
    where = lambda i: (i // nchunks, i % nchunks)
    _, _, ssd_ops, ssd_in, ssd_out, ssd_shape, scratch = _ssd_call(*ssd_args, where=where)
    attn_ops, attn_in, attn_out, attn_shape = _attn_sample_call(*attn_args)

    def body(*refs):
        ssd_in_refs, refs = refs[:N_SSD_INPUTS], refs[N_SSD_INPUTS:]
        attn_in_refs, refs = refs[:len(attn_ops)], refs[len(attn_ops):]
        y_ref, hn_ref, o_ref, xpad, aux = refs
        _attn_sample_body(*attn_in_refs, o_ref)
        _ssd_body(False, lambda: pl.program_id(0) % nchunks == 0, *ssd_in_refs, y_ref, hn_ref,
                  xpad, aux)

    y, h_new, attn = pl.pallas_call(
        body,
        grid=(steps,),
        in_specs=ssd_in + attn_in,
        out_specs=ssd_out + [attn_out],
        out_shape=ssd_shape + [attn_shape],
        scratch_shapes=scratch,
        compiler_params=pltpu.CompilerParams(dimension_semantics=("arbitrary",),
                                             vmem_limit_bytes=VMEM_LIMIT),
        name="ssd_attn_sample",
    )(*ssd_ops, *attn_ops)
    return y, h_new, attn


def _row(v, width=None):
    v = v.reshape(1, -1).astype(F32)
    if width is not None and v.shape[1] < width:
        v = jnp.pad(v, ((0, 0), (0, width - v.shape[1])))
    return v


def kernel(x_prompt, x_sample, cache_k, cache_v, state_conv, state_ssm, p_prompt, p_sample,
           ln_in_g, ln_in_b, w_in, conv_w, conv_b, dt_bias, a_log, d_skip, ssd_norm_w, w_out,
           ln1_g, ln1_b, w_up, w_down, ln2_g, ln2_b, w_gate, w_ple, ln3_g, ln3_b):
    depth = w_in.shape[0]
    assert depth == 1, "single-layer step"
    alpha = (2 * depth) ** 0.25
    bsz, seq, _ = x_prompt.shape
    nd, dec_t, _ = x_sample.shape
    lyr = 0

    w_proj = jnp.pad(w_in[lyr], ((0, 0), (0, D_PROJ - w_in.shape[2]))).astype(BF16)
    gin, bin_ = _row(ln_in_g), _row(ln_in_b)
    ssd_params = (conv_w[lyr].astype(F32), _row(conv_b[lyr]), _row(dt_bias[lyr], DT_PAD),
                  _row(a_log[lyr], DT_PAD), _row(jnp.repeat(d_skip[lyr], HEAD_DIM)),
                  _row(ssd_norm_w[lyr]))
    post_params = (gin, bin_, w_out[lyr].astype(BF16), _row(ln1_g[lyr]), _row(ln1_b[lyr]),
                   w_up[lyr].astype(BF16), w_down[lyr].astype(BF16), _row(ln2_g[lyr]),
                   _row(ln2_b[lyr]), w_gate[lyr].astype(BF16), w_ple[lyr].astype(BF16),
                   _row(ln3_g[lyr]), _row(ln3_b[lyr]))

    q, k, v, k_t, v_t, z, xbc, dt = _in_proj(x_prompt, gin, bin_, w_proj, head_major=True, tm=512)
    n_tok = nd * dec_t
    flat = lambda t: t.reshape(1, n_tok, t.shape[-1])
    toks = lambda t: t.reshape(nd, dec_t, t.shape[-1])
    qs, ks, vs, zs, xbcs, dts = _in_proj(flat(x_sample), gin, bin_, w_proj, head_major=False, tm=512)

    attn = _attn_prompt(q, k, v)
    transposed = lambda c: jnp.transpose(c, (0, 2, 3, 1)).reshape(nd, D_ATTN, CACHE_LEN)
    ssd_args = (z, xbc, dt, jnp.zeros((bsz, PAD_ROWS, CONV_DIM), F32),
                jnp.zeros((bsz, D_SSD, D_STATE), F32)) + ssd_params
    attn_args = (toks(qs), toks(ks), toks(vs), transposed(cache_k[lyr]), transposed(cache_v[lyr]))
    if bsz * (seq // CHUNK) == nd:
        ssd_y, ssm_p, attn_s = _ssd_with_sample_attn(ssd_args, attn_args)
    else:
        ssd_y, ssm_p = _ssd(*ssd_args)
        attn_s = _attn_sample(*attn_args)
    conv_prev = jnp.pad(state_conv[lyr].astype(F32), ((0, 0), (PAD_ROWS - (CONV_W - 1), 0), (0, 0)))
    ssd_s, ssm_s = _ssd(toks(zs), toks(xbcs), toks(dts), conv_prev,
                        state_ssm[lyr].reshape(nd, D_SSD, D_STATE).astype(F32), *ssd_params)

    y_prompt = _post(x_prompt, attn, ssd_y, p_prompt[lyr], *post_params, alpha=alpha, tm=512)
    y_sample = _post(flat(x_sample), attn_s, flat(ssd_s), flat(p_sample[lyr]), *post_params,
                     alpha=alpha, tm=256)

    from_t = lambda t: jnp.transpose(t.reshape(1, bsz, N_HEADS, HEAD_DIM, seq), (0, 1, 4, 2, 3))
    heads = lambda t: t.reshape(1, nd, dec_t, N_HEADS, HEAD_DIM)
    tail = lambda t: t[None, :, -(CONV_W - 1):, :]
    state = lambda t, n: t.reshape(1, n, N_HEADS, HEAD_DIM, D_STATE)
    return (y_prompt, y_sample.reshape(nd, dec_t, D_MODEL), from_t(k_t), from_t(v_t),
            heads(ks), heads(vs), tail(xbc), tail(toks(xbcs)), state(ssm_p, bsz), state(ssm_s, nd))
```

```python
import functools

import numpy as np
import jax
import jax.numpy as jnp
from jax import lax
from jax.experimental import pallas as pl
from jax.experimental.pallas import tpu as pltpu

F32 = jnp.float32
BF16 = jnp.bfloat16

D_MODEL = 1024
HEAD_DIM = 64
D_ATTN = 512
D_SSD = 512
N_HEADS = 8
SSD_GROUPS = 2
GROUP_W = D_SSD // SSD_GROUPS
D_STATE = 128
CONV_W = 4
CONV_DIM = D_SSD + 2 * SSD_GROUPS * D_STATE
CHUNK = 128
D_FF = 4096
D_PLE = 256
LN_EPS = 1e-5
WINDOW_STEPS = 128
DILATIONS = (1, 4, 16)
PHASES = 16
CACHE_LEN = 2048
DT_PAD = 128
D_PROJ = 3 * D_ATTN + D_SSD + CONV_DIM + DT_PAD
SUBLANES = 8
LANES = 128
VMEM_LIMIT = 52 * 1024 * 1024
NEG_INF = float("-inf")


def _slopes():
    return np.array([2.0 ** (-8.0 * (h + 1) / N_HEADS) for h in range(N_HEADS)], dtype=np.float64)


def _layer_norm(x, g, b):
    mu = jnp.mean(x, -1, keepdims=True)
    xc = x - mu
    var = jnp.mean(xc * xc, -1, keepdims=True)
    return xc * lax.rsqrt(var + LN_EPS) * g + b


def _silu(x):
    return x * (1.0 / (1.0 + jnp.exp(-x)))


def _dot(a, b):
    return jnp.dot(a, b, preferred_element_type=F32)


def _dot_nt(a, b):
    return lax.dot_general(a, b, (((1,), (1,)), ((), ())), preferred_element_type=F32)


def _const_spec(shape):
    nd = len(shape)
    return pl.BlockSpec(shape, lambda *_: (0,) * nd, pipeline_mode=pl.Buffered(1))


HEAD_PAIRS = N_HEADS // 2


def _inproj_body(head_major, x_ref, g_ref, b_ref, w_ref, *out_refs):
    h = _layer_norm(x_ref[...], g_ref[...], b_ref[...]).astype(BF16)
    proj = lambda lo, width: _dot(h, w_ref[:, lo:lo + width])
    if head_major:
        q_ref, k_ref, v_ref, kt_ref, vt_ref, z_ref, xbc_ref, dt_ref = out_refs
        for idx, (ref, t_ref) in enumerate(((q_ref, None), (k_ref, kt_ref), (v_ref, vt_ref))):
            res = proj(idx * D_ATTN, D_ATTN)
            for c in range(HEAD_PAIRS):
                ref[c] = res[:, c * LANES:(c + 1) * LANES]
            if t_ref is not None:
                t_ref[...] = res.T
    else:
        q_ref, k_ref, v_ref, z_ref, xbc_ref, dt_ref = out_refs
        for idx, ref in enumerate((q_ref, k_ref, v_ref)):
            ref[...] = proj(idx * D_ATTN, D_ATTN)
    lo = 3 * D_ATTN
    for ref, width in ((z_ref, D_SSD), (xbc_ref, CONV_DIM), (dt_ref, DT_PAD)):
        ref[...] = proj(lo, width)
        lo += width


def _in_proj(x, g, b, w, *, head_major, tm):
    bsz, seq, _ = x.shape
    row = lambda width: pl.BlockSpec((None, tm, width), lambda i, j: (i, j, 0))
    row_shape = lambda width: jax.ShapeDtypeStruct((bsz, seq, width), F32)
    if head_major:
        slab = pl.BlockSpec((None, HEAD_PAIRS, tm, LANES), lambda i, j: (i, 0, j, 0))
        slab_shape = jax.ShapeDtypeStruct((bsz, HEAD_PAIRS, seq, LANES), F32)
        tr = pl.BlockSpec((None, D_ATTN, tm), lambda i, j: (i, 0, j))
        tr_shape = jax.ShapeDtypeStruct((bsz, D_ATTN, seq), F32)
        qkv_specs, qkv_shapes = [slab, slab, slab, tr, tr], [slab_shape] * 3 + [tr_shape] * 2
    else:
        qkv_specs, qkv_shapes = [row(D_ATTN)] * 3, [row_shape(D_ATTN)] * 3
    rest = (D_SSD, CONV_DIM, DT_PAD)
    return pl.pallas_call(
        functools.partial(_inproj_body, head_major),
        grid=(bsz, seq // tm),
        in_specs=[row(D_MODEL), _const_spec((1, D_MODEL)), _const_spec((1, D_MODEL)),
                  _const_spec((D_MODEL, D_PROJ))],
        out_specs=qkv_specs + [row(wd) for wd in rest],
        out_shape=qkv_shapes + [row_shape(wd) for wd in rest],
        compiler_params=pltpu.CompilerParams(dimension_semantics=("parallel", "parallel"),
                                             vmem_limit_bytes=VMEM_LIMIT),
        name="in_proj",
    )(x, g, b, w)


FF_CHUNK = 1024


def _post_body(alpha, x_ref, attn_ref, ssd_ref, pe_ref, gin_ref, bin_ref, wout_ref, g1_ref, b1_ref,
               wup_ref, wdown_ref, g2_ref, b2_ref, wgate_ref, wple_ref, g3_ref, b3_ref, y_ref):
    xn = _layer_norm(x_ref[...], gin_ref[...], bin_ref[...])
    mixed = jnp.concatenate([attn_ref[c] for c in range(HEAD_PAIRS)] + [ssd_ref[...]],
                            axis=-1).astype(BF16)
    h = _layer_norm(alpha * xn + _dot(mixed, wout_ref[...]), g1_ref[...], b1_ref[...])
    hb = h.astype(BF16)
    u = None
    for c in range(D_FF // FF_CHUNK):
        a = jnp.maximum(_dot(hb, wup_ref[:, c * FF_CHUNK:(c + 1) * FF_CHUNK]), 0.0)
        part = _dot((a * a).astype(BF16), wdown_ref[c * FF_CHUNK:(c + 1) * FF_CHUNK, :])
        u = part if u is None else u + part
    h = _layer_norm(alpha * h + u, g2_ref[...], b2_ref[...])
    gate = 1.0 / (1.0 + jnp.exp(-_dot(h.astype(BF16), wgate_ref[...])))
    g = gate * _dot(pe_ref[...].astype(BF16), wple_ref[...])
    y_ref[...] = _layer_norm(alpha * h + g, g3_ref[...], b3_ref[...])


def _post(x, attn, ssd, pe, gin, bin_, wout, g1, b1, wup, wdown, g2, b2, wgate, wple, g3, b3, *,
          alpha, tm):
    bsz, seq, _ = x.shape
    row = lambda width: pl.BlockSpec((None, tm, width), lambda i, j: (i, j, 0))
    slab = pl.BlockSpec((None, HEAD_PAIRS, tm, LANES), lambda i, j: (i, 0, j, 0))
    vec = _const_spec((1, D_MODEL))
    return pl.pallas_call(
        functools.partial(_post_body, alpha),
        grid=(bsz, seq // tm),
        in_specs=[row(D_MODEL), slab, row(D_SSD), row(D_PLE), vec, vec,
                  _const_spec((D_MODEL, D_MODEL)), vec, vec,
                  _const_spec((D_MODEL, D_FF)), _const_spec((D_FF, D_MODEL)), vec, vec,
                  _const_spec((D_MODEL, D_MODEL)), _const_spec((D_PLE, D_MODEL)), vec, vec],
        out_specs=row(D_MODEL),
        out_shape=jax.ShapeDtypeStruct((bsz, seq, D_MODEL), F32),
        compiler_params=pltpu.CompilerParams(dimension_semantics=("parallel", "parallel"),
                                             vmem_limit_bytes=VMEM_LIMIT),
        name="post",
    )(x, attn, ssd, pe, gin, bin_, wout, g1, b1, wup, wdown, g2, b2, wgate, wple, g3, b3)


U_ROWS = CACHE_LEN // PHASES
TAB3_LO, TAB2_LO, TAB1_LO, TAB_W = 0, 128, 384, 640
BLOCKS_IN_FLIGHT = 8
B1_IN_FLIGHT = 5


def _prompt_bias_tables():
    slopes = _slopes()
    u = np.arange(U_ROWS)
    d3 = (u[:, None] - u[None, :]).astype(np.float64)
    k4, ul4 = np.meshgrid(np.arange(4), np.arange(32), indexing="ij")
    j2 = (4 * ul4 + k4).reshape(-1)
    d2 = j2[:, None] - np.concatenate([j2 - WINDOW_STEPS, j2])[None, :]
    p16, ul16 = np.meshgrid(np.arange(16), np.arange(8), indexing="ij")
    t1 = (16 * ul16 + p16).reshape(-1)
    d1 = t1[:, None] - (np.arange(2 * WINDOW_STEPS) - WINDOW_STEPS)[None, :]
    tabs = []
    for dist, dil in ((d3, 16), (d2, 4), (d1, 1)):
        valid = (dist >= 0) & (dist <= WINDOW_STEPS)
        per_head = [np.where(valid, -slopes[h] * dist * dil, NEG_INF) for h in range(N_HEADS)]
        tabs.append(np.stack(per_head))
    tab = np.concatenate(tabs, axis=-1)
    return jnp.asarray(tab.reshape(HEAD_PAIRS, 2 * U_ROWS, TAB_W), dtype=F32)


def _attn_blocks(blocks, first_head):
    scores = [_dot_nt(q2, k) + bias for q2, k, _, bias in blocks]
    tops = [jnp.max(s, -1, keepdims=True) for s in scores]
    exps = [jnp.exp(s - m) for s, m in zip(scores, tops)]
    dens = [jnp.sum(e, -1, keepdims=True) for e in exps]
    pvs = [_dot(e.astype(BF16), blk[2]) for e, blk in zip(exps, blocks)]
    half = U_ROWS
    outs = []
    for pv, m, den in zip(pvs, tops, dens):
        o = jnp.where(first_head, pv[:half], pv[half:])
        m2 = jnp.where(first_head, m[:half], m[half:])
        den2 = jnp.where(first_head, den[:half], den[half:])
        outs.append((o / den2, m2 + jnp.log(den2)))
    return outs


def _attn_prompt_body(q_ref, k_ref, v_ref, tab_ref, o_ref, qs, q0p, q1p, kp, vp, kn, vn, quarter,
                      o_scr, l_scr):
    lane = lax.broadcasted_iota(jnp.int32, (U_ROWS, LANES), 1)
    first_head = lane < HEAD_DIM
    scale = HEAD_DIM ** -0.5

    def natural(i, carry):
        rows = pl.ds(pl.multiple_of(i * U_ROWS, U_ROWS), U_ROWS)
        kn[rows, :] = k_ref[rows, :].astype(BF16)
        vn[rows, :] = v_ref[rows, :].astype(BF16)
        return carry
    lax.fori_loop(0, PHASES, natural, 0)

    def split(src_ref, emit):
        for r in range(4):
            quarter[r] = src_ref[pl.ds(r, CACHE_LEN // 4, stride=4), :]
        for r in range(4):
            for kk in range(4):
                emit(4 * kk + r, quarter[r, pl.ds(kk, U_ROWS, stride=4), :])

    def emit_q(p, tile):
        tile = tile * scale
        qs[p] = tile
        q0p[p] = jnp.where(first_head, tile, 0.0).astype(BF16)
        q1p[p] = jnp.where(first_head, 0.0, tile).astype(BF16)

    def emit_to(dst):
        def emit(p, tile):
            dst[p] = tile.astype(BF16)
        return emit

    split(q_ref, emit_q)
    split(k_ref, emit_to(kp))
    split(v_ref, emit_to(vp))

    def branch3(g, carry):
        phases = [g * BLOCKS_IN_FLIGHT + j for j in range(BLOCKS_IN_FLIGHT)]
        bias = tab_ref[:, TAB3_LO:TAB3_LO + 128]
        outs = _attn_blocks([(jnp.concatenate([q0p[p], q1p[p]], 0), kp[p], vp[p], bias)
                             for p in phases], first_head)
        for p, (o, l) in zip(phases, outs):
            o_scr[0, p] = o
            l_scr[0, p] = l
        return carry
    lax.fori_loop(0, PHASES // BLOCKS_IN_FLIGHT, branch3, 0)

    def branch2(g, carry):
        blocks, where = [], []
        for r in (2 * g, 2 * g + 1):
            gather = lambda src, rr, r=r: [src[4 * kk + r, rr, :] for kk in range(4)]
            for n in range(4):
                rows = pl.ds(32 * n, 32)
                q2 = jnp.concatenate(gather(q0p, rows) + gather(q1p, rows), 0)
                if n == 0:
                    k = jnp.concatenate(gather(kp, rows), 0)
                    v = jnp.concatenate(gather(vp, rows), 0)
                    bias = tab_ref[:, TAB2_LO + 128:TAB2_LO + 256]
                else:
                    prev = pl.ds(32 * (n - 1), 32)
                    k = jnp.concatenate(gather(kp, prev) + gather(kp, rows), 0)
                    v = jnp.concatenate(gather(vp, prev) + gather(vp, rows), 0)
                    bias = tab_ref[:, TAB2_LO:TAB2_LO + 256]
                blocks.append((q2, k, v, bias))
                where.append((r, rows))
        for (r, rows), (o, l) in zip(where, _attn_blocks(blocks, first_head)):
            for kk in range(4):
                o_scr[1, 4 * kk + r, rows, :] = o[32 * kk:32 * kk + 32]
                l_scr[1, 4 * kk + r, rows, :] = l[32 * kk:32 * kk + 32]
        return carry
    lax.fori_loop(0, 2, branch2, 0)

    def branch1_blocks(ns, with_prev):
        blocks = []
        for n in ns:
            rows = pl.ds(pl.multiple_of(n * SUBLANES, SUBLANES), SUBLANES)
            q = jnp.concatenate([qs[p, rows, :] for p in range(PHASES)], 0)
            q2 = jnp.concatenate([jnp.where(first_head, q, 0.0), jnp.where(first_head, 0.0, q)],
                                 0).astype(BF16)
            if with_prev:
                keys = pl.ds(pl.multiple_of((n - 1) * U_ROWS, U_ROWS), 2 * U_ROWS)
                bias = tab_ref[:, TAB1_LO:TAB1_LO + 256]
            else:
                keys = pl.ds(0, U_ROWS)
                bias = tab_ref[:, TAB1_LO + 128:TAB1_LO + 256]
            blocks.append((q2, kn[keys, :], vn[keys, :], bias))
        for n, (o, l) in zip(ns, _attn_blocks(blocks, first_head)):
            rows = pl.ds(pl.multiple_of(n * SUBLANES, SUBLANES), SUBLANES)
            for p in range(PHASES):
                o_scr[2, p, rows, :] = o[SUBLANES * p:SUBLANES * (p + 1)]
                l_scr[2, p, rows, :] = l[SUBLANES * p:SUBLANES * (p + 1)]

    branch1_blocks([0], False)

    def branch1(g, carry):
        branch1_blocks([1 + g * B1_IN_FLIGHT + j for j in range(B1_IN_FLIGHT)], True)
        return carry
    lax.fori_loop(0, (U_ROWS // SUBLANES - 1) // B1_IN_FLIGHT, branch1, 0)

    def phase_rows(p):
        return pl.ds(p, U_ROWS, stride=PHASES)

    def merge(p, carry):
        l3, l2, l1 = l_scr[0, p], l_scr[1, p], l_scr[2, p]
        top = jnp.maximum(jnp.maximum(l3, l2), l1)
        e3, e2, e1 = jnp.exp(l3 - top), jnp.exp(l2 - top), jnp.exp(l1 - top)
        total = e3 + e2 + e1
        o_ref[phase_rows(p), :] = (e3 * o_scr[0, p] + e2 * o_scr[1, p] + e1 * o_scr[2, p]) / total
        return carry
    lax.fori_loop(0, PHASES, merge, 0)


def _attn_prompt(q, k, v):
    bsz, _, seq, _ = k.shape
    assert seq == CACHE_LEN, "prompt attention is laid out for a 2048-token prompt"
    slab = pl.BlockSpec((None, None, seq, LANES), lambda b, c: (b, c, 0, 0))
    tile = (PHASES, U_ROWS, LANES)
    return pl.pallas_call(
        _attn_prompt_body,
        grid=(bsz, HEAD_PAIRS),
        in_specs=[slab, slab, slab,
                  pl.BlockSpec((None, 2 * U_ROWS, TAB_W), lambda b, c: (c, 0, 0))],
        out_specs=slab,
        out_shape=jax.ShapeDtypeStruct((bsz, HEAD_PAIRS, seq, LANES), F32),
        scratch_shapes=[pltpu.VMEM(tile, F32)] + [pltpu.VMEM(tile, BF16)] * 4
                       + [pltpu.VMEM((seq, LANES), BF16)] * 2
                       + [pltpu.VMEM((4, seq // 4, LANES), F32),
                          pltpu.VMEM((3,) + tile, F32), pltpu.VMEM((3,) + tile, F32)],
        compiler_params=pltpu.CompilerParams(dimension_semantics=("parallel", "parallel"),
                                             vmem_limit_bytes=VMEM_LIMIT),
        name="attn_prompt",
    )(q, k, v, _prompt_bias_tables())


DEC_T = 8
QH = DEC_T * N_HEADS
B2_SPAN, B1_SPAN = 512, 128
SAMPLE_TAB_W = CACHE_LEN + B2_SPAN + B1_SPAN


def _sample_bias_tables():
    slopes = _slopes()
    i = np.repeat(np.arange(DEC_T), N_HEADS)[:, None]
    sl = np.tile(slopes, DEC_T)[:, None]

    def cache_bias(span, dil):
        t = CACHE_LEN - span + np.arange(span)[None, :]
        dist = CACHE_LEN + i - t
        valid = (dist % dil == 0) & (dist <= WINDOW_STEPS * dil)
        return np.where(valid, -sl * dist, NEG_INF)

    cache_tab = np.concatenate([cache_bias(CACHE_LEN, 16), cache_bias(B2_SPAN, 4),
                                cache_bias(B1_SPAN, 1)], axis=1)
    c = np.arange(CHUNK)[None, :]
    dn = i - c
    is_new = c < DEC_T
    n1 = np.where(is_new & (dn >= 0), -sl * dn, NEG_INF)
    n2 = np.where(is_new & ((dn == 0) | (dn == 4)), -sl * dn, NEG_INF)
    n3 = np.where(is_new & (dn == 0), 0.0, NEG_INF)
    return jnp.asarray(cache_tab, dtype=F32), jnp.asarray(np.stack([n1, n2, n3]), dtype=F32)


def _head_selector():
    h_row = np.tile(np.arange(N_HEADS), DEC_T)[:, None]
    h_col = (np.arange(D_ATTN) // HEAD_DIM)[None, :]
    return jnp.asarray((h_row == h_col).astype(np.float32))


def _attn_sample_stages(q_ref, kn_ref, vn_ref, kt_ref, vt_ref, tab_ref, tabn_ref, sel_ref, o_ref):
    sel = sel_ref[...]
    q = q_ref[...] * (HEAD_DIM ** -0.5)
    q_rows = jnp.concatenate([jnp.broadcast_to(q[i:i + 1, :], (N_HEADS, D_ATTN))
                              for i in range(DEC_T)], 0)
    qall = (q_rows * sel).astype(BF16)
    pad = jnp.zeros((CHUNK - DEC_T, D_ATTN), F32)
    kn = jnp.concatenate([kn_ref[...], pad], 0).astype(BF16)
    vn = jnp.concatenate([vn_ref[...], pad], 0).astype(BF16)

    s_all = _dot(qall, kt_ref[...].astype(BF16))
    s_new = _dot_nt(qall, kn)
    yield
    lo2, lo1 = CACHE_LEN - B2_SPAN, CACHE_LEN - B1_SPAN

    def branch(s_cache, s_fresh, v_t):
        m = jnp.maximum(jnp.max(s_cache, -1, keepdims=True), jnp.max(s_fresh, -1, keepdims=True))
        e_c, e_f = jnp.exp(s_cache - m), jnp.exp(s_fresh - m)
        den = jnp.sum(e_c, -1, keepdims=True) + jnp.sum(e_f, -1, keepdims=True)
        acc = _dot_nt(e_c.astype(BF16), v_t.astype(BF16)) + _dot(e_f.astype(BF16), vn)
        return m, den, acc

    m3, den3, acc3 = branch(s_all + tab_ref[:, :CACHE_LEN], s_new + tabn_ref[2], vt_ref[...])
    m2, den2, acc2 = branch(s_all[:, lo2:] + tab_ref[:, CACHE_LEN:CACHE_LEN + B2_SPAN],
                            s_new + tabn_ref[1], vt_ref[:, lo2:])
    m1, den1, acc1 = branch(s_all[:, lo1:] + tab_ref[:, CACHE_LEN + B2_SPAN:],
                            s_new + tabn_ref[0], vt_ref[:, lo1:])

    top = jnp.maximum(jnp.maximum(m1, m2), m3)
    sc1, sc2, sc3 = den1 * jnp.exp(m1 - top), den2 * jnp.exp(m2 - top), den3 * jnp.exp(m3 - top)
    total = sc1 + sc2 + sc3
    mixed = ((sc1 / total) * (acc1 / den1) + (sc2 / total) * (acc2 / den2)
             + (sc3 / total) * (acc3 / den3)) * sel
    for i in range(DEC_T):
        row = jnp.sum(mixed[N_HEADS * i:N_HEADS * (i + 1)], axis=0, keepdims=True)
        for c in range(HEAD_PAIRS):
            o_ref[c, i:i + 1, :] = row[:, c * LANES:(c + 1) * LANES]


def _attn_sample_body(*refs):
    for _ in _attn_sample_stages(*refs):
        pass


def _attn_sample_call(q, k_new, v_new, cache_kt, cache_vt):
    n = q.shape[0]
    assert cache_kt.shape[1:] == (D_ATTN, CACHE_LEN) and q.shape[1] == DEC_T
    tok = pl.BlockSpec((None, DEC_T, D_ATTN), lambda i: (i, 0, 0))
    cache = pl.BlockSpec((None, D_ATTN, CACHE_LEN), lambda i: (i, 0, 0))
    cache_tab, new_tab = _sample_bias_tables()
    operands = (q, k_new, v_new, cache_kt, cache_vt, cache_tab, new_tab, _head_selector())
    in_specs = [tok, tok, tok, cache, cache, _const_spec((QH, SAMPLE_TAB_W)),
                _const_spec((3, QH, CHUNK)), _const_spec((QH, D_ATTN))]
    out_spec = pl.BlockSpec((None, HEAD_PAIRS, DEC_T, LANES), lambda i: (0, 0, i, 0))
    out_shape = jax.ShapeDtypeStruct((1, HEAD_PAIRS, n * DEC_T, LANES), F32)
    return operands, in_specs, out_spec, out_shape


def _attn_sample(q, k_new, v_new, cache_kt, cache_vt):
    operands, in_specs, out_spec, out_shape = _attn_sample_call(q, k_new, v_new, cache_kt, cache_vt)
    return pl.pallas_call(
        _attn_sample_body,
        grid=(q.shape[0],),
        in_specs=in_specs,
        out_specs=out_spec,
        out_shape=out_shape,
        compiler_params=pltpu.CompilerParams(dimension_semantics=("parallel",),
                                             vmem_limit_bytes=VMEM_LIMIT),
        name="attn_sample",
    )(*operands)


HEADS_PER_GROUP = N_HEADS // SSD_GROUPS
PAD_ROWS = SUBLANES
SHORT_SEQ = SUBLANES
SEQS_PER_TILE = CHUNK // SHORT_SEQ


def _ssd_stages(packed, first_chunk, z_ref, xbc_ref, dt_ref, cp_ref, h0_ref, cw_ref, cb_ref, dtb_ref,
                alog_ref, dsk_ref, nw_ref, y_ref, hn_ref, xpad, aux):
    if packed:
        aux[0:CHUNK, :] = cp_ref[...]
        aux[CHUNK:, :] = jnp.zeros((PAD_ROWS, CONV_DIM), F32)
        xpad[0:PAD_ROWS, :] = jnp.zeros((PAD_ROWS, CONV_DIM), F32)
    else:
        @pl.when(first_chunk())
        def _init():
            xpad[0:PAD_ROWS, :] = cp_ref[...]
            aux[...] = h0_ref[...]
    yield
    xpad[PAD_ROWS:, :] = xbc_ref[...]

    step = lax.broadcasted_iota(jnp.int32, (CHUNK, CONV_DIM), 0) % SHORT_SEQ
    conv = cb_ref[...]
    for back in range(CONV_W):
        tap = CONV_W - 1 - back
        rows_back = xpad[pl.ds(PAD_ROWS - back, CHUNK), :]
        if packed and back:
            rows_back = jnp.where(step < back, aux[pl.ds(PAD_ROWS - back, CHUNK), :], rows_back)
        conv = conv + rows_back * cw_ref[tap:tap + 1, :]
    xc = _silu(conv)
    xs = xc[:, :D_SSD]

    lane = lax.broadcasted_iota(jnp.int32, (CHUNK, DT_PAD), 1)
    dt_raw = dt_ref[...] + dtb_ref[...]
    dt = jnp.maximum(dt_raw, 0.0) + jnp.log1p(jnp.exp(-jnp.abs(dt_raw)))
    dt = jnp.where(lane < N_HEADS, dt, 0.0)
    adt = dt * (-jnp.exp(alog_ref[...]))

    ri = lax.broadcasted_iota(jnp.int32, (CHUNK, CHUNK), 0)
    ci = lax.broadcasted_iota(jnp.int32, (CHUNK, CHUNK), 1)
    causal = ri >= ci
    if packed:
        causal = causal & (ri // SHORT_SEQ == ci // SHORT_SEQ)
    exact_dot = functools.partial(jnp.dot, precision=lax.Precision.HIGHEST,
                                  preferred_element_type=F32)
    cs = exact_dot(jnp.where(causal, 1.0, 0.0).astype(F32), adt)
    cs_t = cs.T
    if packed:
        pick_last = ci == (ri // SHORT_SEQ) * SHORT_SEQ + (SHORT_SEQ - 1)
        cs_end = exact_dot(jnp.where(pick_last, 1.0, 0.0).astype(F32), cs)
    else:
        cs_end = cs[CHUNK - 1:CHUNK, :]

    head_of_lane = lax.broadcasted_iota(jnp.int32, (CHUNK, GROUP_W), 1) // HEAD_DIM

    def per_head(cols):
        out = cols[HEADS_PER_GROUP - 1]
        for hl in range(HEADS_PER_GROUP - 2, -1, -1):
            out = jnp.where(head_of_lane == hl, cols[hl], out)
        return out

    y_groups = []
    for g in range(SSD_GROUPS):
        heads = range(g * HEADS_PER_GROUP, (g + 1) * HEADS_PER_GROUP)
        grp = slice(g * GROUP_W, (g + 1) * GROUP_W)
        b_g = xc[:, D_SSD + g * D_STATE:D_SSD + (g + 1) * D_STATE].astype(BF16)
        c_g = xc[:, D_SSD + (SSD_GROUPS + g) * D_STATE:
                 D_SSD + (SSD_GROUPS + g + 1) * D_STATE].astype(BF16)
        cs_cols = [cs[:, h:h + 1] for h in heads]
        xdt_g = xs[:, grp] * per_head([dt[:, h:h + 1] for h in heads])
        xdt_b = xdt_g.astype(BF16)
        gram = _dot_nt(c_g, b_g)

        y_diag = jnp.zeros((CHUNK, GROUP_W), F32)
        for hl, h in enumerate(heads):
            seg = jnp.where(causal, cs_cols[hl] - cs_t[h:h + 1, :], NEG_INF)
            weights = (gram * jnp.exp(seg)).astype(BF16)
            y_diag = y_diag + jnp.where(head_of_lane == hl, _dot(weights, xdt_b), 0.0)

        to_end = per_head([jnp.exp(cs_end[:, h:h + 1] - cs_cols[hl]) for hl, h in enumerate(heads)])
        decayed_t = (xdt_g * to_end).T
        carried = per_head([jnp.exp(col) for col in cs_cols])
        if packed:
            h_prev = h0_ref[:, grp, :]
            wide = _dot_nt(c_g, h_prev.reshape(SEQS_PER_TILE * GROUP_W, D_STATE).astype(BF16))
            y_off = jnp.concatenate(
                [wide[s * SHORT_SEQ:(s + 1) * SHORT_SEQ, s * GROUP_W:(s + 1) * GROUP_W]
                 for s in range(SEQS_PER_TILE)], 0) * carried
            seq_of_step = lax.broadcasted_iota(jnp.int32, (GROUP_W, CHUNK), 1) // SHORT_SEQ
            per_seq = jnp.concatenate([jnp.where(seq_of_step == s, decayed_t, 0.0)
                                       for s in range(SEQS_PER_TILE)], 0).astype(BF16)
            new_states = _dot(per_seq, b_g).reshape(SEQS_PER_TILE, GROUP_W, D_STATE)
            for s in range(SEQS_PER_TILE):
                row = s * SHORT_SEQ
                keep = jnp.concatenate(
                    [jnp.broadcast_to(jnp.exp(cs_end[row:row + 1, h:h + 1]), (HEAD_DIM, D_STATE))
                     for h in heads], 0)
                hn_ref[s, grp, :] = h_prev[s] * keep + new_states[s]
        else:
            h_prev = aux[grp, :]
            y_off = _dot_nt(c_g, h_prev.astype(BF16)) * carried
            keep = jnp.concatenate(
                [jnp.broadcast_to(jnp.exp(cs_end[:, h:h + 1]), (HEAD_DIM, D_STATE)) for h in heads], 0)
            aux[grp, :] = h_prev * keep + _dot(decayed_t.astype(BF16), b_g)
        y_groups.append(y_diag + y_off)

    y = jnp.concatenate(y_groups, axis=-1) + dsk_ref[...] * xs
    y = y * _silu(z_ref[...])
    normed = []
    for g in range(SSD_GROUPS):
        yg = y[:, g * GROUP_W:(g + 1) * GROUP_W]
        normed.append(yg * lax.rsqrt(jnp.mean(yg * yg, -1, keepdims=True) + LN_EPS))
    y_ref[...] = jnp.concatenate(normed, axis=-1) * nw_ref[...]

    if not packed:
        hn_ref[...] = aux[...]
        xpad[0:PAD_ROWS, :] = xpad[CHUNK:, :]


def _ssd_body(*args):
    for _ in _ssd_stages(*args):
        pass


N_SSD_INPUTS = 11


def _ssd_call(z, xbc, dt, conv_prev, h0, cw, cb, dtb, alog, dsk, nw, where=None):
    n, seq, _ = z.shape
    packed = seq == SHORT_SEQ
    if packed:
        assert n % SEQS_PER_TILE == 0 and where is None
        grid = (n // SEQS_PER_TILE, 1)
        fold = lambda t: t.reshape(grid[0], CHUNK, t.shape[-1])
        z, xbc, dt, conv_prev = fold(z), fold(xbc), fold(dt), fold(conv_prev)
        history_shape, state_shape = (None, CHUNK, CONV_DIM), (SEQS_PER_TILE, D_SSD, D_STATE)
        aux = pltpu.VMEM((CHUNK + PAD_ROWS, CONV_DIM), F32)
    else:
        assert seq % CHUNK == 0
        grid = (n, seq // CHUNK)
        history_shape, state_shape = (None, PAD_ROWS, CONV_DIM), (None, D_SSD, D_STATE)
        aux = pltpu.VMEM((D_SSD, D_STATE), F32)
    where = where or (lambda s, c: (s, c))
    tile = lambda width: pl.BlockSpec((None, CHUNK, width), lambda *g: where(*g) + (0,))
    per_seq = lambda shape: pl.BlockSpec(shape, lambda *g: (where(*g)[0], 0, 0))
    history, state = per_seq(history_shape), per_seq(state_shape)
    operands = (z, xbc, dt, conv_prev, h0, cw, cb, dtb, alog, dsk, nw)
    in_specs = [tile(D_SSD), tile(CONV_DIM), tile(DT_PAD), history, state,
                _const_spec((CONV_W, CONV_DIM)), _const_spec((1, CONV_DIM)),
                _const_spec((1, DT_PAD)), _const_spec((1, DT_PAD)),
                _const_spec((1, D_SSD)), _const_spec((1, D_SSD))]
    out_specs = [tile(D_SSD), state]
    out_shape = [jax.ShapeDtypeStruct(z.shape, F32), jax.ShapeDtypeStruct((n, D_SSD, D_STATE), F32)]
    scratch = [pltpu.VMEM((PAD_ROWS + CHUNK, CONV_DIM), F32), aux]
    return packed, grid, operands, in_specs, out_specs, out_shape, scratch


def _ssd(z, *rest):
    packed, grid, operands, in_specs, out_specs, out_shape, scratch = _ssd_call(z, *rest)
    y, h_new = pl.pallas_call(
        functools.partial(_ssd_body, packed, lambda: pl.program_id(1) == 0),
        grid=grid,
        in_specs=in_specs,
        out_specs=out_specs,
        out_shape=out_shape,
        scratch_shapes=scratch,
        compiler_params=pltpu.CompilerParams(dimension_semantics=("parallel", "arbitrary"),
                                             vmem_limit_bytes=VMEM_LIMIT),
        name="ssd",
    )(*operands)
    return y.reshape(z.shape), h_new


def _ssd_with_sample_attn(ssd_args, attn_args):
    z = ssd_args[0]
    nchunks = z.shape[1] // CHUNK
    steps = z.shape[0] * nchunks
    assert steps == attn_args[0].shape[0]
    where = lambda i: (i // nchunks, i % nchunks)
    _, _, ssd_ops, ssd_in, ssd_out, ssd_shape, scratch = _ssd_call(*ssd_args, where=where)
    attn_ops, attn_in, attn_out, attn_shape = _attn_sample_call(*attn_args)

    def body(*refs):
        ssd_in_refs, refs = refs[:N_SSD_INPUTS], refs[N_SSD_INPUTS:]
        attn_in_refs, refs = refs[:len(attn_ops)], refs[len(attn_ops):]
        y_ref, hn_ref, o_ref, xpad, aux = refs
        ssd = _ssd_stages(False, lambda: pl.program_id(0) % nchunks == 0, *ssd_in_refs, y_ref,
                          hn_ref, xpad, aux)
        attn = _attn_sample_stages(*attn_in_refs, o_ref)
        for stage in (ssd, attn, ssd, attn):
            next(stage, None)

    y, h_new, attn = pl.pallas_call(
        body,
        grid=(steps,),
        in_specs=ssd_in + attn_in,
        out_specs=ssd_out + [attn_out],
        out_shape=ssd_shape + [attn_shape],
        scratch_shapes=scratch,
        compiler_params=pltpu.CompilerParams(dimension_semantics=("arbitrary",),
                                             vmem_limit_bytes=VMEM_LIMIT),
        name="ssd_attn_sample",
    )(*ssd_ops, *attn_ops)
    return y, h_new, attn


def _row(v, width=None):
    v = v.reshape(1, -1).astype(F32)
    if width is not None and v.shape[1] < width:
        v = jnp.pad(v, ((0, 0), (0, width - v.shape[1])))
    return v


def kernel(x_prompt, x_sample, cache_k, cache_v, state_conv, state_ssm, p_prompt, p_sample,
           ln_in_g, ln_in_b, w_in, conv_w, conv_b, dt_bias, a_log, d_skip, ssd_norm_w, w_out,
           ln1_g, ln1_b, w_up, w_down, ln2_g, ln2_b, w_gate, w_ple, ln3_g, ln3_b):
    depth = w_in.shape[0]
    assert depth == 1, "single-layer step"
    alpha = (2 * depth) ** 0.25
    bsz, seq, _ = x_prompt.shape
    nd, dec_t, _ = x_sample.shape
    lyr = 0

    w_proj = jnp.pad(w_in[lyr], ((0, 0), (0, D_PROJ - w_in.shape[2]))).astype(BF16)
    gin, bin_ = _row(ln_in_g), _row(ln_in_b)
    ssd_params = (conv_w[lyr].astype(F32), _row(conv_b[lyr]), _row(dt_bias[lyr], DT_PAD),
                  _row(a_log[lyr], DT_PAD), _row(jnp.repeat(d_skip[lyr], HEAD_DIM)),
                  _row(ssd_norm_w[lyr]))
    post_params = (gin, bin_, w_out[lyr].astype(BF16), _row(ln1_g[lyr]), _row(ln1_b[lyr]),
                   w_up[lyr].astype(BF16), w_down[lyr].astype(BF16), _row(ln2_g[lyr]),
                   _row(ln2_b[lyr]), w_gate[lyr].astype(BF16), w_ple[lyr].astype(BF16),
                   _row(ln3_g[lyr]), _row(ln3_b[lyr]))

    q, k, v, k_t, v_t, z, xbc, dt = _in_proj(x_prompt, gin, bin_, w_proj, head_major=True, tm=512)
    n_tok = nd * dec_t
    flat = lambda t: t.reshape(1, n_tok, t.shape[-1])
    toks = lambda t: t.reshape(nd, dec_t, t.shape[-1])
    qs, ks, vs, zs, xbcs, dts = _in_proj(flat(x_sample), gin, bin_, w_proj, head_major=False, tm=512)

    attn = _attn_prompt(q, k, v)
    transposed = lambda c: jnp.transpose(c, (0, 2, 3, 1)).reshape(nd, D_ATTN, CACHE_LEN)
    ssd_args = (z, xbc, dt, jnp.zeros((bsz, PAD_ROWS, CONV_DIM), F32),
                jnp.zeros((bsz, D_SSD, D_STATE), F32)) + ssd_params
    attn_args = (toks(qs), toks(ks), toks(vs), transposed(cache_k[lyr]), transposed(cache_v[lyr]))
    if bsz * (seq // CHUNK) == nd:
        ssd_y, ssm_p, attn_s = _ssd_with_sample_attn(ssd_args, attn_args)
    else:
        ssd_y, ssm_p = _ssd(*ssd_args)
        attn_s = _attn_sample(*attn_args)
    conv_prev = jnp.pad(state_conv[lyr].astype(F32), ((0, 0), (PAD_ROWS - (CONV_W - 1), 0), (0, 0)))
    ssd_s, ssm_s = _ssd(toks(zs), toks(xbcs), toks(dts), conv_prev,
                        state_ssm[lyr].reshape(nd, D_SSD, D_STATE).astype(F32), *ssd_params)

    y_prompt = _post(x_prompt, attn, ssd_y, p_prompt[lyr], *post_params, alpha=alpha, tm=512)
    y_sample = _post(flat(x_sample), attn_s, flat(ssd_s), flat(p_sample[lyr]), *post_params,
                     alpha=alpha, tm=256)

    from_t = lambda t: jnp.transpose(t.reshape(1, bsz, N_HEADS, HEAD_DIM, seq), (0, 1, 4, 2, 3))
    heads = lambda t: t.reshape(1, nd, dec_t, N_HEADS, HEAD_DIM)
    tail = lambda t: t[None, :, -(CONV_W - 1):, :]
    state = lambda t, n: t.reshape(1, n, N_HEADS, HEAD_DIM, D_STATE)
    return (y_prompt, y_sample.reshape(nd, dec_t, D_MODEL), from_t(k_t), from_t(v_t),
            heads(ks), heads(vs), tail(xbc), tail(toks(xbcs)), state(ssm_p, bsz), state(ssm_s, nd))
```

```python
import functools

import numpy as np
import jax
import jax.numpy as jnp
from jax import lax
from jax.experimental import pallas as pl
from jax.experimental.pallas import tpu as pltpu

F32 = jnp.float32
BF16 = jnp.bfloat16

D_MODEL = 1024
HEAD_DIM = 64
D_ATTN = 512
D_SSD = 512
N_HEADS = 8
SSD_GROUPS = 2
GROUP_W = D_SSD // SSD_GROUPS
D_STATE = 128
CONV_W = 4
CONV_DIM = D_SSD + 2 * SSD_GROUPS * D_STATE
CHUNK = 128
D_FF = 4096
D_PLE = 256
LN_EPS = 1e-5
WINDOW_STEPS = 128
DILATIONS = (1, 4, 16)
PHASES = 16
CACHE_LEN = 2048
DT_PAD = 128
D_PROJ = 3 * D_ATTN + D_SSD + CONV_DIM + DT_PAD
SUBLANES = 8
LANES = 128
VMEM_LIMIT = 52 * 1024 * 1024
NEG_INF = float("-inf")


def _slopes():
    return np.array([2.0 ** (-8.0 * (h + 1) / N_HEADS) for h in range(N_HEADS)], dtype=np.float64)


def _layer_norm(x, g, b):
    mu = jnp.mean(x, -1, keepdims=True)
    xc = x - mu
    var = jnp.mean(xc * xc, -1, keepdims=True)
    return xc * lax.rsqrt(var + LN_EPS) * g + b


def _silu(x):
    return x * (1.0 / (1.0 + jnp.exp(-x)))


def _dot(a, b):
    return jnp.dot(a, b, preferred_element_type=F32)


def _dot_nt(a, b):
    return lax.dot_general(a, b, (((1,), (1,)), ((), ())), preferred_element_type=F32)


def _const_spec(shape):
    nd = len(shape)
    return pl.BlockSpec(shape, lambda *_: (0,) * nd, pipeline_mode=pl.Buffered(1))


HEAD_PAIRS = N_HEADS // 2


def _inproj_body(head_major, x_ref, g_ref, b_ref, w_ref, *out_refs):
    h = _layer_norm(x_ref[...], g_ref[...], b_ref[...]).astype(BF16)
    proj = lambda lo, width: _dot(h, w_ref[:, lo:lo + width])
    if head_major:
        q_ref, k_ref, v_ref, kt_ref, vt_ref, z_ref, xbc_ref, dt_ref = out_refs
        for idx, (ref, t_ref) in enumerate(((q_ref, None), (k_ref, kt_ref), (v_ref, vt_ref))):
            res = proj(idx * D_ATTN, D_ATTN)
            for c in range(HEAD_PAIRS):
                ref[c] = res[:, c * LANES:(c + 1) * LANES]
            if t_ref is not None:
                t_ref[...] = res.T
    else:
        q_ref, k_ref, v_ref, z_ref, xbc_ref, dt_ref = out_refs
        for idx, ref in enumerate((q_ref, k_ref, v_ref)):
            ref[...] = proj(idx * D_ATTN, D_ATTN)
    lo = 3 * D_ATTN
    for ref, width in ((z_ref, D_SSD), (xbc_ref, CONV_DIM), (dt_ref, DT_PAD)):
        ref[...] = proj(lo, width)
        lo += width


def _in_proj(x, g, b, w, *, head_major, tm):
    bsz, seq, _ = x.shape
    row = lambda width: pl.BlockSpec((None, tm, width), lambda i, j: (i, j, 0))
    row_shape = lambda width: jax.ShapeDtypeStruct((bsz, seq, width), F32)
    if head_major:
        slab = pl.BlockSpec((None, HEAD_PAIRS, tm, LANES), lambda i, j: (i, 0, j, 0))
        slab_shape = jax.ShapeDtypeStruct((bsz, HEAD_PAIRS, seq, LANES), F32)
        tr = pl.BlockSpec((None, D_ATTN, tm), lambda i, j: (i, 0, j))
        tr_shape = jax.ShapeDtypeStruct((bsz, D_ATTN, seq), F32)
        qkv_specs, qkv_shapes = [slab, slab, slab, tr, tr], [slab_shape] * 3 + [tr_shape] * 2
    else:
        qkv_specs, qkv_shapes = [row(D_ATTN)] * 3, [row_shape(D_ATTN)] * 3
    rest = (D_SSD, CONV_DIM, DT_PAD)
    return pl.pallas_call(
        functools.partial(_inproj_body, head_major),
        grid=(bsz, seq // tm),
        in_specs=[row(D_MODEL), _const_spec((1, D_MODEL)), _const_spec((1, D_MODEL)),
                  _const_spec((D_MODEL, D_PROJ))],
        out_specs=qkv_specs + [row(wd) for wd in rest],
        out_shape=qkv_shapes + [row_shape(wd) for wd in rest],
        compiler_params=pltpu.CompilerParams(dimension_semantics=("parallel", "parallel"),
                                             vmem_limit_bytes=VMEM_LIMIT),
        name="in_proj",
    )(x, g, b, w)


FF_CHUNK = 1024


def _post_body(alpha, x_ref, attn_ref, ssd_ref, pe_ref, gin_ref, bin_ref, wout_ref, g1_ref, b1_ref,
               wup_ref, wdown_ref, g2_ref, b2_ref, wgate_ref, wple_ref, g3_ref, b3_ref, y_ref):
    xn = _layer_norm(x_ref[...], gin_ref[...], bin_ref[...])
    mixed = jnp.concatenate([attn_ref[c] for c in range(HEAD_PAIRS)] + [ssd_ref[...]],
                            axis=-1).astype(BF16)
    h = _layer_norm(alpha * xn + _dot(mixed, wout_ref[...]), g1_ref[...], b1_ref[...])
    hb = h.astype(BF16)
    u = None
    for c in range(D_FF // FF_CHUNK):
        a = jnp.maximum(_dot(hb, wup_ref[:, c * FF_CHUNK:(c + 1) * FF_CHUNK]), 0.0)
        part = _dot((a * a).astype(BF16), wdown_ref[c * FF_CHUNK:(c + 1) * FF_CHUNK, :])
        u = part if u is None else u + part
    h = _layer_norm(alpha * h + u, g2_ref[...], b2_ref[...])
    gate = 1.0 / (1.0 + jnp.exp(-_dot(h.astype(BF16), wgate_ref[...])))
    g = gate * _dot(pe_ref[...].astype(BF16), wple_ref[...])
    y_ref[...] = _layer_norm(alpha * h + g, g3_ref[...], b3_ref[...])


def _post(x, attn, ssd, pe, gin, bin_, wout, g1, b1, wup, wdown, g2, b2, wgate, wple, g3, b3, *,
          alpha, tm):
    bsz, seq, _ = x.shape
    row = lambda width: pl.BlockSpec((None, tm, width), lambda i, j: (i, j, 0))
    slab = pl.BlockSpec((None, HEAD_PAIRS, tm, LANES), lambda i, j: (i, 0, j, 0))
    vec = _const_spec((1, D_MODEL))
    return pl.pallas_call(
        functools.partial(_post_body, alpha),
        grid=(bsz, seq // tm),
        in_specs=[row(D_MODEL), slab, row(D_SSD), row(D_PLE), vec, vec,
                  _const_spec((D_MODEL, D_MODEL)), vec, vec,
                  _const_spec((D_MODEL, D_FF)), _const_spec((D_FF, D_MODEL)), vec, vec,
                  _const_spec((D_MODEL, D_MODEL)), _const_spec((D_PLE, D_MODEL)), vec, vec],
        out_specs=row(D_MODEL),
        out_shape=jax.ShapeDtypeStruct((bsz, seq, D_MODEL), F32),
        compiler_params=pltpu.CompilerParams(dimension_semantics=("parallel", "parallel"),
                                             vmem_limit_bytes=VMEM_LIMIT),
        name="post",
    )(x, attn, ssd, pe, gin, bin_, wout, g1, b1, wup, wdown, g2, b2, wgate, wple, g3, b3)


U_ROWS = CACHE_LEN // PHASES
TAB3_LO, TAB2_LO, TAB1_LO, TAB_W = 0, 128, 384, 640
BLOCKS_IN_FLIGHT = 8
B1_IN_FLIGHT = 5


def _prompt_bias_tables():
    slopes = _slopes()
    u = np.arange(U_ROWS)
    d3 = (u[:, None] - u[None, :]).astype(np.float64)
    k4, ul4 = np.meshgrid(np.arange(4), np.arange(32), indexing="ij")
    j2 = (4 * ul4 + k4).reshape(-1)
    d2 = j2[:, None] - np.concatenate([j2 - WINDOW_STEPS, j2])[None, :]
    p16, ul16 = np.meshgrid(np.arange(16), np.arange(8), indexing="ij")
    t1 = (16 * ul16 + p16).reshape(-1)
    d1 = t1[:, None] - (np.arange(2 * WINDOW_STEPS) - WINDOW_STEPS)[None, :]
    tabs = []
    for dist, dil in ((d3, 16), (d2, 4), (d1, 1)):
        valid = (dist >= 0) & (dist <= WINDOW_STEPS)
        per_head = [np.where(valid, -slopes[h] * dist * dil, NEG_INF) for h in range(N_HEADS)]
        tabs.append(np.stack(per_head))
    tab = np.concatenate(tabs, axis=-1)
    return jnp.asarray(tab.reshape(HEAD_PAIRS, 2 * U_ROWS, TAB_W), dtype=F32)


def _attn_blocks(blocks, first_head):
    scores = [_dot_nt(q2, k) + bias for q2, k, _, bias in blocks]
    tops = [jnp.max(s, -1, keepdims=True) for s in scores]
    exps = [jnp.exp(s - m) for s, m in zip(scores, tops)]
    dens = [jnp.sum(e, -1, keepdims=True) for e in exps]
    pvs = [_dot(e.astype(BF16), blk[2]) for e, blk in zip(exps, blocks)]
    half = U_ROWS
    outs = []
    for pv, m, den in zip(pvs, tops, dens):
        o = jnp.where(first_head, pv[:half], pv[half:])
        m2 = jnp.where(first_head, m[:half], m[half:])
        den2 = jnp.where(first_head, den[:half], den[half:])
        outs.append((o / den2, m2 + jnp.log(den2)))
    return outs


def _attn_prompt_body(q_ref, k_ref, v_ref, tab_ref, o_ref, qs, q0p, q1p, kp, vp, kn, vn, quarter,
                      o_scr, l_scr):
    lane = lax.broadcasted_iota(jnp.int32, (U_ROWS, LANES), 1)
    first_head = lane < HEAD_DIM
    scale = HEAD_DIM ** -0.5

    def natural(i, carry):
        rows = pl.ds(pl.multiple_of(i * U_ROWS, U_ROWS), U_ROWS)
        kn[rows, :] = k_ref[rows, :].astype(BF16)
        vn[rows, :] = v_ref[rows, :].astype(BF16)
        return carry
    lax.fori_loop(0, PHASES, natural, 0)

    def split(src_ref, emit):
        for r in range(4):
            quarter[r] = src_ref[pl.ds(r, CACHE_LEN // 4, stride=4), :]
        for r in range(4):
            for kk in range(4):
                emit(4 * kk + r, quarter[r, pl.ds(kk, U_ROWS, stride=4), :])

    def emit_q(p, tile):
        tile = tile * scale
        qs[p] = tile
        q0p[p] = jnp.where(first_head, tile, 0.0).astype(BF16)
        q1p[p] = jnp.where(first_head, 0.0, tile).astype(BF16)

    def emit_to(dst):
        def emit(p, tile):
            dst[p] = tile.astype(BF16)
        return emit

    split(q_ref, emit_q)
    split(k_ref, emit_to(kp))
    split(v_ref, emit_to(vp))

    def branch3(g, carry):
        phases = [g * BLOCKS_IN_FLIGHT + j for j in range(BLOCKS_IN_FLIGHT)]
        bias = tab_ref[:, TAB3_LO:TAB3_LO + 128]
        outs = _attn_blocks([(jnp.concatenate([q0p[p], q1p[p]], 0), kp[p], vp[p], bias)
                             for p in phases], first_head)
        for p, (o, l) in zip(phases, outs):
            o_scr[0, p] = o
            l_scr[0, p] = l
        return carry
    lax.fori_loop(0, PHASES // BLOCKS_IN_FLIGHT, branch3, 0)

    def branch2(g, carry):
        blocks, where = [], []
        for r in (2 * g, 2 * g + 1):
            gather = lambda src, rr, r=r: [src[4 * kk + r, rr, :] for kk in range(4)]
            for n in range(4):
                rows = pl.ds(32 * n, 32)
                q2 = jnp.concatenate(gather(q0p, rows) + gather(q1p, rows), 0)
                if n == 0:
                    k = jnp.concatenate(gather(kp, rows), 0)
                    v = jnp.concatenate(gather(vp, rows), 0)
                    bias = tab_ref[:, TAB2_LO + 128:TAB2_LO + 256]
                else:
                    prev = pl.ds(32 * (n - 1), 32)
                    k = jnp.concatenate(gather(kp, prev) + gather(kp, rows), 0)
                    v = jnp.concatenate(gather(vp, prev) + gather(vp, rows), 0)
                    bias = tab_ref[:, TAB2_LO:TAB2_LO + 256]
                blocks.append((q2, k, v, bias))
                where.append((r, rows))
        for (r, rows), (o, l) in zip(where, _attn_blocks(blocks, first_head)):
            for kk in range(4):
                o_scr[1, 4 * kk + r, rows, :] = o[32 * kk:32 * kk + 32]
                l_scr[1, 4 * kk + r, rows, :] = l[32 * kk:32 * kk + 32]
        return carry
    lax.fori_loop(0, 2, branch2, 0)

    def branch1_blocks(ns, with_prev):
        blocks = []
        for n in ns:
            rows = pl.ds(pl.multiple_of(n * SUBLANES, SUBLANES), SUBLANES)
            q = jnp.concatenate([qs[p, rows, :] for p in range(PHASES)], 0)
            q2 = jnp.concatenate([jnp.where(first_head, q, 0.0), jnp.where(first_head, 0.0, q)],
                                 0).astype(BF16)
            if with_prev:
                keys = pl.ds(pl.multiple_of((n - 1) * U_ROWS, U_ROWS), 2 * U_ROWS)
                bias = tab_ref[:, TAB1_LO:TAB1_LO + 256]
            else:
                keys = pl.ds(0, U_ROWS)
                bias = tab_ref[:, TAB1_LO + 128:TAB1_LO + 256]
            blocks.append((q2, kn[keys, :], vn[keys, :], bias))
        for n, (o, l) in zip(ns, _attn_blocks(blocks, first_head)):
            rows = pl.ds(pl.multiple_of(n * SUBLANES, SUBLANES), SUBLANES)
            for p in range(PHASES):
                o_scr[2, p, rows, :] = o[SUBLANES * p:SUBLANES * (p + 1)]
                l_scr[2, p, rows, :] = l[SUBLANES * p:SUBLANES * (p + 1)]

    branch1_blocks([0], False)

    def branch1(g, carry):
        branch1_blocks([1 + g * B1_IN_FLIGHT + j for j in range(B1_IN_FLIGHT)], True)
        return carry
    lax.fori_loop(0, (U_ROWS // SUBLANES - 1) // B1_IN_FLIGHT, branch1, 0)

    def phase_rows(p):
        return pl.ds(p, U_ROWS, stride=PHASES)

    def merge(p, carry):
        l3, l2, l1 = l_scr[0, p], l_scr[1, p], l_scr[2, p]
        top = jnp.maximum(jnp.maximum(l3, l2), l1)
        e3, e2, e1 = jnp.exp(l3 - top), jnp.exp(l2 - top), jnp.exp(l1 - top)
        total = e3 + e2 + e1
        o_ref[phase_rows(p), :] = (e3 * o_scr[0, p] + e2 * o_scr[1, p] + e1 * o_scr[2, p]) / total
        return carry
    lax.fori_loop(0, PHASES, merge, 0)


def _attn_prompt(q, k, v):
    bsz, _, seq, _ = k.shape
    assert seq == CACHE_LEN, "prompt attention is laid out for a 2048-token prompt"
    slab = pl.BlockSpec((None, None, seq, LANES), lambda b, c: (b, c, 0, 0))
    tile = (PHASES, U_ROWS, LANES)
    return pl.pallas_call(
        _attn_prompt_body,
        grid=(bsz, HEAD_PAIRS),
        in_specs=[slab, slab, slab,
                  pl.BlockSpec((None, 2 * U_ROWS, TAB_W), lambda b, c: (c, 0, 0))],
        out_specs=slab,
        out_shape=jax.ShapeDtypeStruct((bsz, HEAD_PAIRS, seq, LANES), F32),
        scratch_shapes=[pltpu.VMEM(tile, F32)] + [pltpu.VMEM(tile, BF16)] * 4
                       + [pltpu.VMEM((seq, LANES), BF16)] * 2
                       + [pltpu.VMEM((4, seq // 4, LANES), F32),
                          pltpu.VMEM((3,) + tile, F32), pltpu.VMEM((3,) + tile, F32)],
        compiler_params=pltpu.CompilerParams(dimension_semantics=("parallel", "parallel"),
                                             vmem_limit_bytes=VMEM_LIMIT),
        name="attn_prompt",
    )(q, k, v, _prompt_bias_tables())


DEC_T = 8
QH = DEC_T * N_HEADS
B2_SPAN, B1_SPAN = 512, 128
SAMPLE_TAB_W = CACHE_LEN + B2_SPAN + B1_SPAN


def _sample_bias_tables():
    slopes = _slopes()
    i = np.repeat(np.arange(DEC_T), N_HEADS)[:, None]
    sl = np.tile(slopes, DEC_T)[:, None]

    def cache_bias(span, dil):
        t = CACHE_LEN - span + np.arange(span)[None, :]
        dist = CACHE_LEN + i - t
        valid = (dist % dil == 0) & (dist <= WINDOW_STEPS * dil)
        return np.where(valid, -sl * dist, NEG_INF)

    cache_tab = np.concatenate([cache_bias(CACHE_LEN, 16), cache_bias(B2_SPAN, 4),
                                cache_bias(B1_SPAN, 1)], axis=1)
    c = np.arange(CHUNK)[None, :]
    dn = i - c
    is_new = c < DEC_T
    n1 = np.where(is_new & (dn >= 0), -sl * dn, NEG_INF)
    n2 = np.where(is_new & ((dn == 0) | (dn == 4)), -sl * dn, NEG_INF)
    n3 = np.where(is_new & (dn == 0), 0.0, NEG_INF)
    return jnp.asarray(cache_tab, dtype=F32), jnp.asarray(np.stack([n1, n2, n3]), dtype=F32)


def _head_selector():
    h_row = np.tile(np.arange(N_HEADS), DEC_T)[:, None]
    h_col = (np.arange(D_ATTN) // HEAD_DIM)[None, :]
    return jnp.asarray((h_row == h_col).astype(np.float32))


def _attn_sample_stages(q_ref, kn_ref, vn_ref, kt_ref, vt_ref, tab_ref, tabn_ref, sel_ref, o_ref):
    sel = sel_ref[...]
    q = q_ref[...] * (HEAD_DIM ** -0.5)
    q_rows = jnp.concatenate([jnp.broadcast_to(q[i:i + 1, :], (N_HEADS, D_ATTN))
                              for i in range(DEC_T)], 0)
    qall = (q_rows * sel).astype(BF16)
    pad = jnp.zeros((CHUNK - DEC_T, D_ATTN), F32)
    kn = jnp.concatenate([kn_ref[...], pad], 0).astype(BF16)
    vn = jnp.concatenate([vn_ref[...], pad], 0).astype(BF16)

    s_all = _dot(qall, kt_ref[...].astype(BF16))
    s_new = _dot_nt(qall, kn)
    yield
    lo2, lo1 = CACHE_LEN - B2_SPAN, CACHE_LEN - B1_SPAN

    def branch(s_cache, s_fresh, v_t):
        m = jnp.maximum(jnp.max(s_cache, -1, keepdims=True), jnp.max(s_fresh, -1, keepdims=True))
        e_c, e_f = jnp.exp(s_cache - m), jnp.exp(s_fresh - m)
        den = jnp.sum(e_c, -1, keepdims=True) + jnp.sum(e_f, -1, keepdims=True)
        acc = _dot_nt(e_c.astype(BF16), v_t.astype(BF16)) + _dot(e_f.astype(BF16), vn)
        return m, den, acc

    m3, den3, acc3 = branch(s_all + tab_ref[:, :CACHE_LEN], s_new + tabn_ref[2], vt_ref[...])
    m2, den2, acc2 = branch(s_all[:, lo2:] + tab_ref[:, CACHE_LEN:CACHE_LEN + B2_SPAN],
                            s_new + tabn_ref[1], vt_ref[:, lo2:])
    m1, den1, acc1 = branch(s_all[:, lo1:] + tab_ref[:, CACHE_LEN + B2_SPAN:],
                            s_new + tabn_ref[0], vt_ref[:, lo1:])

    top = jnp.maximum(jnp.maximum(m1, m2), m3)
    sc1, sc2, sc3 = den1 * jnp.exp(m1 - top), den2 * jnp.exp(m2 - top), den3 * jnp.exp(m3 - top)
    total = sc1 + sc2 + sc3
    mixed = ((sc1 / total) * (acc1 / den1) + (sc2 / total) * (acc2 / den2)
             + (sc3 / total) * (acc3 / den3)) * sel
    for i in range(DEC_T):
        row = jnp.sum(mixed[N_HEADS * i:N_HEADS * (i + 1)], axis=0, keepdims=True)
        for c in range(HEAD_PAIRS):
            o_ref[c, i:i + 1, :] = row[:, c * LANES:(c + 1) * LANES]


def _attn_sample_body(*refs):
    for _ in _attn_sample_stages(*refs):
        pass


CACHE_KT_ARG, CACHE_VT_ARG = 3, 4
CACHE_SLOTS = 3


def _attn_sample_call(q, k_new, v_new, cache_kt, cache_vt):
    n = q.shape[0]
    assert cache_kt.shape[1:] == (D_ATTN, CACHE_LEN) and q.shape[1] == DEC_T
    tok = pl.BlockSpec((None, DEC_T, D_ATTN), lambda i: (i, 0, 0))
    cache = pl.BlockSpec((None, D_ATTN, CACHE_LEN), lambda i: (i, 0, 0))
    cache_tab, new_tab = _sample_bias_tables()
    operands = (q, k_new, v_new, cache_kt, cache_vt, cache_tab, new_tab, _head_selector())
    in_specs = [tok, tok, tok, cache, cache, _const_spec((QH, SAMPLE_TAB_W)),
                _const_spec((3, QH, CHUNK)), _const_spec((QH, D_ATTN))]
    out_spec = pl.BlockSpec((None, HEAD_PAIRS, DEC_T, LANES), lambda i: (0, 0, i, 0))
    out_shape = jax.ShapeDtypeStruct((1, HEAD_PAIRS, n * DEC_T, LANES), F32)
    return operands, in_specs, out_spec, out_shape


def _attn_sample(q, k_new, v_new, cache_kt, cache_vt):
    operands, in_specs, out_spec, out_shape = _attn_sample_call(q, k_new, v_new, cache_kt, cache_vt)
    return pl.pallas_call(
        _attn_sample_body,
        grid=(q.shape[0],),
        in_specs=in_specs,
        out_specs=out_spec,
        out_shape=out_shape,
        compiler_params=pltpu.CompilerParams(dimension_semantics=("parallel",),
                                             vmem_limit_bytes=VMEM_LIMIT),
        name="attn_sample",
    )(*operands)


HEADS_PER_GROUP = N_HEADS // SSD_GROUPS
PAD_ROWS = SUBLANES
SHORT_SEQ = SUBLANES
SEQS_PER_TILE = CHUNK // SHORT_SEQ


def _ssd_stages(packed, first_chunk, z_ref, xbc_ref, dt_ref, cp_ref, h0_ref, cw_ref, cb_ref, dtb_ref,
                alog_ref, dsk_ref, nw_ref, y_ref, hn_ref, xpad, aux):
    if packed:
        aux[0:CHUNK, :] = cp_ref[...]
        aux[CHUNK:, :] = jnp.zeros((PAD_ROWS, CONV_DIM), F32)
        xpad[0:PAD_ROWS, :] = jnp.zeros((PAD_ROWS, CONV_DIM), F32)
    else:
        @pl.when(first_chunk())
        def _init():
            xpad[0:PAD_ROWS, :] = cp_ref[...]
            aux[...] = h0_ref[...]
    yield
    xpad[PAD_ROWS:, :] = xbc_ref[...]

    step = lax.broadcasted_iota(jnp.int32, (CHUNK, CONV_DIM), 0) % SHORT_SEQ
    conv = cb_ref[...]
    for back in range(CONV_W):
        tap = CONV_W - 1 - back
        rows_back = xpad[pl.ds(PAD_ROWS - back, CHUNK), :]
        if packed and back:
            rows_back = jnp.where(step < back, aux[pl.ds(PAD_ROWS - back, CHUNK), :], rows_back)
        conv = conv + rows_back * cw_ref[tap:tap + 1, :]
    xc = _silu(conv)
    xs = xc[:, :D_SSD]

    lane = lax.broadcasted_iota(jnp.int32, (CHUNK, DT_PAD), 1)
    dt_raw = dt_ref[...] + dtb_ref[...]
    dt = jnp.maximum(dt_raw, 0.0) + jnp.log1p(jnp.exp(-jnp.abs(dt_raw)))
    dt = jnp.where(lane < N_HEADS, dt, 0.0)
    adt = dt * (-jnp.exp(alog_ref[...]))

    ri = lax.broadcasted_iota(jnp.int32, (CHUNK, CHUNK), 0)
    ci = lax.broadcasted_iota(jnp.int32, (CHUNK, CHUNK), 1)
    causal = ri >= ci
    if packed:
        causal = causal & (ri // SHORT_SEQ == ci // SHORT_SEQ)
    exact_dot = functools.partial(jnp.dot, precision=lax.Precision.HIGHEST,
                                  preferred_element_type=F32)
    cs = exact_dot(jnp.where(causal, 1.0, 0.0).astype(F32), adt)
    cs_t = cs.T
    if packed:
        pick_last = ci == (ri // SHORT_SEQ) * SHORT_SEQ + (SHORT_SEQ - 1)
        cs_end = exact_dot(jnp.where(pick_last, 1.0, 0.0).astype(F32), cs)
    else:
        cs_end = cs[CHUNK - 1:CHUNK, :]

    head_of_lane = lax.broadcasted_iota(jnp.int32, (CHUNK, GROUP_W), 1) // HEAD_DIM

    def per_head(cols):
        out = cols[HEADS_PER_GROUP - 1]
        for hl in range(HEADS_PER_GROUP - 2, -1, -1):
            out = jnp.where(head_of_lane == hl, cols[hl], out)
        return out

    y_groups = []
    for g in range(SSD_GROUPS):
        heads = range(g * HEADS_PER_GROUP, (g + 1) * HEADS_PER_GROUP)
        grp = slice(g * GROUP_W, (g + 1) * GROUP_W)
        b_g = xc[:, D_SSD + g * D_STATE:D_SSD + (g + 1) * D_STATE].astype(BF16)
        c_g = xc[:, D_SSD + (SSD_GROUPS + g) * D_STATE:
                 D_SSD + (SSD_GROUPS + g + 1) * D_STATE].astype(BF16)
        cs_cols = [cs[:, h:h + 1] for h in heads]
        xdt_g = xs[:, grp] * per_head([dt[:, h:h + 1] for h in heads])
        xdt_b = xdt_g.astype(BF16)
        gram = _dot_nt(c_g, b_g)

        y_diag = jnp.zeros((CHUNK, GROUP_W), F32)
        for hl, h in enumerate(heads):
            seg = jnp.where(causal, cs_cols[hl] - cs_t[h:h + 1, :], NEG_INF)
            weights = (gram * jnp.exp(seg)).astype(BF16)
            y_diag = y_diag + jnp.where(head_of_lane == hl, _dot(weights, xdt_b), 0.0)

        to_end = per_head([jnp.exp(cs_end[:, h:h + 1] - cs_cols[hl]) for hl, h in enumerate(heads)])
        decayed_t = (xdt_g * to_end).T
        carried = per_head([jnp.exp(col) for col in cs_cols])
        if packed:
            h_prev = h0_ref[:, grp, :]
            wide = _dot_nt(c_g, h_prev.reshape(SEQS_PER_TILE * GROUP_W, D_STATE).astype(BF16))
            y_off = jnp.concatenate(
                [wide[s * SHORT_SEQ:(s + 1) * SHORT_SEQ, s * GROUP_W:(s + 1) * GROUP_W]
                 for s in range(SEQS_PER_TILE)], 0) * carried
            seq_of_step = lax.broadcasted_iota(jnp.int32, (GROUP_W, CHUNK), 1) // SHORT_SEQ
            per_seq = jnp.concatenate([jnp.where(seq_of_step == s, decayed_t, 0.0)
                                       for s in range(SEQS_PER_TILE)], 0).astype(BF16)
            new_states = _dot(per_seq, b_g).reshape(SEQS_PER_TILE, GROUP_W, D_STATE)
            for s in range(SEQS_PER_TILE):
                row = s * SHORT_SEQ
                keep = jnp.concatenate(
                    [jnp.broadcast_to(jnp.exp(cs_end[row:row + 1, h:h + 1]), (HEAD_DIM, D_STATE))
                     for h in heads], 0)
                hn_ref[s, grp, :] = h_prev[s] * keep + new_states[s]
        else:
            h_prev = aux[grp, :]
            y_off = _dot_nt(c_g, h_prev.astype(BF16)) * carried
            keep = jnp.concatenate(
                [jnp.broadcast_to(jnp.exp(cs_end[:, h:h + 1]), (HEAD_DIM, D_STATE)) for h in heads], 0)
            aux[grp, :] = h_prev * keep + _dot(decayed_t.astype(BF16), b_g)
        y_groups.append(y_diag + y_off)

    y = jnp.concatenate(y_groups, axis=-1) + dsk_ref[...] * xs
    y = y * _silu(z_ref[...])
    normed = []
    for g in range(SSD_GROUPS):
        yg = y[:, g * GROUP_W:(g + 1) * GROUP_W]
        normed.append(yg * lax.rsqrt(jnp.mean(yg * yg, -1, keepdims=True) + LN_EPS))
    y_ref[...] = jnp.concatenate(normed, axis=-1) * nw_ref[...]

    if not packed:
        hn_ref[...] = aux[...]
        xpad[0:PAD_ROWS, :] = xpad[CHUNK:, :]


def _ssd_body(*args):
    for _ in _ssd_stages(*args):
        pass


N_SSD_INPUTS = 11


def _ssd_call(z, xbc, dt, conv_prev, h0, cw, cb, dtb, alog, dsk, nw, where=None):
    n, seq, _ = z.shape
    packed = seq == SHORT_SEQ
    if packed:
        assert n % SEQS_PER_TILE == 0 and where is None
        grid = (n // SEQS_PER_TILE, 1)
        fold = lambda t: t.reshape(grid[0], CHUNK, t.shape[-1])
        z, xbc, dt, conv_prev = fold(z), fold(xbc), fold(dt), fold(conv_prev)
        history_shape, state_shape = (None, CHUNK, CONV_DIM), (SEQS_PER_TILE, D_SSD, D_STATE)
        aux = pltpu.VMEM((CHUNK + PAD_ROWS, CONV_DIM), F32)
    else:
        assert seq % CHUNK == 0
        grid = (n, seq // CHUNK)
        history_shape, state_shape = (None, PAD_ROWS, CONV_DIM), (None, D_SSD, D_STATE)
        aux = pltpu.VMEM((D_SSD, D_STATE), F32)
    where = where or (lambda s, c: (s, c))
    tile = lambda width: pl.BlockSpec((None, CHUNK, width), lambda *g: where(*g) + (0,))
    per_seq = lambda shape: pl.BlockSpec(shape, lambda *g: (where(*g)[0], 0, 0))
    history, state = per_seq(history_shape), per_seq(state_shape)
    operands = (z, xbc, dt, conv_prev, h0, cw, cb, dtb, alog, dsk, nw)
    in_specs = [tile(D_SSD), tile(CONV_DIM), tile(DT_PAD), history, state,
                _const_spec((CONV_W, CONV_DIM)), _const_spec((1, CONV_DIM)),
                _const_spec((1, DT_PAD)), _const_spec((1, DT_PAD)),
                _const_spec((1, D_SSD)), _const_spec((1, D_SSD))]
    out_specs = [tile(D_SSD), state]
    out_shape = [jax.ShapeDtypeStruct(z.shape, F32), jax.ShapeDtypeStruct((n, D_SSD, D_STATE), F32)]
    scratch = [pltpu.VMEM((PAD_ROWS + CHUNK, CONV_DIM), F32), aux]
    return packed, grid, operands, in_specs, out_specs, out_shape, scratch


def _ssd(z, *rest):
    packed, grid, operands, in_specs, out_specs, out_shape, scratch = _ssd_call(z, *rest)
    y, h_new = pl.pallas_call(
        functools.partial(_ssd_body, packed, lambda: pl.program_id(1) == 0),
        grid=grid,
        in_specs=in_specs,
        out_specs=out_specs,
        out_shape=out_shape,
        scratch_shapes=scratch,
        compiler_params=pltpu.CompilerParams(dimension_semantics=("parallel", "arbitrary"),
                                             vmem_limit_bytes=VMEM_LIMIT),
        name="ssd",
    )(*operands)
    return y.reshape(z.shape), h_new


def _ssd_with_sample_attn(ssd_args, attn_args):
    z = ssd_args[0]
    nchunks = z.shape[1] // CHUNK
    steps = z.shape[0] * nchunks
    assert steps == attn_args[0].shape[0]
    where = lambda i: (i // nchunks, i % nchunks)
    _, _, ssd_ops, ssd_in, ssd_out, ssd_shape, scratch = _ssd_call(*ssd_args, where=where)
    attn_ops, attn_in, attn_out, attn_shape = _attn_sample_call(*attn_args)
    caches = (CACHE_KT_ARG, CACHE_VT_ARG)
    for arg in caches:
        attn_in[arg] = pl.BlockSpec(memory_space=pl.ANY)
    ring = pltpu.VMEM((CACHE_SLOTS, D_ATTN, CACHE_LEN), F32)

    def body(*refs):
        ssd_in_refs, refs = refs[:N_SSD_INPUTS], refs[N_SSD_INPUTS:]
        attn_in_refs, refs = list(refs[:len(attn_ops)]), refs[len(attn_ops):]
        y_ref, hn_ref, o_ref, xpad, aux, kt_ring, vt_ring, sems = refs
        step = pl.program_id(0)

        def fetch(s):
            slot = s % CACHE_SLOTS
            return [pltpu.make_async_copy(attn_in_refs[arg].at[s], buf.at[slot], sems.at[j, slot])
                    for j, (arg, buf) in enumerate(zip(caches, (kt_ring, vt_ring)))]

        @pl.when(step == 0)
        def _prime():
            for s in range(CACHE_SLOTS - 1):
                for copy in fetch(s):
                    copy.start()

        @pl.when(step + (CACHE_SLOTS - 1) < steps)
        def _ahead():
            for copy in fetch(step + (CACHE_SLOTS - 1)):
                copy.start()

        for copy in fetch(step):
            copy.wait()
        for arg, buf in zip(caches, (kt_ring, vt_ring)):
            attn_in_refs[arg] = buf.at[step % CACHE_SLOTS]

        ssd = _ssd_stages(False, lambda: step % nchunks == 0, *ssd_in_refs, y_ref, hn_ref, xpad, aux)
        attn = _attn_sample_stages(*attn_in_refs, o_ref)
        for stage in (ssd, attn, ssd, attn):
            next(stage, None)

    y, h_new, attn = pl.pallas_call(
        body,
        grid=(steps,),
        in_specs=ssd_in + attn_in,
        out_specs=ssd_out + [attn_out],
        out_shape=ssd_shape + [attn_shape],
        scratch_shapes=scratch + [ring, ring, pltpu.SemaphoreType.DMA((len(caches), CACHE_SLOTS))],
        compiler_params=pltpu.CompilerParams(dimension_semantics=("arbitrary",),
                                             vmem_limit_bytes=VMEM_LIMIT),
        name="ssd_attn_sample",
    )(*ssd_ops, *attn_ops)
    return y, h_new, attn


def _row(v, width=None):
    v = v.reshape(1, -1).astype(F32)
    if width is not None and v.shape[1] < width:
        v = jnp.pad(v, ((0, 0), (0, width - v.shape[1])))
    return v


def kernel(x_prompt, x_sample, cache_k, cache_v, state_conv, state_ssm, p_prompt, p_sample,
           ln_in_g, ln_in_b, w_in, conv_w, conv_b, dt_bias, a_log, d_skip, ssd_norm_w, w_out,
           ln1_g, ln1_b, w_up, w_down, ln2_g, ln2_b, w_gate, w_ple, ln3_g, ln3_b):
    depth = w_in.shape[0]
    assert depth == 1, "single-layer step"
    alpha = (2 * depth) ** 0.25
    bsz, seq, _ = x_prompt.shape
    nd, dec_t, _ = x_sample.shape
    lyr = 0

    w_proj = jnp.pad(w_in[lyr], ((0, 0), (0, D_PROJ - w_in.shape[2]))).astype(BF16)
    gin, bin_ = _row(ln_in_g), _row(ln_in_b)
    ssd_params = (conv_w[lyr].astype(F32), _row(conv_b[lyr]), _row(dt_bias[lyr], DT_PAD),
                  _row(a_log[lyr], DT_PAD), _row(jnp.repeat(d_skip[lyr], HEAD_DIM)),
                  _row(ssd_norm_w[lyr]))
    post_params = (gin, bin_, w_out[lyr].astype(BF16), _row(ln1_g[lyr]), _row(ln1_b[lyr]),
                   w_up[lyr].astype(BF16), w_down[lyr].astype(BF16), _row(ln2_g[lyr]),
                   _row(ln2_b[lyr]), w_gate[lyr].astype(BF16), w_ple[lyr].astype(BF16),
                   _row(ln3_g[lyr]), _row(ln3_b[lyr]))

    q, k, v, k_t, v_t, z, xbc, dt = _in_proj(x_prompt, gin, bin_, w_proj, head_major=True, tm=512)
    n_tok = nd * dec_t
    flat = lambda t: t.reshape(1, n_tok, t.shape[-1])
    toks = lambda t: t.reshape(nd, dec_t, t.shape[-1])
    qs, ks, vs, zs, xbcs, dts = _in_proj(flat(x_sample), gin, bin_, w_proj, head_major=False, tm=512)

    attn = _attn_prompt(q, k, v)
    transposed = lambda c: jnp.transpose(c, (0, 2, 3, 1)).reshape(nd, D_ATTN, CACHE_LEN)
    ssd_args = (z, xbc, dt, jnp.zeros((bsz, PAD_ROWS, CONV_DIM), F32),
                jnp.zeros((bsz, D_SSD, D_STATE), F32)) + ssd_params
    attn_args = (toks(qs), toks(ks), toks(vs), transposed(cache_k[lyr]), transposed(cache_v[lyr]))
    if bsz * (seq // CHUNK) == nd:
        ssd_y, ssm_p, attn_s = _ssd_with_sample_attn(ssd_args, attn_args)
    else:
        ssd_y, ssm_p = _ssd(*ssd_args)
        attn_s = _attn_sample(*attn_args)
    conv_prev = jnp.pad(state_conv[lyr].astype(F32), ((0, 0), (PAD_ROWS - (CONV_W - 1), 0), (0, 0)))
    ssd_s, ssm_s = _ssd(toks(zs), toks(xbcs), toks(dts), conv_prev,
                        state_ssm[lyr].reshape(nd, D_SSD, D_STATE).astype(F32), *ssd_params)

    y_prompt = _post(x_prompt, attn, ssd_y, p_prompt[lyr], *post_params, alpha=alpha, tm=512)
    y_sample = _post(flat(x_sample), attn_s, flat(ssd_s), flat(p_sample[lyr]), *post_params,
                     alpha=alpha, tm=256)

    from_t = lambda t: jnp.transpose(t.reshape(1, bsz, N_HEADS, HEAD_DIM, seq), (0, 1, 4, 2, 3))
    heads = lambda t: t.reshape(1, nd, dec_t, N_HEADS, HEAD_DIM)
    tail = lambda t: t[None, :, -(CONV_W - 1):, :]
    state = lambda t, n: t.reshape(1, n, N_HEADS, HEAD_DIM, D_STATE)
    return (y_prompt, y_sample.reshape(nd, dec_t, D_MODEL), from_t(k_t), from_t(v_t),
            heads(ks), heads(vs), tail(xbc), tail(toks(xbcs)), state(ssm_p, bsz), state(ssm_s, nd))
```

```python
import functools

import numpy as np
import jax
import jax.numpy as jnp
from jax import lax
from jax.experimental import pallas as pl
from jax.experimental.pallas import tpu as pltpu

F32 = jnp.float32
BF16 = jnp.bfloat16

D_MODEL = 1024
HEAD_DIM = 64
D_ATTN = 512
D_SSD = 512
N_HEADS = 8
SSD_GROUPS = 2
GROUP_W = D_SSD // SSD_GROUPS
D_STATE = 128
CONV_W = 4
CONV_DIM = D_SSD + 2 * SSD_GROUPS * D_STATE
CHUNK = 128
D_FF = 4096
D_PLE = 256
LN_EPS = 1e-5
WINDOW_STEPS = 128
DILATIONS = (1, 4, 16)
PHASES = 16
CACHE_LEN = 2048
DT_PAD = 128
D_PROJ = 3 * D_ATTN + D_SSD + CONV_DIM + DT_PAD
SUBLANES = 8
LANES = 128
VMEM_LIMIT = 52 * 1024 * 1024
NEG_INF = float("-inf")


def _slopes():
    return np.array([2.0 ** (-8.0 * (h + 1) / N_HEADS) for h in range(N_HEADS)], dtype=np.float64)


def _layer_norm(x, g, b):
    mu = jnp.mean(x, -1, keepdims=True)
    xc = x - mu
    var = jnp.mean(xc * xc, -1, keepdims=True)
    return xc * lax.rsqrt(var + LN_EPS) * g + b


def _silu(x):
    return x * (1.0 / (1.0 + jnp.exp(-x)))


def _dot(a, b):
    return jnp.dot(a, b, preferred_element_type=F32)


def _dot_nt(a, b):
    return lax.dot_general(a, b, (((1,), (1,)), ((), ())), preferred_element_type=F32)


def _const_spec(shape):
    nd = len(shape)
    return pl.BlockSpec(shape, lambda *_: (0,) * nd, pipeline_mode=pl.Buffered(1))


HEAD_PAIRS = N_HEADS // 2


def _inproj_body(head_major, x_ref, g_ref, b_ref, w_ref, *out_refs):
    tm = x_ref.shape[0]
    parts = [slice(i * tm // 2, (i + 1) * tm // 2) for i in range(2)]
    h = [_layer_norm(x_ref[r, :], g_ref[...], b_ref[...]).astype(BF16) for r in parts]
    proj = lambda lo, width: [_dot(v, w_ref[:, lo:lo + width]) for v in h]
    if head_major:
        q_ref, k_ref, v_ref, kt_ref, vt_ref, z_ref, xbc_ref, dt_ref = out_refs
        for idx, (ref, t_ref) in enumerate(((q_ref, None), (k_ref, kt_ref), (v_ref, vt_ref))):
            for r, res in zip(parts, proj(idx * D_ATTN, D_ATTN)):
                for c in range(HEAD_PAIRS):
                    ref[c, r, :] = res[:, c * LANES:(c + 1) * LANES]
                if t_ref is not None:
                    t_ref[:, r] = res.T
    else:
        q_ref, k_ref, v_ref, z_ref, xbc_ref, dt_ref = out_refs
        for idx, ref in enumerate((q_ref, k_ref, v_ref)):
            for r, res in zip(parts, proj(idx * D_ATTN, D_ATTN)):
                ref[r, :] = res
    lo = 3 * D_ATTN
    for ref, width in ((z_ref, D_SSD), (xbc_ref, CONV_DIM), (dt_ref, DT_PAD)):
        for r, res in zip(parts, proj(lo, width)):
            ref[r, :] = res
        lo += width


def _in_proj(x, g, b, w, *, head_major, tm):
    bsz, seq, _ = x.shape
    row = lambda width: pl.BlockSpec((None, tm, width), lambda i, j: (i, j, 0))
    row_shape = lambda width: jax.ShapeDtypeStruct((bsz, seq, width), F32)
    if head_major:
        slab = pl.BlockSpec((None, HEAD_PAIRS, tm, LANES), lambda i, j: (i, 0, j, 0))
        slab_shape = jax.ShapeDtypeStruct((bsz, HEAD_PAIRS, seq, LANES), F32)
        tr = pl.BlockSpec((None, D_ATTN, tm), lambda i, j: (i, 0, j))
        tr_shape = jax.ShapeDtypeStruct((bsz, D_ATTN, seq), F32)
        qkv_specs, qkv_shapes = [slab, slab, slab, tr, tr], [slab_shape] * 3 + [tr_shape] * 2
    else:
        qkv_specs, qkv_shapes = [row(D_ATTN)] * 3, [row_shape(D_ATTN)] * 3
    rest = (D_SSD, CONV_DIM, DT_PAD)
    return pl.pallas_call(
        functools.partial(_inproj_body, head_major),
        grid=(bsz, seq // tm),
        in_specs=[row(D_MODEL), _const_spec((1, D_MODEL)), _const_spec((1, D_MODEL)),
                  _const_spec((D_MODEL, D_PROJ))],
        out_specs=qkv_specs + [row(wd) for wd in rest],
        out_shape=qkv_shapes + [row_shape(wd) for wd in rest],
        compiler_params=pltpu.CompilerParams(dimension_semantics=("parallel", "parallel"),
                                             vmem_limit_bytes=VMEM_LIMIT),
        name="in_proj",
    )(x, g, b, w)


FF_CHUNK = 1024
POST_STREAMS = 2


def _post_body(alpha, x_ref, attn_ref, ssd_ref, pe_ref, gin_ref, bin_ref, wout_ref, g1_ref, b1_ref,
               wup_ref, wdown_ref, g2_ref, b2_ref, wgate_ref, wple_ref, g3_ref, b3_ref, y_ref):
    tm = x_ref.shape[0]
    parts = [slice(i * tm // POST_STREAMS, (i + 1) * tm // POST_STREAMS) for i in range(POST_STREAMS)]
    each = lambda fn, *lists: [fn(*args) for args in zip(*lists)]
    xn = [_layer_norm(x_ref[r, :], gin_ref[...], bin_ref[...]) for r in parts]
    mixed = [jnp.concatenate([attn_ref[c, r, :] for c in range(HEAD_PAIRS)] + [ssd_ref[r, :]],
                             axis=-1).astype(BF16) for r in parts]
    proj = [_dot(m, wout_ref[...]) for m in mixed]
    h = each(lambda x, p: _layer_norm(alpha * x + p, g1_ref[...], b1_ref[...]), xn, proj)
    hb = [v.astype(BF16) for v in h]
    u = [None] * POST_STREAMS
    for c in range(D_FF // FF_CHUNK):
        cols = slice(c * FF_CHUNK, (c + 1) * FF_CHUNK)
        a = [jnp.maximum(_dot(v, wup_ref[:, cols]), 0.0) for v in hb]
        part = [_dot((v * v).astype(BF16), wdown_ref[cols, :]) for v in a]
        u = part if c == 0 else each(lambda s, p: s + p, u, part)
    h = each(lambda v, w: _layer_norm(alpha * v + w, g2_ref[...], b2_ref[...]), h, u)
    gate = [1.0 / (1.0 + jnp.exp(-_dot(v.astype(BF16), wgate_ref[...]))) for v in h]
    emb = [_dot(pe_ref[r, :].astype(BF16), wple_ref[...]) for r in parts]
    for r, v, g, e in zip(parts, h, gate, emb):
        y_ref[r, :] = _layer_norm(alpha * v + g * e, g3_ref[...], b3_ref[...])


def _post(x, attn, ssd, pe, gin, bin_, wout, g1, b1, wup, wdown, g2, b2, wgate, wple, g3, b3, *,
          alpha, tm):
    bsz, seq, _ = x.shape
    row = lambda width: pl.BlockSpec((None, tm, width), lambda i, j: (i, j, 0))
    slab = pl.BlockSpec((None, HEAD_PAIRS, tm, LANES), lambda i, j: (i, 0, j, 0))
    vec = _const_spec((1, D_MODEL))
    return pl.pallas_call(
        functools.partial(_post_body, alpha),
        grid=(bsz, seq // tm),
        in_specs=[row(D_MODEL), slab, row(D_SSD), row(D_PLE), vec, vec,
                  _const_spec((D_MODEL, D_MODEL)), vec, vec,
                  _const_spec((D_MODEL, D_FF)), _const_spec((D_FF, D_MODEL)), vec, vec,
                  _const_spec((D_MODEL, D_MODEL)), _const_spec((D_PLE, D_MODEL)), vec, vec],
        out_specs=row(D_MODEL),
        out_shape=jax.ShapeDtypeStruct((bsz, seq, D_MODEL), F32),
        compiler_params=pltpu.CompilerParams(dimension_semantics=("parallel", "parallel"),
                                             vmem_limit_bytes=VMEM_LIMIT),
        name="post",
    )(x, attn, ssd, pe, gin, bin_, wout, g1, b1, wup, wdown, g2, b2, wgate, wple, g3, b3)


U_ROWS = CACHE_LEN // PHASES
TAB3_LO, TAB2_LO, TAB1_LO, TAB_W = 0, 128, 384, 640
BLOCKS_IN_FLIGHT = 8
B1_IN_FLIGHT = 5


def _prompt_bias_tables():
    slopes = _slopes()
    u = np.arange(U_ROWS)
    d3 = (u[:, None] - u[None, :]).astype(np.float64)
    k4, ul4 = np.meshgrid(np.arange(4), np.arange(32), indexing="ij")
    j2 = (4 * ul4 + k4).reshape(-1)
    d2 = j2[:, None] - np.concatenate([j2 - WINDOW_STEPS, j2])[None, :]
    p16, ul16 = np.meshgrid(np.arange(16), np.arange(8), indexing="ij")
    t1 = (16 * ul16 + p16).reshape(-1)
    d1 = t1[:, None] - (np.arange(2 * WINDOW_STEPS) - WINDOW_STEPS)[None, :]
    tabs = []
    for dist, dil in ((d3, 16), (d2, 4), (d1, 1)):
        valid = (dist >= 0) & (dist <= WINDOW_STEPS)
        per_head = [np.where(valid, -slopes[h] * dist * dil, NEG_INF) for h in range(N_HEADS)]
        tabs.append(np.stack(per_head))
    tab = np.concatenate(tabs, axis=-1)
    return jnp.asarray(tab.reshape(HEAD_PAIRS, 2 * U_ROWS, TAB_W), dtype=F32)


def _attn_blocks(blocks, first_head):
    scores = [_dot_nt(q2, k) + bias for q2, k, _, bias in blocks]
    tops = [jnp.max(s, -1, keepdims=True) for s in scores]
    exps = [jnp.exp(s - m) for s, m in zip(scores, tops)]
    dens = [jnp.sum(e, -1, keepdims=True) for e in exps]
    pvs = [_dot(e.astype(BF16), blk[2]) for e, blk in zip(exps, blocks)]
    half = U_ROWS
    outs = []
    for pv, m, den in zip(pvs, tops, dens):
        o = jnp.where(first_head, pv[:half], pv[half:])
        m2 = jnp.where(first_head, m[:half], m[half:])
        den2 = jnp.where(first_head, den[:half], den[half:])
        outs.append((o / den2, m2 + jnp.log(den2)))
    return outs


def _attn_prompt_body(q_ref, k_ref, v_ref, tab_ref, o_ref, qs, q0p, q1p, kp, vp, kn, vn, quarter,
                      o_scr, l_scr):
    lane = lax.broadcasted_iota(jnp.int32, (U_ROWS, LANES), 1)
    first_head = lane < HEAD_DIM
    scale = HEAD_DIM ** -0.5

    def natural(i, carry):
        rows = pl.ds(pl.multiple_of(i * U_ROWS, U_ROWS), U_ROWS)
        kn[rows, :] = k_ref[rows, :].astype(BF16)
        vn[rows, :] = v_ref[rows, :].astype(BF16)
        return carry
    lax.fori_loop(0, PHASES, natural, 0)

    def split(src_ref, emit):
        for r in range(4):
            quarter[r] = src_ref[pl.ds(r, CACHE_LEN // 4, stride=4), :]
        for r in range(4):
            for kk in range(4):
                emit(4 * kk + r, quarter[r, pl.ds(kk, U_ROWS, stride=4), :])

    def emit_q(p, tile):
        tile = tile * scale
        qs[p] = tile
        q0p[p] = jnp.where(first_head, tile, 0.0).astype(BF16)
        q1p[p] = jnp.where(first_head, 0.0, tile).astype(BF16)

    def emit_to(dst):
        def emit(p, tile):
            dst[p] = tile.astype(BF16)
        return emit

    split(q_ref, emit_q)
    split(k_ref, emit_to(kp))
    split(v_ref, emit_to(vp))

    def branch3(g, carry):
        phases = [g * BLOCKS_IN_FLIGHT + j for j in range(BLOCKS_IN_FLIGHT)]
        bias = tab_ref[:, TAB3_LO:TAB3_LO + 128]
        outs = _attn_blocks([(jnp.concatenate([q0p[p], q1p[p]], 0), kp[p], vp[p], bias)
                             for p in phases], first_head)
        for p, (o, l) in zip(phases, outs):
            o_scr[0, p] = o
            l_scr[0, p] = l
        return carry
    lax.fori_loop(0, PHASES // BLOCKS_IN_FLIGHT, branch3, 0)

    def branch2(g, carry):
        blocks, where = [], []
        for r in (2 * g, 2 * g + 1):
            gather = lambda src, rr, r=r: [src[4 * kk + r, rr, :] for kk in range(4)]
            for n in range(4):
                rows = pl.ds(32 * n, 32)
                q2 = jnp.concatenate(gather(q0p, rows) + gather(q1p, rows), 0)
                if n == 0:
                    k = jnp.concatenate(gather(kp, rows), 0)
                    v = jnp.concatenate(gather(vp, rows), 0)
                    bias = tab_ref[:, TAB2_LO + 128:TAB2_LO + 256]
                else:
                    prev = pl.ds(32 * (n - 1), 32)
                    k = jnp.concatenate(gather(kp, prev) + gather(kp, rows), 0)
                    v = jnp.concatenate(gather(vp, prev) + gather(vp, rows), 0)
                    bias = tab_ref[:, TAB2_LO:TAB2_LO + 256]
                blocks.append((q2, k, v, bias))
                where.append((r, rows))
        for (r, rows), (o, l) in zip(where, _attn_blocks(blocks, first_head)):
            for kk in range(4):
                o_scr[1, 4 * kk + r, rows, :] = o[32 * kk:32 * kk + 32]
                l_scr[1, 4 * kk + r, rows, :] = l[32 * kk:32 * kk + 32]
        return carry
    lax.fori_loop(0, 2, branch2, 0)

    def branch1_blocks(ns, with_prev):
        blocks = []
        for n in ns:
            rows = pl.ds(pl.multiple_of(n * SUBLANES, SUBLANES), SUBLANES)
            q = jnp.concatenate([qs[p, rows, :] for p in range(PHASES)], 0)
            q2 = jnp.concatenate([jnp.where(first_head, q, 0.0), jnp.where(first_head, 0.0, q)],
                                 0).astype(BF16)
            if with_prev:
                keys = pl.ds(pl.multiple_of((n - 1) * U_ROWS, U_ROWS), 2 * U_ROWS)
                bias = tab_ref[:, TAB1_LO:TAB1_LO + 256]
            else:
                keys = pl.ds(0, U_ROWS)
                bias = tab_ref[:, TAB1_LO + 128:TAB1_LO + 256]
            blocks.append((q2, kn[keys, :], vn[keys, :], bias))
        for n, (o, l) in zip(ns, _attn_blocks(blocks, first_head)):
            rows = pl.ds(pl.multiple_of(n * SUBLANES, SUBLANES), SUBLANES)
            for p in range(PHASES):
                o_scr[2, p, rows, :] = o[SUBLANES * p:SUBLANES * (p + 1)]
                l_scr[2, p, rows, :] = l[SUBLANES * p:SUBLANES * (p + 1)]

    branch1_blocks([0], False)

    def branch1(g, carry):
        branch1_blocks([1 + g * B1_IN_FLIGHT + j for j in range(B1_IN_FLIGHT)], True)
        return carry
    lax.fori_loop(0, (U_ROWS // SUBLANES - 1) // B1_IN_FLIGHT, branch1, 0)

    def phase_rows(p):
        return pl.ds(p, U_ROWS, stride=PHASES)

    def merge(p, carry):
        l3, l2, l1 = l_scr[0, p], l_scr[1, p], l_scr[2, p]
        top = jnp.maximum(jnp.maximum(l3, l2), l1)
        e3, e2, e1 = jnp.exp(l3 - top), jnp.exp(l2 - top), jnp.exp(l1 - top)
        total = e3 + e2 + e1
        o_ref[phase_rows(p), :] = (e3 * o_scr[0, p] + e2 * o_scr[1, p] + e1 * o_scr[2, p]) / total
        return carry
    lax.fori_loop(0, PHASES, merge, 0)


def _attn_prompt(q, k, v):
    bsz, _, seq, _ = k.shape
    assert seq == CACHE_LEN, "prompt attention is laid out for a 2048-token prompt"
    slab = pl.BlockSpec((None, None, seq, LANES), lambda b, c: (b, c, 0, 0))
    tile = (PHASES, U_ROWS, LANES)
    return pl.pallas_call(
        _attn_prompt_body,
        grid=(bsz, HEAD_PAIRS),
        in_specs=[slab, slab, slab,
                  pl.BlockSpec((None, 2 * U_ROWS, TAB_W), lambda b, c: (c, 0, 0))],
        out_specs=slab,
        out_shape=jax.ShapeDtypeStruct((bsz, HEAD_PAIRS, seq, LANES), F32),
        scratch_shapes=[pltpu.VMEM(tile, F32)] + [pltpu.VMEM(tile, BF16)] * 4
                       + [pltpu.VMEM((seq, LANES), BF16)] * 2
                       + [pltpu.VMEM((4, seq // 4, LANES), F32),
                          pltpu.VMEM((3,) + tile, F32), pltpu.VMEM((3,) + tile, F32)],
        compiler_params=pltpu.CompilerParams(dimension_semantics=("parallel", "parallel"),
                                             vmem_limit_bytes=VMEM_LIMIT),
        name="attn_prompt",
    )(q, k, v, _prompt_bias_tables())


DEC_T = 8
QH = DEC_T * N_HEADS
B2_SPAN, B1_SPAN = 512, 128
SAMPLE_TAB_W = CACHE_LEN + B2_SPAN + B1_SPAN


def _sample_bias_tables():
    slopes = _slopes()
    i = np.repeat(np.arange(DEC_T), N_HEADS)[:, None]
    sl = np.tile(slopes, DEC_T)[:, None]

    def cache_bias(span, dil):
        t = CACHE_LEN - span + np.arange(span)[None, :]
        dist = CACHE_LEN + i - t
        valid = (dist % dil == 0) & (dist <= WINDOW_STEPS * dil)
        return np.where(valid, -sl * dist, NEG_INF)

    cache_tab = np.concatenate([cache_bias(CACHE_LEN, 16), cache_bias(B2_SPAN, 4),
                                cache_bias(B1_SPAN, 1)], axis=1)
    c = np.arange(CHUNK)[None, :]
    dn = i - c
    is_new = c < DEC_T
    n1 = np.where(is_new & (dn >= 0), -sl * dn, NEG_INF)
    n2 = np.where(is_new & ((dn == 0) | (dn == 4)), -sl * dn, NEG_INF)
    n3 = np.where(is_new & (dn == 0), 0.0, NEG_INF)
    return jnp.asarray(cache_tab, dtype=F32), jnp.asarray(np.stack([n1, n2, n3]), dtype=F32)


def _head_selector():
    h_row = np.tile(np.arange(N_HEADS), DEC_T)[:, None]
    h_col = (np.arange(D_ATTN) // HEAD_DIM)[None, :]
    return jnp.asarray((h_row == h_col).astype(np.float32))


def _attn_sample_stages(q_ref, kn_ref, vn_ref, kt_ref, vt_ref, tab_ref, tabn_ref, sel_ref, o_ref):
    sel = sel_ref[...]
    q = q_ref[...] * (HEAD_DIM ** -0.5)
    q_rows = jnp.concatenate([jnp.broadcast_to(q[i:i + 1, :], (N_HEADS, D_ATTN))
                              for i in range(DEC_T)], 0)
    qall = (q_rows * sel).astype(BF16)
    pad = jnp.zeros((CHUNK - DEC_T, D_ATTN), F32)
    kn = jnp.concatenate([kn_ref[...], pad], 0).astype(BF16)
    vn = jnp.concatenate([vn_ref[...], pad], 0).astype(BF16)

    s_all = _dot(qall, kt_ref[...].astype(BF16))
    s_new = _dot_nt(qall, kn)
    yield
    lo2, lo1 = CACHE_LEN - B2_SPAN, CACHE_LEN - B1_SPAN

    def branch(s_cache, s_fresh, v_t):
        m = jnp.maximum(jnp.max(s_cache, -1, keepdims=True), jnp.max(s_fresh, -1, keepdims=True))
        e_c, e_f = jnp.exp(s_cache - m), jnp.exp(s_fresh - m)
        den = jnp.sum(e_c, -1, keepdims=True) + jnp.sum(e_f, -1, keepdims=True)
        acc = _dot_nt(e_c.astype(BF16), v_t.astype(BF16)) + _dot(e_f.astype(BF16), vn)
        return m, den, acc

    m3, den3, acc3 = branch(s_all + tab_ref[:, :CACHE_LEN], s_new + tabn_ref[2], vt_ref[...])
    m2, den2, acc2 = branch(s_all[:, lo2:] + tab_ref[:, CACHE_LEN:CACHE_LEN + B2_SPAN],
                            s_new + tabn_ref[1], vt_ref[:, lo2:])
    m1, den1, acc1 = branch(s_all[:, lo1:] + tab_ref[:, CACHE_LEN + B2_SPAN:],
                            s_new + tabn_ref[0], vt_ref[:, lo1:])

    top = jnp.maximum(jnp.maximum(m1, m2), m3)
    sc1, sc2, sc3 = den1 * jnp.exp(m1 - top), den2 * jnp.exp(m2 - top), den3 * jnp.exp(m3 - top)
    total = sc1 + sc2 + sc3
    mixed = ((sc1 / total) * (acc1 / den1) + (sc2 / total) * (acc2 / den2)
             + (sc3 / total) * (acc3 / den3)) * sel
    for i in range(DEC_T):
        row = jnp.sum(mixed[N_HEADS * i:N_HEADS * (i + 1)], axis=0, keepdims=True)
        for c in range(HEAD_PAIRS):
            o_ref[c, i:i + 1, :] = row[:, c * LANES:(c + 1) * LANES]


def _attn_sample_body(*refs):
    for _ in _attn_sample_stages(*refs):
        pass


CACHE_KT_ARG, CACHE_VT_ARG = 3, 4
CACHE_SLOTS = 3


def _attn_sample_call(q, k_new, v_new, cache_kt, cache_vt):
    n = q.shape[0]
    assert cache_kt.shape[1:] == (D_ATTN, CACHE_LEN) and q.shape[1] == DEC_T
    tok = pl.BlockSpec((None, DEC_T, D_ATTN), lambda i: (i, 0, 0))
    cache = pl.BlockSpec((None, D_ATTN, CACHE_LEN), lambda i: (i, 0, 0))
    cache_tab, new_tab = _sample_bias_tables()
    operands = (q, k_new, v_new, cache_kt, cache_vt, cache_tab, new_tab, _head_selector())
    in_specs = [tok, tok, tok, cache, cache, _const_spec((QH, SAMPLE_TAB_W)),
                _const_spec((3, QH, CHUNK)), _const_spec((QH, D_ATTN))]
    out_spec = pl.BlockSpec((None, HEAD_PAIRS, DEC_T, LANES), lambda i: (0, 0, i, 0))
    out_shape = jax.ShapeDtypeStruct((1, HEAD_PAIRS, n * DEC_T, LANES), F32)
    return operands, in_specs, out_spec, out_shape


def _attn_sample(q, k_new, v_new, cache_kt, cache_vt):
    operands, in_specs, out_spec, out_shape = _attn_sample_call(q, k_new, v_new, cache_kt, cache_vt)
    return pl.pallas_call(
        _attn_sample_body,
        grid=(q.shape[0],),
        in_specs=in_specs,
        out_specs=out_spec,
        out_shape=out_shape,
        compiler_params=pltpu.CompilerParams(dimension_semantics=("parallel",),
                                             vmem_limit_bytes=VMEM_LIMIT),
        name="attn_sample",
    )(*operands)


HEADS_PER_GROUP = N_HEADS // SSD_GROUPS
PAD_ROWS = SUBLANES
SHORT_SEQ = SUBLANES
SEQS_PER_TILE = CHUNK // SHORT_SEQ


def _ssd_stages(packed, first_chunk, z_ref, xbc_ref, dt_ref, cp_ref, h0_ref, cw_ref, cb_ref, dtb_ref,
                alog_ref, dsk_ref, nw_ref, y_ref, hn_ref, xpad, aux):
    if packed:
        aux[0:CHUNK, :] = cp_ref[...]
        aux[CHUNK:, :] = jnp.zeros((PAD_ROWS, CONV_DIM), F32)
        xpad[0:PAD_ROWS, :] = jnp.zeros((PAD_ROWS, CONV_DIM), F32)
    else:
        @pl.when(first_chunk())
        def _init():
            xpad[0:PAD_ROWS, :] = cp_ref[...]
            aux[...] = h0_ref[...]
    yield
    xpad[PAD_ROWS:, :] = xbc_ref[...]

    step = lax.broadcasted_iota(jnp.int32, (CHUNK, CONV_DIM), 0) % SHORT_SEQ
    conv = cb_ref[...]
    for back in range(CONV_W):
        tap = CONV_W - 1 - back
        rows_back = xpad[pl.ds(PAD_ROWS - back, CHUNK), :]
        if packed and back:
            rows_back = jnp.where(step < back, aux[pl.ds(PAD_ROWS - back, CHUNK), :], rows_back)
        conv = conv + rows_back * cw_ref[tap:tap + 1, :]
    xc = _silu(conv)
    xs = xc[:, :D_SSD]

    lane = lax.broadcasted_iota(jnp.int32, (CHUNK, DT_PAD), 1)
    dt_raw = dt_ref[...] + dtb_ref[...]
    dt = jnp.maximum(dt_raw, 0.0) + jnp.log1p(jnp.exp(-jnp.abs(dt_raw)))
    dt = jnp.where(lane < N_HEADS, dt, 0.0)
    adt = dt * (-jnp.exp(alog_ref[...]))

    ri = lax.broadcasted_iota(jnp.int32, (CHUNK, CHUNK), 0)
    ci = lax.broadcasted_iota(jnp.int32, (CHUNK, CHUNK), 1)
    causal = ri >= ci
    if packed:
        causal = causal & (ri // SHORT_SEQ == ci // SHORT_SEQ)
    exact_dot = functools.partial(jnp.dot, precision=lax.Precision.HIGHEST,
                                  preferred_element_type=F32)
    cs = exact_dot(jnp.where(causal, 1.0, 0.0).astype(F32), adt)
    cs_t = cs.T
    if packed:
        pick_last = ci == (ri // SHORT_SEQ) * SHORT_SEQ + (SHORT_SEQ - 1)
        cs_end = exact_dot(jnp.where(pick_last, 1.0, 0.0).astype(F32), cs)
    else:
        cs_end = cs[CHUNK - 1:CHUNK, :]

    head_of_lane = lax.broadcasted_iota(jnp.int32, (CHUNK, GROUP_W), 1) // HEAD_DIM

    def per_head(cols):
        out = cols[HEADS_PER_GROUP - 1]
        for hl in range(HEADS_PER_GROUP - 2, -1, -1):
            out = jnp.where(head_of_lane == hl, cols[hl], out)
        return out

    y_groups = []
    for g in range(SSD_GROUPS):
        heads = range(g * HEADS_PER_GROUP, (g + 1) * HEADS_PER_GROUP)
        grp = slice(g * GROUP_W, (g + 1) * GROUP_W)
        b_g = xc[:, D_SSD + g * D_STATE:D_SSD + (g + 1) * D_STATE].astype(BF16)
        c_g = xc[:, D_SSD + (SSD_GROUPS + g) * D_STATE:
                 D_SSD + (SSD_GROUPS + g + 1) * D_STATE].astype(BF16)
        cs_cols = [cs[:, h:h + 1] for h in heads]
        xdt_g = xs[:, grp] * per_head([dt[:, h:h + 1] for h in heads])
        xdt_b = xdt_g.astype(BF16)
        gram = _dot_nt(c_g, b_g)

        y_diag = jnp.zeros((CHUNK, GROUP_W), F32)
        for hl, h in enumerate(heads):
            seg = jnp.where(causal, cs_cols[hl] - cs_t[h:h + 1, :], NEG_INF)
            weights = (gram * jnp.exp(seg)).astype(BF16)
            y_diag = y_diag + jnp.where(head_of_lane == hl, _dot(weights, xdt_b), 0.0)

        to_end = per_head([jnp.exp(cs_end[:, h:h + 1] - cs_cols[hl]) for hl, h in enumerate(heads)])
        decayed_t = (xdt_g * to_end).T
        carried = per_head([jnp.exp(col) for col in cs_cols])
        if packed:
            h_prev = h0_ref[:, grp, :]
            wide = _dot_nt(c_g, h_prev.reshape(SEQS_PER_TILE * GROUP_W, D_STATE).astype(BF16))
            y_off = jnp.concatenate(
                [wide[s * SHORT_SEQ:(s + 1) * SHORT_SEQ, s * GROUP_W:(s + 1) * GROUP_W]
                 for s in range(SEQS_PER_TILE)], 0) * carried
            seq_of_step = lax.broadcasted_iota(jnp.int32, (GROUP_W, CHUNK), 1) // SHORT_SEQ
            per_seq = jnp.concatenate([jnp.where(seq_of_step == s, decayed_t, 0.0)
                                       for s in range(SEQS_PER_TILE)], 0).astype(BF16)
            new_states = _dot(per_seq, b_g).reshape(SEQS_PER_TILE, GROUP_W, D_STATE)
            for s in range(SEQS_PER_TILE):
                row = s * SHORT_SEQ
                keep = jnp.concatenate(
                    [jnp.broadcast_to(jnp.exp(cs_end[row:row + 1, h:h + 1]), (HEAD_DIM, D_STATE))
                     for h in heads], 0)
                hn_ref[s, grp, :] = h_prev[s] * keep + new_states[s]
        else:
            h_prev = aux[grp, :]
            y_off = _dot_nt(c_g, h_prev.astype(BF16)) * carried
            keep = jnp.concatenate(
                [jnp.broadcast_to(jnp.exp(cs_end[:, h:h + 1]), (HEAD_DIM, D_STATE)) for h in heads], 0)
            aux[grp, :] = h_prev * keep + _dot(decayed_t.astype(BF16), b_g)
        y_groups.append(y_diag + y_off)

    y = jnp.concatenate(y_groups, axis=-1) + dsk_ref[...] * xs
    y = y * _silu(z_ref[...])
    normed = []
    for g in range(SSD_GROUPS):
        yg = y[:, g * GROUP_W:(g + 1) * GROUP_W]
        normed.append(yg * lax.rsqrt(jnp.mean(yg * yg, -1, keepdims=True) + LN_EPS))
    y_ref[...] = jnp.concatenate(normed, axis=-1) * nw_ref[...]

    if not packed:
        hn_ref[...] = aux[...]
        xpad[0:PAD_ROWS, :] = xpad[CHUNK:, :]


def _ssd_body(*args):
    for _ in _ssd_stages(*args):
        pass


N_SSD_INPUTS = 11


def _ssd_call(z, xbc, dt, conv_prev, h0, cw, cb, dtb, alog, dsk, nw, where=None):
    n, seq, _ = z.shape
    packed = seq == SHORT_SEQ
    if packed:
        assert n % SEQS_PER_TILE == 0 and where is None
        grid = (n // SEQS_PER_TILE, 1)
        fold = lambda t: t.reshape(grid[0], CHUNK, t.shape[-1])
        z, xbc, dt, conv_prev = fold(z), fold(xbc), fold(dt), fold(conv_prev)
        history_shape, state_shape = (None, CHUNK, CONV_DIM), (SEQS_PER_TILE, D_SSD, D_STATE)
        aux = pltpu.VMEM((CHUNK + PAD_ROWS, CONV_DIM), F32)
    else:
        assert seq % CHUNK == 0
        grid = (n, seq // CHUNK)
        history_shape, state_shape = (None, PAD_ROWS, CONV_DIM), (None, D_SSD, D_STATE)
        aux = pltpu.VMEM((D_SSD, D_STATE), F32)
    where = where or (lambda s, c: (s, c))
    tile = lambda width: pl.BlockSpec((None, CHUNK, width), lambda *g: where(*g) + (0,))
    per_seq = lambda shape: pl.BlockSpec(shape, lambda *g: (where(*g)[0], 0, 0))
    history, state = per_seq(history_shape), per_seq(state_shape)
    operands = (z, xbc, dt, conv_prev, h0, cw, cb, dtb, alog, dsk, nw)
    in_specs = [tile(D_SSD), tile(CONV_DIM), tile(DT_PAD), history, state,
                _const_spec((CONV_W, CONV_DIM)), _const_spec((1, CONV_DIM)),
                _const_spec((1, DT_PAD)), _const_spec((1, DT_PAD)),
                _const_spec((1, D_SSD)), _const_spec((1, D_SSD))]
    out_specs = [tile(D_SSD), state]
    out_shape = [jax.ShapeDtypeStruct(z.shape, F32), jax.ShapeDtypeStruct((n, D_SSD, D_STATE), F32)]
    scratch = [pltpu.VMEM((PAD_ROWS + CHUNK, CONV_DIM), F32), aux]
    return packed, grid, operands, in_specs, out_specs, out_shape, scratch


def _ssd(z, *rest):
    packed, grid, operands, in_specs, out_specs, out_shape, scratch = _ssd_call(z, *rest)
    y, h_new = pl.pallas_call(
        functools.partial(_ssd_body, packed, lambda: pl.program_id(1) == 0),
        grid=grid,
        in_specs=in_specs,
        out_specs=out_specs,
        out_shape=out_shape,
        scratch_shapes=scratch,
        compiler_params=pltpu.CompilerParams(dimension_semantics=("parallel", "arbitrary"),
                                             vmem_limit_bytes=VMEM_LIMIT),
        name="ssd",
    )(*operands)
    return y.reshape(z.shape), h_new


def _ssd_with_sample_attn(ssd_args, attn_args):
    z = ssd_args[0]
    nchunks = z.shape[1] // CHUNK
    steps = z.shape[0] * nchunks
    assert steps == attn_args[0].shape[0]
    where = lambda i: (i // nchunks, i % nchunks)
    _, _, ssd_ops, ssd_in, ssd_out, ssd_shape, scratch = _ssd_call(*ssd_args, where=where)
    attn_ops, attn_in, attn_out, attn_shape = _attn_sample_call(*attn_args)
    caches = (CACHE_KT_ARG, CACHE_VT_ARG)
    for arg in caches:
        attn_in[arg] = pl.BlockSpec(memory_space=pl.ANY)
    ring = pltpu.VMEM((CACHE_SLOTS, D_ATTN, CACHE_LEN), F32)

    def body(*refs):
        ssd_in_refs, refs = refs[:N_SSD_INPUTS], refs[N_SSD_INPUTS:]
        attn_in_refs, refs = list(refs[:len(attn_ops)]), refs[len(attn_ops):]
        y_ref, hn_ref, o_ref, xpad, aux, kt_ring, vt_ring, sems = refs
        step = pl.program_id(0)

        def fetch(s):
            slot = s % CACHE_SLOTS
            return [pltpu.make_async_copy(attn_in_refs[arg].at[s], buf.at[slot], sems.at[j, slot])
                    for j, (arg, buf) in enumerate(zip(caches, (kt_ring, vt_ring)))]

        @pl.when(step == 0)
        def _prime():
            for s in range(CACHE_SLOTS - 1):
                for copy in fetch(s):
                    copy.start()

        @pl.when(step + (CACHE_SLOTS - 1) < steps)
        def _ahead():
            for copy in fetch(step + (CACHE_SLOTS - 1)):
                copy.start()

        for copy in fetch(step):
            copy.wait()
        for arg, buf in zip(caches, (kt_ring, vt_ring)):
            attn_in_refs[arg] = buf.at[step % CACHE_SLOTS]

        ssd = _ssd_stages(False, lambda: step % nchunks == 0, *ssd_in_refs, y_ref, hn_ref, xpad, aux)
        attn = _attn_sample_stages(*attn_in_refs, o_ref)
        for stage in (ssd, attn, ssd, attn):
            next(stage, None)

    y, h_new, attn = pl.pallas_call(
        body,
        grid=(steps,),
        in_specs=ssd_in + attn_in,
        out_specs=ssd_out + [attn_out],
        out_shape=ssd_shape + [attn_shape],
        scratch_shapes=scratch + [ring, ring, pltpu.SemaphoreType.DMA((len(caches), CACHE_SLOTS))],
        compiler_params=pltpu.CompilerParams(dimension_semantics=("arbitrary",),
                                             vmem_limit_bytes=VMEM_LIMIT),
        name="ssd_attn_sample",
    )(*ssd_ops, *attn_ops)
    return y, h_new, attn


def _row(v, width=None):
    v = v.reshape(1, -1).astype(F32)
    if width is not None and v.shape[1] < width:
        v = jnp.pad(v, ((0, 0), (0, width - v.shape[1])))
    return v


def kernel(x_prompt, x_sample, cache_k, cache_v, state_conv, state_ssm, p_prompt, p_sample,
           ln_in_g, ln_in_b, w_in, conv_w, conv_b, dt_bias, a_log, d_skip, ssd_norm_w, w_out,
           ln1_g, ln1_b, w_up, w_down, ln2_g, ln2_b, w_gate, w_ple, ln3_g, ln3_b):
    depth = w_in.shape[0]
    assert depth == 1, "single-layer step"
    alpha = (2 * depth) ** 0.25
    bsz, seq, _ = x_prompt.shape
    nd, dec_t, _ = x_sample.shape
    lyr = 0

    w_proj = jnp.pad(w_in[lyr], ((0, 0), (0, D_PROJ - w_in.shape[2]))).astype(BF16)
    gin, bin_ = _row(ln_in_g), _row(ln_in_b)
    ssd_params = (conv_w[lyr].astype(F32), _row(conv_b[lyr]), _row(dt_bias[lyr], DT_PAD),
                  _row(a_log[lyr], DT_PAD), _row(jnp.repeat(d_skip[lyr], HEAD_DIM)),
                  _row(ssd_norm_w[lyr]))
    post_params = (gin, bin_, w_out[lyr].astype(BF16), _row(ln1_g[lyr]), _row(ln1_b[lyr]),
                   w_up[lyr].astype(BF16), w_down[lyr].astype(BF16), _row(ln2_g[lyr]),
                   _row(ln2_b[lyr]), w_gate[lyr].astype(BF16), w_ple[lyr].astype(BF16),
                   _row(ln3_g[lyr]), _row(ln3_b[lyr]))

    q, k, v, k_t, v_t, z, xbc, dt = _in_proj(x_prompt, gin, bin_, w_proj, head_major=True, tm=512)
    n_tok = nd * dec_t
    flat = lambda t: t.reshape(1, n_tok, t.shape[-1])
    toks = lambda t: t.reshape(nd, dec_t, t.shape[-1])
    qs, ks, vs, zs, xbcs, dts = _in_proj(flat(x_sample), gin, bin_, w_proj, head_major=False, tm=512)

    attn = _attn_prompt(q, k, v)
    transposed = lambda c: jnp.transpose(c, (0, 2, 3, 1)).reshape(nd, D_ATTN, CACHE_LEN)
    ssd_args = (z, xbc, dt, jnp.zeros((bsz, PAD_ROWS, CONV_DIM), F32),
                jnp.zeros((bsz, D_SSD, D_STATE), F32)) + ssd_params
    attn_args = (toks(qs), toks(ks), toks(vs), transposed(cache_k[lyr]), transposed(cache_v[lyr]))
    if bsz * (seq // CHUNK) == nd:
        ssd_y, ssm_p, attn_s = _ssd_with_sample_attn(ssd_args, attn_args)
    else:
        ssd_y, ssm_p = _ssd(*ssd_args)
        attn_s = _attn_sample(*attn_args)
    conv_prev = jnp.pad(state_conv[lyr].astype(F32), ((0, 0), (PAD_ROWS - (CONV_W - 1), 0), (0, 0)))
    ssd_s, ssm_s = _ssd(toks(zs), toks(xbcs), toks(dts), conv_prev,
                        state_ssm[lyr].reshape(nd, D_SSD, D_STATE).astype(F32), *ssd_params)

    y_prompt = _post(x_prompt, attn, ssd_y, p_prompt[lyr], *post_params, alpha=alpha, tm=512)
    y_sample = _post(flat(x_sample), attn_s, flat(ssd_s), flat(p_sample[lyr]), *post_params,
                     alpha=alpha, tm=256)

    from_t = lambda t: jnp.transpose(t.reshape(1, bsz, N_HEADS, HEAD_DIM, seq), (0, 1, 4, 2, 3))
    heads = lambda t: t.reshape(1, nd, dec_t, N_HEADS, HEAD_DIM)
    tail = lambda t: t[None, :, -(CONV_W - 1):, :]
    state = lambda t, n: t.reshape(1, n, N_HEADS, HEAD_DIM, D_STATE)
    return (y_prompt, y_sample.reshape(nd, dec_t, D_MODEL), from_t(k_t), from_t(v_t),
            heads(ks), heads(vs), tail(xbc), tail(toks(xbcs)), state(ssm_p, bsz), state(ssm_s, nd))
```

```python
import functools

import numpy as np
import jax
import jax.numpy as jnp
from jax import lax
from jax.experimental import pallas as pl
from jax.experimental.pallas import tpu as pltpu

F32 = jnp.float32
BF16 = jnp.bfloat16

D_MODEL = 1024
HEAD_DIM = 64
D_ATTN = 512
D_SSD = 512
N_HEADS = 8
SSD_GROUPS = 2
GROUP_W = D_SSD // SSD_GROUPS
D_STATE = 128
CONV_W = 4
CONV_DIM = D_SSD + 2 * SSD_GROUPS * D_STATE
CHUNK = 128
D_FF = 4096
D_PLE = 256
LN_EPS = 1e-5
WINDOW_STEPS = 128
DILATIONS = (1, 4, 16)
PHASES = 16
CACHE_LEN = 2048
DT_PAD = 128
D_PROJ = 3 * D_ATTN + D_SSD + CONV_DIM + DT_PAD
SUBLANES = 8
LANES = 128
VMEM_LIMIT = 52 * 1024 * 1024
NEG_INF = float("-inf")


def _slopes():
    return np.array([2.0 ** (-8.0 * (h + 1) / N_HEADS) for h in range(N_HEADS)], dtype=np.float64)


def _layer_norm(x, g, b):
    mu = jnp.mean(x, -1, keepdims=True)
    xc = x - mu
    var = jnp.mean(xc * xc, -1, keepdims=True)
    return xc * lax.rsqrt(var + LN_EPS) * g + b


def _silu(x):
    return x * (1.0 / (1.0 + jnp.exp(-x)))


def _dot(a, b):
    return jnp.dot(a, b, preferred_element_type=F32)


def _dot_nt(a, b):
    return lax.dot_general(a, b, (((1,), (1,)), ((), ())), preferred_element_type=F32)


def _const_spec(shape):
    nd = len(shape)
    return pl.BlockSpec(shape, lambda *_: (0,) * nd, pipeline_mode=pl.Buffered(1))


HEAD_PAIRS = N_HEADS // 2


def _inproj_body(head_major, x_ref, g_ref, b_ref, w_ref, *out_refs):
    tm = x_ref.shape[0]
    parts = [slice(i * tm // 2, (i + 1) * tm // 2) for i in range(2)]
    h = [_layer_norm(x_ref[r, :], g_ref[...], b_ref[...]).astype(BF16) for r in parts]
    proj = lambda lo, width: [_dot(v, w_ref[:, lo:lo + width]) for v in h]
    if head_major:
        q_ref, k_ref, v_ref, kt_ref, vt_ref, z_ref, xbc_ref, dt_ref = out_refs
        for idx, (ref, t_ref) in enumerate(((q_ref, None), (k_ref, kt_ref), (v_ref, vt_ref))):
            for r, res in zip(parts, proj(idx * D_ATTN, D_ATTN)):
                for c in range(HEAD_PAIRS):
                    ref[c, r, :] = res[:, c * LANES:(c + 1) * LANES]
                if t_ref is not None:
                    t_ref[:, r] = res.T
    else:
        q_ref, k_ref, v_ref, z_ref, xbc_ref, dt_ref = out_refs
        for idx, ref in enumerate((q_ref, k_ref, v_ref)):
            for r, res in zip(parts, proj(idx * D_ATTN, D_ATTN)):
                ref[r, :] = res
    lo = 3 * D_ATTN
    for ref, width in ((z_ref, D_SSD), (xbc_ref, CONV_DIM), (dt_ref, DT_PAD)):
        for r, res in zip(parts, proj(lo, width)):
            ref[r, :] = res
        lo += width


def _in_proj(x, g, b, w, *, head_major, tm):
    bsz, seq, _ = x.shape
    row = lambda width: pl.BlockSpec((None, tm, width), lambda i, j: (i, j, 0))
    row_shape = lambda width: jax.ShapeDtypeStruct((bsz, seq, width), F32)
    if head_major:
        slab = pl.BlockSpec((None, HEAD_PAIRS, tm, LANES), lambda i, j: (i, 0, j, 0))
        slab_shape = jax.ShapeDtypeStruct((bsz, HEAD_PAIRS, seq, LANES), F32)
        tr = pl.BlockSpec((None, D_ATTN, tm), lambda i, j: (i, 0, j))
        tr_shape = jax.ShapeDtypeStruct((bsz, D_ATTN, seq), F32)
        qkv_specs, qkv_shapes = [slab, slab, slab, tr, tr], [slab_shape] * 3 + [tr_shape] * 2
    else:
        qkv_specs, qkv_shapes = [row(D_ATTN)] * 3, [row_shape(D_ATTN)] * 3
    rest = (D_SSD, CONV_DIM, DT_PAD)
    return pl.pallas_call(
        functools.partial(_inproj_body, head_major),
        grid=(bsz, seq // tm),
        in_specs=[row(D_MODEL), _const_spec((1, D_MODEL)), _const_spec((1, D_MODEL)),
                  _const_spec((D_MODEL, D_PROJ))],
        out_specs=qkv_specs + [row(wd) for wd in rest],
        out_shape=qkv_shapes + [row_shape(wd) for wd in rest],
        compiler_params=pltpu.CompilerParams(dimension_semantics=("parallel", "parallel"),
                                             vmem_limit_bytes=VMEM_LIMIT),
        name="in_proj",
    )(x, g, b, w)


FF_CHUNK = 1024
POST_STREAMS = 2


def _post_body(alpha, x_ref, attn_ref, ssd_ref, pe_ref, gin_ref, bin_ref, wout_ref, g1_ref, b1_ref,
               wup_ref, wdown_ref, g2_ref, b2_ref, wgate_ref, wple_ref, g3_ref, b3_ref, y_ref):
    tm = x_ref.shape[0]
    parts = [slice(i * tm // POST_STREAMS, (i + 1) * tm // POST_STREAMS) for i in range(POST_STREAMS)]
    each = lambda fn, *lists: [fn(*args) for args in zip(*lists)]
    xn = [_layer_norm(x_ref[r, :], gin_ref[...], bin_ref[...]) for r in parts]
    mixed = [jnp.concatenate([attn_ref[c, r, :] for c in range(HEAD_PAIRS)] + [ssd_ref[r, :]],
                             axis=-1).astype(BF16) for r in parts]
    proj = [_dot(m, wout_ref[...]) for m in mixed]
    h = each(lambda x, p: _layer_norm(alpha * x + p, g1_ref[...], b1_ref[...]), xn, proj)
    hb = [v.astype(BF16) for v in h]
    u = [None] * POST_STREAMS
    for c in range(D_FF // FF_CHUNK):
        cols = slice(c * FF_CHUNK, (c + 1) * FF_CHUNK)
        a = [jnp.maximum(_dot(v, wup_ref[:, cols]), 0.0) for v in hb]
        part = [_dot((v * v).astype(BF16), wdown_ref[cols, :]) for v in a]
        u = part if c == 0 else each(lambda s, p: s + p, u, part)
    h = each(lambda v, w: _layer_norm(alpha * v + w, g2_ref[...], b2_ref[...]), h, u)
    gate = [1.0 / (1.0 + jnp.exp(-_dot(v.astype(BF16), wgate_ref[...]))) for v in h]
    emb = [_dot(pe_ref[r, :].astype(BF16), wple_ref[...]) for r in parts]
    for r, v, g, e in zip(parts, h, gate, emb):
        y_ref[r, :] = _layer_norm(alpha * v + g * e, g3_ref[...], b3_ref[...])


def _post(x, attn, ssd, pe, gin, bin_, wout, g1, b1, wup, wdown, g2, b2, wgate, wple, g3, b3, *,
          alpha, tm):
    bsz, seq, _ = x.shape
    row = lambda width: pl.BlockSpec((None, tm, width), lambda i, j: (i, j, 0))
    slab = pl.BlockSpec((None, HEAD_PAIRS, tm, LANES), lambda i, j: (i, 0, j, 0))
    vec = _const_spec((1, D_MODEL))
    return pl.pallas_call(
        functools.partial(_post_body, alpha),
        grid=(bsz, seq // tm),
        in_specs=[row(D_MODEL), slab, row(D_SSD), row(D_PLE), vec, vec,
                  _const_spec((D_MODEL, D_MODEL)), vec, vec,
                  _const_spec((D_MODEL, D_FF)), _const_spec((D_FF, D_MODEL)), vec, vec,
                  _const_spec((D_MODEL, D_MODEL)), _const_spec((D_PLE, D_MODEL)), vec, vec],
        out_specs=row(D_MODEL),
        out_shape=jax.ShapeDtypeStruct((bsz, seq, D_MODEL), F32),
        compiler_params=pltpu.CompilerParams(dimension_semantics=("parallel", "parallel"),
                                             vmem_limit_bytes=VMEM_LIMIT),
        name="post",
    )(x, attn, ssd, pe, gin, bin_, wout, g1, b1, wup, wdown, g2, b2, wgate, wple, g3, b3)


U_ROWS = CACHE_LEN // PHASES
TAB3_LO, TAB2_LO, TAB1_LO, TAB_W = 0, 128, 384, 640
BLOCKS_IN_FLIGHT = 16
WAVE = 4
B2_SUBSEQS = 4
B1_IN_FLIGHT = 8


def _prompt_bias_tables():
    slopes = _slopes()
    u = np.arange(U_ROWS)
    d3 = (u[:, None] - u[None, :]).astype(np.float64)
    k4, ul4 = np.meshgrid(np.arange(4), np.arange(32), indexing="ij")
    j2 = (4 * ul4 + k4).reshape(-1)
    d2 = j2[:, None] - np.concatenate([j2 - WINDOW_STEPS, j2])[None, :]
    p16, ul16 = np.meshgrid(np.arange(16), np.arange(8), indexing="ij")
    t1 = (16 * ul16 + p16).reshape(-1)
    d1 = t1[:, None] - (np.arange(2 * WINDOW_STEPS) - WINDOW_STEPS)[None, :]
    tabs = []
    for dist, dil in ((d3, 16), (d2, 4), (d1, 1)):
        valid = (dist >= 0) & (dist <= WINDOW_STEPS)
        per_head = [np.where(valid, -slopes[h] * dist * dil, NEG_INF) for h in range(N_HEADS)]
        tabs.append(np.stack(per_head))
    tab = np.concatenate(tabs, axis=-1)
    return jnp.asarray(tab.reshape(HEAD_PAIRS, 2 * U_ROWS, TAB_W), dtype=F32)


def _attn_blocks(blocks, first_head):
    half = U_ROWS
    state = [dict() for _ in blocks]

    def scores(i):
        q2, k, _, bias = blocks[i]
        state[i]["s"] = _dot_nt(q2, k) + bias

    def top(i):
        state[i]["m"] = jnp.max(state[i]["s"], -1, keepdims=True)

    def weights(i):
        e = jnp.exp(state[i].pop("s") - state[i]["m"])
        state[i]["den"] = jnp.sum(e, -1, keepdims=True)
        state[i]["e"] = e.astype(BF16)

    def values(i):
        state[i]["pv"] = _dot(state[i].pop("e"), blocks[i][2])

    def finish(i):
        pv, m, den = state[i]["pv"], state[i]["m"], state[i]["den"]
        o = jnp.where(first_head, pv[:half], pv[half:])
        m2 = jnp.where(first_head, m[:half], m[half:])
        den2 = jnp.where(first_head, den[:half], den[half:])
        state[i] = (o / den2, m2 + jnp.log(den2))

    stages = (scores, top, weights, values, finish)
    groups = [range(j, min(j + WAVE, len(blocks))) for j in range(0, len(blocks), WAVE)]
    for t in range(len(groups) + len(stages) - 1):
        for g, members in enumerate(groups):
            if 0 <= t - g < len(stages):
                for i in members:
                    stages[t - g](i)
    return state


def _attn_prompt_body(q_ref, k_ref, v_ref, tab_ref, o_ref, qs, q0p, q1p, kp, vp, kn, vn, quarter,
                      o_scr, l_scr):
    lane = lax.broadcasted_iota(jnp.int32, (U_ROWS, LANES), 1)
    first_head = lane < HEAD_DIM
    scale = HEAD_DIM ** -0.5

    def natural(i, carry):
        rows = pl.ds(pl.multiple_of(i * U_ROWS, U_ROWS), U_ROWS)
        kn[rows, :] = k_ref[rows, :].astype(BF16)
        vn[rows, :] = v_ref[rows, :].astype(BF16)
        return carry
    lax.fori_loop(0, PHASES, natural, 0)

    def split(src_ref, emit):
        for r in range(4):
            quarter[r] = src_ref[pl.ds(r, CACHE_LEN // 4, stride=4), :]
        for r in range(4):
            for kk in range(4):
                emit(4 * kk + r, quarter[r, pl.ds(kk, U_ROWS, stride=4), :])

    def emit_q(p, tile):
        tile = tile * scale
        qs[p] = tile
        q0p[p] = jnp.where(first_head, tile, 0.0).astype(BF16)
        q1p[p] = jnp.where(first_head, 0.0, tile).astype(BF16)

    def emit_to(dst):
        def emit(p, tile):
            dst[p] = tile.astype(BF16)
        return emit

    split(q_ref, emit_q)
    split(k_ref, emit_to(kp))
    split(v_ref, emit_to(vp))

    def branch3(g, carry):
        phases = [g * BLOCKS_IN_FLIGHT + j for j in range(BLOCKS_IN_FLIGHT)]
        bias = tab_ref[:, TAB3_LO:TAB3_LO + 128]
        outs = _attn_blocks([(jnp.concatenate([q0p[p], q1p[p]], 0), kp[p], vp[p], bias)
                             for p in phases], first_head)
        for p, (o, l) in zip(phases, outs):
            o_scr[0, p] = o
            l_scr[0, p] = l
        return carry
    lax.fori_loop(0, PHASES // BLOCKS_IN_FLIGHT, branch3, 0)

    def branch2(g, carry):
        blocks, where = [], []
        for r in [B2_SUBSEQS * g + j for j in range(B2_SUBSEQS)]:
            gather = lambda src, rr, r=r: [src[4 * kk + r, rr, :] for kk in range(4)]
            for n in range(4):
                rows = pl.ds(32 * n, 32)
                q2 = jnp.concatenate(gather(q0p, rows) + gather(q1p, rows), 0)
                if n == 0:
                    k = jnp.concatenate(gather(kp, rows), 0)
                    v = jnp.concatenate(gather(vp, rows), 0)
                    bias = tab_ref[:, TAB2_LO + 128:TAB2_LO + 256]
                else:
                    prev = pl.ds(32 * (n - 1), 32)
                    k = jnp.concatenate(gather(kp, prev) + gather(kp, rows), 0)
                    v = jnp.concatenate(gather(vp, prev) + gather(vp, rows), 0)
                    bias = tab_ref[:, TAB2_LO:TAB2_LO + 256]
                blocks.append((q2, k, v, bias))
                where.append((r, rows))
        for (r, rows), (o, l) in zip(where, _attn_blocks(blocks, first_head)):
            for kk in range(4):
                o_scr[1, 4 * kk + r, rows, :] = o[32 * kk:32 * kk + 32]
                l_scr[1, 4 * kk + r, rows, :] = l[32 * kk:32 * kk + 32]
        return carry
    lax.fori_loop(0, 4 // B2_SUBSEQS, branch2, 0)

    def branch1_blocks(ns):
        blocks = []
        for n in ns:
            rows = pl.ds(n * SUBLANES, SUBLANES)
            q = jnp.concatenate([qs[p, rows, :] for p in range(PHASES)], 0)
            q2 = jnp.concatenate([jnp.where(first_head, q, 0.0), jnp.where(first_head, 0.0, q)],
                                 0).astype(BF16)
            if n > 0:
                keys = pl.ds((n - 1) * U_ROWS, 2 * U_ROWS)
                bias = tab_ref[:, TAB1_LO:TAB1_LO + 256]
            else:
                keys = pl.ds(0, U_ROWS)
                bias = tab_ref[:, TAB1_LO + 128:TAB1_LO + 256]
            blocks.append((q2, kn[keys, :], vn[keys, :], bias))
        for n, (o, l) in zip(ns, _attn_blocks(blocks, first_head)):
            rows = pl.ds(n * SUBLANES, SUBLANES)
            for p in range(PHASES):
                o_scr[2, p, rows, :] = o[SUBLANES * p:SUBLANES * (p + 1)]
                l_scr[2, p, rows, :] = l[SUBLANES * p:SUBLANES * (p + 1)]

    n_blocks = U_ROWS // SUBLANES
    for first in range(0, n_blocks, B1_IN_FLIGHT):
        branch1_blocks(range(first, min(first + B1_IN_FLIGHT, n_blocks)))

    def phase_rows(p):
        return pl.ds(p, U_ROWS, stride=PHASES)

    def merge(p, carry):
        l3, l2, l1 = l_scr[0, p], l_scr[1, p], l_scr[2, p]
        top = jnp.maximum(jnp.maximum(l3, l2), l1)
        e3, e2, e1 = jnp.exp(l3 - top), jnp.exp(l2 - top), jnp.exp(l1 - top)
        total = e3 + e2 + e1
        o_ref[phase_rows(p), :] = (e3 * o_scr[0, p] + e2 * o_scr[1, p] + e1 * o_scr[2, p]) / total
        return carry
    lax.fori_loop(0, PHASES, merge, 0)


def _attn_prompt(q, k, v):
    bsz, _, seq, _ = k.shape
    assert seq == CACHE_LEN, "prompt attention is laid out for a 2048-token prompt"
    slab = pl.BlockSpec((None, None, seq, LANES), lambda b, c: (b, c, 0, 0))
    tile = (PHASES, U_ROWS, LANES)
    return pl.pallas_call(
        _attn_prompt_body,
        grid=(bsz, HEAD_PAIRS),
        in_specs=[slab, slab, slab,
                  pl.BlockSpec((None, 2 * U_ROWS, TAB_W), lambda b, c: (c, 0, 0))],
        out_specs=slab,
        out_shape=jax.ShapeDtypeStruct((bsz, HEAD_PAIRS, seq, LANES), F32),
        scratch_shapes=[pltpu.VMEM(tile, F32)] + [pltpu.VMEM(tile, BF16)] * 4
                       + [pltpu.VMEM((seq, LANES), BF16)] * 2
                       + [pltpu.VMEM((4, seq // 4, LANES), F32),
                          pltpu.VMEM((3,) + tile, F32), pltpu.VMEM((3,) + tile, F32)],
        compiler_params=pltpu.CompilerParams(dimension_semantics=("parallel", "parallel"),
                                             vmem_limit_bytes=VMEM_LIMIT),
        name="attn_prompt",
    )(q, k, v, _prompt_bias_tables())


DEC_T = 8
QH = DEC_T * N_HEADS
B2_SPAN, B1_SPAN = 512, 128
SAMPLE_TAB_W = CACHE_LEN + B2_SPAN + B1_SPAN


def _sample_bias_tables():
    slopes = _slopes()
    i = np.repeat(np.arange(DEC_T), N_HEADS)[:, None]
    sl = np.tile(slopes, DEC_T)[:, None]

    def cache_bias(span, dil):
        t = CACHE_LEN - span + np.arange(span)[None, :]
        dist = CACHE_LEN + i - t
        valid = (dist % dil == 0) & (dist <= WINDOW_STEPS * dil)
        return np.where(valid, -sl * dist, NEG_INF)

    cache_tab = np.concatenate([cache_bias(CACHE_LEN, 16), cache_bias(B2_SPAN, 4),
                                cache_bias(B1_SPAN, 1)], axis=1)
    c = np.arange(CHUNK)[None, :]
    dn = i - c
    is_new = c < DEC_T
    n1 = np.where(is_new & (dn >= 0), -sl * dn, NEG_INF)
    n2 = np.where(is_new & ((dn == 0) | (dn == 4)), -sl * dn, NEG_INF)
    n3 = np.where(is_new & (dn == 0), 0.0, NEG_INF)
    return jnp.asarray(cache_tab, dtype=F32), jnp.asarray(np.stack([n1, n2, n3]), dtype=F32)


def _head_selector():
    h_row = np.tile(np.arange(N_HEADS), DEC_T)[:, None]
    h_col = (np.arange(D_ATTN) // HEAD_DIM)[None, :]
    return jnp.asarray((h_row == h_col).astype(np.float32))


def _attn_sample_stages(q_ref, kn_ref, vn_ref, kt_ref, vt_ref, tab_ref, tabn_ref, sel_ref, o_ref):
    sel = sel_ref[...]
    q = q_ref[...] * (HEAD_DIM ** -0.5)
    q_rows = jnp.concatenate([jnp.broadcast_to(q[i:i + 1, :], (N_HEADS, D_ATTN))
                              for i in range(DEC_T)], 0)
    qall = (q_rows * sel).astype(BF16)
    pad = jnp.zeros((CHUNK - DEC_T, D_ATTN), F32)
    kn = jnp.concatenate([kn_ref[...], pad], 0).astype(BF16)
    vn = jnp.concatenate([vn_ref[...], pad], 0).astype(BF16)

    s_all = _dot(qall, kt_ref[...].astype(BF16))
    s_new = _dot_nt(qall, kn)
    yield
    lo2, lo1 = CACHE_LEN - B2_SPAN, CACHE_LEN - B1_SPAN

    def branch(s_cache, s_fresh, v_t):
        m = jnp.maximum(jnp.max(s_cache, -1, keepdims=True), jnp.max(s_fresh, -1, keepdims=True))
        e_c, e_f = jnp.exp(s_cache - m), jnp.exp(s_fresh - m)
        den = jnp.sum(e_c, -1, keepdims=True) + jnp.sum(e_f, -1, keepdims=True)
        acc = _dot_nt(e_c.astype(BF16), v_t.astype(BF16)) + _dot(e_f.astype(BF16), vn)
        return m, den, acc

    m3, den3, acc3 = branch(s_all + tab_ref[:, :CACHE_LEN], s_new + tabn_ref[2], vt_ref[...])
    m2, den2, acc2 = branch(s_all[:, lo2:] + tab_ref[:, CACHE_LEN:CACHE_LEN + B2_SPAN],
                            s_new + tabn_ref[1], vt_ref[:, lo2:])
    m1, den1, acc1 = branch(s_all[:, lo1:] + tab_ref[:, CACHE_LEN + B2_SPAN:],
                            s_new + tabn_ref[0], vt_ref[:, lo1:])

    top = jnp.maximum(jnp.maximum(m1, m2), m3)
    sc1, sc2, sc3 = den1 * jnp.exp(m1 - top), den2 * jnp.exp(m2 - top), den3 * jnp.exp(m3 - top)
    total = sc1 + sc2 + sc3
    mixed = ((sc1 / total) * (acc1 / den1) + (sc2 / total) * (acc2 / den2)
             + (sc3 / total) * (acc3 / den3)) * sel
    for i in range(DEC_T):
        row = jnp.sum(mixed[N_HEADS * i:N_HEADS * (i + 1)], axis=0, keepdims=True)
        for c in range(HEAD_PAIRS):
            o_ref[c, i:i + 1, :] = row[:, c * LANES:(c + 1) * LANES]


def _attn_sample_body(*refs):
    for _ in _attn_sample_stages(*refs):
        pass


CACHE_KT_ARG, CACHE_VT_ARG = 3, 4
CACHE_SLOTS = 3


def _attn_sample_call(q, k_new, v_new, cache_kt, cache_vt):
    n = q.shape[0]
    assert cache_kt.shape[1:] == (D_ATTN, CACHE_LEN) and q.shape[1] == DEC_T
    tok = pl.BlockSpec((None, DEC_T, D_ATTN), lambda i: (i, 0, 0))
    cache = pl.BlockSpec((None, D_ATTN, CACHE_LEN), lambda i: (i, 0, 0))
    cache_tab, new_tab = _sample_bias_tables()
    operands = (q, k_new, v_new, cache_kt, cache_vt, cache_tab, new_tab, _head_selector())
    in_specs = [tok, tok, tok, cache, cache, _const_spec((QH, SAMPLE_TAB_W)),
                _const_spec((3, QH, CHUNK)), _const_spec((QH, D_ATTN))]
    out_spec = pl.BlockSpec((None, HEAD_PAIRS, DEC_T, LANES), lambda i: (0, 0, i, 0))
    out_shape = jax.ShapeDtypeStruct((1, HEAD_PAIRS, n * DEC_T, LANES), F32)
    return operands, in_specs, out_spec, out_shape


def _attn_sample(q, k_new, v_new, cache_kt, cache_vt):
    operands, in_specs, out_spec, out_shape = _attn_sample_call(q, k_new, v_new, cache_kt, cache_vt)
    return pl.pallas_call(
        _attn_sample_body,
        grid=(q.shape[0],),
        in_specs=in_specs,
        out_specs=out_spec,
        out_shape=out_shape,
        compiler_params=pltpu.CompilerParams(dimension_semantics=("parallel",),
                                             vmem_limit_bytes=VMEM_LIMIT),
        name="attn_sample",
    )(*operands)


HEADS_PER_GROUP = N_HEADS // SSD_GROUPS
PAD_ROWS = SUBLANES
SHORT_SEQ = SUBLANES
SEQS_PER_TILE = CHUNK // SHORT_SEQ


def _ssd_stages(packed, first_chunk, z_ref, xbc_ref, dt_ref, cp_ref, h0_ref, cw_ref, cb_ref, dtb_ref,
                alog_ref, dsk_ref, nw_ref, y_ref, hn_ref, xpad, aux):
    if packed:
        aux[0:CHUNK, :] = cp_ref[...]
        aux[CHUNK:, :] = jnp.zeros((PAD_ROWS, CONV_DIM), F32)
        xpad[0:PAD_ROWS, :] = jnp.zeros((PAD_ROWS, CONV_DIM), F32)
    else:
        @pl.when(first_chunk())
        def _init():
            xpad[0:PAD_ROWS, :] = cp_ref[...]
            aux[...] = h0_ref[...]
    yield
    xpad[PAD_ROWS:, :] = xbc_ref[...]

    step = lax.broadcasted_iota(jnp.int32, (CHUNK, CONV_DIM), 0) % SHORT_SEQ
    conv = cb_ref[...]
    for back in range(CONV_W):
        tap = CONV_W - 1 - back
        rows_back = xpad[pl.ds(PAD_ROWS - back, CHUNK), :]
        if packed and back:
            rows_back = jnp.where(step < back, aux[pl.ds(PAD_ROWS - back, CHUNK), :], rows_back)
        conv = conv + rows_back * cw_ref[tap:tap + 1, :]
    xc = _silu(conv)
    xs = xc[:, :D_SSD]

    lane = lax.broadcasted_iota(jnp.int32, (CHUNK, DT_PAD), 1)
    dt_raw = dt_ref[...] + dtb_ref[...]
    dt = jnp.maximum(dt_raw, 0.0) + jnp.log1p(jnp.exp(-jnp.abs(dt_raw)))
    dt = jnp.where(lane < N_HEADS, dt, 0.0)
    adt = dt * (-jnp.exp(alog_ref[...]))

    ri = lax.broadcasted_iota(jnp.int32, (CHUNK, CHUNK), 0)
    ci = lax.broadcasted_iota(jnp.int32, (CHUNK, CHUNK), 1)
    causal = ri >= ci
    if packed:
        causal = causal & (ri // SHORT_SEQ == ci // SHORT_SEQ)
    exact_dot = functools.partial(jnp.dot, precision=lax.Precision.HIGHEST,
                                  preferred_element_type=F32)
    cs = exact_dot(jnp.where(causal, 1.0, 0.0).astype(F32), adt)
    cs_t = cs.T
    if packed:
        pick_last = ci == (ri // SHORT_SEQ) * SHORT_SEQ + (SHORT_SEQ - 1)
        cs_end = exact_dot(jnp.where(pick_last, 1.0, 0.0).astype(F32), cs)
    else:
        cs_end = cs[CHUNK - 1:CHUNK, :]

    head_of_lane = lax.broadcasted_iota(jnp.int32, (CHUNK, GROUP_W), 1) // HEAD_DIM

    def per_head(cols):
        out = cols[HEADS_PER_GROUP - 1]
        for hl in range(HEADS_PER_GROUP - 2, -1, -1):
            out = jnp.where(head_of_lane == hl, cols[hl], out)
        return out

    y_groups = []
    for g in range(SSD_GROUPS):
        heads = range(g * HEADS_PER_GROUP, (g + 1) * HEADS_PER_GROUP)
        grp = slice(g * GROUP_W, (g + 1) * GROUP_W)
        b_g = xc[:, D_SSD + g * D_STATE:D_SSD + (g + 1) * D_STATE].astype(BF16)
        c_g = xc[:, D_SSD + (SSD_GROUPS + g) * D_STATE:
                 D_SSD + (SSD_GROUPS + g + 1) * D_STATE].astype(BF16)
        cs_cols = [cs[:, h:h + 1] for h in heads]
        xdt_g = xs[:, grp] * per_head([dt[:, h:h + 1] for h in heads])
        xdt_b = xdt_g.astype(BF16)
        gram = _dot_nt(c_g, b_g)

        y_diag = jnp.zeros((CHUNK, GROUP_W), F32)
        for hl, h in enumerate(heads):
            seg = jnp.where(causal, cs_cols[hl] - cs_t[h:h + 1, :], NEG_INF)
            weights = (gram * jnp.exp(seg)).astype(BF16)
            y_diag = y_diag + jnp.where(head_of_lane == hl, _dot(weights, xdt_b), 0.0)

        to_end = per_head([jnp.exp(cs_end[:, h:h + 1] - cs_cols[hl]) for hl, h in enumerate(heads)])
        decayed_t = (xdt_g * to_end).T
        carried = per_head([jnp.exp(col) for col in cs_cols])
        if packed:
            h_prev = h0_ref[:, grp, :]
            wide = _dot_nt(c_g, h_prev.reshape(SEQS_PER_TILE * GROUP_W, D_STATE).astype(BF16))
            y_off = jnp.concatenate(
                [wide[s * SHORT_SEQ:(s + 1) * SHORT_SEQ, s * GROUP_W:(s + 1) * GROUP_W]
                 for s in range(SEQS_PER_TILE)], 0) * carried
            seq_of_step = lax.broadcasted_iota(jnp.int32, (GROUP_W, CHUNK), 1) // SHORT_SEQ
            per_seq = jnp.concatenate([jnp.where(seq_of_step == s, decayed_t, 0.0)
                                       for s in range(SEQS_PER_TILE)], 0).astype(BF16)
            new_states = _dot(per_seq, b_g).reshape(SEQS_PER_TILE, GROUP_W, D_STATE)
            for s in range(SEQS_PER_TILE):
                row = s * SHORT_SEQ
                keep = jnp.concatenate(
                    [jnp.broadcast_to(jnp.exp(cs_end[row:row + 1, h:h + 1]), (HEAD_DIM, D_STATE))
                     for h in heads], 0)
                hn_ref[s, grp, :] = h_prev[s] * keep + new_states[s]
        else:
            h_prev = aux[grp, :]
            y_off = _dot_nt(c_g, h_prev.astype(BF16)) * carried
            keep = jnp.concatenate(
                [jnp.broadcast_to(jnp.exp(cs_end[:, h:h + 1]), (HEAD_DIM, D_STATE)) for h in heads], 0)
            aux[grp, :] = h_prev * keep + _dot(decayed_t.astype(BF16), b_g)
        y_groups.append(y_diag + y_off)

    y = jnp.concatenate(y_groups, axis=-1) + dsk_ref[...] * xs
    y = y * _silu(z_ref[...])
    normed = []
    for g in range(SSD_GROUPS):
        yg = y[:, g * GROUP_W:(g + 1) * GROUP_W]
        normed.append(yg * lax.rsqrt(jnp.mean(yg * yg, -1, keepdims=True) + LN_EPS))
    y_ref[...] = jnp.concatenate(normed, axis=-1) * nw_ref[...]

    if not packed:
        hn_ref[...] = aux[...]
        xpad[0:PAD_ROWS, :] = xpad[CHUNK:, :]


def _ssd_body(*args):
    for _ in _ssd_stages(*args):
        pass


N_SSD_INPUTS = 11


def _ssd_call(z, xbc, dt, conv_prev, h0, cw, cb, dtb, alog, dsk, nw, where=None):
    n, seq, _ = z.shape
    packed = seq == SHORT_SEQ
    if packed:
        assert n % SEQS_PER_TILE == 0 and where is None
        grid = (n // SEQS_PER_TILE, 1)
        fold = lambda t: t.reshape(grid[0], CHUNK, t.shape[-1])
        z, xbc, dt, conv_prev = fold(z), fold(xbc), fold(dt), fold(conv_prev)
        history_shape, state_shape = (None, CHUNK, CONV_DIM), (SEQS_PER_TILE, D_SSD, D_STATE)
        aux = pltpu.VMEM((CHUNK + PAD_ROWS, CONV_DIM), F32)
    else:
        assert seq % CHUNK == 0
        grid = (n, seq // CHUNK)
        history_shape, state_shape = (None, PAD_ROWS, CONV_DIM), (None, D_SSD, D_STATE)
        aux = pltpu.VMEM((D_SSD, D_STATE), F32)
    where = where or (lambda s, c: (s, c))
    tile = lambda width: pl.BlockSpec((None, CHUNK, width), lambda *g: where(*g) + (0,))
    per_seq = lambda shape: pl.BlockSpec(shape, lambda *g: (where(*g)[0], 0, 0))
    history, state = per_seq(history_shape), per_seq(state_shape)
    operands = (z, xbc, dt, conv_prev, h0, cw, cb, dtb, alog, dsk, nw)
    in_specs = [tile(D_SSD), tile(CONV_DIM), tile(DT_PAD), history, state,
                _const_spec((CONV_W, CONV_DIM)), _const_spec((1, CONV_DIM)),
                _const_spec((1, DT_PAD)), _const_spec((1, DT_PAD)),
                _const_spec((1, D_SSD)), _const_spec((1, D_SSD))]
    out_specs = [tile(D_SSD), state]
    out_shape = [jax.ShapeDtypeStruct(z.shape, F32), jax.ShapeDtypeStruct((n, D_SSD, D_STATE), F32)]
    scratch = [pltpu.VMEM((PAD_ROWS + CHUNK, CONV_DIM), F32), aux]
    return packed, grid, operands, in_specs, out_specs, out_shape, scratch


def _ssd(z, *rest):
    packed, grid, operands, in_specs, out_specs, out_shape, scratch = _ssd_call(z, *rest)
    y, h_new = pl.pallas_call(
        functools.partial(_ssd_body, packed, lambda: pl.program_id(1) == 0),
        grid=grid,
        in_specs=in_specs,
        out_specs=out_specs,
        out_shape=out_shape,
        scratch_shapes=scratch,
        compiler_params=pltpu.CompilerParams(dimension_semantics=("parallel", "arbitrary"),
                                             vmem_limit_bytes=VMEM_LIMIT),
        name="ssd",
    )(*operands)
    return y.reshape(z.shape), h_new


def _ssd_with_sample_attn(ssd_args, attn_args):
    z = ssd_args[0]
    nchunks = z.shape[1] // CHUNK
    steps = z.shape[0] * nchunks
    assert steps == attn_args[0].shape[0]
    where = lambda i: (i // nchunks, i % nchunks)
    _, _, ssd_ops, ssd_in, ssd_out, ssd_shape, scratch = _ssd_call(*ssd_args, where=where)
    attn_ops, attn_in, attn_out, attn_shape = _attn_sample_call(*attn_args)
    caches = (CACHE_KT_ARG, CACHE_VT_ARG)
    for arg in caches:
        attn_in[arg] = pl.BlockSpec(memory_space=pl.ANY)
    ring = pltpu.VMEM((CACHE_SLOTS, D_ATTN, CACHE_LEN), F32)

    def body(*refs):
        ssd_in_refs, refs = refs[:N_SSD_INPUTS], refs[N_SSD_INPUTS:]
        attn_in_refs, refs = list(refs[:len(attn_ops)]), refs[len(attn_ops):]
        y_ref, hn_ref, o_ref, xpad, aux, kt_ring, vt_ring, sems = refs
        step = pl.program_id(0)

        def fetch(s):
            slot = s % CACHE_SLOTS
            return [pltpu.make_async_copy(attn_in_refs[arg].at[s], buf.at[slot], sems.at[j, slot])
                    for j, (arg, buf) in enumerate(zip(caches, (kt_ring, vt_ring)))]

        @pl.when(step == 0)
        def _prime():
            for s in range(CACHE_SLOTS - 1):
                for copy in fetch(s):
                    copy.start()

        @pl.when(step + (CACHE_SLOTS - 1) < steps)
        def _ahead():
            for copy in fetch(step + (CACHE_SLOTS - 1)):
                copy.start()

        for copy in fetch(step):
            copy.wait()
        for arg, buf in zip(caches, (kt_ring, vt_ring)):
            attn_in_refs[arg] = buf.at[step % CACHE_SLOTS]

        ssd = _ssd_stages(False, lambda: step % nchunks == 0, *ssd_in_refs, y_ref, hn_ref, xpad, aux)
        attn = _attn_sample_stages(*attn_in_refs, o_ref)
        for stage in (ssd, attn, ssd, attn):
            next(stage, None)

    y, h_new, attn = pl.pallas_call(
        body,
        grid=(steps,),
        in_specs=ssd_in + attn_in,
        out_specs=ssd_out + [attn_out],
        out_shape=ssd_shape + [attn_shape],
        scratch_shapes=scratch + [ring, ring, pltpu.SemaphoreType.DMA((len(caches), CACHE_SLOTS))],
        compiler_params=pltpu.CompilerParams(dimension_semantics=("arbitrary",),
                                             vmem_limit_bytes=VMEM_LIMIT),
        name="ssd_attn_sample",
    )(*ssd_ops, *attn_ops)
    return y, h_new, attn


def _row(v, width=None):
    v = v.reshape(1, -1).astype(F32)
    if width is not None and v.shape[1] < width:
        v = jnp.pad(v, ((0, 0), (0, width - v.shape[1])))
    return v


def kernel(x_prompt, x_sample, cache_k, cache_v, state_conv, state_ssm, p_prompt, p_sample,
           ln_in_g, ln_in_b, w_in, conv_w, conv_b, dt_bias, a_log, d_skip, ssd_norm_w, w_out,
           ln1_g, ln1_b, w_up, w_down, ln2_g, ln2_b, w_gate, w_ple, ln3_g, ln3_b):
    depth = w_in.shape[0]
    assert depth == 1, "single-layer step"
    alpha = (2 * depth) ** 0.25
    bsz, seq, _ = x_prompt.shape
    nd, dec_t, _ = x_sample.shape
    lyr = 0

    w_proj = jnp.pad(w_in[lyr], ((0, 0), (0, D_PROJ - w_in.shape[2]))).astype(BF16)
    gin, bin_ = _row(ln_in_g), _row(ln_in_b)
    ssd_params = (conv_w[lyr].astype(F32), _row(conv_b[lyr]), _row(dt_bias[lyr], DT_PAD),
                  _row(a_log[lyr], DT_PAD), _row(jnp.repeat(d_skip[lyr], HEAD_DIM)),
                  _row(ssd_norm_w[lyr]))
    post_params = (gin, bin_, w_out[lyr].astype(BF16), _row(ln1_g[lyr]), _row(ln1_b[lyr]),
                   w_up[lyr].astype(BF16), w_down[lyr].astype(BF16), _row(ln2_g[lyr]),
                   _row(ln2_b[lyr]), w_gate[lyr].astype(BF16), w_ple[lyr].astype(BF16),
                   _row(ln3_g[lyr]), _row(ln3_b[lyr]))

    q, k, v, k_t, v_t, z, xbc, dt = _in_proj(x_prompt, gin, bin_, w_proj, head_major=True, tm=512)
    n_tok = nd * dec_t
    flat = lambda t: t.reshape(1, n_tok, t.shape[-1])
    toks = lambda t: t.reshape(nd, dec_t, t.shape[-1])
    qs, ks, vs, zs, xbcs, dts = _in_proj(flat(x_sample), gin, bin_, w_proj, head_major=False, tm=512)

    attn = _attn_prompt(q, k, v)
    transposed = lambda c: jnp.transpose(c, (0, 2, 3, 1)).reshape(nd, D_ATTN, CACHE_LEN)
    ssd_args = (z, xbc, dt, jnp.zeros((bsz, PAD_ROWS, CONV_DIM), F32),
                jnp.zeros((bsz, D_SSD, D_STATE), F32)) + ssd_params
    attn_args = (toks(qs), toks(ks), toks(vs), transposed(cache_k[lyr]), transposed(cache_v[lyr]))
    if bsz * (seq // CHUNK) == nd:
        ssd_y, ssm_p, attn_s = _ssd_with_sample_attn(ssd_args, attn_args)
    else:
        ssd_y, ssm_p = _ssd(*ssd_args)
        attn_s = _attn_sample(*attn_args)
    conv_prev = jnp.pad(state_conv[lyr].astype(F32), ((0, 0), (PAD_ROWS - (CONV_W - 1), 0), (0, 0)))
    ssd_s, ssm_s = _ssd(toks(zs), toks(xbcs), toks(dts), conv_prev,
                        state_ssm[lyr].reshape(nd, D_SSD, D_STATE).astype(F32), *ssd_params)

    y_prompt = _post(x_prompt, attn, ssd_y, p_prompt[lyr], *post_params, alpha=alpha, tm=512)
    y_sample = _post(flat(x_sample), attn_s, flat(ssd_s), flat(p_sample[lyr]), *post_params,
                     alpha=alpha, tm=256)

    from_t = lambda t: jnp.transpose(t.reshape(1, bsz, N_HEADS, HEAD_DIM, seq), (0, 1, 4, 2, 3))
    heads = lambda t: t.reshape(1, nd, dec_t, N_HEADS, HEAD_DIM)
    tail = lambda t: t[None, :, -(CONV_W - 1):, :]
    state = lambda t, n: t.reshape(1, n, N_HEADS, HEAD_DIM, D_STATE)
    return (y_prompt, y_sample.reshape(nd, dec_t, D_MODEL), from_t(k_t), from_t(v_t),
            heads(ks), heads(vs), tail(xbc), tail(toks(xbcs)), state(ssm_p, bsz), state(ssm_s, nd))
```

```python
import functools

import numpy as np
import jax
import jax.numpy as jnp
from jax import lax
from jax.experimental import pallas as pl
from jax.experimental.pallas import tpu as pltpu

F32 = jnp.float32
BF16 = jnp.bfloat16

D_MODEL = 1024
HEAD_DIM = 64
D_ATTN = 512
D_SSD = 512
N_HEADS = 8
SSD_GROUPS = 2
GROUP_W = D_SSD // SSD_GROUPS
D_STATE = 128
CONV_W = 4
CONV_DIM = D_SSD + 2 * SSD_GROUPS * D_STATE
CHUNK = 128
D_FF = 4096
D_PLE = 256
LN_EPS = 1e-5
WINDOW_STEPS = 128
DILATIONS = (1, 4, 16)
PHASES = 16
CACHE_LEN = 2048
DT_PAD = 128
D_PROJ = 3 * D_ATTN + D_SSD + CONV_DIM
SUBLANES = 8
LANES = 128
VMEM_LIMIT = 56 * 1024 * 1024
NEG_INF = float("-inf")


def _slopes():
    return np.array([2.0 ** (-8.0 * (h + 1) / N_HEADS) for h in range(N_HEADS)], dtype=np.float64)


def _layer_norm(x, g, b):
    mu = jnp.mean(x, -1, keepdims=True)
    xc = x - mu
    var = jnp.mean(xc * xc, -1, keepdims=True)
    return xc * lax.rsqrt(var + LN_EPS) * g + b


def _silu(x):
    return x * (1.0 / (1.0 + jnp.exp(-x)))


def _dot(a, b):
    return jnp.dot(a, b, preferred_element_type=F32)


def _dot_nt(a, b):
    return lax.dot_general(a, b, (((1,), (1,)), ((), ())), preferred_element_type=F32)


def _const_spec(shape):
    nd = len(shape)
    return pl.BlockSpec(shape, lambda *_: (0,) * nd, pipeline_mode=pl.Buffered(1))


HEAD_PAIRS = N_HEADS // 2


def _inproj_body(head_major, x_ref, g_ref, b_ref, w_ref, wdt_ref, *out_refs):
    tm = x_ref.shape[0]
    parts = [slice(i * tm // 2, (i + 1) * tm // 2) for i in range(2)]
    h = [_layer_norm(x_ref[r, :], g_ref[...], b_ref[...]).astype(BF16) for r in parts]
    proj = lambda lo, width: [_dot(v, w_ref[:, lo:lo + width]) for v in h]
    if head_major:
        q_ref, k_ref, v_ref, kt_ref, vt_ref, z_ref, xbc_ref, dt_ref = out_refs
        for idx, (ref, t_ref) in enumerate(((q_ref, None), (k_ref, kt_ref), (v_ref, vt_ref))):
            for r, res in zip(parts, proj(idx * D_ATTN, D_ATTN)):
                for c in range(HEAD_PAIRS):
                    ref[c, r, :] = res[:, c * LANES:(c + 1) * LANES]
                if t_ref is not None:
                    t_ref[:, r] = res.T
    else:
        q_ref, k_ref, v_ref, z_ref, xbc_ref, dt_ref = out_refs
        for idx, ref in enumerate((q_ref, k_ref, v_ref)):
            for r, res in zip(parts, proj(idx * D_ATTN, D_ATTN)):
                ref[r, :] = res
    lo = 3 * D_ATTN
    for ref, width in ((z_ref, D_SSD), (xbc_ref, CONV_DIM)):
        for r, res in zip(parts, proj(lo, width)):
            ref[r, :] = res
        lo += width
    for r, v in zip(parts, h):
        dt_ref[r, :] = _dot(v, wdt_ref[...])


def _in_proj(x, g, b, w, w_dt, *, head_major, tm):
    bsz, seq, _ = x.shape
    row = lambda width: pl.BlockSpec((None, tm, width), lambda i, j: (i, j, 0))
    row_shape = lambda width: jax.ShapeDtypeStruct((bsz, seq, width), F32)
    if head_major:
        slab = pl.BlockSpec((None, HEAD_PAIRS, tm, LANES), lambda i, j: (i, 0, j, 0))
        slab_shape = jax.ShapeDtypeStruct((bsz, HEAD_PAIRS, seq, LANES), F32)
        tr = pl.BlockSpec((None, D_ATTN, tm), lambda i, j: (i, 0, j))
        tr_shape = jax.ShapeDtypeStruct((bsz, D_ATTN, seq), F32)
        qkv_specs, qkv_shapes = [slab, slab, slab, tr, tr], [slab_shape] * 3 + [tr_shape] * 2
    else:
        qkv_specs, qkv_shapes = [row(D_ATTN)] * 3, [row_shape(D_ATTN)] * 3
    rest = (D_SSD, CONV_DIM, DT_PAD)
    return pl.pallas_call(
        functools.partial(_inproj_body, head_major),
        grid=(bsz, seq // tm),
        in_specs=[row(D_MODEL), _const_spec((1, D_MODEL)), _const_spec((1, D_MODEL)),
                  _const_spec((D_MODEL, D_PROJ)), _const_spec((D_MODEL, DT_PAD))],
        out_specs=qkv_specs + [row(wd) for wd in rest],
        out_shape=qkv_shapes + [row_shape(wd) for wd in rest],
        compiler_params=pltpu.CompilerParams(dimension_semantics=("parallel", "parallel"),
                                             vmem_limit_bytes=VMEM_LIMIT),
        name="in_proj",
    )(x, g, b, w, w_dt)


FF_CHUNK = 1024
POST_STREAMS = 2


def _post_body(alpha, x_ref, attn_ref, ssd_ref, pe_ref, gin_ref, bin_ref, wout_ref, g1_ref, b1_ref,
               wup_ref, wdown_ref, g2_ref, b2_ref, wgate_ref, wple_ref, g3_ref, b3_ref, y_ref):
    tm = x_ref.shape[0]
    parts = [slice(i * tm // POST_STREAMS, (i + 1) * tm // POST_STREAMS) for i in range(POST_STREAMS)]
    each = lambda fn, *lists: [fn(*args) for args in zip(*lists)]
    xn = [_layer_norm(x_ref[r, :], gin_ref[...], bin_ref[...]) for r in parts]
    mixed = [jnp.concatenate([attn_ref[c, r, :] for c in range(HEAD_PAIRS)] + [ssd_ref[r, :]],
                             axis=-1).astype(BF16) for r in parts]
    proj = [_dot(m, wout_ref[...]) for m in mixed]
    h = each(lambda x, p: _layer_norm(alpha * x + p, g1_ref[...], b1_ref[...]), xn, proj)
    hb = [v.astype(BF16) for v in h]
    u = [None] * POST_STREAMS
    for c in range(D_FF // FF_CHUNK):
        cols = slice(c * FF_CHUNK, (c + 1) * FF_CHUNK)
        a = [jnp.maximum(_dot(v, wup_ref[:, cols]), 0.0) for v in hb]
        part = [_dot((v * v).astype(BF16), wdown_ref[cols, :]) for v in a]
        u = part if c == 0 else each(lambda s, p: s + p, u, part)
    h = each(lambda v, w: _layer_norm(alpha * v + w, g2_ref[...], b2_ref[...]), h, u)
    gate = [1.0 / (1.0 + jnp.exp(-_dot(v.astype(BF16), wgate_ref[...]))) for v in h]
    emb = [_dot(pe_ref[r, :].astype(BF16), wple_ref[...]) for r in parts]
    for r, v, g, e in zip(parts, h, gate, emb):
        y_ref[r, :] = _layer_norm(alpha * v + g * e, g3_ref[...], b3_ref[...])


def _post(x, attn, ssd, pe, gin, bin_, wout, g1, b1, wup, wdown, g2, b2, wgate, wple, g3, b3, *,
          alpha, tm):
    bsz, seq, _ = x.shape
    row = lambda width: pl.BlockSpec((None, tm, width), lambda i, j: (i, j, 0))
    slab = pl.BlockSpec((None, HEAD_PAIRS, tm, LANES), lambda i, j: (i, 0, j, 0))
    vec = _const_spec((1, D_MODEL))
    return pl.pallas_call(
        functools.partial(_post_body, alpha),
        grid=(bsz, seq // tm),
        in_specs=[row(D_MODEL), slab, row(D_SSD), row(D_PLE), vec, vec,
                  _const_spec((D_MODEL, D_MODEL)), vec, vec,
                  _const_spec((D_MODEL, D_FF)), _const_spec((D_FF, D_MODEL)), vec, vec,
                  _const_spec((D_MODEL, D_MODEL)), _const_spec((D_PLE, D_MODEL)), vec, vec],
        out_specs=row(D_MODEL),
        out_shape=jax.ShapeDtypeStruct((bsz, seq, D_MODEL), F32),
        compiler_params=pltpu.CompilerParams(dimension_semantics=("parallel", "parallel"),
                                             vmem_limit_bytes=VMEM_LIMIT),
        name="post",
    )(x, attn, ssd, pe, gin, bin_, wout, g1, b1, wup, wdown, g2, b2, wgate, wple, g3, b3)


U_ROWS = CACHE_LEN // PHASES
TAB3_LO, TAB2_LO, TAB1_LO, TAB_W = 0, 128, 384, 640
BLOCKS_IN_FLIGHT = 16
WAVE = 4
B2_SUBSEQS = 4
B1_IN_FLIGHT = 8


def _prompt_bias_tables():
    slopes = _slopes()
    u = np.arange(U_ROWS)
    d3 = (u[:, None] - u[None, :]).astype(np.float64)
    k4, ul4 = np.meshgrid(np.arange(4), np.arange(32), indexing="ij")
    j2 = (4 * ul4 + k4).reshape(-1)
    d2 = j2[:, None] - np.concatenate([j2 - WINDOW_STEPS, j2])[None, :]
    p16, ul16 = np.meshgrid(np.arange(16), np.arange(8), indexing="ij")
    t1 = (16 * ul16 + p16).reshape(-1)
    d1 = t1[:, None] - (np.arange(2 * WINDOW_STEPS) - WINDOW_STEPS)[None, :]
    tabs = []
    for dist, dil in ((d3, 16), (d2, 4), (d1, 1)):
        valid = (dist >= 0) & (dist <= WINDOW_STEPS)
        per_head = [np.where(valid, -slopes[h] * dist * dil, NEG_INF) for h in range(N_HEADS)]
        tabs.append(np.stack(per_head))
    tab = np.concatenate(tabs, axis=-1)
    return jnp.asarray(tab.reshape(HEAD_PAIRS, 2 * U_ROWS, TAB_W), dtype=F32)


def _attn_blocks(blocks, first_head):
    half = U_ROWS
    state = [dict() for _ in blocks]

    def scores(i):
        q2, k, _, bias = blocks[i]
        state[i]["s"] = _dot_nt(q2, k) + bias

    def top(i):
        state[i]["m"] = jnp.max(state[i]["s"], -1, keepdims=True)

    def weights(i):
        e = jnp.exp(state[i].pop("s") - state[i]["m"])
        state[i]["den"] = jnp.sum(e, -1, keepdims=True)
        state[i]["e"] = e.astype(BF16)

    def values(i):
        state[i]["pv"] = _dot(state[i].pop("e"), blocks[i][2])

    def finish(i):
        pv, m, den = state[i]["pv"], state[i]["m"], state[i]["den"]
        o = jnp.where(first_head, pv[:half], pv[half:])
        m2 = jnp.where(first_head, m[:half], m[half:])
        den2 = jnp.where(first_head, den[:half], den[half:])
        state[i] = (o / den2, m2 + jnp.log(den2))

    stages = (scores, top, weights, values, finish)
    groups = [range(j, min(j + WAVE, len(blocks))) for j in range(0, len(blocks), WAVE)]
    for t in range(len(groups) + len(stages) - 1):
        for g, members in enumerate(groups):
            if 0 <= t - g < len(stages):
                for i in members:
                    stages[t - g](i)
    return state


def _attn_prompt_body(q_ref, k_ref, v_ref, tab_ref, o_ref, qs, q0p, q1p, kp, vp, kn, vn, quarter,
                      o_scr, l_scr):
    lane = lax.broadcasted_iota(jnp.int32, (U_ROWS, LANES), 1)
    first_head = lane < HEAD_DIM
    scale = HEAD_DIM ** -0.5

    kn[...] = k_ref[...].astype(BF16)
    vn[...] = v_ref[...].astype(BF16)

    def split(src_ref, emit):
        for r in range(4):
            quarter[r] = src_ref[pl.ds(r, CACHE_LEN // 4, stride=4), :]
        for r in range(4):
            for kk in range(4):
                emit(4 * kk + r, quarter[r, pl.ds(kk, U_ROWS, stride=4), :])

    def emit_q(p, tile):
        tile = tile * scale
        qs[p] = tile
        q0p[p] = jnp.where(first_head, tile, 0.0).astype(BF16)
        q1p[p] = jnp.where(first_head, 0.0, tile).astype(BF16)

    def emit_to(dst):
        def emit(p, tile):
            dst[p] = tile.astype(BF16)
        return emit

    split(q_ref, emit_q)
    split(k_ref, emit_to(kp))
    split(v_ref, emit_to(vp))

    def branch3(g, carry):
        phases = [g * BLOCKS_IN_FLIGHT + j for j in range(BLOCKS_IN_FLIGHT)]
        bias = tab_ref[:, TAB3_LO:TAB3_LO + 128]
        outs = _attn_blocks([(jnp.concatenate([q0p[p], q1p[p]], 0), kp[p], vp[p], bias)
                             for p in phases], first_head)
        for p, (o, l) in zip(phases, outs):
            o_scr[0, p] = o
            l_scr[0, p] = l
        return carry
    lax.fori_loop(0, PHASES // BLOCKS_IN_FLIGHT, branch3, 0)

    def branch2(g, carry):
        blocks, where = [], []
        for r in [B2_SUBSEQS * g + j for j in range(B2_SUBSEQS)]:
            gather = lambda src, rr, r=r: [src[4 * kk + r, rr, :] for kk in range(4)]
            for n in range(4):
                rows = pl.ds(32 * n, 32)
                q2 = jnp.concatenate(gather(q0p, rows) + gather(q1p, rows), 0)
                if n == 0:
                    k = jnp.concatenate(gather(kp, rows), 0)
                    v = jnp.concatenate(gather(vp, rows), 0)
                    bias = tab_ref[:, TAB2_LO + 128:TAB2_LO + 256]
                else:
                    prev = pl.ds(32 * (n - 1), 32)
                    k = jnp.concatenate(gather(kp, prev) + gather(kp, rows), 0)
                    v = jnp.concatenate(gather(vp, prev) + gather(vp, rows), 0)
                    bias = tab_ref[:, TAB2_LO:TAB2_LO + 256]
                blocks.append((q2, k, v, bias))
                where.append((r, rows))
        for (r, rows), (o, l) in zip(where, _attn_blocks(blocks, first_head)):
            for kk in range(4):
                o_scr[1, 4 * kk + r, rows, :] = o[32 * kk:32 * kk + 32]
                l_scr[1, 4 * kk + r, rows, :] = l[32 * kk:32 * kk + 32]
        return carry
    lax.fori_loop(0, 4 // B2_SUBSEQS, branch2, 0)

    def branch1_blocks(ns):
        blocks = []
        for n in ns:
            rows = pl.ds(n * SUBLANES, SUBLANES)
            q = jnp.concatenate([qs[p, rows, :] for p in range(PHASES)], 0)
            q2 = jnp.concatenate([jnp.where(first_head, q, 0.0), jnp.where(first_head, 0.0, q)],
                                 0).astype(BF16)
            if n > 0:
                keys = pl.ds((n - 1) * U_ROWS, 2 * U_ROWS)
                bias = tab_ref[:, TAB1_LO:TAB1_LO + 256]
            else:
                keys = pl.ds(0, U_ROWS)
                bias = tab_ref[:, TAB1_LO + 128:TAB1_LO + 256]
            blocks.append((q2, kn[keys, :], vn[keys, :], bias))
        for n, (o, l) in zip(ns, _attn_blocks(blocks, first_head)):
            rows = pl.ds(n * SUBLANES, SUBLANES)
            for p in range(PHASES):
                o_scr[2, p, rows, :] = o[SUBLANES * p:SUBLANES * (p + 1)]
                l_scr[2, p, rows, :] = l[SUBLANES * p:SUBLANES * (p + 1)]

    n_blocks = U_ROWS // SUBLANES
    for first in range(0, n_blocks, B1_IN_FLIGHT):
        branch1_blocks(range(first, min(first + B1_IN_FLIGHT, n_blocks)))

    def phase_rows(p):
        return pl.ds(p, U_ROWS, stride=PHASES)

    for p in range(PHASES):
        l3, l2, l1 = l_scr[0, p], l_scr[1, p], l_scr[2, p]
        top = jnp.maximum(jnp.maximum(l3, l2), l1)
        e3, e2, e1 = jnp.exp(l3 - top), jnp.exp(l2 - top), jnp.exp(l1 - top)
        total = e3 + e2 + e1
        o_ref[phase_rows(p), :] = (e3 * o_scr[0, p] + e2 * o_scr[1, p] + e1 * o_scr[2, p]) / total


def _attn_prompt(q, k, v):
    bsz, _, seq, _ = k.shape
    assert seq == CACHE_LEN, "prompt attention is laid out for a 2048-token prompt"
    slab = pl.BlockSpec((None, None, seq, LANES), lambda b, c: (b, c, 0, 0))
    tile = (PHASES, U_ROWS, LANES)
    return pl.pallas_call(
        _attn_prompt_body,
        grid=(bsz, HEAD_PAIRS),
        in_specs=[slab, slab, slab,
                  pl.BlockSpec((None, 2 * U_ROWS, TAB_W), lambda b, c: (c, 0, 0))],
        out_specs=slab,
        out_shape=jax.ShapeDtypeStruct((bsz, HEAD_PAIRS, seq, LANES), F32),
        scratch_shapes=[pltpu.VMEM(tile, F32)] + [pltpu.VMEM(tile, BF16)] * 4
                       + [pltpu.VMEM((seq, LANES), BF16)] * 2
                       + [pltpu.VMEM((4, seq // 4, LANES), F32),
                          pltpu.VMEM((3,) + tile, F32), pltpu.VMEM((3,) + tile, F32)],
        compiler_params=pltpu.CompilerParams(dimension_semantics=("parallel", "parallel"),
                                             vmem_limit_bytes=VMEM_LIMIT),
        name="attn_prompt",
    )(q, k, v, _prompt_bias_tables())


DEC_T = 8
QH = DEC_T * N_HEADS
B2_SPAN, B1_SPAN = 512, 128
SAMPLE_TAB_W = CACHE_LEN + B2_SPAN + B1_SPAN


def _sample_bias_tables():
    slopes = _slopes()
    i = np.repeat(np.arange(DEC_T), N_HEADS)[:, None]
    sl = np.tile(slopes, DEC_T)[:, None]

    def cache_bias(span, dil):
        t = CACHE_LEN - span + np.arange(span)[None, :]
        dist = CACHE_LEN + i - t
        valid = (dist % dil == 0) & (dist <= WINDOW_STEPS * dil)
        return np.where(valid, -sl * dist, NEG_INF)

    cache_tab = np.concatenate([cache_bias(CACHE_LEN, 16), cache_bias(B2_SPAN, 4),
                                cache_bias(B1_SPAN, 1)], axis=1)
    c = np.arange(CHUNK)[None, :]
    dn = i - c
    is_new = c < DEC_T
    n1 = np.where(is_new & (dn >= 0), -sl * dn, NEG_INF)
    n2 = np.where(is_new & ((dn == 0) | (dn == 4)), -sl * dn, NEG_INF)
    n3 = np.where(is_new & (dn == 0), 0.0, NEG_INF)
    return jnp.asarray(cache_tab, dtype=F32), jnp.asarray(np.stack([n1, n2, n3]), dtype=F32)


def _head_selector():
    h_row = np.tile(np.arange(N_HEADS), DEC_T)[:, None]
    h_col = (np.arange(D_ATTN) // HEAD_DIM)[None, :]
    return jnp.asarray((h_row == h_col).astype(np.float32))


def _attn_sample_stages(q_ref, kn_ref, vn_ref, kt_ref, vt_ref, tab_ref, tabn_ref, sel_ref, o_ref):
    sel = sel_ref[...]
    q = q_ref[...] * (HEAD_DIM ** -0.5)
    q_rows = jnp.concatenate([jnp.broadcast_to(q[i:i + 1, :], (N_HEADS, D_ATTN))
                              for i in range(DEC_T)], 0)
    qall = (q_rows * sel).astype(BF16)
    pad = jnp.zeros((CHUNK - DEC_T, D_ATTN), F32)
    kn = jnp.concatenate([kn_ref[...], pad], 0).astype(BF16)
    vn = jnp.concatenate([vn_ref[...], pad], 0).astype(BF16)

    s_all = _dot(qall, kt_ref[...].astype(BF16))
    s_new = _dot_nt(qall, kn)
    yield
    lo2, lo1 = CACHE_LEN - B2_SPAN, CACHE_LEN - B1_SPAN

    def branch(s_cache, s_fresh, v_t):
        m = jnp.maximum(jnp.max(s_cache, -1, keepdims=True), jnp.max(s_fresh, -1, keepdims=True))
        e_c, e_f = jnp.exp(s_cache - m), jnp.exp(s_fresh - m)
        den = jnp.sum(e_c, -1, keepdims=True) + jnp.sum(e_f, -1, keepdims=True)
        acc = _dot_nt(e_c.astype(BF16), v_t.astype(BF16)) + _dot(e_f.astype(BF16), vn)
        return m, den, acc

    m3, den3, acc3 = branch(s_all + tab_ref[:, :CACHE_LEN], s_new + tabn_ref[2], vt_ref[...])
    m2, den2, acc2 = branch(s_all[:, lo2:] + tab_ref[:, CACHE_LEN:CACHE_LEN + B2_SPAN],
                            s_new + tabn_ref[1], vt_ref[:, lo2:])
    m1, den1, acc1 = branch(s_all[:, lo1:] + tab_ref[:, CACHE_LEN + B2_SPAN:],
                            s_new + tabn_ref[0], vt_ref[:, lo1:])

    top = jnp.maximum(jnp.maximum(m1, m2), m3)
    sc1, sc2, sc3 = den1 * jnp.exp(m1 - top), den2 * jnp.exp(m2 - top), den3 * jnp.exp(m3 - top)
    total = sc1 + sc2 + sc3
    mixed = ((sc1 / total) * (acc1 / den1) + (sc2 / total) * (acc2 / den2)
             + (sc3 / total) * (acc3 / den3)) * sel
    for i in range(DEC_T):
        row = jnp.sum(mixed[N_HEADS * i:N_HEADS * (i + 1)], axis=0, keepdims=True)
        for c in range(HEAD_PAIRS):
            o_ref[c, i:i + 1, :] = row[:, c * LANES:(c + 1) * LANES]


def _attn_sample_body(*refs):
    for _ in _attn_sample_stages(*refs):
        pass


CACHE_KT_ARG, CACHE_VT_ARG = 3, 4
CACHE_SLOTS = 3


def _attn_sample_call(q, k_new, v_new, cache_kt, cache_vt):
    n = q.shape[0]
    assert cache_kt.shape[1:] == (D_ATTN, CACHE_LEN) and q.shape[1] == DEC_T
    tok = pl.BlockSpec((None, DEC_T, D_ATTN), lambda i: (i, 0, 0))
    cache = pl.BlockSpec((None, D_ATTN, CACHE_LEN), lambda i: (i, 0, 0))
    cache_tab, new_tab = _sample_bias_tables()
    operands = (q, k_new, v_new, cache_kt, cache_vt, cache_tab, new_tab, _head_selector())
    in_specs = [tok, tok, tok, cache, cache, _const_spec((QH, SAMPLE_TAB_W)),
                _const_spec((3, QH, CHUNK)), _const_spec((QH, D_ATTN))]
    out_spec = pl.BlockSpec((None, HEAD_PAIRS, DEC_T, LANES), lambda i: (0, 0, i, 0))
    out_shape = jax.ShapeDtypeStruct((1, HEAD_PAIRS, n * DEC_T, LANES), F32)
    return operands, in_specs, out_spec, out_shape


def _attn_sample(q, k_new, v_new, cache_kt, cache_vt):
    operands, in_specs, out_spec, out_shape = _attn_sample_call(q, k_new, v_new, cache_kt, cache_vt)
    return pl.pallas_call(
        _attn_sample_body,
        grid=(q.shape[0],),
        in_specs=in_specs,
        out_specs=out_spec,
        out_shape=out_shape,
        compiler_params=pltpu.CompilerParams(dimension_semantics=("parallel",),
                                             vmem_limit_bytes=VMEM_LIMIT),
        name="attn_sample",
    )(*operands)


HEADS_PER_GROUP = N_HEADS // SSD_GROUPS
PAD_ROWS = SUBLANES
SHORT_SEQ = SUBLANES
SEQS_PER_TILE = CHUNK // SHORT_SEQ


def _ssd_stages(packed, first_chunk, z_ref, xbc_ref, dt_ref, cp_ref, h0_ref, cw_ref, cb_ref, dtb_ref,
                alog_ref, dsk_ref, nw_ref, y_ref, hn_ref, xpad, aux):
    if packed:
        aux[0:CHUNK, :] = cp_ref[...]
        aux[CHUNK:, :] = jnp.zeros((PAD_ROWS, CONV_DIM), F32)
        xpad[0:PAD_ROWS, :] = jnp.zeros((PAD_ROWS, CONV_DIM), F32)
    else:
        @pl.when(first_chunk())
        def _init():
            xpad[0:PAD_ROWS, :] = cp_ref[...]
            aux[...] = h0_ref[...]
    yield
    xpad[PAD_ROWS:, :] = xbc_ref[...]

    step = lax.broadcasted_iota(jnp.int32, (CHUNK, CONV_DIM), 0) % SHORT_SEQ
    conv = cb_ref[...]
    for back in range(CONV_W):
        tap = CONV_W - 1 - back
        rows_back = xpad[pl.ds(PAD_ROWS - back, CHUNK), :]
        if packed and back:
            rows_back = jnp.where(step < back, aux[pl.ds(PAD_ROWS - back, CHUNK), :], rows_back)
        conv = conv + rows_back * cw_ref[tap:tap + 1, :]
    xc = _silu(conv)
    xs = xc[:, :D_SSD]

    lane = lax.broadcasted_iota(jnp.int32, (CHUNK, DT_PAD), 1)
    dt_raw = dt_ref[...] + dtb_ref[...]
    dt = jnp.maximum(dt_raw, 0.0) + jnp.log1p(jnp.exp(-jnp.abs(dt_raw)))
    dt = jnp.where(lane < N_HEADS, dt, 0.0)
    adt = dt * (-jnp.exp(alog_ref[...]))

    ri = lax.broadcasted_iota(jnp.int32, (CHUNK, CHUNK), 0)
    ci = lax.broadcasted_iota(jnp.int32, (CHUNK, CHUNK), 1)
    causal = ri >= ci
    if packed:
        causal = causal & (ri // SHORT_SEQ == ci // SHORT_SEQ)
    exact_dot = functools.partial(jnp.dot, precision=lax.Precision.HIGHEST,
                                  preferred_element_type=F32)
    cs = exact_dot(jnp.where(causal, 1.0, 0.0).astype(F32), adt)
    cs_t = cs.T
    if packed:
        pick_last = ci == (ri // SHORT_SEQ) * SHORT_SEQ + (SHORT_SEQ - 1)
        cs_end = exact_dot(jnp.where(pick_last, 1.0, 0.0).astype(F32), cs)
    else:
        cs_end = cs[CHUNK - 1:CHUNK, :]

    head_of_lane = lax.broadcasted_iota(jnp.int32, (CHUNK, GROUP_W), 1) // HEAD_DIM

    def per_head(cols):
        out = cols[HEADS_PER_GROUP - 1]
        for hl in range(HEADS_PER_GROUP - 2, -1, -1):
            out = jnp.where(head_of_lane == hl, cols[hl], out)
        return out

    y_groups = []
    for g in range(SSD_GROUPS):
        heads = range(g * HEADS_PER_GROUP, (g + 1) * HEADS_PER_GROUP)
        grp = slice(g * GROUP_W, (g + 1) * GROUP_W)
        b_g = xc[:, D_SSD + g * D_STATE:D_SSD + (g + 1) * D_STATE].astype(BF16)
        c_g = xc[:, D_SSD + (SSD_GROUPS + g) * D_STATE:
                 D_SSD + (SSD_GROUPS + g + 1) * D_STATE].astype(BF16)
        cs_cols = [cs[:, h:h + 1] for h in heads]
        xdt_g = xs[:, grp] * per_head([dt[:, h:h + 1] for h in heads])
        xdt_b = xdt_g.astype(BF16)
        gram = _dot_nt(c_g, b_g)

        y_diag = jnp.zeros((CHUNK, GROUP_W), F32)
        for hl, h in enumerate(heads):
            seg = jnp.where(causal, cs_cols[hl] - cs_t[h:h + 1, :], NEG_INF)
            weights = (gram * jnp.exp(seg)).astype(BF16)
            y_diag = y_diag + jnp.where(head_of_lane == hl, _dot(weights, xdt_b), 0.0)

        to_end = per_head([jnp.exp(cs_end[:, h:h + 1] - cs_cols[hl]) for hl, h in enumerate(heads)])
        decayed_t = (xdt_g * to_end).T
        carried = per_head([jnp.exp(col) for col in cs_cols])
        if packed:
            h_prev = h0_ref[:, grp, :]
            wide = _dot_nt(c_g, h_prev.reshape(SEQS_PER_TILE * GROUP_W, D_STATE).astype(BF16))
            y_off = jnp.concatenate(
                [wide[s * SHORT_SEQ:(s + 1) * SHORT_SEQ, s * GROUP_W:(s + 1) * GROUP_W]
                 for s in range(SEQS_PER_TILE)], 0) * carried
            seq_of_step = lax.broadcasted_iota(jnp.int32, (GROUP_W, CHUNK), 1) // SHORT_SEQ
            per_seq = jnp.concatenate([jnp.where(seq_of_step == s, decayed_t, 0.0)
                                       for s in range(SEQS_PER_TILE)], 0).astype(BF16)
            new_states = _dot(per_seq, b_g).reshape(SEQS_PER_TILE, GROUP_W, D_STATE)
            for s in range(SEQS_PER_TILE):
                row = s * SHORT_SEQ
                keep = jnp.concatenate(
                    [jnp.broadcast_to(jnp.exp(cs_end[row:row + 1, h:h + 1]), (HEAD_DIM, D_STATE))
                     for h in heads], 0)
                hn_ref[s, grp, :] = h_prev[s] * keep + new_states[s]
        else:
            h_prev = aux[grp, :]
            y_off = _dot_nt(c_g, h_prev.astype(BF16)) * carried
            keep = jnp.concatenate(
                [jnp.broadcast_to(jnp.exp(cs_end[:, h:h + 1]), (HEAD_DIM, D_STATE)) for h in heads], 0)
            aux[grp, :] = h_prev * keep + _dot(decayed_t.astype(BF16), b_g)
        y_groups.append(y_diag + y_off)

    y = jnp.concatenate(y_groups, axis=-1) + dsk_ref[...] * xs
    y = y * _silu(z_ref[...])
    normed = []
    for g in range(SSD_GROUPS):
        yg = y[:, g * GROUP_W:(g + 1) * GROUP_W]
        normed.append(yg * lax.rsqrt(jnp.mean(yg * yg, -1, keepdims=True) + LN_EPS))
    y_ref[...] = jnp.concatenate(normed, axis=-1) * nw_ref[...]

    if not packed:
        hn_ref[...] = aux[...]
        xpad[0:PAD_ROWS, :] = xpad[CHUNK:, :]


def _ssd_body(*args):
    for _ in _ssd_stages(*args):
        pass


N_SSD_INPUTS = 11


def _ssd_call(z, xbc, dt, conv_prev, h0, cw, cb, dtb, alog, dsk, nw, where=None):
    n, seq, _ = z.shape
    packed = seq == SHORT_SEQ
    if packed:
        assert n % SEQS_PER_TILE == 0 and where is None
        grid = (n // SEQS_PER_TILE, 1)
        fold = lambda t: t.reshape(grid[0], CHUNK, t.shape[-1])
        z, xbc, dt, conv_prev = fold(z), fold(xbc), fold(dt), fold(conv_prev)
        history_shape, state_shape = (None, CHUNK, CONV_DIM), (SEQS_PER_TILE, D_SSD, D_STATE)
        aux = pltpu.VMEM((CHUNK + PAD_ROWS, CONV_DIM), F32)
    else:
        assert seq % CHUNK == 0
        grid = (n, seq // CHUNK)
        history_shape, state_shape = (None, PAD_ROWS, CONV_DIM), (None, D_SSD, D_STATE)
        aux = pltpu.VMEM((D_SSD, D_STATE), F32)
    where = where or (lambda s, c: (s, c))
    tile = lambda width: pl.BlockSpec((None, CHUNK, width), lambda *g: where(*g) + (0,))
    per_seq = lambda shape: pl.BlockSpec(shape, lambda *g: (where(*g)[0], 0, 0))
    history, state = per_seq(history_shape), per_seq(state_shape)
    operands = (z, xbc, dt, conv_prev, h0, cw, cb, dtb, alog, dsk, nw)
    in_specs = [tile(D_SSD), tile(CONV_DIM), tile(DT_PAD), history, state,
                _const_spec((CONV_W, CONV_DIM)), _const_spec((1, CONV_DIM)),
                _const_spec((1, DT_PAD)), _const_spec((1, DT_PAD)),
                _const_spec((1, D_SSD)), _const_spec((1, D_SSD))]
    out_specs = [tile(D_SSD), state]
    out_shape = [jax.ShapeDtypeStruct(z.shape, F32), jax.ShapeDtypeStruct((n, D_SSD, D_STATE), F32)]
    scratch = [pltpu.VMEM((PAD_ROWS + CHUNK, CONV_DIM), F32), aux]
    return packed, grid, operands, in_specs, out_specs, out_shape, scratch


def _ssd(z, *rest):
    packed, grid, operands, in_specs, out_specs, out_shape, scratch = _ssd_call(z, *rest)
    y, h_new = pl.pallas_call(
        functools.partial(_ssd_body, packed, lambda: pl.program_id(1) == 0),
        grid=grid,
        in_specs=in_specs,
        out_specs=out_specs,
        out_shape=out_shape,
        scratch_shapes=scratch,
        compiler_params=pltpu.CompilerParams(dimension_semantics=("parallel", "arbitrary"),
                                             vmem_limit_bytes=VMEM_LIMIT),
        name="ssd",
    )(*operands)
    return y.reshape(z.shape), h_new


def _ssd_with_sample_attn(ssd_args, attn_args):
    z = ssd_args[0]
    nchunks = z.shape[1] // CHUNK
    steps = z.shape[0] * nchunks
    assert steps == attn_args[0].shape[0]
    where = lambda i: (i // nchunks, i % nchunks)
    _, _, ssd_ops, ssd_in, ssd_out, ssd_shape, scratch = _ssd_call(*ssd_args, where=where)
    attn_ops, attn_in, attn_out, attn_shape = _attn_sample_call(*attn_args)
    caches = (CACHE_KT_ARG, CACHE_VT_ARG)
    for arg in caches:
        attn_in[arg] = pl.BlockSpec(memory_space=pl.ANY)
    ring = pltpu.VMEM((CACHE_SLOTS, D_ATTN, CACHE_LEN), F32)

    def body(*refs):
        ssd_in_refs, refs = refs[:N_SSD_INPUTS], refs[N_SSD_INPUTS:]
        attn_in_refs, refs = list(refs[:len(attn_ops)]), refs[len(attn_ops):]
        y_ref, hn_ref, o_ref, xpad, aux, kt_ring, vt_ring, sems = refs
        step = pl.program_id(0)

        def fetch(s):
            slot = s % CACHE_SLOTS
            return [pltpu.make_async_copy(attn_in_refs[arg].at[s], buf.at[slot], sems.at[j, slot])
                    for j, (arg, buf) in enumerate(zip(caches, (kt_ring, vt_ring)))]

        @pl.when(step == 0)
        def _prime():
            for s in range(CACHE_SLOTS - 1):
                for copy in fetch(s):
                    copy.start()

        @pl.when(step + (CACHE_SLOTS - 1) < steps)
        def _ahead():
            for copy in fetch(step + (CACHE_SLOTS - 1)):
                copy.start()

        for copy in fetch(step):
            copy.wait()
        for arg, buf in zip(caches, (kt_ring, vt_ring)):
            attn_in_refs[arg] = buf.at[step % CACHE_SLOTS]

        ssd = _ssd_stages(False, lambda: step % nchunks == 0, *ssd_in_refs, y_ref, hn_ref, xpad, aux)
        attn = _attn_sample_stages(*attn_in_refs, o_ref)
        for stage in (ssd, attn, ssd, attn):
            next(stage, None)

    y, h_new, attn = pl.pallas_call(
        body,
        grid=(steps,),
        in_specs=ssd_in + attn_in,
        out_specs=ssd_out + [attn_out],
        out_shape=ssd_shape + [attn_shape],
        scratch_shapes=scratch + [ring, ring, pltpu.SemaphoreType.DMA((len(caches), CACHE_SLOTS))],
        compiler_params=pltpu.CompilerParams(dimension_semantics=("arbitrary",),
                                             vmem_limit_bytes=VMEM_LIMIT),
        name="ssd_attn_sample",
    )(*ssd_ops, *attn_ops)
    return y, h_new, attn


def _row(v, width=None):
    v = v.reshape(1, -1).astype(F32)
    if width is not None and v.shape[1] < width:
        v = jnp.pad(v, ((0, 0), (0, width - v.shape[1])))
    return v


def kernel(x_prompt, x_sample, cache_k, cache_v, state_conv, state_ssm, p_prompt, p_sample,
           ln_in_g, ln_in_b, w_in, conv_w, conv_b, dt_bias, a_log, d_skip, ssd_norm_w, w_out,
           ln1_g, ln1_b, w_up, w_down, ln2_g, ln2_b, w_gate, w_ple, ln3_g, ln3_b):
    depth = w_in.shape[0]
    assert depth == 1, "single-layer step"
    alpha = (2 * depth) ** 0.25
    bsz, seq, _ = x_prompt.shape
    nd, dec_t, _ = x_sample.shape
    lyr = 0

    w_proj = w_in[lyr][:, :D_PROJ].astype(BF16)
    w_dt = jnp.pad(w_in[lyr][:, D_PROJ:], ((0, 0), (0, DT_PAD - N_HEADS))).astype(BF16)
    gin, bin_ = _row(ln_in_g), _row(ln_in_b)
    ssd_params = (conv_w[lyr].astype(F32), _row(conv_b[lyr]), _row(dt_bias[lyr], DT_PAD),
                  _row(a_log[lyr], DT_PAD), _row(jnp.repeat(d_skip[lyr], HEAD_DIM)),
                  _row(ssd_norm_w[lyr]))
    post_params = (gin, bin_, w_out[lyr].astype(BF16), _row(ln1_g[lyr]), _row(ln1_b[lyr]),
                   w_up[lyr].astype(BF16), w_down[lyr].astype(BF16), _row(ln2_g[lyr]),
                   _row(ln2_b[lyr]), w_gate[lyr].astype(BF16), w_ple[lyr].astype(BF16),
                   _row(ln3_g[lyr]), _row(ln3_b[lyr]))

    q, k, v, k_t, v_t, z, xbc, dt = _in_proj(x_prompt, gin, bin_, w_proj, w_dt, head_major=True, tm=1024)
    n_tok = nd * dec_t
    flat = lambda t: t.reshape(1, n_tok, t.shape[-1])
    toks = lambda t: t.reshape(nd, dec_t, t.shape[-1])
    qs, ks, vs, zs, xbcs, dts = _in_proj(flat(x_sample), gin, bin_, w_proj, w_dt,
                                       head_major=False, tm=512)

    attn = _attn_prompt(q, k, v)
    transposed = lambda c: jnp.transpose(c, (0, 2, 3, 1)).reshape(nd, D_ATTN, CACHE_LEN)
    ssd_args = (z, xbc, dt, jnp.zeros((bsz, PAD_ROWS, CONV_DIM), F32),
                jnp.zeros((bsz, D_SSD, D_STATE), F32)) + ssd_params
    attn_args = (toks(qs), toks(ks), toks(vs), transposed(cache_k[lyr]), transposed(cache_v[lyr]))
    if bsz * (seq // CHUNK) == nd:
        ssd_y, ssm_p, attn_s = _ssd_with_sample_attn(ssd_args, attn_args)
    else:
        ssd_y, ssm_p = _ssd(*ssd_args)
        attn_s = _attn_sample(*attn_args)
    conv_prev = jnp.pad(state_conv[lyr].astype(F32), ((0, 0), (PAD_ROWS - (CONV_W - 1), 0), (0, 0)))
    ssd_s, ssm_s = _ssd(toks(zs), toks(xbcs), toks(dts), conv_prev,
                        state_ssm[lyr].reshape(nd, D_SSD, D_STATE).astype(F32), *ssd_params)

    y_prompt = _post(x_prompt, attn, ssd_y, p_prompt[lyr], *post_params, alpha=alpha, tm=512)
    y_sample = _post(flat(x_sample), attn_s, flat(ssd_s), flat(p_sample[lyr]), *post_params,
                     alpha=alpha, tm=256)

    from_t = lambda t: jnp.transpose(t.reshape(1, bsz, N_HEADS, HEAD_DIM, seq), (0, 1, 4, 2, 3))
    heads = lambda t: t.reshape(1, nd, dec_t, N_HEADS, HEAD_DIM)
    tail = lambda t: t[None, :, -(CONV_W - 1):, :]
    state = lambda t, n: t.reshape(1, n, N_HEADS, HEAD_DIM, D_STATE)
    return (y_prompt, y_sample.reshape(nd, dec_t, D_MODEL), from_t(k_t), from_t(v_t),
            heads(ks), heads(vs), tail(xbc), tail(toks(xbcs)), state(ssm_p, bsz), state(ssm_s, nd))
```

```python
import functools

import numpy as np
import jax
import jax.numpy as jnp
from jax import lax
from jax.experimental import pallas as pl
from jax.experimental.pallas import tpu as pltpu

F32 = jnp.float32
BF16 = jnp.bfloat16

D_MODEL = 1024
HEAD_DIM = 64
D_ATTN = 512
D_SSD = 512
N_HEADS = 8
SSD_GROUPS = 2
GROUP_W = D_SSD // SSD_GROUPS
D_STATE = 128
CONV_W = 4
CONV_DIM = D_SSD + 2 * SSD_GROUPS * D_STATE
CHUNK = 128
D_FF = 4096
D_PLE = 256
LN_EPS = 1e-5
WINDOW_STEPS = 128
DILATIONS = (1, 4, 16)
PHASES = DILATIONS[2]
MID_PHASES = DILATIONS[1]
CACHE_LEN = 2048
DT_PAD = 128
D_PROJ = 3 * D_ATTN + D_SSD + CONV_DIM
SUBLANES = 8
LANES = 128
VMEM_LIMIT = 56 * 1024 * 1024
NEG_INF = float("-inf")


def _slopes():
    return np.array([2.0 ** (-8.0 * (h + 1) / N_HEADS) for h in range(N_HEADS)], dtype=np.float64)


def _layer_norm(x, g, b):
    mu = jnp.mean(x, -1, keepdims=True)
    xc = x - mu
    var = jnp.mean(xc * xc, -1, keepdims=True)
    return xc * lax.rsqrt(var + LN_EPS) * g + b


def _silu(x):
    return x * (1.0 / (1.0 + jnp.exp(-x)))


def _dot(a, b):
    return jnp.dot(a, b, preferred_element_type=F32)


def _dot_nt(a, b):
    return lax.dot_general(a, b, (((1,), (1,)), ((), ())), preferred_element_type=F32)


def _const_spec(shape):
    nd = len(shape)
    return pl.BlockSpec(shape, lambda *_: (0,) * nd, pipeline_mode=pl.Buffered(1))


HEAD_PAIRS = N_HEADS // 2


def _inproj_body(head_major, x_ref, g_ref, b_ref, w_ref, wdt_ref, *out_refs):
    tm = x_ref.shape[0]
    parts = [slice(i * tm // 2, (i + 1) * tm // 2) for i in range(2)]
    h = [_layer_norm(x_ref[r, :], g_ref[...], b_ref[...]).astype(BF16) for r in parts]
    proj = lambda lo, width: [_dot(v, w_ref[:, lo:lo + width]) for v in h]
    if head_major:
        q_ref, k_ref, v_ref, kt_ref, vt_ref, z_ref, xbc_ref, dt_ref = out_refs
        for idx, (ref, t_ref) in enumerate(((q_ref, None), (k_ref, kt_ref), (v_ref, vt_ref))):
            for r, res in zip(parts, proj(idx * D_ATTN, D_ATTN)):
                for c in range(HEAD_PAIRS):
                    ref[c, r, :] = res[:, c * LANES:(c + 1) * LANES]
                if t_ref is not None:
                    t_ref[:, r] = res.T
    else:
        q_ref, k_ref, v_ref, z_ref, xbc_ref, dt_ref = out_refs
        for idx, ref in enumerate((q_ref, k_ref, v_ref)):
            for r, res in zip(parts, proj(idx * D_ATTN, D_ATTN)):
                ref[r, :] = res
    lo = 3 * D_ATTN
    for ref, width in ((z_ref, D_SSD), (xbc_ref, CONV_DIM)):
        for r, res in zip(parts, proj(lo, width)):
            ref[r, :] = res
        lo += width
    for r, v in zip(parts, h):
        dt_ref[r, :] = _dot(v, wdt_ref[...])


def _in_proj(x, g, b, w, w_dt, *, head_major, tm):
    bsz, seq, _ = x.shape
    row = lambda width: pl.BlockSpec((None, tm, width), lambda i, j: (i, j, 0))
    row_shape = lambda width: jax.ShapeDtypeStruct((bsz, seq, width), F32)
    if head_major:
        slab = pl.BlockSpec((None, HEAD_PAIRS, tm, LANES), lambda i, j: (i, 0, j, 0))
        slab_shape = jax.ShapeDtypeStruct((bsz, HEAD_PAIRS, seq, LANES), F32)
        tr = pl.BlockSpec((None, D_ATTN, tm), lambda i, j: (i, 0, j))
        tr_shape = jax.ShapeDtypeStruct((bsz, D_ATTN, seq), F32)
        qkv_specs, qkv_shapes = [slab, slab, slab, tr, tr], [slab_shape] * 3 + [tr_shape] * 2
    else:
        qkv_specs, qkv_shapes = [row(D_ATTN)] * 3, [row_shape(D_ATTN)] * 3
    rest = (D_SSD, CONV_DIM, DT_PAD)
    return pl.pallas_call(
        functools.partial(_inproj_body, head_major),
        grid=(bsz, seq // tm),
        in_specs=[row(D_MODEL), _const_spec((1, D_MODEL)), _const_spec((1, D_MODEL)),
                  _const_spec((D_MODEL, D_PROJ)), _const_spec((D_MODEL, DT_PAD))],
        out_specs=qkv_specs + [row(wd) for wd in rest],
        out_shape=qkv_shapes + [row_shape(wd) for wd in rest],
        compiler_params=pltpu.CompilerParams(dimension_semantics=("parallel", "parallel"),
                                             vmem_limit_bytes=VMEM_LIMIT),
        name="in_proj",
    )(x, g, b, w, w_dt)


FF_CHUNK = 1024
POST_STREAMS = 2


def _post_body(alpha, x_ref, attn_ref, ssd_ref, pe_ref, gin_ref, bin_ref, wout_ref, g1_ref, b1_ref,
               wup_ref, wdown_ref, g2_ref, b2_ref, wgate_ref, wple_ref, g3_ref, b3_ref, y_ref):
    tm = x_ref.shape[0]
    parts = [slice(i * tm // POST_STREAMS, (i + 1) * tm // POST_STREAMS) for i in range(POST_STREAMS)]
    each = lambda fn, *lists: [fn(*args) for args in zip(*lists)]
    xn = [_layer_norm(x_ref[r, :], gin_ref[...], bin_ref[...]) for r in parts]
    mixed = [jnp.concatenate([attn_ref[c, r, :] for c in range(HEAD_PAIRS)] + [ssd_ref[r, :]],
                             axis=-1).astype(BF16) for r in parts]
    proj = [_dot(m, wout_ref[...]) for m in mixed]
    h = each(lambda x, p: _layer_norm(alpha * x + p, g1_ref[...], b1_ref[...]), xn, proj)
    hb = [v.astype(BF16) for v in h]
    u = [None] * POST_STREAMS
    for c in range(D_FF // FF_CHUNK):
        cols = slice(c * FF_CHUNK, (c + 1) * FF_CHUNK)
        a = [jnp.maximum(_dot(v, wup_ref[:, cols]), 0.0) for v in hb]
        part = [_dot((v * v).astype(BF16), wdown_ref[cols, :]) for v in a]
        u = part if c == 0 else each(lambda s, p: s + p, u, part)
    h = each(lambda v, w: _layer_norm(alpha * v + w, g2_ref[...], b2_ref[...]), h, u)
    gate = [1.0 / (1.0 + jnp.exp(-_dot(v.astype(BF16), wgate_ref[...]))) for v in h]
    emb = [_dot(pe_ref[r, :].astype(BF16), wple_ref[...]) for r in parts]
    for r, v, g, e in zip(parts, h, gate, emb):
        y_ref[r, :] = _layer_norm(alpha * v + g * e, g3_ref[...], b3_ref[...])


def _post(x, attn, ssd, pe, gin, bin_, wout, g1, b1, wup, wdown, g2, b2, wgate, wple, g3, b3, *,
          alpha, tm):
    bsz, seq, _ = x.shape
    row = lambda width: pl.BlockSpec((None, tm, width), lambda i, j: (i, j, 0))
    slab = pl.BlockSpec((None, HEAD_PAIRS, tm, LANES), lambda i, j: (i, 0, j, 0))
    vec = _const_spec((1, D_MODEL))
    return pl.pallas_call(
        functools.partial(_post_body, alpha),
        grid=(bsz, seq // tm),
        in_specs=[row(D_MODEL), slab, row(D_SSD), row(D_PLE), vec, vec,
                  _const_spec((D_MODEL, D_MODEL)), vec, vec,
                  _const_spec((D_MODEL, D_FF)), _const_spec((D_FF, D_MODEL)), vec, vec,
                  _const_spec((D_MODEL, D_MODEL)), _const_spec((D_PLE, D_MODEL)), vec, vec],
        out_specs=row(D_MODEL),
        out_shape=jax.ShapeDtypeStruct((bsz, seq, D_MODEL), F32),
        compiler_params=pltpu.CompilerParams(dimension_semantics=("parallel", "parallel"),
                                             vmem_limit_bytes=VMEM_LIMIT),
        name="post",
    )(x, attn, ssd, pe, gin, bin_, wout, g1, b1, wup, wdown, g2, b2, wgate, wple, g3, b3)


U_ROWS = CACHE_LEN // PHASES
B2_ROWS = U_ROWS // (PHASES // MID_PHASES)
TAB3_LO, TAB2_LO, TAB1_LO, TAB_W = 0, WINDOW_STEPS, 3 * WINDOW_STEPS, 5 * WINDOW_STEPS
WAVE = 4
B1_IN_FLIGHT = 8


def _prompt_bias_tables():
    slopes = _slopes()
    u = np.arange(U_ROWS)
    d3 = (u[:, None] - u[None, :]).astype(np.float64)
    k4, ul4 = np.meshgrid(np.arange(PHASES // MID_PHASES), np.arange(B2_ROWS), indexing="ij")
    j2 = (PHASES // MID_PHASES * ul4 + k4).reshape(-1)
    d2 = j2[:, None] - np.concatenate([j2 - WINDOW_STEPS, j2])[None, :]
    p16, ul16 = np.meshgrid(np.arange(PHASES), np.arange(SUBLANES), indexing="ij")
    t1 = (PHASES * ul16 + p16).reshape(-1)
    d1 = t1[:, None] - (np.arange(2 * WINDOW_STEPS) - WINDOW_STEPS)[None, :]
    tabs = []
    for dist, dil in zip((d3, d2, d1), reversed(DILATIONS)):
        valid = (dist >= 0) & (dist <= WINDOW_STEPS)
        per_head = [np.where(valid, -slopes[h] * dist * dil, NEG_INF) for h in range(N_HEADS)]
        tabs.append(np.stack(per_head))
    tab = np.concatenate(tabs, axis=-1)
    return jnp.asarray(tab.reshape(HEAD_PAIRS, 2 * U_ROWS, TAB_W), dtype=F32)


def _attn_blocks(blocks, first_head):
    half = U_ROWS
    state = [dict() for _ in blocks]

    def scores(i):
        q2, k, _, bias = blocks[i]
        state[i]["s"] = _dot_nt(q2, k) + bias

    def top(i):
        state[i]["m"] = jnp.max(state[i]["s"], -1, keepdims=True)

    def weights(i):
        e = jnp.exp(state[i].pop("s") - state[i]["m"])
        state[i]["den"] = jnp.sum(e, -1, keepdims=True)
        state[i]["e"] = e.astype(BF16)

    def values(i):
        state[i]["pv"] = _dot(state[i].pop("e"), blocks[i][2])

    def finish(i):
        pv, m, den = state[i]["pv"], state[i]["m"], state[i]["den"]
        o = jnp.where(first_head, pv[:half], pv[half:])
        m2 = jnp.where(first_head, m[:half], m[half:])
        den2 = jnp.where(first_head, den[:half], den[half:])
        state[i] = (o / den2, m2 + jnp.log(den2))

    stages = (scores, top, weights, values, finish)
    groups = [range(j, min(j + WAVE, len(blocks))) for j in range(0, len(blocks), WAVE)]
    for t in range(len(groups) + len(stages) - 1):
        for g, members in enumerate(groups):
            if 0 <= t - g < len(stages):
                for i in members:
                    stages[t - g](i)
    return state


def _attn_prompt_body(q_ref, k_ref, v_ref, tab_ref, o_ref, qs, q0p, q1p, kp, vp, kn, vn, quarter,
                      o_scr, l_scr):
    lane = lax.broadcasted_iota(jnp.int32, (U_ROWS, LANES), 1)
    first_head = lane < HEAD_DIM
    scale = HEAD_DIM ** -0.5

    kn[...] = k_ref[...].astype(BF16)
    vn[...] = v_ref[...].astype(BF16)

    def split(src_ref, emit):
        for r in range(MID_PHASES):
            quarter[r] = src_ref[pl.ds(r, CACHE_LEN // MID_PHASES, stride=MID_PHASES), :]
        for r in range(MID_PHASES):
            for kk in range(PHASES // MID_PHASES):
                emit(MID_PHASES * kk + r,
                     quarter[r, pl.ds(kk, U_ROWS, stride=PHASES // MID_PHASES), :])

    def emit_q(p, tile):
        tile = tile * scale
        qs[p] = tile
        q0p[p] = jnp.where(first_head, tile, 0.0).astype(BF16)
        q1p[p] = jnp.where(first_head, 0.0, tile).astype(BF16)

    def emit_to(dst):
        def emit(p, tile):
            dst[p] = tile.astype(BF16)
        return emit

    split(q_ref, emit_q)
    split(k_ref, emit_to(kp))
    split(v_ref, emit_to(vp))

    bias = tab_ref[:, TAB3_LO:TAB3_LO + U_ROWS]
    outs = _attn_blocks([(jnp.concatenate([q0p[p], q1p[p]], 0), kp[p], vp[p], bias)
                         for p in range(PHASES)], first_head)
    for p, (o, l) in enumerate(outs):
        o_scr[0, p] = o
        l_scr[0, p] = l

    blocks, where = [], []
    for r in range(MID_PHASES):
        tiles = [MID_PHASES * kk + r for kk in range(PHASES // MID_PHASES)]
        gather = lambda src, rr, tiles=tiles: [src[p, rr, :] for p in tiles]
        for n in range(U_ROWS // B2_ROWS):
            rows = pl.ds(B2_ROWS * n, B2_ROWS)
            q2 = jnp.concatenate(gather(q0p, rows) + gather(q1p, rows), 0)
            if n == 0:
                k = jnp.concatenate(gather(kp, rows), 0)
                v = jnp.concatenate(gather(vp, rows), 0)
                bias = tab_ref[:, TAB2_LO + WINDOW_STEPS:TAB2_LO + 2 * WINDOW_STEPS]
            else:
                prev = pl.ds(B2_ROWS * (n - 1), B2_ROWS)
                k = jnp.concatenate(gather(kp, prev) + gather(kp, rows), 0)
                v = jnp.concatenate(gather(vp, prev) + gather(vp, rows), 0)
                bias = tab_ref[:, TAB2_LO:TAB2_LO + 2 * WINDOW_STEPS]
            blocks.append((q2, k, v, bias))
            where.append((tiles, rows))
    for (tiles, rows), (o, l) in zip(where, _attn_blocks(blocks, first_head)):
        for kk, p in enumerate(tiles):
            o_scr[1, p, rows, :] = o[B2_ROWS * kk:B2_ROWS * (kk + 1)]
            l_scr[1, p, rows, :] = l[B2_ROWS * kk:B2_ROWS * (kk + 1)]

    def branch1_blocks(ns):
        blocks = []
        for n in ns:
            rows = pl.ds(n * SUBLANES, SUBLANES)
            q = jnp.concatenate([qs[p, rows, :] for p in range(PHASES)], 0)
            q2 = jnp.concatenate([jnp.where(first_head, q, 0.0), jnp.where(first_head, 0.0, q)],
                                 0).astype(BF16)
            if n > 0:
                keys = pl.ds((n - 1) * U_ROWS, 2 * U_ROWS)
                bias = tab_ref[:, TAB1_LO:TAB1_LO + 2 * WINDOW_STEPS]
            else:
                keys = pl.ds(0, U_ROWS)
                bias = tab_ref[:, TAB1_LO + WINDOW_STEPS:TAB1_LO + 2 * WINDOW_STEPS]
            blocks.append((q2, kn[keys, :], vn[keys, :], bias))
        for n, (o, l) in zip(ns, _attn_blocks(blocks, first_head)):
            rows = pl.ds(n * SUBLANES, SUBLANES)
            for p in range(PHASES):
                o_scr[2, p, rows, :] = o[SUBLANES * p:SUBLANES * (p + 1)]
                l_scr[2, p, rows, :] = l[SUBLANES * p:SUBLANES * (p + 1)]

    n_blocks = U_ROWS // SUBLANES
    for first in range(0, n_blocks, B1_IN_FLIGHT):
        branch1_blocks(range(first, min(first + B1_IN_FLIGHT, n_blocks)))

    def phase_rows(p):
        return pl.ds(p, U_ROWS, stride=PHASES)

    for p in range(PHASES):
        l3, l2, l1 = l_scr[0, p], l_scr[1, p], l_scr[2, p]
        top = jnp.maximum(jnp.maximum(l3, l2), l1)
        e3, e2, e1 = jnp.exp(l3 - top), jnp.exp(l2 - top), jnp.exp(l1 - top)
        total = e3 + e2 + e1
        o_ref[phase_rows(p), :] = (e3 * o_scr[0, p] + e2 * o_scr[1, p] + e1 * o_scr[2, p]) / total


def _attn_prompt(q, k, v):
    bsz, _, seq, _ = k.shape
    assert seq == CACHE_LEN, "prompt attention is laid out for a 2048-token prompt"
    slab = pl.BlockSpec((None, None, seq, LANES), lambda b, c: (b, c, 0, 0))
    tile = (PHASES, U_ROWS, LANES)
    return pl.pallas_call(
        _attn_prompt_body,
        grid=(bsz, HEAD_PAIRS),
        in_specs=[slab, slab, slab,
                  pl.BlockSpec((None, 2 * U_ROWS, TAB_W), lambda b, c: (c, 0, 0))],
        out_specs=slab,
        out_shape=jax.ShapeDtypeStruct((bsz, HEAD_PAIRS, seq, LANES), F32),
        scratch_shapes=[pltpu.VMEM(tile, F32)] + [pltpu.VMEM(tile, BF16)] * 4
                       + [pltpu.VMEM((seq, LANES), BF16)] * 2
                       + [pltpu.VMEM((4, seq // 4, LANES), F32),
                          pltpu.VMEM((3,) + tile, F32), pltpu.VMEM((3,) + tile, F32)],
        compiler_params=pltpu.CompilerParams(dimension_semantics=("parallel", "parallel"),
                                             vmem_limit_bytes=VMEM_LIMIT),
        name="attn_prompt",
    )(q, k, v, _prompt_bias_tables())


DEC_T = 8
QH = DEC_T * N_HEADS
B2_SPAN, B1_SPAN = 512, 128
SAMPLE_TAB_W = CACHE_LEN + B2_SPAN + B1_SPAN


def _sample_bias_tables():
    slopes = _slopes()
    i = np.repeat(np.arange(DEC_T), N_HEADS)[:, None]
    sl = np.tile(slopes, DEC_T)[:, None]

    def cache_bias(span, dil):
        t = CACHE_LEN - span + np.arange(span)[None, :]
        dist = CACHE_LEN + i - t
        valid = (dist % dil == 0) & (dist <= WINDOW_STEPS * dil)
        return np.where(valid, -sl * dist, NEG_INF)

    cache_tab = np.concatenate([cache_bias(CACHE_LEN, 16), cache_bias(B2_SPAN, 4),
                                cache_bias(B1_SPAN, 1)], axis=1)
    c = np.arange(CHUNK)[None, :]
    dn = i - c
    is_new = c < DEC_T
    n1 = np.where(is_new & (dn >= 0), -sl * dn, NEG_INF)
    n2 = np.where(is_new & ((dn == 0) | (dn == 4)), -sl * dn, NEG_INF)
    n3 = np.where(is_new & (dn == 0), 0.0, NEG_INF)
    return jnp.asarray(cache_tab, dtype=F32), jnp.asarray(np.stack([n1, n2, n3]), dtype=F32)


def _head_selector():
    h_row = np.tile(np.arange(N_HEADS), DEC_T)[:, None]
    h_col = (np.arange(D_ATTN) // HEAD_DIM)[None, :]
    return jnp.asarray((h_row == h_col).astype(np.float32))


def _attn_sample_stages(q_ref, kn_ref, vn_ref, kt_ref, vt_ref, tab_ref, tabn_ref, sel_ref, o_ref):
    sel = sel_ref[...]
    q = q_ref[...] * (HEAD_DIM ** -0.5)
    q_rows = jnp.concatenate([jnp.broadcast_to(q[i:i + 1, :], (N_HEADS, D_ATTN))
                              for i in range(DEC_T)], 0)
    qall = (q_rows * sel).astype(BF16)
    pad = jnp.zeros((CHUNK - DEC_T, D_ATTN), F32)
    kn = jnp.concatenate([kn_ref[...], pad], 0).astype(BF16)
    vn = jnp.concatenate([vn_ref[...], pad], 0).astype(BF16)

    s_all = _dot(qall, kt_ref[...].astype(BF16))
    s_new = _dot_nt(qall, kn)
    yield
    lo2, lo1 = CACHE_LEN - B2_SPAN, CACHE_LEN - B1_SPAN

    def branch(s_cache, s_fresh, v_t):
        m = jnp.maximum(jnp.max(s_cache, -1, keepdims=True), jnp.max(s_fresh, -1, keepdims=True))
        e_c, e_f = jnp.exp(s_cache - m), jnp.exp(s_fresh - m)
        den = jnp.sum(e_c, -1, keepdims=True) + jnp.sum(e_f, -1, keepdims=True)
        acc = _dot_nt(e_c.astype(BF16), v_t.astype(BF16)) + _dot(e_f.astype(BF16), vn)
        return m, den, acc

    m3, den3, acc3 = branch(s_all + tab_ref[:, :CACHE_LEN], s_new + tabn_ref[2], vt_ref[...])
    m2, den2, acc2 = branch(s_all[:, lo2:] + tab_ref[:, CACHE_LEN:CACHE_LEN + B2_SPAN],
                            s_new + tabn_ref[1], vt_ref[:, lo2:])
    m1, den1, acc1 = branch(s_all[:, lo1:] + tab_ref[:, CACHE_LEN + B2_SPAN:],
                            s_new + tabn_ref[0], vt_ref[:, lo1:])

    top = jnp.maximum(jnp.maximum(m1, m2), m3)
    sc1, sc2, sc3 = den1 * jnp.exp(m1 - top), den2 * jnp.exp(m2 - top), den3 * jnp.exp(m3 - top)
    total = sc1 + sc2 + sc3
    mixed = ((sc1 / total) * (acc1 / den1) + (sc2 / total) * (acc2 / den2)
             + (sc3 / total) * (acc3 / den3)) * sel
    for i in range(DEC_T):
        row = jnp.sum(mixed[N_HEADS * i:N_HEADS * (i + 1)], axis=0, keepdims=True)
        for c in range(HEAD_PAIRS):
            o_ref[c, i:i + 1, :] = row[:, c * LANES:(c + 1) * LANES]


def _attn_sample_body(*refs):
    for _ in _attn_sample_stages(*refs):
        pass


CACHE_KT_ARG, CACHE_VT_ARG = 3, 4
CACHE_SLOTS = 3


def _attn_sample_call(q, k_new, v_new, cache_kt, cache_vt):
    n = q.shape[0]
    assert cache_kt.shape[1:] == (D_ATTN, CACHE_LEN) and q.shape[1] == DEC_T
    tok = pl.BlockSpec((None, DEC_T, D_ATTN), lambda i: (i, 0, 0))
    cache = pl.BlockSpec((None, D_ATTN, CACHE_LEN), lambda i: (i, 0, 0))
    cache_tab, new_tab = _sample_bias_tables()
    operands = (q, k_new, v_new, cache_kt, cache_vt, cache_tab, new_tab, _head_selector())
    in_specs = [tok, tok, tok, cache, cache, _const_spec((QH, SAMPLE_TAB_W)),
                _const_spec((3, QH, CHUNK)), _const_spec((QH, D_ATTN))]
    out_spec = pl.BlockSpec((None, HEAD_PAIRS, DEC_T, LANES), lambda i: (0, 0, i, 0))
    out_shape = jax.ShapeDtypeStruct((1, HEAD_PAIRS, n * DEC_T, LANES), F32)
    return operands, in_specs, out_spec, out_shape


def _attn_sample(q, k_new, v_new, cache_kt, cache_vt):
    operands, in_specs, out_spec, out_shape = _attn_sample_call(q, k_new, v_new, cache_kt, cache_vt)
    return pl.pallas_call(
        _attn_sample_body,
        grid=(q.shape[0],),
        in_specs=in_specs,
        out_specs=out_spec,
        out_shape=out_shape,
        compiler_params=pltpu.CompilerParams(dimension_semantics=("parallel",),
                                             vmem_limit_bytes=VMEM_LIMIT),
        name="attn_sample",
    )(*operands)


HEADS_PER_GROUP = N_HEADS // SSD_GROUPS
PAD_ROWS = SUBLANES
SHORT_SEQ = SUBLANES
SEQS_PER_TILE = CHUNK // SHORT_SEQ


def _ssd_stages(packed, first_chunk, z_ref, xbc_ref, dt_ref, cp_ref, h0_ref, cw_ref, cb_ref, dtb_ref,
                alog_ref, dsk_ref, nw_ref, y_ref, hn_ref, xpad, aux):
    if packed:
        aux[0:CHUNK, :] = cp_ref[...]
        aux[CHUNK:, :] = jnp.zeros((PAD_ROWS, CONV_DIM), F32)
        xpad[0:PAD_ROWS, :] = jnp.zeros((PAD_ROWS, CONV_DIM), F32)
    else:
        @pl.when(first_chunk())
        def _init():
            xpad[0:PAD_ROWS, :] = cp_ref[...]
            aux[...] = h0_ref[...]
    yield
    xpad[PAD_ROWS:, :] = xbc_ref[...]

    step = lax.broadcasted_iota(jnp.int32, (CHUNK, CONV_DIM), 0) % SHORT_SEQ
    conv = cb_ref[...]
    for back in range(CONV_W):
        tap = CONV_W - 1 - back
        rows_back = xpad[pl.ds(PAD_ROWS - back, CHUNK), :]
        if packed and back:
            rows_back = jnp.where(step < back, aux[pl.ds(PAD_ROWS - back, CHUNK), :], rows_back)
        conv = conv + rows_back * cw_ref[tap:tap + 1, :]
    xc = _silu(conv)
    xs = xc[:, :D_SSD]

    lane = lax.broadcasted_iota(jnp.int32, (CHUNK, DT_PAD), 1)
    dt_raw = dt_ref[...] + dtb_ref[...]
    dt = jnp.maximum(dt_raw, 0.0) + jnp.log1p(jnp.exp(-jnp.abs(dt_raw)))
    dt = jnp.where(lane < N_HEADS, dt, 0.0)
    adt = dt * (-jnp.exp(alog_ref[...]))

    ri = lax.broadcasted_iota(jnp.int32, (CHUNK, CHUNK), 0)
    ci = lax.broadcasted_iota(jnp.int32, (CHUNK, CHUNK), 1)
    causal = ri >= ci
    if packed:
        causal = causal & (ri // SHORT_SEQ == ci // SHORT_SEQ)
    exact_dot = functools.partial(jnp.dot, precision=lax.Precision.HIGHEST,
                                  preferred_element_type=F32)
    cs = exact_dot(jnp.where(causal, 1.0, 0.0).astype(F32), adt)
    cs_t = cs.T
    if packed:
        pick_last = ci == (ri // SHORT_SEQ) * SHORT_SEQ + (SHORT_SEQ - 1)
        cs_end = exact_dot(jnp.where(pick_last, 1.0, 0.0).astype(F32), cs)
    else:
        cs_end = cs[CHUNK - 1:CHUNK, :]

    head_of_lane = lax.broadcasted_iota(jnp.int32, (CHUNK, GROUP_W), 1) // HEAD_DIM

    def per_head(cols):
        out = cols[HEADS_PER_GROUP - 1]
        for hl in range(HEADS_PER_GROUP - 2, -1, -1):
            out = jnp.where(head_of_lane == hl, cols[hl], out)
        return out

    y_groups = []
    for g in range(SSD_GROUPS):
        heads = range(g * HEADS_PER_GROUP, (g + 1) * HEADS_PER_GROUP)
        grp = slice(g * GROUP_W, (g + 1) * GROUP_W)
        b_g = xc[:, D_SSD + g * D_STATE:D_SSD + (g + 1) * D_STATE].astype(BF16)
        c_g = xc[:, D_SSD + (SSD_GROUPS + g) * D_STATE:
                 D_SSD + (SSD_GROUPS + g + 1) * D_STATE].astype(BF16)
        cs_cols = [cs[:, h:h + 1] for h in heads]
        xdt_g = xs[:, grp] * per_head([dt[:, h:h + 1] for h in heads])
        xdt_b = xdt_g.astype(BF16)
        gram = _dot_nt(c_g, b_g)

        y_diag = jnp.zeros((CHUNK, GROUP_W), F32)
        for hl, h in enumerate(heads):
            seg = jnp.where(causal, cs_cols[hl] - cs_t[h:h + 1, :], NEG_INF)
            weights = (gram * jnp.exp(seg)).astype(BF16)
            y_diag = y_diag + jnp.where(head_of_lane == hl, _dot(weights, xdt_b), 0.0)

        to_end = per_head([jnp.exp(cs_end[:, h:h + 1] - cs_cols[hl]) for hl, h in enumerate(heads)])
        decayed_t = (xdt_g * to_end).T
        carried = per_head([jnp.exp(col) for col in cs_cols])
        if packed:
            h_prev = h0_ref[:, grp, :]
            wide = _dot_nt(c_g, h_prev.reshape(SEQS_PER_TILE * GROUP_W, D_STATE).astype(BF16))
            y_off = jnp.concatenate(
                [wide[s * SHORT_SEQ:(s + 1) * SHORT_SEQ, s * GROUP_W:(s + 1) * GROUP_W]
                 for s in range(SEQS_PER_TILE)], 0) * carried
            seq_of_step = lax.broadcasted_iota(jnp.int32, (GROUP_W, CHUNK), 1) // SHORT_SEQ
            per_seq = jnp.concatenate([jnp.where(seq_of_step == s, decayed_t, 0.0)
                                       for s in range(SEQS_PER_TILE)], 0).astype(BF16)
            new_states = _dot(per_seq, b_g).reshape(SEQS_PER_TILE, GROUP_W, D_STATE)
            for s in range(SEQS_PER_TILE):
                row = s * SHORT_SEQ
                keep = jnp.concatenate(
                    [jnp.broadcast_to(jnp.exp(cs_end[row:row + 1, h:h + 1]), (HEAD_DIM, D_STATE))
                     for h in heads], 0)
                hn_ref[s, grp, :] = h_prev[s] * keep + new_states[s]
        else:
            h_prev = aux[grp, :]
            y_off = _dot_nt(c_g, h_prev.astype(BF16)) * carried
            keep = jnp.concatenate(
                [jnp.broadcast_to(jnp.exp(cs_end[:, h:h + 1]), (HEAD_DIM, D_STATE)) for h in heads], 0)
            aux[grp, :] = h_prev * keep + _dot(decayed_t.astype(BF16), b_g)
        y_groups.append(y_diag + y_off)

    y = jnp.concatenate(y_groups, axis=-1) + dsk_ref[...] * xs
    y = y * _silu(z_ref[...])
    normed = []
    for g in range(SSD_GROUPS):
        yg = y[:, g * GROUP_W:(g + 1) * GROUP_W]
        normed.append(yg * lax.rsqrt(jnp.mean(yg * yg, -1, keepdims=True) + LN_EPS))
    y_ref[...] = jnp.concatenate(normed, axis=-1) * nw_ref[...]

    if not packed:
        hn_ref[...] = aux[...]
        xpad[0:PAD_ROWS, :] = xpad[CHUNK:, :]


def _ssd_body(*args):
    for _ in _ssd_stages(*args):
        pass


N_SSD_INPUTS = 11


def _ssd_call(z, xbc, dt, conv_prev, h0, cw, cb, dtb, alog, dsk, nw, where=None):
    n, seq, _ = z.shape
    packed = seq == SHORT_SEQ
    if packed:
        assert n % SEQS_PER_TILE == 0 and where is None
        grid = (n // SEQS_PER_TILE, 1)
        fold = lambda t: t.reshape(grid[0], CHUNK, t.shape[-1])
        z, xbc, dt, conv_prev = fold(z), fold(xbc), fold(dt), fold(conv_prev)
        history_shape, state_shape = (None, CHUNK, CONV_DIM), (SEQS_PER_TILE, D_SSD, D_STATE)
        aux = pltpu.VMEM((CHUNK + PAD_ROWS, CONV_DIM), F32)
    else:
        assert seq % CHUNK == 0
        grid = (n, seq // CHUNK)
        history_shape, state_shape = (None, PAD_ROWS, CONV_DIM), (None, D_SSD, D_STATE)
        aux = pltpu.VMEM((D_SSD, D_STATE), F32)
    where = where or (lambda s, c: (s, c))
    tile = lambda width: pl.BlockSpec((None, CHUNK, width), lambda *g: where(*g) + (0,))
    per_seq = lambda shape: pl.BlockSpec(shape, lambda *g: (where(*g)[0], 0, 0))
    history, state = per_seq(history_shape), per_seq(state_shape)
    operands = (z, xbc, dt, conv_prev, h0, cw, cb, dtb, alog, dsk, nw)
    in_specs = [tile(D_SSD), tile(CONV_DIM), tile(DT_PAD), history, state,
                _const_spec((CONV_W, CONV_DIM)), _const_spec((1, CONV_DIM)),
                _const_spec((1, DT_PAD)), _const_spec((1, DT_PAD)),
                _const_spec((1, D_SSD)), _const_spec((1, D_SSD))]
    out_specs = [tile(D_SSD), state]
    out_shape = [jax.ShapeDtypeStruct(z.shape, F32), jax.ShapeDtypeStruct((n, D_SSD, D_STATE), F32)]
    scratch = [pltpu.VMEM((PAD_ROWS + CHUNK, CONV_DIM), F32), aux]
    return packed, grid, operands, in_specs, out_specs, out_shape, scratch


def _ssd(z, *rest):
    packed, grid, operands, in_specs, out_specs, out_shape, scratch = _ssd_call(z, *rest)
    y, h_new = pl.pallas_call(
        functools.partial(_ssd_body, packed, lambda: pl.program_id(1) == 0),
        grid=grid,
        in_specs=in_specs,
        out_specs=out_specs,
        out_shape=out_shape,
        scratch_shapes=scratch,
        compiler_params=pltpu.CompilerParams(dimension_semantics=("parallel", "arbitrary"),
                                             vmem_limit_bytes=VMEM_LIMIT),
        name="ssd",
    )(*operands)
    return y.reshape(z.shape), h_new


def _ssd_with_sample_attn(ssd_args, attn_args):
    z = ssd_args[0]
    nchunks = z.shape[1] // CHUNK
    steps = z.shape[0] * nchunks
    assert steps == attn_args[0].shape[0]
    where = lambda i: (i // nchunks, i % nchunks)
    _, _, ssd_ops, ssd_in, ssd_out, ssd_shape, scratch = _ssd_call(*ssd_args, where=where)
    attn_ops, attn_in, attn_out, attn_shape = _attn_sample_call(*attn_args)
    caches = (CACHE_KT_ARG, CACHE_VT_ARG)
    for arg in caches:
        attn_in[arg] = pl.BlockSpec(memory_space=pl.ANY)
    ring = pltpu.VMEM((CACHE_SLOTS, D_ATTN, CACHE_LEN), F32)

    def body(*refs):
        ssd_in_refs, refs = refs[:N_SSD_INPUTS], refs[N_SSD_INPUTS:]
        attn_in_refs, refs = list(refs[:len(attn_ops)]), refs[len(attn_ops):]
        y_ref, hn_ref, o_ref, xpad, aux, kt_ring, vt_ring, sems = refs
        step = pl.program_id(0)

        def fetch(s):
            slot = s % CACHE_SLOTS
            return [pltpu.make_async_copy(attn_in_refs[arg].at[s], buf.at[slot], sems.at[j, slot])
                    for j, (arg, buf) in enumerate(zip(caches, (kt_ring, vt_ring)))]

        @pl.when(step == 0)
        def _prime():
            for s in range(CACHE_SLOTS - 1):
                for copy in fetch(s):
                    copy.start()

        @pl.when(step + (CACHE_SLOTS - 1) < steps)
        def _ahead():
            for copy in fetch(step + (CACHE_SLOTS - 1)):
                copy.start()

        for copy in fetch(step):
            copy.wait()
        for arg, buf in zip(caches, (kt_ring, vt_ring)):
            attn_in_refs[arg] = buf.at[step % CACHE_SLOTS]

        ssd = _ssd_stages(False, lambda: step % nchunks == 0, *ssd_in_refs, y_ref, hn_ref, xpad, aux)
        attn = _attn_sample_stages(*attn_in_refs, o_ref)
        for stage in (ssd, attn, ssd, attn):
            next(stage, None)

    y, h_new, attn = pl.pallas_call(
        body,
        grid=(steps,),
        in_specs=ssd_in + attn_in,
        out_specs=ssd_out + [attn_out],
        out_shape=ssd_shape + [attn_shape],
        scratch_shapes=scratch + [ring, ring, pltpu.SemaphoreType.DMA((len(caches), CACHE_SLOTS))],
        compiler_params=pltpu.CompilerParams(dimension_semantics=("arbitrary",),
                                             vmem_limit_bytes=VMEM_LIMIT),
        name="ssd_attn_sample",
    )(*ssd_ops, *attn_ops)
    return y, h_new, attn


def _row(v, width=None):
    v = v.reshape(1, -1).astype(F32)
    if width is not None and v.shape[1] < width:
        v = jnp.pad(v, ((0, 0), (0, width - v.shape[1])))
    return v


def kernel(x_prompt, x_sample, cache_k, cache_v, state_conv, state_ssm, p_prompt, p_sample,
           ln_in_g, ln_in_b, w_in, conv_w, conv_b, dt_bias, a_log, d_skip, ssd_norm_w, w_out,
           ln1_g, ln1_b, w_up, w_down, ln2_g, ln2_b, w_gate, w_ple, ln3_g, ln3_b):
    depth = w_in.shape[0]
    assert depth == 1, "single-layer step"
    alpha = (2 * depth) ** 0.25
    bsz, seq, _ = x_prompt.shape
    nd, dec_t, _ = x_sample.shape
    lyr = 0

    w_proj = w_in[lyr][:, :D_PROJ].astype(BF16)
    w_dt = jnp.pad(w_in[lyr][:, D_PROJ:], ((0, 0), (0, DT_PAD - N_HEADS))).astype(BF16)
    gin, bin_ = _row(ln_in_g), _row(ln_in_b)
    ssd_params = (conv_w[lyr].astype(F32), _row(conv_b[lyr]), _row(dt_bias[lyr], DT_PAD),
                  _row(a_log[lyr], DT_PAD), _row(jnp.repeat(d_skip[lyr], HEAD_DIM)),
                  _row(ssd_norm_w[lyr]))
    post_params = (gin, bin_, w_out[lyr].astype(BF16), _row(ln1_g[lyr]), _row(ln1_b[lyr]),
                   w_up[lyr].astype(BF16), w_down[lyr].astype(BF16), _row(ln2_g[lyr]),
                   _row(ln2_b[lyr]), w_gate[lyr].astype(BF16), w_ple[lyr].astype(BF16),
                   _row(ln3_g[lyr]), _row(ln3_b[lyr]))

    q, k, v, k_t, v_t, z, xbc, dt = _in_proj(x_prompt, gin, bin_, w_proj, w_dt, head_major=True, tm=1024)
    n_tok = nd * dec_t
    flat = lambda t: t.reshape(1, n_tok, t.shape[-1])
    toks = lambda t: t.reshape(nd, dec_t, t.shape[-1])
    qs, ks, vs, zs, xbcs, dts = _in_proj(flat(x_sample), gin, bin_, w_proj, w_dt,
                                       head_major=False, tm=512)

    attn = _attn_prompt(q, k, v)
    transposed = lambda c: jnp.transpose(c, (0, 2, 3, 1)).reshape(nd, D_ATTN, CACHE_LEN)
    ssd_args = (z, xbc, dt, jnp.zeros((bsz, PAD_ROWS, CONV_DIM), F32),
                jnp.zeros((bsz, D_SSD, D_STATE), F32)) + ssd_params
    attn_args = (toks(qs), toks(ks), toks(vs), transposed(cache_k[lyr]), transposed(cache_v[lyr]))
    if bsz * (seq // CHUNK) == nd:
        ssd_y, ssm_p, attn_s = _ssd_with_sample_attn(ssd_args, attn_args)
    else:
        ssd_y, ssm_p = _ssd(*ssd_args)
        attn_s = _attn_sample(*attn_args)
    conv_prev = jnp.pad(state_conv[lyr].astype(F32), ((0, 0), (PAD_ROWS - (CONV_W - 1), 0), (0, 0)))
    ssd_s, ssm_s = _ssd(toks(zs), toks(xbcs), toks(dts), conv_prev,
                        state_ssm[lyr].reshape(nd, D_SSD, D_STATE).astype(F32), *ssd_params)

    y_prompt = _post(x_prompt, attn, ssd_y, p_prompt[lyr], *post_params, alpha=alpha, tm=512)
    y_sample = _post(flat(x_sample), attn_s, flat(ssd_s), flat(p_sample[lyr]), *post_params,
                     alpha=alpha, tm=256)

    from_t = lambda t: jnp.transpose(t.reshape(1, bsz, N_HEADS, HEAD_DIM, seq), (0, 1, 4, 2, 3))
    heads = lambda t: t.reshape(1, nd, dec_t, N_HEADS, HEAD_DIM)
    tail = lambda t: t[None, :, -(CONV_W - 1):, :]
    state = lambda t, n: t.reshape(1, n, N_HEADS, HEAD_DIM, D_STATE)
    return (y_prompt, y_sample.reshape(nd, dec_t, D_MODEL), from_t(k_t), from_t(v_t),
            heads(ks), heads(vs), tail(xbc), tail(toks(xbcs)), state(ssm_p, bsz), state(ssm_s, nd))
```

```python
import functools

import numpy as np
import jax
import jax.numpy as jnp
from jax import lax
from jax.experimental import pallas as pl
from jax.experimental.pallas import tpu as pltpu

F32 = jnp.float32
BF16 = jnp.bfloat16

D_MODEL = 1024
HEAD_DIM = 64
D_ATTN = 512
D_SSD = 512
N_HEADS = 8
SSD_GROUPS = 2
GROUP_W = D_SSD // SSD_GROUPS
D_STATE = 128
CONV_W = 4
CONV_DIM = D_SSD + 2 * SSD_GROUPS * D_STATE
CHUNK = 128
D_FF = 4096
D_PLE = 256
LN_EPS = 1e-5
WINDOW_STEPS = 128
DILATIONS = (1, 4, 16)
PHASES = DILATIONS[2]
MID_PHASES = DILATIONS[1]
CACHE_LEN = 2048
DT_PAD = 128
D_PROJ = 3 * D_ATTN + D_SSD + CONV_DIM
SUBLANES = 8
LANES = 128
VMEM_LIMIT = 56 * 1024 * 1024
NEG_INF = float("-inf")


def _slopes():
    return np.array([2.0 ** (-8.0 * (h + 1) / N_HEADS) for h in range(N_HEADS)], dtype=np.float64)


def _layer_norm(x, g, b):
    mu = jnp.mean(x, -1, keepdims=True)
    xc = x - mu
    var = jnp.mean(xc * xc, -1, keepdims=True)
    return xc * lax.rsqrt(var + LN_EPS) * g + b


def _silu(x):
    return x * (1.0 / (1.0 + jnp.exp(-x)))


def _dot(a, b):
    return jnp.dot(a, b, preferred_element_type=F32)


def _dot_nt(a, b):
    return lax.dot_general(a, b, (((1,), (1,)), ((), ())), preferred_element_type=F32)


def _const_spec(shape):
    nd = len(shape)
    return pl.BlockSpec(shape, lambda *_: (0,) * nd, pipeline_mode=pl.Buffered(1))


HEAD_PAIRS = N_HEADS // 2


def _inproj_body(head_major, x_ref, g_ref, b_ref, w_ref, wdt_ref, *out_refs):
    tm = x_ref.shape[0]
    parts = [slice(i * tm // 2, (i + 1) * tm // 2) for i in range(2)]
    h = [_layer_norm(x_ref[r, :], g_ref[...], b_ref[...]).astype(BF16) for r in parts]
    proj = lambda lo, width: [_dot_nt(v, w_ref[lo:lo + width, :]) for v in h]
    if head_major:
        q_ref, k_ref, v_ref, kt_ref, vt_ref, z_ref, xbc_ref, dt_ref = out_refs
        for idx, (ref, t_ref) in enumerate(((q_ref, None), (k_ref, kt_ref), (v_ref, vt_ref))):
            for r, res in zip(parts, proj(idx * D_ATTN, D_ATTN)):
                for c in range(HEAD_PAIRS):
                    ref[c, r, :] = res[:, c * LANES:(c + 1) * LANES]
                if t_ref is not None:
                    t_ref[:, r] = res.T
    else:
        q_ref, k_ref, v_ref, z_ref, xbc_ref, dt_ref = out_refs
        for idx, ref in enumerate((q_ref, k_ref, v_ref)):
            for r, res in zip(parts, proj(idx * D_ATTN, D_ATTN)):
                ref[r, :] = res
    lo = 3 * D_ATTN
    for ref, width in ((z_ref, D_SSD), (xbc_ref, CONV_DIM)):
        for r, res in zip(parts, proj(lo, width)):
            ref[r, :] = res
        lo += width
    for r, v in zip(parts, h):
        dt_ref[r, :] = _dot_nt(v, wdt_ref[...])


def _in_proj(x, g, b, w, w_dt, *, head_major, tm):
    bsz, seq, _ = x.shape
    row = lambda width: pl.BlockSpec((None, tm, width), lambda i, j: (i, j, 0))
    row_shape = lambda width: jax.ShapeDtypeStruct((bsz, seq, width), F32)
    if head_major:
        slab = pl.BlockSpec((None, HEAD_PAIRS, tm, LANES), lambda i, j: (i, 0, j, 0))
        slab_shape = jax.ShapeDtypeStruct((bsz, HEAD_PAIRS, seq, LANES), F32)
        tr = pl.BlockSpec((None, D_ATTN, tm), lambda i, j: (i, 0, j))
        tr_shape = jax.ShapeDtypeStruct((bsz, D_ATTN, seq), F32)
        qkv_specs, qkv_shapes = [slab, slab, slab, tr, tr], [slab_shape] * 3 + [tr_shape] * 2
    else:
        qkv_specs, qkv_shapes = [row(D_ATTN)] * 3, [row_shape(D_ATTN)] * 3
    rest = (D_SSD, CONV_DIM, DT_PAD)
    return pl.pallas_call(
        functools.partial(_inproj_body, head_major),
        grid=(bsz, seq // tm),
        in_specs=[row(D_MODEL), _const_spec((1, D_MODEL)), _const_spec((1, D_MODEL)),
                  _const_spec((D_PROJ, D_MODEL)), _const_spec((DT_PAD, D_MODEL))],
        out_specs=qkv_specs + [row(wd) for wd in rest],
        out_shape=qkv_shapes + [row_shape(wd) for wd in rest],
        compiler_params=pltpu.CompilerParams(dimension_semantics=("parallel", "parallel"),
                                             vmem_limit_bytes=VMEM_LIMIT),
        name="in_proj",
    )(x, g, b, w, w_dt)


FF_CHUNK = 1024
POST_STREAMS = 2


def _post_body(alpha, x_ref, attn_ref, ssd_ref, pe_ref, gin_ref, bin_ref, wout_ref, g1_ref, b1_ref,
               wup_ref, wdown_ref, g2_ref, b2_ref, wgate_ref, wple_ref, g3_ref, b3_ref, y_ref):
    tm = x_ref.shape[0]
    parts = [slice(i * tm // POST_STREAMS, (i + 1) * tm // POST_STREAMS) for i in range(POST_STREAMS)]
    each = lambda fn, *lists: [fn(*args) for args in zip(*lists)]
    xn = [_layer_norm(x_ref[r, :], gin_ref[...], bin_ref[...]) for r in parts]
    mixed = [jnp.concatenate([attn_ref[c, r, :] for c in range(HEAD_PAIRS)] + [ssd_ref[r, :]],
                             axis=-1).astype(BF16) for r in parts]
    proj = [_dot(m, wout_ref[...]) for m in mixed]
    h = each(lambda x, p: _layer_norm(alpha * x + p, g1_ref[...], b1_ref[...]), xn, proj)
    hb = [v.astype(BF16) for v in h]
    u = [None] * POST_STREAMS
    for c in range(D_FF // FF_CHUNK):
        cols = slice(c * FF_CHUNK, (c + 1) * FF_CHUNK)
        a = [jnp.maximum(_dot(v, wup_ref[:, cols]), 0.0) for v in hb]
        part = [_dot((v * v).astype(BF16), wdown_ref[cols, :]) for v in a]
        u = part if c == 0 else each(lambda s, p: s + p, u, part)
    h = each(lambda v, w: _layer_norm(alpha * v + w, g2_ref[...], b2_ref[...]), h, u)
    gate = [1.0 / (1.0 + jnp.exp(-_dot(v.astype(BF16), wgate_ref[...]))) for v in h]
    emb = [_dot(pe_ref[r, :].astype(BF16), wple_ref[...]) for r in parts]
    for r, v, g, e in zip(parts, h, gate, emb):
        y_ref[r, :] = _layer_norm(alpha * v + g * e, g3_ref[...], b3_ref[...])


def _post(x, attn, ssd, pe, gin, bin_, wout, g1, b1, wup, wdown, g2, b2, wgate, wple, g3, b3, *,
          alpha, tm):
    bsz, seq, _ = x.shape
    row = lambda width: pl.BlockSpec((None, tm, width), lambda i, j: (i, j, 0))
    slab = pl.BlockSpec((None, HEAD_PAIRS, tm, LANES), lambda i, j: (i, 0, j, 0))
    vec = _const_spec((1, D_MODEL))
    return pl.pallas_call(
        functools.partial(_post_body, alpha),
        grid=(bsz, seq // tm),
        in_specs=[row(D_MODEL), slab, row(D_SSD), row(D_PLE), vec, vec,
                  _const_spec((D_MODEL, D_MODEL)), vec, vec,
                  _const_spec((D_MODEL, D_FF)), _const_spec((D_FF, D_MODEL)), vec, vec,
                  _const_spec((D_MODEL, D_MODEL)), _const_spec((D_PLE, D_MODEL)), vec, vec],
        out_specs=row(D_MODEL),
        out_shape=jax.ShapeDtypeStruct((bsz, seq, D_MODEL), F32),
        compiler_params=pltpu.CompilerParams(dimension_semantics=("parallel", "parallel"),
                                             vmem_limit_bytes=VMEM_LIMIT),
        name="post",
    )(x, attn, ssd, pe, gin, bin_, wout, g1, b1, wup, wdown, g2, b2, wgate, wple, g3, b3)


U_ROWS = CACHE_LEN // PHASES
B2_ROWS = U_ROWS // (PHASES // MID_PHASES)
TAB3_LO, TAB2_LO, TAB1_LO, TAB_W = 0, WINDOW_STEPS, 3 * WINDOW_STEPS, 5 * WINDOW_STEPS
WAVE = 4
B1_IN_FLIGHT = 8


def _prompt_bias_tables():
    slopes = _slopes()
    u = np.arange(U_ROWS)
    d3 = (u[:, None] - u[None, :]).astype(np.float64)
    k4, ul4 = np.meshgrid(np.arange(PHASES // MID_PHASES), np.arange(B2_ROWS), indexing="ij")
    j2 = (PHASES // MID_PHASES * ul4 + k4).reshape(-1)
    d2 = j2[:, None] - np.concatenate([j2 - WINDOW_STEPS, j2])[None, :]
    p16, ul16 = np.meshgrid(np.arange(PHASES), np.arange(SUBLANES), indexing="ij")
    t1 = (PHASES * ul16 + p16).reshape(-1)
    d1 = t1[:, None] - (np.arange(2 * WINDOW_STEPS) - WINDOW_STEPS)[None, :]
    tabs = []
    for dist, dil in zip((d3, d2, d1), reversed(DILATIONS)):
        valid = (dist >= 0) & (dist <= WINDOW_STEPS)
        per_head = [np.where(valid, -slopes[h] * dist * dil, NEG_INF) for h in range(N_HEADS)]
        tabs.append(np.stack(per_head))
    tab = np.concatenate(tabs, axis=-1)
    return jnp.asarray(tab.reshape(HEAD_PAIRS, 2 * U_ROWS, TAB_W), dtype=F32)


def _attn_blocks(blocks, first_head):
    half = U_ROWS
    state = [dict() for _ in blocks]

    def scores(i):
        q2, k, _, bias = blocks[i]
        state[i]["s"] = _dot_nt(q2, k) + bias

    def top(i):
        state[i]["m"] = jnp.max(state[i]["s"], -1, keepdims=True)

    def weights(i):
        e = jnp.exp(state[i].pop("s") - state[i]["m"])
        state[i]["den"] = jnp.sum(e, -1, keepdims=True)
        state[i]["e"] = e.astype(BF16)

    def values(i):
        state[i]["pv"] = _dot(state[i].pop("e"), blocks[i][2])

    def finish(i):
        pv, m, den = state[i]["pv"], state[i]["m"], state[i]["den"]
        o = jnp.where(first_head, pv[:half], pv[half:])
        m2 = jnp.where(first_head, m[:half], m[half:])
        den2 = jnp.where(first_head, den[:half], den[half:])
        state[i] = (o / den2, m2 + jnp.log(den2))

    stages = (scores, top, weights, values, finish)
    groups = [range(j, min(j + WAVE, len(blocks))) for j in range(0, len(blocks), WAVE)]
    for t in range(len(groups) + len(stages) - 1):
        for g, members in enumerate(groups):
            if 0 <= t - g < len(stages):
                for i in members:
                    stages[t - g](i)
    return state


def _attn_prompt_body(q_ref, k_ref, v_ref, tab_ref, o_ref, qs, q0p, q1p, kp, vp, kn, vn, quarter,
                      o_scr, l_scr):
    lane = lax.broadcasted_iota(jnp.int32, (U_ROWS, LANES), 1)
    first_head = lane < HEAD_DIM
    scale = HEAD_DIM ** -0.5

    kn[...] = k_ref[...].astype(BF16)
    vn[...] = v_ref[...].astype(BF16)

    def split(src_ref, emit):
        for r in range(MID_PHASES):
            quarter[r] = src_ref[pl.ds(r, CACHE_LEN // MID_PHASES, stride=MID_PHASES), :]
        for r in range(MID_PHASES):
            for kk in range(PHASES // MID_PHASES):
                emit(MID_PHASES * kk + r,
                     quarter[r, pl.ds(kk, U_ROWS, stride=PHASES // MID_PHASES), :])

    def emit_q(p, tile):
        tile = tile * scale
        qs[p] = tile
        q0p[p] = jnp.where(first_head, tile, 0.0).astype(BF16)
        q1p[p] = jnp.where(first_head, 0.0, tile).astype(BF16)

    def emit_to(dst):
        def emit(p, tile):
            dst[p] = tile.astype(BF16)
        return emit

    split(q_ref, emit_q)
    split(k_ref, emit_to(kp))
    split(v_ref, emit_to(vp))

    bias = tab_ref[:, TAB3_LO:TAB3_LO + U_ROWS]
    outs = _attn_blocks([(jnp.concatenate([q0p[p], q1p[p]], 0), kp[p], vp[p], bias)
                         for p in range(PHASES)], first_head)
    for p, (o, l) in enumerate(outs):
        o_scr[0, p] = o
        l_scr[0, p] = l

    blocks, where = [], []
    for r in range(MID_PHASES):
        tiles = [MID_PHASES * kk + r for kk in range(PHASES // MID_PHASES)]
        gather = lambda src, rr, tiles=tiles: [src[p, rr, :] for p in tiles]
        for n in range(U_ROWS // B2_ROWS):
            rows = pl.ds(B2_ROWS * n, B2_ROWS)
            q2 = jnp.concatenate(gather(q0p, rows) + gather(q1p, rows), 0)
            if n == 0:
                k = jnp.concatenate(gather(kp, rows), 0)
                v = jnp.concatenate(gather(vp, rows), 0)
                bias = tab_ref[:, TAB2_LO + WINDOW_STEPS:TAB2_LO + 2 * WINDOW_STEPS]
            else:
                prev = pl.ds(B2_ROWS * (n - 1), B2_ROWS)
                k = jnp.concatenate(gather(kp, prev) + gather(kp, rows), 0)
                v = jnp.concatenate(gather(vp, prev) + gather(vp, rows), 0)
                bias = tab_ref[:, TAB2_LO:TAB2_LO + 2 * WINDOW_STEPS]
            blocks.append((q2, k, v, bias))
            where.append((tiles, rows))
    for (tiles, rows), (o, l) in zip(where, _attn_blocks(blocks, first_head)):
        for kk, p in enumerate(tiles):
            o_scr[1, p, rows, :] = o[B2_ROWS * kk:B2_ROWS * (kk + 1)]
            l_scr[1, p, rows, :] = l[B2_ROWS * kk:B2_ROWS * (kk + 1)]

    def branch1_blocks(ns):
        blocks = []
        for n in ns:
            rows = pl.ds(n * SUBLANES, SUBLANES)
            q = jnp.concatenate([qs[p, rows, :] for p in range(PHASES)], 0)
            q2 = jnp.concatenate([jnp.where(first_head, q, 0.0), jnp.where(first_head, 0.0, q)],
                                 0).astype(BF16)
            if n > 0:
                keys = pl.ds((n - 1) * U_ROWS, 2 * U_ROWS)
                bias = tab_ref[:, TAB1_LO:TAB1_LO + 2 * WINDOW_STEPS]
            else:
                keys = pl.ds(0, U_ROWS)
                bias = tab_ref[:, TAB1_LO + WINDOW_STEPS:TAB1_LO + 2 * WINDOW_STEPS]
            blocks.append((q2, kn[keys, :], vn[keys, :], bias))
        for n, (o, l) in zip(ns, _attn_blocks(blocks, first_head)):
            rows = pl.ds(n * SUBLANES, SUBLANES)
            for p in range(PHASES):
                o_scr[2, p, rows, :] = o[SUBLANES * p:SUBLANES * (p + 1)]
                l_scr[2, p, rows, :] = l[SUBLANES * p:SUBLANES * (p + 1)]

    n_blocks = U_ROWS // SUBLANES
    for first in range(0, n_blocks, B1_IN_FLIGHT):
        branch1_blocks(range(first, min(first + B1_IN_FLIGHT, n_blocks)))

    def phase_rows(p):
        return pl.ds(p, U_ROWS, stride=PHASES)

    for p in range(PHASES):
        l3, l2, l1 = l_scr[0, p], l_scr[1, p], l_scr[2, p]
        top = jnp.maximum(jnp.maximum(l3, l2), l1)
        e3, e2, e1 = jnp.exp(l3 - top), jnp.exp(l2 - top), jnp.exp(l1 - top)
        total = e3 + e2 + e1
        o_ref[phase_rows(p), :] = (e3 * o_scr[0, p] + e2 * o_scr[1, p] + e1 * o_scr[2, p]) / total


def _attn_prompt(q, k, v):
    bsz, _, seq, _ = k.shape
    assert seq == CACHE_LEN, "prompt attention is laid out for a 2048-token prompt"
    slab = pl.BlockSpec((None, None, seq, LANES), lambda b, c: (b, c, 0, 0))
    tile = (PHASES, U_ROWS, LANES)
    return pl.pallas_call(
        _attn_prompt_body,
        grid=(bsz, HEAD_PAIRS),
        in_specs=[slab, slab, slab,
                  pl.BlockSpec((None, 2 * U_ROWS, TAB_W), lambda b, c: (c, 0, 0))],
        out_specs=slab,
        out_shape=jax.ShapeDtypeStruct((bsz, HEAD_PAIRS, seq, LANES), F32),
        scratch_shapes=[pltpu.VMEM(tile, F32)] + [pltpu.VMEM(tile, BF16)] * 4
                       + [pltpu.VMEM((seq, LANES), BF16)] * 2
                       + [pltpu.VMEM((4, seq // 4, LANES), F32),
                          pltpu.VMEM((3,) + tile, F32), pltpu.VMEM((3,) + tile, F32)],
        compiler_params=pltpu.CompilerParams(dimension_semantics=("parallel", "parallel"),
                                             vmem_limit_bytes=VMEM_LIMIT),
        name="attn_prompt",
    )(q, k, v, _prompt_bias_tables())


DEC_T = 8
QH = DEC_T * N_HEADS
B2_SPAN, B1_SPAN = 512, 128
SAMPLE_TAB_W = CACHE_LEN + B2_SPAN + B1_SPAN


def _sample_bias_tables():
    slopes = _slopes()
    i = np.repeat(np.arange(DEC_T), N_HEADS)[:, None]
    sl = np.tile(slopes, DEC_T)[:, None]

    def cache_bias(span, dil):
        t = CACHE_LEN - span + np.arange(span)[None, :]
        dist = CACHE_LEN + i - t
        valid = (dist % dil == 0) & (dist <= WINDOW_STEPS * dil)
        return np.where(valid, -sl * dist, NEG_INF)

    cache_tab = np.concatenate([cache_bias(CACHE_LEN, 16), cache_bias(B2_SPAN, 4),
                                cache_bias(B1_SPAN, 1)], axis=1)
    c = np.arange(CHUNK)[None, :]
    dn = i - c
    is_new = c < DEC_T
    n1 = np.where(is_new & (dn >= 0), -sl * dn, NEG_INF)
    n2 = np.where(is_new & ((dn == 0) | (dn == 4)), -sl * dn, NEG_INF)
    n3 = np.where(is_new & (dn == 0), 0.0, NEG_INF)
    return jnp.asarray(cache_tab, dtype=F32), jnp.asarray(np.stack([n1, n2, n3]), dtype=F32)


def _head_selector():
    h_row = np.tile(np.arange(N_HEADS), DEC_T)[:, None]
    h_col = (np.arange(D_ATTN) // HEAD_DIM)[None, :]
    return jnp.asarray((h_row == h_col).astype(np.float32))


def _attn_sample_stages(q_ref, kn_ref, vn_ref, kt_ref, vt_ref, tab_ref, tabn_ref, sel_ref, o_ref):
    sel = sel_ref[...]
    q = q_ref[...] * (HEAD_DIM ** -0.5)
    q_rows = jnp.concatenate([jnp.broadcast_to(q[i:i + 1, :], (N_HEADS, D_ATTN))
                              for i in range(DEC_T)], 0)
    qall = (q_rows * sel).astype(BF16)
    pad = jnp.zeros((CHUNK - DEC_T, D_ATTN), F32)
    kn = jnp.concatenate([kn_ref[...], pad], 0).astype(BF16)
    vn = jnp.concatenate([vn_ref[...], pad], 0).astype(BF16)

    s_all = _dot(qall, kt_ref[...].astype(BF16))
    s_new = _dot_nt(qall, kn)
    yield
    lo2, lo1 = CACHE_LEN - B2_SPAN, CACHE_LEN - B1_SPAN

    def branch(s_cache, s_fresh, v_t):
        m = jnp.maximum(jnp.max(s_cache, -1, keepdims=True), jnp.max(s_fresh, -1, keepdims=True))
        e_c, e_f = jnp.exp(s_cache - m), jnp.exp(s_fresh - m)
        den = jnp.sum(e_c, -1, keepdims=True) + jnp.sum(e_f, -1, keepdims=True)
        acc = _dot_nt(e_c.astype(BF16), v_t.astype(BF16)) + _dot(e_f.astype(BF16), vn)
        return m, den, acc

    m3, den3, acc3 = branch(s_all + tab_ref[:, :CACHE_LEN], s_new + tabn_ref[2], vt_ref[...])
    m2, den2, acc2 = branch(s_all[:, lo2:] + tab_ref[:, CACHE_LEN:CACHE_LEN + B2_SPAN],
                            s_new + tabn_ref[1], vt_ref[:, lo2:])
    m1, den1, acc1 = branch(s_all[:, lo1:] + tab_ref[:, CACHE_LEN + B2_SPAN:],
                            s_new + tabn_ref[0], vt_ref[:, lo1:])

    top = jnp.maximum(jnp.maximum(m1, m2), m3)
    sc1, sc2, sc3 = den1 * jnp.exp(m1 - top), den2 * jnp.exp(m2 - top), den3 * jnp.exp(m3 - top)
    total = sc1 + sc2 + sc3
    mixed = ((sc1 / total) * (acc1 / den1) + (sc2 / total) * (acc2 / den2)
             + (sc3 / total) * (acc3 / den3)) * sel
    for i in range(DEC_T):
        row = jnp.sum(mixed[N_HEADS * i:N_HEADS * (i + 1)], axis=0, keepdims=True)
        for c in range(HEAD_PAIRS):
            o_ref[c, i:i + 1, :] = row[:, c * LANES:(c + 1) * LANES]


def _attn_sample_body(*refs):
    for _ in _attn_sample_stages(*refs):
        pass


CACHE_KT_ARG, CACHE_VT_ARG = 3, 4
CACHE_SLOTS = 3


def _attn_sample_call(q, k_new, v_new, cache_kt, cache_vt):
    n = q.shape[0]
    assert cache_kt.shape[1:] == (D_ATTN, CACHE_LEN) and q.shape[1] == DEC_T
    tok = pl.BlockSpec((None, DEC_T, D_ATTN), lambda i: (i, 0, 0))
    cache = pl.BlockSpec((None, D_ATTN, CACHE_LEN), lambda i: (i, 0, 0))
    cache_tab, new_tab = _sample_bias_tables()
    operands = (q, k_new, v_new, cache_kt, cache_vt, cache_tab, new_tab, _head_selector())
    in_specs = [tok, tok, tok, cache, cache, _const_spec((QH, SAMPLE_TAB_W)),
                _const_spec((3, QH, CHUNK)), _const_spec((QH, D_ATTN))]
    out_spec = pl.BlockSpec((None, HEAD_PAIRS, DEC_T, LANES), lambda i: (0, 0, i, 0))
    out_shape = jax.ShapeDtypeStruct((1, HEAD_PAIRS, n * DEC_T, LANES), F32)
    return operands, in_specs, out_spec, out_shape


def _attn_sample(q, k_new, v_new, cache_kt, cache_vt):
    operands, in_specs, out_spec, out_shape = _attn_sample_call(q, k_new, v_new, cache_kt, cache_vt)
    return pl.pallas_call(
        _attn_sample_body,
        grid=(q.shape[0],),
        in_specs=in_specs,
        out_specs=out_spec,
        out_shape=out_shape,
        compiler_params=pltpu.CompilerParams(dimension_semantics=("parallel",),
                                             vmem_limit_bytes=VMEM_LIMIT),
        name="attn_sample",
    )(*operands)


HEADS_PER_GROUP = N_HEADS // SSD_GROUPS
PAD_ROWS = SUBLANES
SHORT_SEQ = SUBLANES
SEQS_PER_TILE = CHUNK // SHORT_SEQ


def _ssd_stages(packed, first_chunk, z_ref, xbc_ref, dt_ref, cp_ref, h0_ref, cw_ref, cb_ref, dtb_ref,
                alog_ref, dsk_ref, nw_ref, y_ref, hn_ref, xpad, aux):
    if packed:
        aux[0:CHUNK, :] = cp_ref[...]
        aux[CHUNK:, :] = jnp.zeros((PAD_ROWS, CONV_DIM), F32)
        xpad[0:PAD_ROWS, :] = jnp.zeros((PAD_ROWS, CONV_DIM), F32)
    else:
        @pl.when(first_chunk())
        def _init():
            xpad[0:PAD_ROWS, :] = cp_ref[...]
            aux[...] = h0_ref[...]
    yield
    xpad[PAD_ROWS:, :] = xbc_ref[...]

    step = lax.broadcasted_iota(jnp.int32, (CHUNK, CONV_DIM), 0) % SHORT_SEQ
    conv = cb_ref[...]
    for back in range(CONV_W):
        tap = CONV_W - 1 - back
        rows_back = xpad[pl.ds(PAD_ROWS - back, CHUNK), :]
        if packed and back:
            rows_back = jnp.where(step < back, aux[pl.ds(PAD_ROWS - back, CHUNK), :], rows_back)
        conv = conv + rows_back * cw_ref[tap:tap + 1, :]
    xc = _silu(conv)
    xs = xc[:, :D_SSD]

    lane = lax.broadcasted_iota(jnp.int32, (CHUNK, DT_PAD), 1)
    dt_raw = dt_ref[...] + dtb_ref[...]
    dt = jnp.maximum(dt_raw, 0.0) + jnp.log1p(jnp.exp(-jnp.abs(dt_raw)))
    dt = jnp.where(lane < N_HEADS, dt, 0.0)
    adt = dt * (-jnp.exp(alog_ref[...]))

    ri = lax.broadcasted_iota(jnp.int32, (CHUNK, CHUNK), 0)
    ci = lax.broadcasted_iota(jnp.int32, (CHUNK, CHUNK), 1)
    causal = ri >= ci
    if packed:
        causal = causal & (ri // SHORT_SEQ == ci // SHORT_SEQ)
    exact_dot = functools.partial(jnp.dot, precision=lax.Precision.HIGHEST,
                                  preferred_element_type=F32)
    cs = exact_dot(jnp.where(causal, 1.0, 0.0).astype(F32), adt)
    cs_t = cs.T
    if packed:
        pick_last = ci == (ri // SHORT_SEQ) * SHORT_SEQ + (SHORT_SEQ - 1)
        cs_end = exact_dot(jnp.where(pick_last, 1.0, 0.0).astype(F32), cs)
    else:
        cs_end = cs[CHUNK - 1:CHUNK, :]

    head_of_lane = lax.broadcasted_iota(jnp.int32, (CHUNK, GROUP_W), 1) // HEAD_DIM

    def per_head(cols):
        out = cols[HEADS_PER_GROUP - 1]
        for hl in range(HEADS_PER_GROUP - 2, -1, -1):
            out = jnp.where(head_of_lane == hl, cols[hl], out)
        return out

    y_groups = []
    for g in range(SSD_GROUPS):
        heads = range(g * HEADS_PER_GROUP, (g + 1) * HEADS_PER_GROUP)
        grp = slice(g * GROUP_W, (g + 1) * GROUP_W)
        b_g = xc[:, D_SSD + g * D_STATE:D_SSD + (g + 1) * D_STATE].astype(BF16)
        c_g = xc[:, D_SSD + (SSD_GROUPS + g) * D_STATE:
                 D_SSD + (SSD_GROUPS + g + 1) * D_STATE].astype(BF16)
        cs_cols = [cs[:, h:h + 1] for h in heads]
        xdt_g = xs[:, grp] * per_head([dt[:, h:h + 1] for h in heads])
        xdt_b = xdt_g.astype(BF16)
        gram = _dot_nt(c_g, b_g)

        y_diag = jnp.zeros((CHUNK, GROUP_W), F32)
        for hl, h in enumerate(heads):
            seg = jnp.where(causal, cs_cols[hl] - cs_t[h:h + 1, :], NEG_INF)
            weights = (gram * jnp.exp(seg)).astype(BF16)
            y_diag = y_diag + jnp.where(head_of_lane == hl, _dot(weights, xdt_b), 0.0)

        to_end = per_head([jnp.exp(cs_end[:, h:h + 1] - cs_cols[hl]) for hl, h in enumerate(heads)])
        decayed_t = (xdt_g * to_end).T
        carried = per_head([jnp.exp(col) for col in cs_cols])
        if packed:
            h_prev = h0_ref[:, grp, :]
            wide = _dot_nt(c_g, h_prev.reshape(SEQS_PER_TILE * GROUP_W, D_STATE).astype(BF16))
            y_off = jnp.concatenate(
                [wide[s * SHORT_SEQ:(s + 1) * SHORT_SEQ, s * GROUP_W:(s + 1) * GROUP_W]
                 for s in range(SEQS_PER_TILE)], 0) * carried
            seq_of_step = lax.broadcasted_iota(jnp.int32, (GROUP_W, CHUNK), 1) // SHORT_SEQ
            per_seq = jnp.concatenate([jnp.where(seq_of_step == s, decayed_t, 0.0)
                                       for s in range(SEQS_PER_TILE)], 0).astype(BF16)
            new_states = _dot(per_seq, b_g).reshape(SEQS_PER_TILE, GROUP_W, D_STATE)
            for s in range(SEQS_PER_TILE):
                row = s * SHORT_SEQ
                keep = jnp.concatenate(
                    [jnp.broadcast_to(jnp.exp(cs_end[row:row + 1, h:h + 1]), (HEAD_DIM, D_STATE))
                     for h in heads], 0)
                hn_ref[s, grp, :] = h_prev[s] * keep + new_states[s]
        else:
            h_prev = aux[grp, :]
            y_off = _dot_nt(c_g, h_prev.astype(BF16)) * carried
            keep = jnp.concatenate(
                [jnp.broadcast_to(jnp.exp(cs_end[:, h:h + 1]), (HEAD_DIM, D_STATE)) for h in heads], 0)
            aux[grp, :] = h_prev * keep + _dot(decayed_t.astype(BF16), b_g)
        y_groups.append(y_diag + y_off)

    y = jnp.concatenate(y_groups, axis=-1) + dsk_ref[...] * xs
    y = y * _silu(z_ref[...])
    normed = []
    for g in range(SSD_GROUPS):
        yg = y[:, g * GROUP_W:(g + 1) * GROUP_W]
        normed.append(yg * lax.rsqrt(jnp.mean(yg * yg, -1, keepdims=True) + LN_EPS))
    y_ref[...] = jnp.concatenate(normed, axis=-1) * nw_ref[...]

    if not packed:
        hn_ref[...] = aux[...]
        xpad[0:PAD_ROWS, :] = xpad[CHUNK:, :]


def _ssd_body(*args):
    for _ in _ssd_stages(*args):
        pass


N_SSD_INPUTS = 11


def _ssd_call(z, xbc, dt, conv_prev, h0, cw, cb, dtb, alog, dsk, nw, where=None):
    n, seq, _ = z.shape
    packed = seq == SHORT_SEQ
    if packed:
        assert n % SEQS_PER_TILE == 0 and where is None
        grid = (n // SEQS_PER_TILE, 1)
        fold = lambda t: t.reshape(grid[0], CHUNK, t.shape[-1])
        z, xbc, dt, conv_prev = fold(z), fold(xbc), fold(dt), fold(conv_prev)
        history_shape, state_shape = (None, CHUNK, CONV_DIM), (SEQS_PER_TILE, D_SSD, D_STATE)
        aux = pltpu.VMEM((CHUNK + PAD_ROWS, CONV_DIM), F32)
    else:
        assert seq % CHUNK == 0
        grid = (n, seq // CHUNK)
        history_shape, state_shape = (None, PAD_ROWS, CONV_DIM), (None, D_SSD, D_STATE)
        aux = pltpu.VMEM((D_SSD, D_STATE), F32)
    where = where or (lambda s, c: (s, c))
    tile = lambda width: pl.BlockSpec((None, CHUNK, width), lambda *g: where(*g) + (0,))
    per_seq = lambda shape: pl.BlockSpec(shape, lambda *g: (where(*g)[0], 0, 0))
    history, state = per_seq(history_shape), per_seq(state_shape)
    operands = (z, xbc, dt, conv_prev, h0, cw, cb, dtb, alog, dsk, nw)
    in_specs = [tile(D_SSD), tile(CONV_DIM), tile(DT_PAD), history, state,
                _const_spec((CONV_W, CONV_DIM)), _const_spec((1, CONV_DIM)),
                _const_spec((1, DT_PAD)), _const_spec((1, DT_PAD)),
                _const_spec((1, D_SSD)), _const_spec((1, D_SSD))]
    out_specs = [tile(D_SSD), state]
    out_shape = [jax.ShapeDtypeStruct(z.shape, F32), jax.ShapeDtypeStruct((n, D_SSD, D_STATE), F32)]
    scratch = [pltpu.VMEM((PAD_ROWS + CHUNK, CONV_DIM), F32), aux]
    return packed, grid, operands, in_specs, out_specs, out_shape, scratch


def _ssd(z, *rest):
    packed, grid, operands, in_specs, out_specs, out_shape, scratch = _ssd_call(z, *rest)
    y, h_new = pl.pallas_call(
        functools.partial(_ssd_body, packed, lambda: pl.program_id(1) == 0),
        grid=grid,
        in_specs=in_specs,
        out_specs=out_specs,
        out_shape=out_shape,
        scratch_shapes=scratch,
        compiler_params=pltpu.CompilerParams(dimension_semantics=("parallel", "arbitrary"),
                                             vmem_limit_bytes=VMEM_LIMIT),
        name="ssd",
    )(*operands)
    return y.reshape(z.shape), h_new


def _ssd_with_sample_attn(ssd_args, attn_args):
    z = ssd_args[0]
    nchunks = z.shape[1] // CHUNK
    steps = z.shape[0] * nchunks
    assert steps == attn_args[0].shape[0]
    where = lambda i: (i // nchunks, i % nchunks)
    _, _, ssd_ops, ssd_in, ssd_out, ssd_shape, scratch = _ssd_call(*ssd_args, where=where)
    attn_ops, attn_in, attn_out, attn_shape = _attn_sample_call(*attn_args)
    caches = (CACHE_KT_ARG, CACHE_VT_ARG)
    for arg in caches:
        attn_in[arg] = pl.BlockSpec(memory_space=pl.ANY)
    ring = pltpu.VMEM((CACHE_SLOTS, D_ATTN, CACHE_LEN), F32)

    def body(*refs):
        ssd_in_refs, refs = refs[:N_SSD_INPUTS], refs[N_SSD_INPUTS:]
        attn_in_refs, refs = list(refs[:len(attn_ops)]), refs[len(attn_ops):]
        y_ref, hn_ref, o_ref, xpad, aux, kt_ring, vt_ring, sems = refs
        step = pl.program_id(0)

        def fetch(s):
            slot = s % CACHE_SLOTS
            return [pltpu.make_async_copy(attn_in_refs[arg].at[s], buf.at[slot], sems.at[j, slot])
                    for j, (arg, buf) in enumerate(zip(caches, (kt_ring, vt_ring)))]

        @pl.when(step == 0)
        def _prime():
            for s in range(CACHE_SLOTS - 1):
                for copy in fetch(s):
                    copy.start()

        @pl.when(step + (CACHE_SLOTS - 1) < steps)
        def _ahead():
            for copy in fetch(step + (CACHE_SLOTS - 1)):
                copy.start()

        for copy in fetch(step):
            copy.wait()
        for arg, buf in zip(caches, (kt_ring, vt_ring)):
            attn_in_refs[arg] = buf.at[step % CACHE_SLOTS]

        ssd = _ssd_stages(False, lambda: step % nchunks == 0, *ssd_in_refs, y_ref, hn_ref, xpad, aux)
        attn = _attn_sample_stages(*attn_in_refs, o_ref)
        for stage in (ssd, attn, ssd, attn):
            next(stage, None)

    y, h_new, attn = pl.pallas_call(
        body,
        grid=(steps,),
        in_specs=ssd_in + attn_in,
        out_specs=ssd_out + [attn_out],
        out_shape=ssd_shape + [attn_shape],
        scratch_shapes=scratch + [ring, ring, pltpu.SemaphoreType.DMA((len(caches), CACHE_SLOTS))],
        compiler_params=pltpu.CompilerParams(dimension_semantics=("arbitrary",),
                                             vmem_limit_bytes=VMEM_LIMIT),
        name="ssd_attn_sample",
    )(*ssd_ops, *attn_ops)
    return y, h_new, attn


def _row(v, width=None):
    v = v.reshape(1, -1).astype(F32)
    if width is not None and v.shape[1] < width:
        v = jnp.pad(v, ((0, 0), (0, width - v.shape[1])))
    return v


def kernel(x_prompt, x_sample, cache_k, cache_v, state_conv, state_ssm, p_prompt, p_sample,
           ln_in_g, ln_in_b, w_in, conv_w, conv_b, dt_bias, a_log, d_skip, ssd_norm_w, w_out,
           ln1_g, ln1_b, w_up, w_down, ln2_g, ln2_b, w_gate, w_ple, ln3_g, ln3_b):
    depth = w_in.shape[0]
    assert depth == 1, "single-layer step"
    alpha = (2 * depth) ** 0.25
    bsz, seq, _ = x_prompt.shape
    nd, dec_t, _ = x_sample.shape
    lyr = 0

    w_t = jnp.transpose(w_in[lyr])
    w_proj = w_t[:D_PROJ].astype(BF16)
    w_dt = jnp.pad(w_t[D_PROJ:], ((0, DT_PAD - N_HEADS), (0, 0))).astype(BF16)
    gin, bin_ = _row(ln_in_g), _row(ln_in_b)
    ssd_params = (conv_w[lyr].astype(F32), _row(conv_b[lyr]), _row(dt_bias[lyr], DT_PAD),
                  _row(a_log[lyr], DT_PAD), _row(jnp.repeat(d_skip[lyr], HEAD_DIM)),
                  _row(ssd_norm_w[lyr]))
    post_params = (gin, bin_, w_out[lyr].astype(BF16), _row(ln1_g[lyr]), _row(ln1_b[lyr]),
                   w_up[lyr].astype(BF16), w_down[lyr].astype(BF16), _row(ln2_g[lyr]),
                   _row(ln2_b[lyr]), w_gate[lyr].astype(BF16), w_ple[lyr].astype(BF16),
                   _row(ln3_g[lyr]), _row(ln3_b[lyr]))

    q, k, v, k_t, v_t, z, xbc, dt = _in_proj(x_prompt, gin, bin_, w_proj, w_dt, head_major=True, tm=1024)
    n_tok = nd * dec_t
    flat = lambda t: t.reshape(1, n_tok, t.shape[-1])
    toks = lambda t: t.reshape(nd, dec_t, t.shape[-1])
    qs, ks, vs, zs, xbcs, dts = _in_proj(flat(x_sample), gin, bin_, w_proj, w_dt,
                                       head_major=False, tm=512)

    attn = _attn_prompt(q, k, v)
    transposed = lambda c: jnp.transpose(c, (0, 2, 3, 1)).reshape(nd, D_ATTN, CACHE_LEN)
    ssd_args = (z, xbc, dt, jnp.zeros((bsz, PAD_ROWS, CONV_DIM), F32),
                jnp.zeros((bsz, D_SSD, D_STATE), F32)) + ssd_params
    attn_args = (toks(qs), toks(ks), toks(vs), transposed(cache_k[lyr]), transposed(cache_v[lyr]))
    if bsz * (seq // CHUNK) == nd:
        ssd_y, ssm_p, attn_s = _ssd_with_sample_attn(ssd_args, attn_args)
    else:
        ssd_y, ssm_p = _ssd(*ssd_args)
        attn_s = _attn_sample(*attn_args)
    conv_prev = jnp.pad(state_conv[lyr].astype(F32), ((0, 0), (PAD_ROWS - (CONV_W - 1), 0), (0, 0)))
    ssd_s, ssm_s = _ssd(toks(zs), toks(xbcs), toks(dts), conv_prev,
                        state_ssm[lyr].reshape(nd, D_SSD, D_STATE).astype(F32), *ssd_params)

    y_prompt = _post(x_prompt, attn, ssd_y, p_prompt[lyr], *post_params, alpha=alpha, tm=512)
    y_sample = _post(flat(x_sample), attn_s, flat(ssd_s), flat(p_sample[lyr]), *post_params,
                     alpha=alpha, tm=256)

    from_t = lambda t: jnp.transpose(t.reshape(1, bsz, N_HEADS, HEAD_DIM, seq), (0, 1, 4, 2, 3))
    heads = lambda t: t.reshape(1, nd, dec_t, N_HEADS, HEAD_DIM)
    tail = lambda t: t[None, :, -(CONV_W - 1):, :]
    state = lambda t, n: t.reshape(1, n, N_HEADS, HEAD_DIM, D_STATE)
    return (y_prompt, y_sample.reshape(nd, dec_t, D_MODEL), from_t(k_t), from_t(v_t),
            heads(ks), heads(vs), tail(xbc), tail(toks(xbcs)), state(ssm_p, bsz), state(ssm_s, nd))
```

```python
import functools

import numpy as np
import jax
import jax.numpy as jnp
from jax import lax
from jax.experimental import pallas as pl
from jax.experimental.pallas import tpu as pltpu

F32 = jnp.float32
BF16 = jnp.bfloat16

D_MODEL = 1024
HEAD_DIM = 64
D_ATTN = 512
D_SSD = 512
N_HEADS = 8
SSD_GROUPS = 2
GROUP_W = D_SSD // SSD_GROUPS
D_STATE = 128
CONV_W = 4
CONV_DIM = D_SSD + 2 * SSD_GROUPS * D_STATE
CHUNK = 128
D_FF = 4096
D_PLE = 256
LN_EPS = 1e-5
WINDOW_STEPS = 128
DILATIONS = (1, 4, 16)
PHASES = DILATIONS[2]
MID_PHASES = DILATIONS[1]
CACHE_LEN = 2048
DT_PAD = 128
D_PROJ = 3 * D_ATTN + D_SSD + CONV_DIM
SUBLANES = 8
LANES = 128
VMEM_LIMIT = 56 * 1024 * 1024
NEG_INF = float("-inf")


def _slopes():
    return np.array([2.0 ** (-8.0 * (h + 1) / N_HEADS) for h in range(N_HEADS)], dtype=np.float64)


def _layer_norm(x, g, b):
    mu = jnp.mean(x, -1, keepdims=True)
    xc = x - mu
    var = jnp.mean(xc * xc, -1, keepdims=True)
    return xc * lax.rsqrt(var + LN_EPS) * g + b


def _silu(x):
    return x * (1.0 / (1.0 + jnp.exp(-x)))


def _dot(a, b):
    return jnp.dot(a, b, preferred_element_type=F32)


def _dot_nt(a, b):
    return lax.dot_general(a, b, (((1,), (1,)), ((), ())), preferred_element_type=F32)


def _const_spec(shape):
    nd = len(shape)
    return pl.BlockSpec(shape, lambda *_: (0,) * nd, pipeline_mode=pl.Buffered(1))


HEAD_PAIRS = N_HEADS // 2


def _inproj_body(head_major, x_ref, g_ref, b_ref, w_ref, wdt_ref, *out_refs):
    tm = x_ref.shape[0]
    parts = [slice(i * tm // 2, (i + 1) * tm // 2) for i in range(2)]
    h = [_layer_norm(x_ref[r, :], g_ref[...], b_ref[...]).astype(BF16) for r in parts]
    proj = lambda lo, width: [_dot_nt(v, w_ref[lo:lo + width, :]) for v in h]
    if head_major:
        q_ref, k_ref, v_ref, kt_ref, vt_ref, z_ref, xbc_ref, dt_ref = out_refs
        for idx, (ref, t_ref) in enumerate(((q_ref, None), (k_ref, kt_ref), (v_ref, vt_ref))):
            for r, res in zip(parts, proj(idx * D_ATTN, D_ATTN)):
                for c in range(HEAD_PAIRS):
                    ref[c, r, :] = res[:, c * LANES:(c + 1) * LANES]
                if t_ref is not None:
                    t_ref[:, r] = res.T
    else:
        q_ref, k_ref, v_ref, kts_ref, vts_ref, z_ref, xbc_ref, dt_ref, slabs = out_refs
        for idx, (ref, ts_ref) in enumerate(((q_ref, None), (k_ref, kts_ref), (v_ref, vts_ref))):
            for r, res in zip(parts, proj(idx * D_ATTN, D_ATTN)):
                ref[r, :] = res
                if ts_ref is not None:
                    for c in range(HEAD_PAIRS):
                        slabs[c, r, :] = res[:, c * LANES:(c + 1) * LANES]
            if ts_ref is not None:
                for t in range(SHORT_SEQ):
                    for c in range(HEAD_PAIRS):
                        by_seq = slabs[c, pl.ds(t, tm // SHORT_SEQ, stride=SHORT_SEQ), :]
                        ts_ref[t, c * LANES:(c + 1) * LANES, :] = by_seq.T
    lo = 3 * D_ATTN
    for ref, width in ((z_ref, D_SSD), (xbc_ref, CONV_DIM)):
        for r, res in zip(parts, proj(lo, width)):
            ref[r, :] = res
        lo += width
    for r, v in zip(parts, h):
        dt_ref[r, :] = _dot_nt(v, wdt_ref[...])


def _in_proj(x, g, b, w, w_dt, *, head_major, tm):
    bsz, seq, _ = x.shape
    row = lambda width: pl.BlockSpec((None, tm, width), lambda i, j: (i, j, 0))
    row_shape = lambda width: jax.ShapeDtypeStruct((bsz, seq, width), F32)
    if head_major:
        slab = pl.BlockSpec((None, HEAD_PAIRS, tm, LANES), lambda i, j: (i, 0, j, 0))
        slab_shape = jax.ShapeDtypeStruct((bsz, HEAD_PAIRS, seq, LANES), F32)
        tr = pl.BlockSpec((None, D_ATTN, tm), lambda i, j: (i, 0, j))
        tr_shape = jax.ShapeDtypeStruct((bsz, D_ATTN, seq), F32)
        qkv_specs, qkv_shapes = [slab, slab, slab, tr, tr], [slab_shape] * 3 + [tr_shape] * 2
        scratch = []
    else:
        assert tm == SHORT_SEQ * LANES, "one tile = 128 short sequences"
        by_step = pl.BlockSpec((SHORT_SEQ, D_ATTN, LANES), lambda i, j: (0, 0, i * (seq // tm) + j))
        by_step_shape = jax.ShapeDtypeStruct((SHORT_SEQ, D_ATTN, bsz * seq // SHORT_SEQ), F32)
        qkv_specs = [row(D_ATTN)] * 3 + [by_step] * 2
        qkv_shapes = [row_shape(D_ATTN)] * 3 + [by_step_shape] * 2
        scratch = [pltpu.VMEM((HEAD_PAIRS, tm, LANES), F32)]
    rest = (D_SSD, CONV_DIM, DT_PAD)
    return pl.pallas_call(
        functools.partial(_inproj_body, head_major),
        grid=(bsz, seq // tm),
        in_specs=[row(D_MODEL), _const_spec((1, D_MODEL)), _const_spec((1, D_MODEL)),
                  _const_spec((D_PROJ, D_MODEL)), _const_spec((DT_PAD, D_MODEL))],
        out_specs=qkv_specs + [row(wd) for wd in rest],
        out_shape=qkv_shapes + [row_shape(wd) for wd in rest],
        scratch_shapes=scratch,
        compiler_params=pltpu.CompilerParams(dimension_semantics=("parallel", "parallel"),
                                             vmem_limit_bytes=VMEM_LIMIT),
        name="in_proj",
    )(x, g, b, w, w_dt)


FF_CHUNK = 1024
POST_STREAMS = 2


def _post_body(alpha, x_ref, attn_ref, ssd_ref, pe_ref, gin_ref, bin_ref, wout_ref, g1_ref, b1_ref,
               wup_ref, wdown_ref, g2_ref, b2_ref, wgate_ref, wple_ref, g3_ref, b3_ref, y_ref):
    tm = x_ref.shape[0]
    parts = [slice(i * tm // POST_STREAMS, (i + 1) * tm // POST_STREAMS) for i in range(POST_STREAMS)]
    each = lambda fn, *lists: [fn(*args) for args in zip(*lists)]
    xn = [_layer_norm(x_ref[r, :], gin_ref[...], bin_ref[...]) for r in parts]
    mixed = [jnp.concatenate([attn_ref[c, r, :] for c in range(HEAD_PAIRS)] + [ssd_ref[r, :]],
                             axis=-1).astype(BF16) for r in parts]
    proj = [_dot(m, wout_ref[...]) for m in mixed]
    h = each(lambda x, p: _layer_norm(alpha * x + p, g1_ref[...], b1_ref[...]), xn, proj)
    hb = [v.astype(BF16) for v in h]
    u = [None] * POST_STREAMS
    for c in range(D_FF // FF_CHUNK):
        cols = slice(c * FF_CHUNK, (c + 1) * FF_CHUNK)
        a = [jnp.maximum(_dot(v, wup_ref[:, cols]), 0.0) for v in hb]
        part = [_dot((v * v).astype(BF16), wdown_ref[cols, :]) for v in a]
        u = part if c == 0 else each(lambda s, p: s + p, u, part)
    h = each(lambda v, w: _layer_norm(alpha * v + w, g2_ref[...], b2_ref[...]), h, u)
    gate = [1.0 / (1.0 + jnp.exp(-_dot(v.astype(BF16), wgate_ref[...]))) for v in h]
    emb = [_dot(pe_ref[r, :].astype(BF16), wple_ref[...]) for r in parts]
    for r, v, g, e in zip(parts, h, gate, emb):
        y_ref[r, :] = _layer_norm(alpha * v + g * e, g3_ref[...], b3_ref[...])


def _post(x, attn, ssd, pe, gin, bin_, wout, g1, b1, wup, wdown, g2, b2, wgate, wple, g3, b3, *,
          alpha, tm):
    bsz, seq, _ = x.shape
    row = lambda width: pl.BlockSpec((None, tm, width), lambda i, j: (i, j, 0))
    slab = pl.BlockSpec((None, HEAD_PAIRS, tm, LANES), lambda i, j: (i, 0, j, 0))
    vec = _const_spec((1, D_MODEL))
    return pl.pallas_call(
        functools.partial(_post_body, alpha),
        grid=(bsz, seq // tm),
        in_specs=[row(D_MODEL), slab, row(D_SSD), row(D_PLE), vec, vec,
                  _const_spec((D_MODEL, D_MODEL)), vec, vec,
                  _const_spec((D_MODEL, D_FF)), _const_spec((D_FF, D_MODEL)), vec, vec,
                  _const_spec((D_MODEL, D_MODEL)), _const_spec((D_PLE, D_MODEL)), vec, vec],
        out_specs=row(D_MODEL),
        out_shape=jax.ShapeDtypeStruct((bsz, seq, D_MODEL), F32),
        compiler_params=pltpu.CompilerParams(dimension_semantics=("parallel", "parallel"),
                                             vmem_limit_bytes=VMEM_LIMIT),
        name="post",
    )(x, attn, ssd, pe, gin, bin_, wout, g1, b1, wup, wdown, g2, b2, wgate, wple, g3, b3)


U_ROWS = CACHE_LEN // PHASES
B2_ROWS = U_ROWS // (PHASES // MID_PHASES)
TAB3_LO, TAB2_LO, TAB1_LO, TAB_W = 0, WINDOW_STEPS, 3 * WINDOW_STEPS, 5 * WINDOW_STEPS
WAVE = 4
B1_IN_FLIGHT = 8


def _prompt_bias_tables():
    slopes = _slopes()
    u = np.arange(U_ROWS)
    d3 = (u[:, None] - u[None, :]).astype(np.float64)
    k4, ul4 = np.meshgrid(np.arange(PHASES // MID_PHASES), np.arange(B2_ROWS), indexing="ij")
    j2 = (PHASES // MID_PHASES * ul4 + k4).reshape(-1)
    d2 = j2[:, None] - np.concatenate([j2 - WINDOW_STEPS, j2])[None, :]
    p16, ul16 = np.meshgrid(np.arange(PHASES), np.arange(SUBLANES), indexing="ij")
    t1 = (PHASES * ul16 + p16).reshape(-1)
    d1 = t1[:, None] - (np.arange(2 * WINDOW_STEPS) - WINDOW_STEPS)[None, :]
    tabs = []
    for dist, dil in zip((d3, d2, d1), reversed(DILATIONS)):
        valid = (dist >= 0) & (dist <= WINDOW_STEPS)
        per_head = [np.where(valid, -slopes[h] * dist * dil, NEG_INF) for h in range(N_HEADS)]
        tabs.append(np.stack(per_head))
    tab = np.concatenate(tabs, axis=-1)
    return jnp.asarray(tab.reshape(HEAD_PAIRS, 2 * U_ROWS, TAB_W), dtype=F32)


def _attn_blocks(blocks, first_head):
    half = U_ROWS
    state = [dict() for _ in blocks]

    def scores(i):
        q2, k, _, bias = blocks[i]
        state[i]["s"] = _dot_nt(q2, k) + bias

    def top(i):
        state[i]["m"] = jnp.max(state[i]["s"], -1, keepdims=True)

    def weights(i):
        e = jnp.exp(state[i].pop("s") - state[i]["m"])
        state[i]["den"] = jnp.sum(e, -1, keepdims=True)
        state[i]["e"] = e.astype(BF16)

    def values(i):
        state[i]["pv"] = _dot(state[i].pop("e"), blocks[i][2])

    def finish(i):
        pv, m, den = state[i]["pv"], state[i]["m"], state[i]["den"]
        o = jnp.where(first_head, pv[:half], pv[half:])
        m2 = jnp.where(first_head, m[:half], m[half:])
        den2 = jnp.where(first_head, den[:half], den[half:])
        state[i] = (o / den2, m2 + jnp.log(den2))

    stages = (scores, top, weights, values, finish)
    groups = [range(j, min(j + WAVE, len(blocks))) for j in range(0, len(blocks), WAVE)]
    for t in range(len(groups) + len(stages) - 1):
        for g, members in enumerate(groups):
            if 0 <= t - g < len(stages):
                for i in members:
                    stages[t - g](i)
    return state


def _attn_prompt_body(q_ref, k_ref, v_ref, tab_ref, o_ref, qs, q0p, q1p, kp, vp, kn, vn, quarter,
                      o_scr, l_scr):
    lane = lax.broadcasted_iota(jnp.int32, (U_ROWS, LANES), 1)
    first_head = lane < HEAD_DIM
    scale = HEAD_DIM ** -0.5

    kn[...] = k_ref[...].astype(BF16)
    vn[...] = v_ref[...].astype(BF16)

    def split(src_ref, emit):
        for r in range(MID_PHASES):
            quarter[r] = src_ref[pl.ds(r, CACHE_LEN // MID_PHASES, stride=MID_PHASES), :]
        for r in range(MID_PHASES):
            for kk in range(PHASES // MID_PHASES):
                emit(MID_PHASES * kk + r,
                     quarter[r, pl.ds(kk, U_ROWS, stride=PHASES // MID_PHASES), :])

    def emit_q(p, tile):
        tile = tile * scale
        qs[p] = tile
        q0p[p] = jnp.where(first_head, tile, 0.0).astype(BF16)
        q1p[p] = jnp.where(first_head, 0.0, tile).astype(BF16)

    def emit_to(dst):
        def emit(p, tile):
            dst[p] = tile.astype(BF16)
        return emit

    split(q_ref, emit_q)
    split(k_ref, emit_to(kp))
    split(v_ref, emit_to(vp))

    bias = tab_ref[:, TAB3_LO:TAB3_LO + U_ROWS]
    outs = _attn_blocks([(jnp.concatenate([q0p[p], q1p[p]], 0), kp[p], vp[p], bias)
                         for p in range(PHASES)], first_head)
    for p, (o, l) in enumerate(outs):
        o_scr[0, p] = o
        l_scr[0, p] = l

    blocks, where = [], []
    for r in range(MID_PHASES):
        tiles = [MID_PHASES * kk + r for kk in range(PHASES // MID_PHASES)]
        gather = lambda src, rr, tiles=tiles: [src[p, rr, :] for p in tiles]
        for n in range(U_ROWS // B2_ROWS):
            rows = pl.ds(B2_ROWS * n, B2_ROWS)
            q2 = jnp.concatenate(gather(q0p, rows) + gather(q1p, rows), 0)
            if n == 0:
                k = jnp.concatenate(gather(kp, rows), 0)
                v = jnp.concatenate(gather(vp, rows), 0)
                bias = tab_ref[:, TAB2_LO + WINDOW_STEPS:TAB2_LO + 2 * WINDOW_STEPS]
            else:
                prev = pl.ds(B2_ROWS * (n - 1), B2_ROWS)
                k = jnp.concatenate(gather(kp, prev) + gather(kp, rows), 0)
                v = jnp.concatenate(gather(vp, prev) + gather(vp, rows), 0)
                bias = tab_ref[:, TAB2_LO:TAB2_LO + 2 * WINDOW_STEPS]
            blocks.append((q2, k, v, bias))
            where.append((tiles, rows))
    for (tiles, rows), (o, l) in zip(where, _attn_blocks(blocks, first_head)):
        for kk, p in enumerate(tiles):
            o_scr[1, p, rows, :] = o[B2_ROWS * kk:B2_ROWS * (kk + 1)]
            l_scr[1, p, rows, :] = l[B2_ROWS * kk:B2_ROWS * (kk + 1)]

    def branch1_blocks(ns):
        blocks = []
        for n in ns:
            rows = pl.ds(n * SUBLANES, SUBLANES)
            q = jnp.concatenate([qs[p, rows, :] for p in range(PHASES)], 0)
            q2 = jnp.concatenate([jnp.where(first_head, q, 0.0), jnp.where(first_head, 0.0, q)],
                                 0).astype(BF16)
            if n > 0:
                keys = pl.ds((n - 1) * U_ROWS, 2 * U_ROWS)
                bias = tab_ref[:, TAB1_LO:TAB1_LO + 2 * WINDOW_STEPS]
            else:
                keys = pl.ds(0, U_ROWS)
                bias = tab_ref[:, TAB1_LO + WINDOW_STEPS:TAB1_LO + 2 * WINDOW_STEPS]
            blocks.append((q2, kn[keys, :], vn[keys, :], bias))
        for n, (o, l) in zip(ns, _attn_blocks(blocks, first_head)):
            rows = pl.ds(n * SUBLANES, SUBLANES)
            for p in range(PHASES):
                o_scr[2, p, rows, :] = o[SUBLANES * p:SUBLANES * (p + 1)]
                l_scr[2, p, rows, :] = l[SUBLANES * p:SUBLANES * (p + 1)]

    n_blocks = U_ROWS // SUBLANES
    for first in range(0, n_blocks, B1_IN_FLIGHT):
        branch1_blocks(range(first, min(first + B1_IN_FLIGHT, n_blocks)))

    def phase_rows(p):
        return pl.ds(p, U_ROWS, stride=PHASES)

    for p in range(PHASES):
        l3, l2, l1 = l_scr[0, p], l_scr[1, p], l_scr[2, p]
        top = jnp.maximum(jnp.maximum(l3, l2), l1)
        e3, e2, e1 = jnp.exp(l3 - top), jnp.exp(l2 - top), jnp.exp(l1 - top)
        total = e3 + e2 + e1
        o_ref[phase_rows(p), :] = (e3 * o_scr[0, p] + e2 * o_scr[1, p] + e1 * o_scr[2, p]) / total


def _attn_prompt(q, k, v):
    bsz, _, seq, _ = k.shape
    assert seq == CACHE_LEN, "prompt attention is laid out for a 2048-token prompt"
    slab = pl.BlockSpec((None, None, seq, LANES), lambda b, c: (b, c, 0, 0))
    tile = (PHASES, U_ROWS, LANES)
    return pl.pallas_call(
        _attn_prompt_body,
        grid=(bsz, HEAD_PAIRS),
        in_specs=[slab, slab, slab,
                  pl.BlockSpec((None, 2 * U_ROWS, TAB_W), lambda b, c: (c, 0, 0))],
        out_specs=slab,
        out_shape=jax.ShapeDtypeStruct((bsz, HEAD_PAIRS, seq, LANES), F32),
        scratch_shapes=[pltpu.VMEM(tile, F32)] + [pltpu.VMEM(tile, BF16)] * 4
                       + [pltpu.VMEM((seq, LANES), BF16)] * 2
                       + [pltpu.VMEM((4, seq // 4, LANES), F32),
                          pltpu.VMEM((3,) + tile, F32), pltpu.VMEM((3,) + tile, F32)],
        compiler_params=pltpu.CompilerParams(dimension_semantics=("parallel", "parallel"),
                                             vmem_limit_bytes=VMEM_LIMIT),
        name="attn_prompt",
    )(q, k, v, _prompt_bias_tables())


DEC_T = 8
QH = DEC_T * N_HEADS
B2_SPAN, B1_SPAN = 512, 128
SAMPLE_TAB_W = CACHE_LEN + B2_SPAN + B1_SPAN


def _sample_bias_tables():
    slopes = _slopes()
    i = np.repeat(np.arange(DEC_T), N_HEADS)[:, None]
    sl = np.tile(slopes, DEC_T)[:, None]

    def cache_bias(span, dil):
        t = CACHE_LEN - span + np.arange(span)[None, :]
        dist = CACHE_LEN + i - t
        valid = (dist % dil == 0) & (dist <= WINDOW_STEPS * dil)
        return np.where(valid, -sl * dist, NEG_INF)

    cache_tab = np.concatenate([cache_bias(CACHE_LEN, 16), cache_bias(B2_SPAN, 4),
                                cache_bias(B1_SPAN, 1)], axis=1)
    c = np.arange(CHUNK)[None, :]
    dn = i - c
    is_new = c < DEC_T
    n1 = np.where(is_new & (dn >= 0), -sl * dn, NEG_INF)
    n2 = np.where(is_new & ((dn == 0) | (dn == 4)), -sl * dn, NEG_INF)
    n3 = np.where(is_new & (dn == 0), 0.0, NEG_INF)
    return jnp.asarray(cache_tab, dtype=F32), jnp.asarray(np.stack([n1, n2, n3]), dtype=F32)


def _head_selector():
    h_row = np.tile(np.arange(N_HEADS), DEC_T)[:, None]
    h_col = (np.arange(D_ATTN) // HEAD_DIM)[None, :]
    return jnp.asarray((h_row == h_col).astype(np.float32))


def _attn_sample_stages(q_ref, kn_ref, vn_ref, kt_ref, vt_ref, tab_ref, tabn_ref, sel_ref, o_ref):
    sel = sel_ref[...]
    q = q_ref[...] * (HEAD_DIM ** -0.5)
    q_rows = jnp.concatenate([jnp.broadcast_to(q[i:i + 1, :], (N_HEADS, D_ATTN))
                              for i in range(DEC_T)], 0)
    qall = (q_rows * sel).astype(BF16)
    pad = jnp.zeros((CHUNK - DEC_T, D_ATTN), F32)
    kn = jnp.concatenate([kn_ref[...], pad], 0).astype(BF16)
    vn = jnp.concatenate([vn_ref[...], pad], 0).astype(BF16)

    s_all = _dot(qall, kt_ref[...].astype(BF16))
    s_new = _dot_nt(qall, kn)
    yield
    lo2, lo1 = CACHE_LEN - B2_SPAN, CACHE_LEN - B1_SPAN

    def branch(s_cache, s_fresh, v_t):
        m = jnp.maximum(jnp.max(s_cache, -1, keepdims=True), jnp.max(s_fresh, -1, keepdims=True))
        e_c, e_f = jnp.exp(s_cache - m), jnp.exp(s_fresh - m)
        den = jnp.sum(e_c, -1, keepdims=True) + jnp.sum(e_f, -1, keepdims=True)
        acc = _dot_nt(e_c.astype(BF16), v_t.astype(BF16)) + _dot(e_f.astype(BF16), vn)
        return m, den, acc

    m3, den3, acc3 = branch(s_all + tab_ref[:, :CACHE_LEN], s_new + tabn_ref[2], vt_ref[...])
    m2, den2, acc2 = branch(s_all[:, lo2:] + tab_ref[:, CACHE_LEN:CACHE_LEN + B2_SPAN],
                            s_new + tabn_ref[1], vt_ref[:, lo2:])
    m1, den1, acc1 = branch(s_all[:, lo1:] + tab_ref[:, CACHE_LEN + B2_SPAN:],
                            s_new + tabn_ref[0], vt_ref[:, lo1:])

    top = jnp.maximum(jnp.maximum(m1, m2), m3)
    sc1, sc2, sc3 = den1 * jnp.exp(m1 - top), den2 * jnp.exp(m2 - top), den3 * jnp.exp(m3 - top)
    total = sc1 + sc2 + sc3
    mixed = ((sc1 / total) * (acc1 / den1) + (sc2 / total) * (acc2 / den2)
             + (sc3 / total) * (acc3 / den3)) * sel
    for i in range(DEC_T):
        row = jnp.sum(mixed[N_HEADS * i:N_HEADS * (i + 1)], axis=0, keepdims=True)
        for c in range(HEAD_PAIRS):
            o_ref[c, i:i + 1, :] = row[:, c * LANES:(c + 1) * LANES]


def _attn_sample_body(*refs):
    for _ in _attn_sample_stages(*refs):
        pass


CACHE_KT_ARG, CACHE_VT_ARG = 3, 4
CACHE_SLOTS = 3


def _attn_sample_call(q, k_new, v_new, cache_kt, cache_vt):
    n = q.shape[0]
    assert cache_kt.shape[1:] == (D_ATTN, CACHE_LEN) and q.shape[1] == DEC_T
    tok = pl.BlockSpec((None, DEC_T, D_ATTN), lambda i: (i, 0, 0))
    cache = pl.BlockSpec((None, D_ATTN, CACHE_LEN), lambda i: (i, 0, 0))
    cache_tab, new_tab = _sample_bias_tables()
    operands = (q, k_new, v_new, cache_kt, cache_vt, cache_tab, new_tab, _head_selector())
    in_specs = [tok, tok, tok, cache, cache, _const_spec((QH, SAMPLE_TAB_W)),
                _const_spec((3, QH, CHUNK)), _const_spec((QH, D_ATTN))]
    out_spec = pl.BlockSpec((None, HEAD_PAIRS, DEC_T, LANES), lambda i: (0, 0, i, 0))
    out_shape = jax.ShapeDtypeStruct((1, HEAD_PAIRS, n * DEC_T, LANES), F32)
    return operands, in_specs, out_spec, out_shape


def _attn_sample(q, k_new, v_new, cache_kt, cache_vt):
    operands, in_specs, out_spec, out_shape = _attn_sample_call(q, k_new, v_new, cache_kt, cache_vt)
    return pl.pallas_call(
        _attn_sample_body,
        grid=(q.shape[0],),
        in_specs=in_specs,
        out_specs=out_spec,
        out_shape=out_shape,
        compiler_params=pltpu.CompilerParams(dimension_semantics=("parallel",),
                                             vmem_limit_bytes=VMEM_LIMIT),
        name="attn_sample",
    )(*operands)


HEADS_PER_GROUP = N_HEADS // SSD_GROUPS
PAD_ROWS = SUBLANES
SHORT_SEQ = SUBLANES
SEQS_PER_TILE = CHUNK // SHORT_SEQ


def _ssd_stages(packed, first_chunk, z_ref, xbc_ref, dt_ref, cp_ref, h0_ref, cw_ref, cb_ref, dtb_ref,
                alog_ref, dsk_ref, nw_ref, y_ref, hn_ref, xpad, aux):
    if packed:
        aux[0:CHUNK, :] = cp_ref[...]
        aux[CHUNK:, :] = jnp.zeros((PAD_ROWS, CONV_DIM), F32)
        xpad[0:PAD_ROWS, :] = jnp.zeros((PAD_ROWS, CONV_DIM), F32)
    else:
        @pl.when(first_chunk())
        def _init():
            xpad[0:PAD_ROWS, :] = cp_ref[...]
            aux[...] = h0_ref[...]
    yield
    xpad[PAD_ROWS:, :] = xbc_ref[...]

    step = lax.broadcasted_iota(jnp.int32, (CHUNK, CONV_DIM), 0) % SHORT_SEQ
    conv = cb_ref[...]
    for back in range(CONV_W):
        tap = CONV_W - 1 - back
        rows_back = xpad[pl.ds(PAD_ROWS - back, CHUNK), :]
        if packed and back:
            rows_back = jnp.where(step < back, aux[pl.ds(PAD_ROWS - back, CHUNK), :], rows_back)
        conv = conv + rows_back * cw_ref[tap:tap + 1, :]
    xc = _silu(conv)
    xs = xc[:, :D_SSD]

    lane = lax.broadcasted_iota(jnp.int32, (CHUNK, DT_PAD), 1)
    dt_raw = dt_ref[...] + dtb_ref[...]
    dt = jnp.maximum(dt_raw, 0.0) + jnp.log1p(jnp.exp(-jnp.abs(dt_raw)))
    dt = jnp.where(lane < N_HEADS, dt, 0.0)
    adt = dt * (-jnp.exp(alog_ref[...]))

    ri = lax.broadcasted_iota(jnp.int32, (CHUNK, CHUNK), 0)
    ci = lax.broadcasted_iota(jnp.int32, (CHUNK, CHUNK), 1)
    causal = ri >= ci
    if packed:
        causal = causal & (ri // SHORT_SEQ == ci // SHORT_SEQ)
    exact_dot = functools.partial(jnp.dot, precision=lax.Precision.HIGHEST,
                                  preferred_element_type=F32)
    cs = exact_dot(jnp.where(causal, 1.0, 0.0).astype(F32), adt)
    cs_t = cs.T
    if packed:
        pick_last = ci == (ri // SHORT_SEQ) * SHORT_SEQ + (SHORT_SEQ - 1)
        cs_end = exact_dot(jnp.where(pick_last, 1.0, 0.0).astype(F32), cs)
    else:
        cs_end = cs[CHUNK - 1:CHUNK, :]

    head_of_lane = lax.broadcasted_iota(jnp.int32, (CHUNK, GROUP_W), 1) // HEAD_DIM

    def per_head(cols):
        out = cols[HEADS_PER_GROUP - 1]
        for hl in range(HEADS_PER_GROUP - 2, -1, -1):
            out = jnp.where(head_of_lane == hl, cols[hl], out)
        return out

    y_groups = []
    for g in range(SSD_GROUPS):
        heads = range(g * HEADS_PER_GROUP, (g + 1) * HEADS_PER_GROUP)
        grp = slice(g * GROUP_W, (g + 1) * GROUP_W)
        b_g = xc[:, D_SSD + g * D_STATE:D_SSD + (g + 1) * D_STATE].astype(BF16)
        c_g = xc[:, D_SSD + (SSD_GROUPS + g) * D_STATE:
                 D_SSD + (SSD_GROUPS + g + 1) * D_STATE].astype(BF16)
        cs_cols = [cs[:, h:h + 1] for h in heads]
        xdt_g = xs[:, grp] * per_head([dt[:, h:h + 1] for h in heads])
        xdt_b = xdt_g.astype(BF16)
        gram = _dot_nt(c_g, b_g)

        y_diag = jnp.zeros((CHUNK, GROUP_W), F32)
        for hl, h in enumerate(heads):
            seg = jnp.where(causal, cs_cols[hl] - cs_t[h:h + 1, :], NEG_INF)
            weights = (gram * jnp.exp(seg)).astype(BF16)
            y_diag = y_diag + jnp.where(head_of_lane == hl, _dot(weights, xdt_b), 0.0)

        to_end = per_head([jnp.exp(cs_end[:, h:h + 1] - cs_cols[hl]) for hl, h in enumerate(heads)])
        decayed_t = (xdt_g * to_end).T
        carried = per_head([jnp.exp(col) for col in cs_cols])
        if packed:
            h_prev = h0_ref[:, grp, :]
            wide = _dot_nt(c_g, h_prev.reshape(SEQS_PER_TILE * GROUP_W, D_STATE).astype(BF16))
            y_off = jnp.concatenate(
                [wide[s * SHORT_SEQ:(s + 1) * SHORT_SEQ, s * GROUP_W:(s + 1) * GROUP_W]
                 for s in range(SEQS_PER_TILE)], 0) * carried
            seq_of_step = lax.broadcasted_iota(jnp.int32, (GROUP_W, CHUNK), 1) // SHORT_SEQ
            per_seq = jnp.concatenate([jnp.where(seq_of_step == s, decayed_t, 0.0)
                                       for s in range(SEQS_PER_TILE)], 0).astype(BF16)
            new_states = _dot(per_seq, b_g).reshape(SEQS_PER_TILE, GROUP_W, D_STATE)
            for s in range(SEQS_PER_TILE):
                row = s * SHORT_SEQ
                keep = jnp.concatenate(
                    [jnp.broadcast_to(jnp.exp(cs_end[row:row + 1, h:h + 1]), (HEAD_DIM, D_STATE))
                     for h in heads], 0)
                hn_ref[s, grp, :] = h_prev[s] * keep + new_states[s]
        else:
            h_prev = aux[grp, :]
            y_off = _dot_nt(c_g, h_prev.astype(BF16)) * carried
            keep = jnp.concatenate(
                [jnp.broadcast_to(jnp.exp(cs_end[:, h:h + 1]), (HEAD_DIM, D_STATE)) for h in heads], 0)
            aux[grp, :] = h_prev * keep + _dot(decayed_t.astype(BF16), b_g)
        y_groups.append(y_diag + y_off)

    y = jnp.concatenate(y_groups, axis=-1) + dsk_ref[...] * xs
    y = y * _silu(z_ref[...])
    normed = []
    for g in range(SSD_GROUPS):
        yg = y[:, g * GROUP_W:(g + 1) * GROUP_W]
        normed.append(yg * lax.rsqrt(jnp.mean(yg * yg, -1, keepdims=True) + LN_EPS))
    y_ref[...] = jnp.concatenate(normed, axis=-1) * nw_ref[...]

    if not packed:
        hn_ref[...] = aux[...]
        xpad[0:PAD_ROWS, :] = xpad[CHUNK:, :]


def _ssd_body(*args):
    for _ in _ssd_stages(*args):
        pass


N_SSD_INPUTS = 11


def _ssd_call(z, xbc, dt, conv_prev, h0, cw, cb, dtb, alog, dsk, nw, where=None):
    n, seq, _ = z.shape
    packed = seq == SHORT_SEQ
    if packed:
        assert n % SEQS_PER_TILE == 0 and where is None
        grid = (n // SEQS_PER_TILE, 1)
        fold = lambda t: t.reshape(grid[0], CHUNK, t.shape[-1])
        z, xbc, dt, conv_prev = fold(z), fold(xbc), fold(dt), fold(conv_prev)
        history_shape, state_shape = (None, CHUNK, CONV_DIM), (SEQS_PER_TILE, D_SSD, D_STATE)
        aux = pltpu.VMEM((CHUNK + PAD_ROWS, CONV_DIM), F32)
    else:
        assert seq % CHUNK == 0
        grid = (n, seq // CHUNK)
        history_shape, state_shape = (None, PAD_ROWS, CONV_DIM), (None, D_SSD, D_STATE)
        aux = pltpu.VMEM((D_SSD, D_STATE), F32)
    where = where or (lambda s, c: (s, c))
    tile = lambda width: pl.BlockSpec((None, CHUNK, width), lambda *g: where(*g) + (0,))
    per_seq = lambda shape: pl.BlockSpec(shape, lambda *g: (where(*g)[0], 0, 0))
    history, state = per_seq(history_shape), per_seq(state_shape)
    operands = (z, xbc, dt, conv_prev, h0, cw, cb, dtb, alog, dsk, nw)
    in_specs = [tile(D_SSD), tile(CONV_DIM), tile(DT_PAD), history, state,
                _const_spec((CONV_W, CONV_DIM)), _const_spec((1, CONV_DIM)),
                _const_spec((1, DT_PAD)), _const_spec((1, DT_PAD)),
                _const_spec((1, D_SSD)), _const_spec((1, D_SSD))]
    out_specs = [tile(D_SSD), state]
    out_shape = [jax.ShapeDtypeStruct(z.shape, F32), jax.ShapeDtypeStruct((n, D_SSD, D_STATE), F32)]
    scratch = [pltpu.VMEM((PAD_ROWS + CHUNK, CONV_DIM), F32), aux]
    return packed, grid, operands, in_specs, out_specs, out_shape, scratch


def _ssd(z, *rest):
    packed, grid, operands, in_specs, out_specs, out_shape, scratch = _ssd_call(z, *rest)
    y, h_new = pl.pallas_call(
        functools.partial(_ssd_body, packed, lambda: pl.program_id(1) == 0),
        grid=grid,
        in_specs=in_specs,
        out_specs=out_specs,
        out_shape=out_shape,
        scratch_shapes=scratch,
        compiler_params=pltpu.CompilerParams(dimension_semantics=("parallel", "arbitrary"),
                                             vmem_limit_bytes=VMEM_LIMIT),
        name="ssd",
    )(*operands)
    return y.reshape(z.shape), h_new


def _ssd_with_sample_attn(ssd_args, attn_args):
    z = ssd_args[0]
    nchunks = z.shape[1] // CHUNK
    steps = z.shape[0] * nchunks
    assert steps == attn_args[0].shape[0]
    where = lambda i: (i // nchunks, i % nchunks)
    _, _, ssd_ops, ssd_in, ssd_out, ssd_shape, scratch = _ssd_call(*ssd_args, where=where)
    attn_ops, attn_in, attn_out, attn_shape = _attn_sample_call(*attn_args)
    caches = (CACHE_KT_ARG, CACHE_VT_ARG)
    for arg in caches:
        attn_in[arg] = pl.BlockSpec(memory_space=pl.ANY)
    ring = pltpu.VMEM((CACHE_SLOTS, D_ATTN, CACHE_LEN), F32)

    def body(*refs):
        ssd_in_refs, refs = refs[:N_SSD_INPUTS], refs[N_SSD_INPUTS:]
        attn_in_refs, refs = list(refs[:len(attn_ops)]), refs[len(attn_ops):]
        y_ref, hn_ref, o_ref, xpad, aux, kt_ring, vt_ring, sems = refs
        step = pl.program_id(0)

        def fetch(s):
            slot = s % CACHE_SLOTS
            return [pltpu.make_async_copy(attn_in_refs[arg].at[s], buf.at[slot], sems.at[j, slot])
                    for j, (arg, buf) in enumerate(zip(caches, (kt_ring, vt_ring)))]

        @pl.when(step == 0)
        def _prime():
            for s in range(CACHE_SLOTS - 1):
                for copy in fetch(s):
                    copy.start()

        @pl.when(step + (CACHE_SLOTS - 1) < steps)
        def _ahead():
            for copy in fetch(step + (CACHE_SLOTS - 1)):
                copy.start()

        for copy in fetch(step):
            copy.wait()
        for arg, buf in zip(caches, (kt_ring, vt_ring)):
            attn_in_refs[arg] = buf.at[step % CACHE_SLOTS]

        ssd = _ssd_stages(False, lambda: step % nchunks == 0, *ssd_in_refs, y_ref, hn_ref, xpad, aux)
        attn = _attn_sample_stages(*attn_in_refs, o_ref)
        for stage in (ssd, attn, ssd, attn):
            next(stage, None)

    y, h_new, attn = pl.pallas_call(
        body,
        grid=(steps,),
        in_specs=ssd_in + attn_in,
        out_specs=ssd_out + [attn_out],
        out_shape=ssd_shape + [attn_shape],
        scratch_shapes=scratch + [ring, ring, pltpu.SemaphoreType.DMA((len(caches), CACHE_SLOTS))],
        compiler_params=pltpu.CompilerParams(dimension_semantics=("arbitrary",),
                                             vmem_limit_bytes=VMEM_LIMIT),
        name="ssd_attn_sample",
    )(*ssd_ops, *attn_ops)
    return y, h_new, attn


def _row(v, width=None):
    v = v.reshape(1, -1).astype(F32)
    if width is not None and v.shape[1] < width:
        v = jnp.pad(v, ((0, 0), (0, width - v.shape[1])))
    return v


def kernel(x_prompt, x_sample, cache_k, cache_v, state_conv, state_ssm, p_prompt, p_sample,
           ln_in_g, ln_in_b, w_in, conv_w, conv_b, dt_bias, a_log, d_skip, ssd_norm_w, w_out,
           ln1_g, ln1_b, w_up, w_down, ln2_g, ln2_b, w_gate, w_ple, ln3_g, ln3_b):
    depth = w_in.shape[0]
    assert depth == 1, "single-layer step"
    alpha = (2 * depth) ** 0.25
    bsz, seq, _ = x_prompt.shape
    nd, dec_t, _ = x_sample.shape
    lyr = 0

    w_t = jnp.transpose(w_in[lyr])
    w_proj = w_t[:D_PROJ].astype(BF16)
    w_dt = jnp.pad(w_t[D_PROJ:], ((0, DT_PAD - N_HEADS), (0, 0))).astype(BF16)
    gin, bin_ = _row(ln_in_g), _row(ln_in_b)
    ssd_params = (conv_w[lyr].astype(F32), _row(conv_b[lyr]), _row(dt_bias[lyr], DT_PAD),
                  _row(a_log[lyr], DT_PAD), _row(jnp.repeat(d_skip[lyr], HEAD_DIM)),
                  _row(ssd_norm_w[lyr]))
    post_params = (gin, bin_, w_out[lyr].astype(BF16), _row(ln1_g[lyr]), _row(ln1_b[lyr]),
                   w_up[lyr].astype(BF16), w_down[lyr].astype(BF16), _row(ln2_g[lyr]),
                   _row(ln2_b[lyr]), w_gate[lyr].astype(BF16), w_ple[lyr].astype(BF16),
                   _row(ln3_g[lyr]), _row(ln3_b[lyr]))

    q, k, v, k_t, v_t, z, xbc, dt = _in_proj(x_prompt, gin, bin_, w_proj, w_dt, head_major=True, tm=1024)
    n_tok = nd * dec_t
    flat = lambda t: t.reshape(1, n_tok, t.shape[-1])
    toks = lambda t: t.reshape(nd, dec_t, t.shape[-1])
    qs, ks, vs, ks_t, vs_t, zs, xbcs, dts = _in_proj(flat(x_sample), gin, bin_, w_proj, w_dt,
                                                   head_major=False, tm=SHORT_SEQ * LANES)

    attn = _attn_prompt(q, k, v)
    transposed = lambda c: jnp.transpose(c, (0, 2, 3, 1)).reshape(nd, D_ATTN, CACHE_LEN)
    ssd_args = (z, xbc, dt, jnp.zeros((bsz, PAD_ROWS, CONV_DIM), F32),
                jnp.zeros((bsz, D_SSD, D_STATE), F32)) + ssd_params
    attn_args = (toks(qs), toks(ks), toks(vs), transposed(cache_k[lyr]), transposed(cache_v[lyr]))
    if bsz * (seq // CHUNK) == nd:
        ssd_y, ssm_p, attn_s = _ssd_with_sample_attn(ssd_args, attn_args)
    else:
        ssd_y, ssm_p = _ssd(*ssd_args)
        attn_s = _attn_sample(*attn_args)
    conv_prev = jnp.pad(state_conv[lyr].astype(F32), ((0, 0), (PAD_ROWS - (CONV_W - 1), 0), (0, 0)))
    ssd_s, ssm_s = _ssd(toks(zs), toks(xbcs), toks(dts), conv_prev,
                        state_ssm[lyr].reshape(nd, D_SSD, D_STATE).astype(F32), *ssd_params)

    y_prompt = _post(x_prompt, attn, ssd_y, p_prompt[lyr], *post_params, alpha=alpha, tm=512)
    y_sample = _post(flat(x_sample), attn_s, flat(ssd_s), flat(p_sample[lyr]), *post_params,
                     alpha=alpha, tm=256)

    from_t = lambda t: jnp.transpose(t.reshape(1, bsz, N_HEADS, HEAD_DIM, seq), (0, 1, 4, 2, 3))
    by_step = lambda t: jnp.transpose(t.reshape(1, dec_t, N_HEADS, HEAD_DIM, nd), (0, 4, 1, 2, 3))
    tail = lambda t: t[None, :, -(CONV_W - 1):, :]
    state = lambda t, n: t.reshape(1, n, N_HEADS, HEAD_DIM, D_STATE)
    return (y_prompt, y_sample.reshape(nd, dec_t, D_MODEL), from_t(k_t), from_t(v_t),
            by_step(ks_t), by_step(vs_t), tail(xbc), tail(toks(xbcs)), state(ssm_p, bsz), state(ssm_s, nd))
```

```python
import functools

import numpy as np
import jax
import jax.numpy as jnp
from jax import lax
from jax.experimental import pallas as pl
from jax.experimental.pallas import tpu as pltpu

F32 = jnp.float32
BF16 = jnp.bfloat16

D_MODEL = 1024
HEAD_DIM = 64
D_ATTN = 512
D_SSD = 512
N_HEADS = 8
SSD_GROUPS = 2
GROUP_W = D_SSD // SSD_GROUPS
D_STATE = 128
CONV_W = 4
CONV_DIM = D_SSD + 2 * SSD_GROUPS * D_STATE
CHUNK = 128
D_FF = 4096
D_PLE = 256
LN_EPS = 1e-5
WINDOW_STEPS = 128
DILATIONS = (1, 4, 16)
PHASES = DILATIONS[2]
MID_PHASES = DILATIONS[1]
CACHE_LEN = 2048
DT_PAD = 128
D_PROJ = 3 * D_ATTN + D_SSD + CONV_DIM
SUBLANES = 8
LANES = 128
VMEM_LIMIT = 56 * 1024 * 1024
NEG_INF = float("-inf")


def _slopes():
    return np.array([2.0 ** (-8.0 * (h + 1) / N_HEADS) for h in range(N_HEADS)], dtype=np.float64)


def _layer_norm(x, g, b):
    mu = jnp.mean(x, -1, keepdims=True)
    xc = x - mu
    var = jnp.mean(xc * xc, -1, keepdims=True)
    return xc * lax.rsqrt(var + LN_EPS) * g + b


def _silu(x):
    return x * (1.0 / (1.0 + jnp.exp(-x)))


def _dot(a, b):
    return jnp.dot(a, b, preferred_element_type=F32)


def _dot_nt(a, b):
    return lax.dot_general(a, b, (((1,), (1,)), ((), ())), preferred_element_type=F32)


def _const_spec(shape):
    nd = len(shape)
    return pl.BlockSpec(shape, lambda *_: (0,) * nd, pipeline_mode=pl.Buffered(1))


HEAD_PAIRS = N_HEADS // 2


def _inproj_body(head_major, x_ref, g_ref, b_ref, w_ref, wdt_ref, *out_refs):
    tm = x_ref.shape[0]
    parts = [slice(i * tm // 2, (i + 1) * tm // 2) for i in range(2)]
    h = [_layer_norm(x_ref[r, :], g_ref[...], b_ref[...]).astype(BF16) for r in parts]
    proj = lambda lo, width: [_dot_nt(v, w_ref[lo:lo + width, :]) for v in h]
    if head_major:
        q_ref, k_ref, v_ref, kt_ref, vt_ref, z_ref, xbc_ref, dt_ref = out_refs
        for idx, (ref, t_ref) in enumerate(((q_ref, None), (k_ref, kt_ref), (v_ref, vt_ref))):
            for r, res in zip(parts, proj(idx * D_ATTN, D_ATTN)):
                for c in range(HEAD_PAIRS):
                    ref[c, r, :] = res[:, c * LANES:(c + 1) * LANES]
                if t_ref is not None:
                    t_ref[:, r] = res.T
    else:
        q_ref, k_ref, v_ref, kts_ref, vts_ref, z_ref, xbc_ref, dt_ref, slabs = out_refs
        for idx, (ref, ts_ref) in enumerate(((q_ref, None), (k_ref, kts_ref), (v_ref, vts_ref))):
            for r, res in zip(parts, proj(idx * D_ATTN, D_ATTN)):
                ref[r, :] = res
                if ts_ref is not None:
                    for c in range(HEAD_PAIRS):
                        slabs[c, r, :] = res[:, c * LANES:(c + 1) * LANES]
            if ts_ref is not None:
                for t in range(SHORT_SEQ):
                    for c in range(HEAD_PAIRS):
                        by_seq = slabs[c, pl.ds(t, tm // SHORT_SEQ, stride=SHORT_SEQ), :]
                        ts_ref[t, c * LANES:(c + 1) * LANES, :] = by_seq.T
    lo = 3 * D_ATTN
    for ref, width in ((z_ref, D_SSD), (xbc_ref, CONV_DIM)):
        for r, res in zip(parts, proj(lo, width)):
            ref[r, :] = res
        lo += width
    for r, v in zip(parts, h):
        dt_ref[r, :] = _dot_nt(v, wdt_ref[...])


def _in_proj(x, g, b, w, w_dt, *, head_major, tm):
    bsz, seq, _ = x.shape
    row = lambda width: pl.BlockSpec((None, tm, width), lambda i, j: (i, j, 0))
    row_shape = lambda width: jax.ShapeDtypeStruct((bsz, seq, width), F32)
    if head_major:
        slab = pl.BlockSpec((None, HEAD_PAIRS, tm, LANES), lambda i, j: (i, 0, j, 0))
        slab_shape = jax.ShapeDtypeStruct((bsz, HEAD_PAIRS, seq, LANES), F32)
        tr = pl.BlockSpec((None, D_ATTN, tm), lambda i, j: (i, 0, j))
        tr_shape = jax.ShapeDtypeStruct((bsz, D_ATTN, seq), F32)
        qkv_specs, qkv_shapes = [slab, slab, slab, tr, tr], [slab_shape] * 3 + [tr_shape] * 2
        scratch = []
    else:
        assert tm == SHORT_SEQ * LANES, "one tile = 128 short sequences"
        by_step = pl.BlockSpec((SHORT_SEQ, D_ATTN, LANES), lambda i, j: (0, 0, i * (seq // tm) + j))
        by_step_shape = jax.ShapeDtypeStruct((SHORT_SEQ, D_ATTN, bsz * seq // SHORT_SEQ), F32)
        qkv_specs = [row(D_ATTN)] * 3 + [by_step] * 2
        qkv_shapes = [row_shape(D_ATTN)] * 3 + [by_step_shape] * 2
        scratch = [pltpu.VMEM((HEAD_PAIRS, tm, LANES), F32)]
    rest = (D_SSD, CONV_DIM, DT_PAD)
    return pl.pallas_call(
        functools.partial(_inproj_body, head_major),
        grid=(bsz, seq // tm),
        in_specs=[row(D_MODEL), _const_spec((1, D_MODEL)), _const_spec((1, D_MODEL)),
                  _const_spec((D_PROJ, D_MODEL)), _const_spec((DT_PAD, D_MODEL))],
        out_specs=qkv_specs + [row(wd) for wd in rest],
        out_shape=qkv_shapes + [row_shape(wd) for wd in rest],
        scratch_shapes=scratch,
        compiler_params=pltpu.CompilerParams(dimension_semantics=("parallel", "parallel"),
                                             vmem_limit_bytes=VMEM_LIMIT),
        name="in_proj",
    )(x, g, b, w, w_dt)


FF_CHUNK = 1024
POST_STREAMS = 2


def _post_body(alpha, x_ref, attn_ref, ssd_ref, pe_ref, gin_ref, bin_ref, wout_ref, g1_ref, b1_ref,
               wup_ref, wdown_ref, g2_ref, b2_ref, wgate_ref, wple_ref, g3_ref, b3_ref, y_ref):
    tm = x_ref.shape[0]
    parts = [slice(i * tm // POST_STREAMS, (i + 1) * tm // POST_STREAMS) for i in range(POST_STREAMS)]
    each = lambda fn, *lists: [fn(*args) for args in zip(*lists)]
    xn = [_layer_norm(x_ref[r, :], gin_ref[...], bin_ref[...]) for r in parts]
    mixed = [jnp.concatenate([attn_ref[c, r, :] for c in range(HEAD_PAIRS)] + [ssd_ref[r, :]],
                             axis=-1).astype(BF16) for r in parts]
    proj = [_dot(m, wout_ref[...]) for m in mixed]
    h = each(lambda x, p: _layer_norm(alpha * x + p, g1_ref[...], b1_ref[...]), xn, proj)
    hb = [v.astype(BF16) for v in h]
    u = [None] * POST_STREAMS
    for c in range(D_FF // FF_CHUNK):
        cols = slice(c * FF_CHUNK, (c + 1) * FF_CHUNK)
        a = [jnp.maximum(_dot(v, wup_ref[:, cols]), 0.0) for v in hb]
        part = [_dot((v * v).astype(BF16), wdown_ref[cols, :]) for v in a]
        u = part if c == 0 else each(lambda s, p: s + p, u, part)
    h = each(lambda v, w: _layer_norm(alpha * v + w, g2_ref[...], b2_ref[...]), h, u)
    gate = [1.0 / (1.0 + jnp.exp(-_dot(v.astype(BF16), wgate_ref[...]))) for v in h]
    emb = [_dot(pe_ref[r, :].astype(BF16), wple_ref[...]) for r in parts]
    for r, v, g, e in zip(parts, h, gate, emb):
        y_ref[r, :] = _layer_norm(alpha * v + g * e, g3_ref[...], b3_ref[...])


def _post(x, attn, ssd, pe, gin, bin_, wout, g1, b1, wup, wdown, g2, b2, wgate, wple, g3, b3, *,
          alpha, tm):
    bsz, seq, _ = x.shape
    row = lambda width: pl.BlockSpec((None, tm, width), lambda i, j: (i, j, 0))
    slab = pl.BlockSpec((None, HEAD_PAIRS, tm, LANES), lambda i, j: (i, 0, j, 0))
    vec = _const_spec((1, D_MODEL))
    return pl.pallas_call(
        functools.partial(_post_body, alpha),
        grid=(bsz, seq // tm),
        in_specs=[row(D_MODEL), slab, row(D_SSD), row(D_PLE), vec, vec,
                  _const_spec((D_MODEL, D_MODEL)), vec, vec,
                  _const_spec((D_MODEL, D_FF)), _const_spec((D_FF, D_MODEL)), vec, vec,
                  _const_spec((D_MODEL, D_MODEL)), _const_spec((D_PLE, D_MODEL)), vec, vec],
        out_specs=row(D_MODEL),
        out_shape=jax.ShapeDtypeStruct((bsz, seq, D_MODEL), F32),
        compiler_params=pltpu.CompilerParams(dimension_semantics=("parallel", "parallel"),
                                             vmem_limit_bytes=VMEM_LIMIT),
        name="post",
    )(x, attn, ssd, pe, gin, bin_, wout, g1, b1, wup, wdown, g2, b2, wgate, wple, g3, b3)


U_ROWS = CACHE_LEN // PHASES
B2_ROWS = U_ROWS // (PHASES // MID_PHASES)
TAB3_LO, TAB2_LO, TAB1_LO, TAB_W = 0, WINDOW_STEPS, 3 * WINDOW_STEPS, 5 * WINDOW_STEPS
WAVE = 4
B1_IN_FLIGHT = 8


def _prompt_bias_tables():
    slopes = _slopes()
    u = np.arange(U_ROWS)
    d3 = (u[:, None] - u[None, :]).astype(np.float64)
    k4, ul4 = np.meshgrid(np.arange(PHASES // MID_PHASES), np.arange(B2_ROWS), indexing="ij")
    j2 = (PHASES // MID_PHASES * ul4 + k4).reshape(-1)
    d2 = j2[:, None] - np.concatenate([j2 - WINDOW_STEPS, j2])[None, :]
    p16, ul16 = np.meshgrid(np.arange(PHASES), np.arange(SUBLANES), indexing="ij")
    t1 = (PHASES * ul16 + p16).reshape(-1)
    d1 = t1[:, None] - (np.arange(2 * WINDOW_STEPS) - WINDOW_STEPS)[None, :]
    tabs = []
    for dist, dil in zip((d3, d2, d1), reversed(DILATIONS)):
        valid = (dist >= 0) & (dist <= WINDOW_STEPS)
        per_head = [np.where(valid, -slopes[h] * dist * dil, NEG_INF) for h in range(N_HEADS)]
        tabs.append(np.stack(per_head))
    tab = np.concatenate(tabs, axis=-1)
    return jnp.asarray(tab.reshape(HEAD_PAIRS, 2 * U_ROWS, TAB_W), dtype=F32)


def _attn_blocks(blocks, first_head):
    half = U_ROWS
    state = [dict() for _ in blocks]

    def scores(i):
        q2, k, _, bias = blocks[i]
        state[i]["s"] = _dot_nt(q2, k) + bias

    def top(i):
        state[i]["m"] = jnp.max(state[i]["s"], -1, keepdims=True)

    def weights(i):
        e = jnp.exp(state[i].pop("s") - state[i]["m"])
        state[i]["den"] = jnp.sum(e, -1, keepdims=True)
        state[i]["e"] = e.astype(BF16)

    def values(i):
        state[i]["pv"] = _dot(state[i].pop("e"), blocks[i][2])

    def finish(i):
        pv, m, den = state[i]["pv"], state[i]["m"], state[i]["den"]
        o = jnp.where(first_head, pv[:half], pv[half:])
        m2 = jnp.where(first_head, m[:half], m[half:])
        den2 = jnp.where(first_head, den[:half], den[half:])
        state[i] = (o / den2, m2 + jnp.log(den2))

    stages = (scores, top, weights, values, finish)
    groups = [range(j, min(j + WAVE, len(blocks))) for j in range(0, len(blocks), WAVE)]
    for t in range(len(groups) + len(stages) - 1):
        for g, members in enumerate(groups):
            if 0 <= t - g < len(stages):
                for i in members:
                    stages[t - g](i)
    return state


def _attn_prompt_body(q_ref, k_ref, v_ref, tab_ref, o_ref, qs, q0p, q1p, kp, vp, kn, vn, quarter,
                      o_scr, l_scr):
    lane = lax.broadcasted_iota(jnp.int32, (U_ROWS, LANES), 1)
    first_head = lane < HEAD_DIM
    scale = HEAD_DIM ** -0.5

    kn[...] = k_ref[...].astype(BF16)
    vn[...] = v_ref[...].astype(BF16)

    def split(src_ref, emit):
        for r in range(MID_PHASES):
            quarter[r] = src_ref[pl.ds(r, CACHE_LEN // MID_PHASES, stride=MID_PHASES), :]
        for r in range(MID_PHASES):
            for kk in range(PHASES // MID_PHASES):
                emit(MID_PHASES * kk + r,
                     quarter[r, pl.ds(kk, U_ROWS, stride=PHASES // MID_PHASES), :])

    def emit_q(p, tile):
        tile = tile * scale
        qs[p] = tile
        q0p[p] = jnp.where(first_head, tile, 0.0).astype(BF16)
        q1p[p] = jnp.where(first_head, 0.0, tile).astype(BF16)

    def emit_to(dst):
        def emit(p, tile):
            dst[p] = tile.astype(BF16)
        return emit

    split(q_ref, emit_q)
    split(k_ref, emit_to(kp))
    split(v_ref, emit_to(vp))

    bias = tab_ref[:, TAB3_LO:TAB3_LO + U_ROWS]
    outs = _attn_blocks([(jnp.concatenate([q0p[p], q1p[p]], 0), kp[p], vp[p], bias)
                         for p in range(PHASES)], first_head)
    for p, (o, l) in enumerate(outs):
        o_scr[0, p] = o
        l_scr[0, p] = l

    blocks, where = [], []
    for r in range(MID_PHASES):
        tiles = [MID_PHASES * kk + r for kk in range(PHASES // MID_PHASES)]
        gather = lambda src, rr, tiles=tiles: [src[p, rr, :] for p in tiles]
        for n in range(U_ROWS // B2_ROWS):
            rows = pl.ds(B2_ROWS * n, B2_ROWS)
            q2 = jnp.concatenate(gather(q0p, rows) + gather(q1p, rows), 0)
            if n == 0:
                k = jnp.concatenate(gather(kp, rows), 0)
                v = jnp.concatenate(gather(vp, rows), 0)
                bias = tab_ref[:, TAB2_LO + WINDOW_STEPS:TAB2_LO + 2 * WINDOW_STEPS]
            else:
                prev = pl.ds(B2_ROWS * (n - 1), B2_ROWS)
                k = jnp.concatenate(gather(kp, prev) + gather(kp, rows), 0)
                v = jnp.concatenate(gather(vp, prev) + gather(vp, rows), 0)
                bias = tab_ref[:, TAB2_LO:TAB2_LO + 2 * WINDOW_STEPS]
            blocks.append((q2, k, v, bias))
            where.append((tiles, rows))
    for (tiles, rows), (o, l) in zip(where, _attn_blocks(blocks, first_head)):
        for kk, p in enumerate(tiles):
            o_scr[1, p, rows, :] = o[B2_ROWS * kk:B2_ROWS * (kk + 1)]
            l_scr[1, p, rows, :] = l[B2_ROWS * kk:B2_ROWS * (kk + 1)]

    def branch1_blocks(ns):
        blocks = []
        for n in ns:
            rows = pl.ds(n * SUBLANES, SUBLANES)
            q = jnp.concatenate([qs[p, rows, :] for p in range(PHASES)], 0)
            q2 = jnp.concatenate([jnp.where(first_head, q, 0.0), jnp.where(first_head, 0.0, q)],
                                 0).astype(BF16)
            if n > 0:
                keys = pl.ds((n - 1) * U_ROWS, 2 * U_ROWS)
                bias = tab_ref[:, TAB1_LO:TAB1_LO + 2 * WINDOW_STEPS]
            else:
                keys = pl.ds(0, U_ROWS)
                bias = tab_ref[:, TAB1_LO + WINDOW_STEPS:TAB1_LO + 2 * WINDOW_STEPS]
            blocks.append((q2, kn[keys, :], vn[keys, :], bias))
        for n, (o, l) in zip(ns, _attn_blocks(blocks, first_head)):
            rows = pl.ds(n * SUBLANES, SUBLANES)
            for p in range(PHASES):
                o_scr[2, p, rows, :] = o[SUBLANES * p:SUBLANES * (p + 1)]
                l_scr[2, p, rows, :] = l[SUBLANES * p:SUBLANES * (p + 1)]

    n_blocks = U_ROWS // SUBLANES
    for first in range(0, n_blocks, B1_IN_FLIGHT):
        branch1_blocks(range(first, min(first + B1_IN_FLIGHT, n_blocks)))

    def phase_rows(p):
        return pl.ds(p, U_ROWS, stride=PHASES)

    for p in range(PHASES):
        l3, l2, l1 = l_scr[0, p], l_scr[1, p], l_scr[2, p]
        top = jnp.maximum(jnp.maximum(l3, l2), l1)
        e3, e2, e1 = jnp.exp(l3 - top), jnp.exp(l2 - top), jnp.exp(l1 - top)
        total = e3 + e2 + e1
        o_ref[phase_rows(p), :] = (e3 * o_scr[0, p] + e2 * o_scr[1, p] + e1 * o_scr[2, p]) / total


def _attn_prompt(q, k, v):
    bsz, _, seq, _ = k.shape
    assert seq == CACHE_LEN, "prompt attention is laid out for a 2048-token prompt"
    slab = pl.BlockSpec((None, None, seq, LANES), lambda b, c: (b, c, 0, 0))
    tile = (PHASES, U_ROWS, LANES)
    return pl.pallas_call(
        _attn_prompt_body,
        grid=(bsz, HEAD_PAIRS),
        in_specs=[slab, slab, slab,
                  pl.BlockSpec((None, 2 * U_ROWS, TAB_W), lambda b, c: (c, 0, 0))],
        out_specs=slab,
        out_shape=jax.ShapeDtypeStruct((bsz, HEAD_PAIRS, seq, LANES), F32),
        scratch_shapes=[pltpu.VMEM(tile, F32)] + [pltpu.VMEM(tile, BF16)] * 4
                       + [pltpu.VMEM((seq, LANES), BF16)] * 2
                       + [pltpu.VMEM((4, seq // 4, LANES), F32),
                          pltpu.VMEM((3,) + tile, F32), pltpu.VMEM((3,) + tile, F32)],
        compiler_params=pltpu.CompilerParams(dimension_semantics=("parallel", "parallel"),
                                             vmem_limit_bytes=VMEM_LIMIT),
        name="attn_prompt",
    )(q, k, v, _prompt_bias_tables())


DEC_T = 8
QH = DEC_T * N_HEADS
B2_SPAN, B1_SPAN = 512, 128
SAMPLE_TAB_W = CACHE_LEN + B2_SPAN + B1_SPAN


def _sample_bias_tables():
    slopes = _slopes()
    i = np.repeat(np.arange(DEC_T), N_HEADS)[:, None]
    sl = np.tile(slopes, DEC_T)[:, None]

    def cache_bias(span, dil):
        t = CACHE_LEN - span + np.arange(span)[None, :]
        dist = CACHE_LEN + i - t
        valid = (dist % dil == 0) & (dist <= WINDOW_STEPS * dil)
        return np.where(valid, -sl * dist, NEG_INF)

    cache_tab = np.concatenate([cache_bias(CACHE_LEN, 16), cache_bias(B2_SPAN, 4),
                                cache_bias(B1_SPAN, 1)], axis=1)
    c = np.arange(CHUNK)[None, :]
    dn = i - c
    is_new = c < DEC_T
    n1 = np.where(is_new & (dn >= 0), -sl * dn, NEG_INF)
    n2 = np.where(is_new & ((dn == 0) | (dn == 4)), -sl * dn, NEG_INF)
    n3 = np.where(is_new & (dn == 0), 0.0, NEG_INF)
    return jnp.asarray(cache_tab, dtype=F32), jnp.asarray(np.stack([n1, n2, n3]), dtype=F32)


def _head_selector():
    h_row = np.tile(np.arange(N_HEADS), DEC_T)[:, None]
    h_col = (np.arange(D_ATTN) // HEAD_DIM)[None, :]
    return jnp.asarray((h_row == h_col).astype(np.float32))


def _attn_sample_stages(q_ref, kn_ref, vn_ref, kt_ref, vt_ref, tab_ref, tabn_ref, sel_ref, o_ref):
    sel = sel_ref[...]
    q = q_ref[...] * (HEAD_DIM ** -0.5)
    q_rows = jnp.concatenate([jnp.broadcast_to(q[i:i + 1, :], (N_HEADS, D_ATTN))
                              for i in range(DEC_T)], 0)
    qall = (q_rows * sel).astype(BF16)
    pad = jnp.zeros((CHUNK - DEC_T, D_ATTN), F32)
    kn = jnp.concatenate([kn_ref[...], pad], 0).astype(BF16)
    vn = jnp.concatenate([vn_ref[...], pad], 0).astype(BF16)

    s_all = _dot(qall, kt_ref[...].astype(BF16))
    s_new = _dot_nt(qall, kn)
    yield
    lo2, lo1 = CACHE_LEN - B2_SPAN, CACHE_LEN - B1_SPAN

    def branch(s_cache, s_fresh, v_t):
        m = jnp.maximum(jnp.max(s_cache, -1, keepdims=True), jnp.max(s_fresh, -1, keepdims=True))
        e_c, e_f = jnp.exp(s_cache - m), jnp.exp(s_fresh - m)
        den = jnp.sum(e_c, -1, keepdims=True) + jnp.sum(e_f, -1, keepdims=True)
        acc = _dot_nt(e_c.astype(BF16), v_t.astype(BF16)) + _dot(e_f.astype(BF16), vn)
        return m, den, acc

    m3, den3, acc3 = branch(s_all + tab_ref[:, :CACHE_LEN], s_new + tabn_ref[2], vt_ref[...])
    m2, den2, acc2 = branch(s_all[:, lo2:] + tab_ref[:, CACHE_LEN:CACHE_LEN + B2_SPAN],
                            s_new + tabn_ref[1], vt_ref[:, lo2:])
    m1, den1, acc1 = branch(s_all[:, lo1:] + tab_ref[:, CACHE_LEN + B2_SPAN:],
                            s_new + tabn_ref[0], vt_ref[:, lo1:])

    top = jnp.maximum(jnp.maximum(m1, m2), m3)
    sc1, sc2, sc3 = den1 * jnp.exp(m1 - top), den2 * jnp.exp(m2 - top), den3 * jnp.exp(m3 - top)
    total = sc1 + sc2 + sc3
    mixed = ((sc1 / total) * (acc1 / den1) + (sc2 / total) * (acc2 / den2)
             + (sc3 / total) * (acc3 / den3)) * sel
    for i in range(DEC_T):
        row = jnp.sum(mixed[N_HEADS * i:N_HEADS * (i + 1)], axis=0, keepdims=True)
        for c in range(HEAD_PAIRS):
            o_ref[c, i:i + 1, :] = row[:, c * LANES:(c + 1) * LANES]


def _attn_sample_body(*refs):
    for _ in _attn_sample_stages(*refs):
        pass


CACHE_KT_ARG, CACHE_VT_ARG = 3, 4
CACHE_SLOTS = 4


def _attn_sample_call(q, k_new, v_new, cache_kt, cache_vt):
    n = q.shape[0]
    assert cache_kt.shape[1:] == (D_ATTN, CACHE_LEN) and q.shape[1] == DEC_T
    tok = pl.BlockSpec((None, DEC_T, D_ATTN), lambda i: (i, 0, 0))
    cache = pl.BlockSpec((None, D_ATTN, CACHE_LEN), lambda i: (i, 0, 0))
    cache_tab, new_tab = _sample_bias_tables()
    operands = (q, k_new, v_new, cache_kt, cache_vt, cache_tab, new_tab, _head_selector())
    in_specs = [tok, tok, tok, cache, cache, _const_spec((QH, SAMPLE_TAB_W)),
                _const_spec((3, QH, CHUNK)), _const_spec((QH, D_ATTN))]
    out_spec = pl.BlockSpec((None, HEAD_PAIRS, DEC_T, LANES), lambda i: (0, 0, i, 0))
    out_shape = jax.ShapeDtypeStruct((1, HEAD_PAIRS, n * DEC_T, LANES), F32)
    return operands, in_specs, out_spec, out_shape


def _attn_sample(q, k_new, v_new, cache_kt, cache_vt):
    operands, in_specs, out_spec, out_shape = _attn_sample_call(q, k_new, v_new, cache_kt, cache_vt)
    return pl.pallas_call(
        _attn_sample_body,
        grid=(q.shape[0],),
        in_specs=in_specs,
        out_specs=out_spec,
        out_shape=out_shape,
        compiler_params=pltpu.CompilerParams(dimension_semantics=("parallel",),
                                             vmem_limit_bytes=VMEM_LIMIT),
        name="attn_sample",
    )(*operands)


HEADS_PER_GROUP = N_HEADS // SSD_GROUPS
PAD_ROWS = SUBLANES
SHORT_SEQ = SUBLANES
SEQS_PER_TILE = CHUNK // SHORT_SEQ


def _ssd_stages(packed, first_chunk, z_ref, xbc_ref, dt_ref, cp_ref, h0_ref, cw_ref, cb_ref, dtb_ref,
                alog_ref, dsk_ref, nw_ref, y_ref, hn_ref, xpad, aux):
    if packed:
        aux[0:CHUNK, :] = cp_ref[...]
        aux[CHUNK:, :] = jnp.zeros((PAD_ROWS, CONV_DIM), F32)
        xpad[0:PAD_ROWS, :] = jnp.zeros((PAD_ROWS, CONV_DIM), F32)
    else:
        @pl.when(first_chunk())
        def _init():
            xpad[0:PAD_ROWS, :] = cp_ref[...]
            aux[...] = h0_ref[...]
    yield
    xpad[PAD_ROWS:, :] = xbc_ref[...]

    step = lax.broadcasted_iota(jnp.int32, (CHUNK, CONV_DIM), 0) % SHORT_SEQ
    conv = cb_ref[...]
    for back in range(CONV_W):
        tap = CONV_W - 1 - back
        rows_back = xpad[pl.ds(PAD_ROWS - back, CHUNK), :]
        if packed and back:
            rows_back = jnp.where(step < back, aux[pl.ds(PAD_ROWS - back, CHUNK), :], rows_back)
        conv = conv + rows_back * cw_ref[tap:tap + 1, :]
    xc = _silu(conv)
    xs = xc[:, :D_SSD]

    lane = lax.broadcasted_iota(jnp.int32, (CHUNK, DT_PAD), 1)
    dt_raw = dt_ref[...] + dtb_ref[...]
    dt = jnp.maximum(dt_raw, 0.0) + jnp.log1p(jnp.exp(-jnp.abs(dt_raw)))
    dt = jnp.where(lane < N_HEADS, dt, 0.0)
    adt = dt * (-jnp.exp(alog_ref[...]))

    ri = lax.broadcasted_iota(jnp.int32, (CHUNK, CHUNK), 0)
    ci = lax.broadcasted_iota(jnp.int32, (CHUNK, CHUNK), 1)
    causal = ri >= ci
    if packed:
        causal = causal & (ri // SHORT_SEQ == ci // SHORT_SEQ)
    exact_dot = functools.partial(jnp.dot, precision=lax.Precision.HIGHEST,
                                  preferred_element_type=F32)
    cs = exact_dot(jnp.where(causal, 1.0, 0.0).astype(F32), adt)
    cs_t = cs.T
    if packed:
        pick_last = ci == (ri // SHORT_SEQ) * SHORT_SEQ + (SHORT_SEQ - 1)
        cs_end = exact_dot(jnp.where(pick_last, 1.0, 0.0).astype(F32), cs)
    else:
        cs_end = cs[CHUNK - 1:CHUNK, :]

    head_of_lane = lax.broadcasted_iota(jnp.int32, (CHUNK, GROUP_W), 1) // HEAD_DIM

    def per_head(cols):
        out = cols[HEADS_PER_GROUP - 1]
        for hl in range(HEADS_PER_GROUP - 2, -1, -1):
            out = jnp.where(head_of_lane == hl, cols[hl], out)
        return out

    y_groups = []
    for g in range(SSD_GROUPS):
        heads = range(g * HEADS_PER_GROUP, (g + 1) * HEADS_PER_GROUP)
        grp = slice(g * GROUP_W, (g + 1) * GROUP_W)
        b_g = xc[:, D_SSD + g * D_STATE:D_SSD + (g + 1) * D_STATE].astype(BF16)
        c_g = xc[:, D_SSD + (SSD_GROUPS + g) * D_STATE:
                 D_SSD + (SSD_GROUPS + g + 1) * D_STATE].astype(BF16)
        cs_cols = [cs[:, h:h + 1] for h in heads]
        xdt_g = xs[:, grp] * per_head([dt[:, h:h + 1] for h in heads])
        xdt_b = xdt_g.astype(BF16)
        gram = _dot_nt(c_g, b_g)

        y_diag = jnp.zeros((CHUNK, GROUP_W), F32)
        for hl, h in enumerate(heads):
            seg = jnp.where(causal, cs_cols[hl] - cs_t[h:h + 1, :], NEG_INF)
            weights = (gram * jnp.exp(seg)).astype(BF16)
            y_diag = y_diag + jnp.where(head_of_lane == hl, _dot(weights, xdt_b), 0.0)

        to_end = per_head([jnp.exp(cs_end[:, h:h + 1] - cs_cols[hl]) for hl, h in enumerate(heads)])
        decayed_t = (xdt_g * to_end).T
        carried = per_head([jnp.exp(col) for col in cs_cols])
        if packed:
            h_prev = h0_ref[:, grp, :]
            wide = _dot_nt(c_g, h_prev.reshape(SEQS_PER_TILE * GROUP_W, D_STATE).astype(BF16))
            y_off = jnp.concatenate(
                [wide[s * SHORT_SEQ:(s + 1) * SHORT_SEQ, s * GROUP_W:(s + 1) * GROUP_W]
                 for s in range(SEQS_PER_TILE)], 0) * carried
            seq_of_step = lax.broadcasted_iota(jnp.int32, (GROUP_W, CHUNK), 1) // SHORT_SEQ
            per_seq = jnp.concatenate([jnp.where(seq_of_step == s, decayed_t, 0.0)
                                       for s in range(SEQS_PER_TILE)], 0).astype(BF16)
            new_states = _dot(per_seq, b_g).reshape(SEQS_PER_TILE, GROUP_W, D_STATE)
            for s in range(SEQS_PER_TILE):
                row = s * SHORT_SEQ
                keep = jnp.concatenate(
                    [jnp.broadcast_to(jnp.exp(cs_end[row:row + 1, h:h + 1]), (HEAD_DIM, D_STATE))
                     for h in heads], 0)
                hn_ref[s, grp, :] = h_prev[s] * keep + new_states[s]
        else:
            h_prev = aux[grp, :]
            y_off = _dot_nt(c_g, h_prev.astype(BF16)) * carried
            keep = jnp.concatenate(
                [jnp.broadcast_to(jnp.exp(cs_end[:, h:h + 1]), (HEAD_DIM, D_STATE)) for h in heads], 0)
            aux[grp, :] = h_prev * keep + _dot(decayed_t.astype(BF16), b_g)
        y_groups.append(y_diag + y_off)

    y = jnp.concatenate(y_groups, axis=-1) + dsk_ref[...] * xs
    y = y * _silu(z_ref[...])
    normed = []
    for g in range(SSD_GROUPS):
        yg = y[:, g * GROUP_W:(g + 1) * GROUP_W]
        normed.append(yg * lax.rsqrt(jnp.mean(yg * yg, -1, keepdims=True) + LN_EPS))
    y_ref[...] = jnp.concatenate(normed, axis=-1) * nw_ref[...]

    if not packed:
        hn_ref[...] = aux[...]
        xpad[0:PAD_ROWS, :] = xpad[CHUNK:, :]


def _ssd_body(*args):
    for _ in _ssd_stages(*args):
        pass


N_SSD_INPUTS = 11


def _ssd_call(z, xbc, dt, conv_prev, h0, cw, cb, dtb, alog, dsk, nw, where=None):
    n, seq, _ = z.shape
    packed = seq == SHORT_SEQ
    if packed:
        assert n % SEQS_PER_TILE == 0 and where is None
        grid = (n // SEQS_PER_TILE, 1)
        fold = lambda t: t.reshape(grid[0], CHUNK, t.shape[-1])
        z, xbc, dt, conv_prev = fold(z), fold(xbc), fold(dt), fold(conv_prev)
        history_shape, state_shape = (None, CHUNK, CONV_DIM), (SEQS_PER_TILE, D_SSD, D_STATE)
        aux = pltpu.VMEM((CHUNK + PAD_ROWS, CONV_DIM), F32)
    else:
        assert seq % CHUNK == 0
        grid = (n, seq // CHUNK)
        history_shape, state_shape = (None, PAD_ROWS, CONV_DIM), (None, D_SSD, D_STATE)
        aux = pltpu.VMEM((D_SSD, D_STATE), F32)
    where = where or (lambda s, c: (s, c))
    tile = lambda width: pl.BlockSpec((None, CHUNK, width), lambda *g: where(*g) + (0,))
    per_seq = lambda shape: pl.BlockSpec(shape, lambda *g: (where(*g)[0], 0, 0))
    history, state = per_seq(history_shape), per_seq(state_shape)
    operands = (z, xbc, dt, conv_prev, h0, cw, cb, dtb, alog, dsk, nw)
    in_specs = [tile(D_SSD), tile(CONV_DIM), tile(DT_PAD), history, state,
                _const_spec((CONV_W, CONV_DIM)), _const_spec((1, CONV_DIM)),
                _const_spec((1, DT_PAD)), _const_spec((1, DT_PAD)),
                _const_spec((1, D_SSD)), _const_spec((1, D_SSD))]
    out_specs = [tile(D_SSD), state]
    out_shape = [jax.ShapeDtypeStruct(z.shape, F32), jax.ShapeDtypeStruct((n, D_SSD, D_STATE), F32)]
    scratch = [pltpu.VMEM((PAD_ROWS + CHUNK, CONV_DIM), F32), aux]
    return packed, grid, operands, in_specs, out_specs, out_shape, scratch


def _ssd(z, *rest):
    packed, grid, operands, in_specs, out_specs, out_shape, scratch = _ssd_call(z, *rest)
    y, h_new = pl.pallas_call(
        functools.partial(_ssd_body, packed, lambda: pl.program_id(1) == 0),
        grid=grid,
        in_specs=in_specs,
        out_specs=out_specs,
        out_shape=out_shape,
        scratch_shapes=scratch,
        compiler_params=pltpu.CompilerParams(dimension_semantics=("parallel", "arbitrary"),
                                             vmem_limit_bytes=VMEM_LIMIT),
        name="ssd",
    )(*operands)
    return y.reshape(z.shape), h_new


def _ssd_with_sample_attn(ssd_args, attn_args):
    z = ssd_args[0]
    nchunks = z.shape[1] // CHUNK
    steps = z.shape[0] * nchunks
    assert steps == attn_args[0].shape[0]
    where = lambda i: (i // nchunks, i % nchunks)
    _, _, ssd_ops, ssd_in, ssd_out, ssd_shape, scratch = _ssd_call(*ssd_args, where=where)
    attn_ops, attn_in, attn_out, attn_shape = _attn_sample_call(*attn_args)
    caches = (CACHE_KT_ARG, CACHE_VT_ARG)
    for arg in caches:
        attn_in[arg] = pl.BlockSpec(memory_space=pl.ANY)
    ring = pltpu.VMEM((CACHE_SLOTS, D_ATTN, CACHE_LEN), F32)

    def body(*refs):
        ssd_in_refs, refs = refs[:N_SSD_INPUTS], refs[N_SSD_INPUTS:]
        attn_in_refs, refs = list(refs[:len(attn_ops)]), refs[len(attn_ops):]
        y_ref, hn_ref, o_ref, xpad, aux, kt_ring, vt_ring, sems = refs
        step = pl.program_id(0)

        def fetch(s):
            slot = s % CACHE_SLOTS
            return [pltpu.make_async_copy(attn_in_refs[arg].at[s], buf.at[slot], sems.at[j, slot])
                    for j, (arg, buf) in enumerate(zip(caches, (kt_ring, vt_ring)))]

        @pl.when(step == 0)
        def _prime():
            for s in range(CACHE_SLOTS - 1):
                for copy in fetch(s):
                    copy.start()

        @pl.when(step + (CACHE_SLOTS - 1) < steps)
        def _ahead():
            for copy in fetch(step + (CACHE_SLOTS - 1)):
                copy.start()

        for copy in fetch(step):
            copy.wait()
        for arg, buf in zip(caches, (kt_ring, vt_ring)):
            attn_in_refs[arg] = buf.at[step % CACHE_SLOTS]

        ssd = _ssd_stages(False, lambda: step % nchunks == 0, *ssd_in_refs, y_ref, hn_ref, xpad, aux)
        attn = _attn_sample_stages(*attn_in_refs, o_ref)
        for stage in (ssd, attn, ssd, attn):
            next(stage, None)

    y, h_new, attn = pl.pallas_call(
        body,
        grid=(steps,),
        in_specs=ssd_in + attn_in,
        out_specs=ssd_out + [attn_out],
        out_shape=ssd_shape + [attn_shape],
        scratch_shapes=scratch + [ring, ring, pltpu.SemaphoreType.DMA((len(caches), CACHE_SLOTS))],
        compiler_params=pltpu.CompilerParams(dimension_semantics=("arbitrary",),
                                             vmem_limit_bytes=VMEM_LIMIT),
        name="ssd_attn_sample",
    )(*ssd_ops, *attn_ops)
    return y, h_new, attn


def _row(v, width=None):
    v = v.reshape(1, -1).astype(F32)
    if width is not None and v.shape[1] < width:
        v = jnp.pad(v, ((0, 0), (0, width - v.shape[1])))
    return v


def kernel(x_prompt, x_sample, cache_k, cache_v, state_conv, state_ssm, p_prompt, p_sample,
           ln_in_g, ln_in_b, w_in, conv_w, conv_b, dt_bias, a_log, d_skip, ssd_norm_w, w_out,
           ln1_g, ln1_b, w_up, w_down, ln2_g, ln2_b, w_gate, w_ple, ln3_g, ln3_b):
    depth = w_in.shape[0]
    assert depth == 1, "single-layer step"
    alpha = (2 * depth) ** 0.25
    bsz, seq, _ = x_prompt.shape
    nd, dec_t, _ = x_sample.shape
    lyr = 0

    w_t = jnp.transpose(w_in[lyr])
    w_proj = w_t[:D_PROJ].astype(BF16)
    w_dt = jnp.pad(w_t[D_PROJ:], ((0, DT_PAD - N_HEADS), (0, 0))).astype(BF16)
    gin, bin_ = _row(ln_in_g), _row(ln_in_b)
    ssd_params = (conv_w[lyr].astype(F32), _row(conv_b[lyr]), _row(dt_bias[lyr], DT_PAD),
                  _row(a_log[lyr], DT_PAD), _row(jnp.repeat(d_skip[lyr], HEAD_DIM)),
                  _row(ssd_norm_w[lyr]))
    post_params = (gin, bin_, w_out[lyr].astype(BF16), _row(ln1_g[lyr]), _row(ln1_b[lyr]),
                   w_up[lyr].astype(BF16), w_down[lyr].astype(BF16), _row(ln2_g[lyr]),
                   _row(ln2_b[lyr]), w_gate[lyr].astype(BF16), w_ple[lyr].astype(BF16),
                   _row(ln3_g[lyr]), _row(ln3_b[lyr]))

    q, k, v, k_t, v_t, z, xbc, dt = _in_proj(x_prompt, gin, bin_, w_proj, w_dt, head_major=True, tm=1024)
    n_tok = nd * dec_t
    flat = lambda t: t.reshape(1, n_tok, t.shape[-1])
    toks = lambda t: t.reshape(nd, dec_t, t.shape[-1])
    qs, ks, vs, ks_t, vs_t, zs, xbcs, dts = _in_proj(flat(x_sample), gin, bin_, w_proj, w_dt,
                                                   head_major=False, tm=SHORT_SEQ * LANES)

    attn = _attn_prompt(q, k, v)
    transposed = lambda c: jnp.transpose(c, (0, 2, 3, 1)).reshape(nd, D_ATTN, CACHE_LEN)
    ssd_args = (z, xbc, dt, jnp.zeros((bsz, PAD_ROWS, CONV_DIM), F32),
                jnp.zeros((bsz, D_SSD, D_STATE), F32)) + ssd_params
    attn_args = (toks(qs), toks(ks), toks(vs), transposed(cache_k[lyr]), transposed(cache_v[lyr]))
    if bsz * (seq // CHUNK) == nd:
        ssd_y, ssm_p, attn_s = _ssd_with_sample_attn(ssd_args, attn_args)
    else:
        ssd_y, ssm_p = _ssd(*ssd_args)
        attn_s = _attn_sample(*attn_args)
    conv_prev = jnp.pad(state_conv[lyr].astype(F32), ((0, 0), (PAD_ROWS - (CONV_W - 1), 0), (0, 0)))
    ssd_s, ssm_s = _ssd(toks(zs), toks(xbcs), toks(dts), conv_prev,
                        state_ssm[lyr].reshape(nd, D_SSD, D_STATE).astype(F32), *ssd_params)

    y_prompt = _post(x_prompt, attn, ssd_y, p_prompt[lyr], *post_params, alpha=alpha, tm=512)
    y_sample = _post(flat(x_sample), attn_s, flat(ssd_s), flat(p_sample[lyr]), *post_params,
                     alpha=alpha, tm=256)

    from_t = lambda t: jnp.transpose(t.reshape(1, bsz, N_HEADS, HEAD_DIM, seq), (0, 1, 4, 2, 3))
    by_step = lambda t: jnp.transpose(t.reshape(1, dec_t, N_HEADS, HEAD_DIM, nd), (0, 4, 1, 2, 3))
    tail = lambda t: t[None, :, -(CONV_W - 1):, :]
    state = lambda t, n: t.reshape(1, n, N_HEADS, HEAD_DIM, D_STATE)
    return (y_prompt, y_sample.reshape(nd, dec_t, D_MODEL), from_t(k_t), from_t(v_t),
            by_step(ks_t), by_step(vs_t), tail(xbc), tail(toks(xbcs)), state(ssm_p, bsz), state(ssm_s, nd))
```

```python
import functools

import numpy as np
import jax
import jax.numpy as jnp
from jax import lax
from jax.experimental import pallas as pl
from jax.experimental.pallas import tpu as pltpu

F32 = jnp.float32
BF16 = jnp.bfloat16

D_MODEL = 1024
HEAD_DIM = 64
D_ATTN = 512
D_SSD = 512
N_HEADS = 8
SSD_GROUPS = 2
GROUP_W = D_SSD // SSD_GROUPS
D_STATE = 128
CONV_W = 4
CONV_DIM = D_SSD + 2 * SSD_GROUPS * D_STATE
CHUNK = 128
D_FF = 4096
D_PLE = 256
LN_EPS = 1e-5
WINDOW_STEPS = 128
DILATIONS = (1, 4, 16)
PHASES = DILATIONS[2]
MID_PHASES = DILATIONS[1]
CACHE_LEN = 2048
DT_PAD = 128
D_PROJ = 3 * D_ATTN + D_SSD + CONV_DIM
SUBLANES = 8
LANES = 128
VMEM_LIMIT = 56 * 1024 * 1024
NEG_INF = float("-inf")


def _slopes():
    return np.array([2.0 ** (-8.0 * (h + 1) / N_HEADS) for h in range(N_HEADS)], dtype=np.float64)


def _layer_norm(x, g, b):
    mu = jnp.mean(x, -1, keepdims=True)
    xc = x - mu
    var = jnp.mean(xc * xc, -1, keepdims=True)
    return xc * lax.rsqrt(var + LN_EPS) * g + b


def _silu(x):
    return x * (1.0 / (1.0 + jnp.exp(-x)))


def _dot(a, b):
    return jnp.dot(a, b, preferred_element_type=F32)


def _dot_nt(a, b):
    return lax.dot_general(a, b, (((1,), (1,)), ((), ())), preferred_element_type=F32)


def _const_spec(shape):
    nd = len(shape)
    return pl.BlockSpec(shape, lambda *_: (0,) * nd, pipeline_mode=pl.Buffered(1))


HEAD_PAIRS = N_HEADS // 2


def _inproj_body(head_major, x_ref, g_ref, b_ref, w_ref, wdt_ref, *out_refs):
    tm = x_ref.shape[0]
    parts = [slice(i * tm // 2, (i + 1) * tm // 2) for i in range(2)]
    h = [_layer_norm(x_ref[r, :], g_ref[...], b_ref[...]).astype(BF16) for r in parts]
    proj = lambda lo, width: [_dot_nt(v, w_ref[lo:lo + width, :]) for v in h]
    if head_major:
        q_ref, k_ref, v_ref, kt_ref, vt_ref, z_ref, xbc_ref, dt_ref = out_refs
        for idx, (ref, t_ref) in enumerate(((q_ref, None), (k_ref, kt_ref), (v_ref, vt_ref))):
            for r, res in zip(parts, proj(idx * D_ATTN, D_ATTN)):
                for c in range(HEAD_PAIRS):
                    ref[c, r, :] = res[:, c * LANES:(c + 1) * LANES]
                if t_ref is not None:
                    t_ref[:, r] = res.T
    else:
        q_ref, k_ref, v_ref, kts_ref, vts_ref, z_ref, xbc_ref, dt_ref, slabs = out_refs
        for idx, (ref, ts_ref) in enumerate(((q_ref, None), (k_ref, kts_ref), (v_ref, vts_ref))):
            for r, res in zip(parts, proj(idx * D_ATTN, D_ATTN)):
                ref[r, :] = res
                if ts_ref is not None:
                    for c in range(HEAD_PAIRS):
                        slabs[c, r, :] = res[:, c * LANES:(c + 1) * LANES]
            if ts_ref is not None:
                for t in range(SHORT_SEQ):
                    for c in range(HEAD_PAIRS):
                        by_seq = slabs[c, pl.ds(t, tm // SHORT_SEQ, stride=SHORT_SEQ), :]
                        ts_ref[t, c * LANES:(c + 1) * LANES, :] = by_seq.T
    lo = 3 * D_ATTN
    for ref, width in ((z_ref, D_SSD), (xbc_ref, CONV_DIM)):
        for r, res in zip(parts, proj(lo, width)):
            ref[r, :] = res
        lo += width
    for r, v in zip(parts, h):
        dt_ref[r, :] = _dot_nt(v, wdt_ref[...])


def _in_proj(x, g, b, w, w_dt, *, head_major, tm):
    bsz, seq, _ = x.shape
    row = lambda width: pl.BlockSpec((None, tm, width), lambda i, j: (i, j, 0))
    row_shape = lambda width: jax.ShapeDtypeStruct((bsz, seq, width), F32)
    if head_major:
        slab = pl.BlockSpec((None, HEAD_PAIRS, tm, LANES), lambda i, j: (i, 0, j, 0))
        slab_shape = jax.ShapeDtypeStruct((bsz, HEAD_PAIRS, seq, LANES), F32)
        tr = pl.BlockSpec((None, D_ATTN, tm), lambda i, j: (i, 0, j))
        tr_shape = jax.ShapeDtypeStruct((bsz, D_ATTN, seq), F32)
        qkv_specs, qkv_shapes = [slab, slab, slab, tr, tr], [slab_shape] * 3 + [tr_shape] * 2
        scratch = []
    else:
        assert tm == SHORT_SEQ * LANES, "one tile = 128 short sequences"
        by_step = pl.BlockSpec((SHORT_SEQ, D_ATTN, LANES), lambda i, j: (0, 0, i * (seq // tm) + j))
        by_step_shape = jax.ShapeDtypeStruct((SHORT_SEQ, D_ATTN, bsz * seq // SHORT_SEQ), F32)
        qkv_specs = [row(D_ATTN)] * 3 + [by_step] * 2
        qkv_shapes = [row_shape(D_ATTN)] * 3 + [by_step_shape] * 2
        scratch = [pltpu.VMEM((HEAD_PAIRS, tm, LANES), F32)]
    rest = (D_SSD, CONV_DIM, DT_PAD)
    return pl.pallas_call(
        functools.partial(_inproj_body, head_major),
        grid=(bsz, seq // tm),
        in_specs=[row(D_MODEL), _const_spec((1, D_MODEL)), _const_spec((1, D_MODEL)),
                  _const_spec((D_PROJ, D_MODEL)), _const_spec((DT_PAD, D_MODEL))],
        out_specs=qkv_specs + [row(wd) for wd in rest],
        out_shape=qkv_shapes + [row_shape(wd) for wd in rest],
        scratch_shapes=scratch,
        compiler_params=pltpu.CompilerParams(dimension_semantics=("parallel", "parallel"),
                                             vmem_limit_bytes=VMEM_LIMIT),
        name="in_proj",
    )(x, g, b, w, w_dt)


FF_CHUNK = 1024
POST_STREAMS = 2


def _post_body(alpha, x_ref, attn_ref, ssd_ref, pe_ref, gin_ref, bin_ref, wout_ref, g1_ref, b1_ref,
               wup_ref, wdown_ref, g2_ref, b2_ref, wgate_ref, wple_ref, g3_ref, b3_ref, y_ref):
    tm = x_ref.shape[0]
    parts = [slice(i * tm // POST_STREAMS, (i + 1) * tm // POST_STREAMS) for i in range(POST_STREAMS)]
    each = lambda fn, *lists: [fn(*args) for args in zip(*lists)]
    xn = [_layer_norm(x_ref[r, :], gin_ref[...], bin_ref[...]) for r in parts]
    mixed = [jnp.concatenate([attn_ref[c, r, :] for c in range(HEAD_PAIRS)] + [ssd_ref[r, :]],
                             axis=-1).astype(BF16) for r in parts]
    proj = [_dot(m, wout_ref[...]) for m in mixed]
    h = each(lambda x, p: _layer_norm(alpha * x + p, g1_ref[...], b1_ref[...]), xn, proj)
    hb = [v.astype(BF16) for v in h]
    u = [None] * POST_STREAMS
    for c in range(D_FF // FF_CHUNK):
        cols = slice(c * FF_CHUNK, (c + 1) * FF_CHUNK)
        a = [jnp.maximum(_dot(v, wup_ref[:, cols]), 0.0) for v in hb]
        part = [_dot((v * v).astype(BF16), wdown_ref[cols, :]) for v in a]
        u = part if c == 0 else each(lambda s, p: s + p, u, part)
    h = each(lambda v, w: _layer_norm(alpha * v + w, g2_ref[...], b2_ref[...]), h, u)
    gate = [1.0 / (1.0 + jnp.exp(-_dot(v.astype(BF16), wgate_ref[...]))) for v in h]
    emb = [_dot(pe_ref[r, :].astype(BF16), wple_ref[...]) for r in parts]
    for r, v, g, e in zip(parts, h, gate, emb):
        y_ref[r, :] = _layer_norm(alpha * v + g * e, g3_ref[...], b3_ref[...])


def _post(x, attn, ssd, pe, gin, bin_, wout, g1, b1, wup, wdown, g2, b2, wgate, wple, g3, b3, *,
          alpha, tm):
    bsz, seq, _ = x.shape
    row = lambda width: pl.BlockSpec((None, tm, width), lambda i, j: (i, j, 0))
    slab = pl.BlockSpec((None, HEAD_PAIRS, tm, LANES), lambda i, j: (i, 0, j, 0))
    vec = _const_spec((1, D_MODEL))
    return pl.pallas_call(
        functools.partial(_post_body, alpha),
        grid=(bsz, seq // tm),
        in_specs=[row(D_MODEL), slab, row(D_SSD), row(D_PLE), vec, vec,
                  _const_spec((D_MODEL, D_MODEL)), vec, vec,
                  _const_spec((D_MODEL, D_FF)), _const_spec((D_FF, D_MODEL)), vec, vec,
                  _const_spec((D_MODEL, D_MODEL)), _const_spec((D_PLE, D_MODEL)), vec, vec],
        out_specs=row(D_MODEL),
        out_shape=jax.ShapeDtypeStruct((bsz, seq, D_MODEL), F32),
        compiler_params=pltpu.CompilerParams(dimension_semantics=("parallel", "parallel"),
                                             vmem_limit_bytes=VMEM_LIMIT),
        name="post",
    )(x, attn, ssd, pe, gin, bin_, wout, g1, b1, wup, wdown, g2, b2, wgate, wple, g3, b3)


U_ROWS = CACHE_LEN // PHASES
B2_ROWS = U_ROWS // (PHASES // MID_PHASES)
TAB3_LO, TAB2_LO, TAB1_LO, TAB_W = 0, WINDOW_STEPS, 3 * WINDOW_STEPS, 5 * WINDOW_STEPS
WAVE = 4
B1_IN_FLIGHT = 8


def _prompt_bias_tables():
    slopes = _slopes()
    u = np.arange(U_ROWS)
    d3 = (u[:, None] - u[None, :]).astype(np.float64)
    k4, ul4 = np.meshgrid(np.arange(PHASES // MID_PHASES), np.arange(B2_ROWS), indexing="ij")
    j2 = (PHASES // MID_PHASES * ul4 + k4).reshape(-1)
    d2 = j2[:, None] - np.concatenate([j2 - WINDOW_STEPS, j2])[None, :]
    p16, ul16 = np.meshgrid(np.arange(PHASES), np.arange(SUBLANES), indexing="ij")
    t1 = (PHASES * ul16 + p16).reshape(-1)
    d1 = t1[:, None] - (np.arange(2 * WINDOW_STEPS) - WINDOW_STEPS)[None, :]
    tabs = []
    for dist, dil in zip((d3, d2, d1), reversed(DILATIONS)):
        valid = (dist >= 0) & (dist <= WINDOW_STEPS)
        per_head = [np.where(valid, -slopes[h] * dist * dil, NEG_INF) for h in range(N_HEADS)]
        tabs.append(np.stack(per_head))
    tab = np.concatenate(tabs, axis=-1)
    return jnp.asarray(tab.reshape(HEAD_PAIRS, 2 * U_ROWS, TAB_W), dtype=F32)


def _attn_blocks(blocks, first_head):
    half = U_ROWS
    state = [dict() for _ in blocks]

    def scores(i):
        q2, k, _, bias = blocks[i]
        state[i]["s"] = _dot_nt(q2, k) + bias

    def top(i):
        state[i]["m"] = jnp.max(state[i]["s"], -1, keepdims=True)

    def weights(i):
        e = jnp.exp(state[i].pop("s") - state[i]["m"])
        state[i]["den"] = jnp.sum(e, -1, keepdims=True)
        state[i]["e"] = e.astype(BF16)

    def values(i):
        state[i]["pv"] = _dot(state[i].pop("e"), blocks[i][2])

    def finish(i):
        pv, m, den = state[i]["pv"], state[i]["m"], state[i]["den"]
        o = jnp.where(first_head, pv[:half], pv[half:])
        m2 = jnp.where(first_head, m[:half], m[half:])
        den2 = jnp.where(first_head, den[:half], den[half:])
        state[i] = (o / den2, m2 + jnp.log(den2))

    stages = (scores, top, weights, values, finish)
    groups = [range(j, min(j + WAVE, len(blocks))) for j in range(0, len(blocks), WAVE)]
    for t in range(len(groups) + len(stages) - 1):
        for g, members in enumerate(groups):
            if 0 <= t - g < len(stages):
                for i in members:
                    stages[t - g](i)
    return state


def _attn_prompt_body(q_ref, k_ref, v_ref, tab_ref, o_ref, qs, q0p, q1p, kp, vp, kn, vn, quarter,
                      o_scr, l_scr):
    lane = lax.broadcasted_iota(jnp.int32, (U_ROWS, LANES), 1)
    first_head = lane < HEAD_DIM
    scale = HEAD_DIM ** -0.5

    kn[...] = k_ref[...].astype(BF16)
    vn[...] = v_ref[...].astype(BF16)

    def split(src_ref, emit):
        for r in range(MID_PHASES):
            quarter[r] = src_ref[pl.ds(r, CACHE_LEN // MID_PHASES, stride=MID_PHASES), :]
        for r in range(MID_PHASES):
            for kk in range(PHASES // MID_PHASES):
                emit(MID_PHASES * kk + r,
                     quarter[r, pl.ds(kk, U_ROWS, stride=PHASES // MID_PHASES), :])

    def emit_q(p, tile):
        tile = tile * scale
        qs[p] = tile
        q0p[p] = jnp.where(first_head, tile, 0.0).astype(BF16)
        q1p[p] = jnp.where(first_head, 0.0, tile).astype(BF16)

    def emit_to(dst):
        def emit(p, tile):
            dst[p] = tile.astype(BF16)
        return emit

    split(q_ref, emit_q)
    split(k_ref, emit_to(kp))
    split(v_ref, emit_to(vp))

    bias = tab_ref[:, TAB3_LO:TAB3_LO + U_ROWS]
    outs = _attn_blocks([(jnp.concatenate([q0p[p], q1p[p]], 0), kp[p], vp[p], bias)
                         for p in range(PHASES)], first_head)
    for p, (o, l) in enumerate(outs):
        o_scr[0, p] = o
        l_scr[0, p] = l

    blocks, where = [], []
    for r in range(MID_PHASES):
        tiles = [MID_PHASES * kk + r for kk in range(PHASES // MID_PHASES)]
        gather = lambda src, rr, tiles=tiles: [src[p, rr, :] for p in tiles]
        for n in range(U_ROWS // B2_ROWS):
            rows = pl.ds(B2_ROWS * n, B2_ROWS)
            q2 = jnp.concatenate(gather(q0p, rows) + gather(q1p, rows), 0)
            if n == 0:
                k = jnp.concatenate(gather(kp, rows), 0)
                v = jnp.concatenate(gather(vp, rows), 0)
                bias = tab_ref[:, TAB2_LO + WINDOW_STEPS:TAB2_LO + 2 * WINDOW_STEPS]
            else:
                prev = pl.ds(B2_ROWS * (n - 1), B2_ROWS)
                k = jnp.concatenate(gather(kp, prev) + gather(kp, rows), 0)
                v = jnp.concatenate(gather(vp, prev) + gather(vp, rows), 0)
                bias = tab_ref[:, TAB2_LO:TAB2_LO + 2 * WINDOW_STEPS]
            blocks.append((q2, k, v, bias))
            where.append((tiles, rows))
    for (tiles, rows), (o, l) in zip(where, _attn_blocks(blocks, first_head)):
        for kk, p in enumerate(tiles):
            o_scr[1, p, rows, :] = o[B2_ROWS * kk:B2_ROWS * (kk + 1)]
            l_scr[1, p, rows, :] = l[B2_ROWS * kk:B2_ROWS * (kk + 1)]

    def branch1_blocks(ns):
        blocks = []
        for n in ns:
            rows = pl.ds(n * SUBLANES, SUBLANES)
            q = jnp.concatenate([qs[p, rows, :] for p in range(PHASES)], 0)
            q2 = jnp.concatenate([jnp.where(first_head, q, 0.0), jnp.where(first_head, 0.0, q)],
                                 0).astype(BF16)
            if n > 0:
                keys = pl.ds((n - 1) * U_ROWS, 2 * U_ROWS)
                bias = tab_ref[:, TAB1_LO:TAB1_LO + 2 * WINDOW_STEPS]
            else:
                keys = pl.ds(0, U_ROWS)
                bias = tab_ref[:, TAB1_LO + WINDOW_STEPS:TAB1_LO + 2 * WINDOW_STEPS]
            blocks.append((q2, kn[keys, :], vn[keys, :], bias))
        for n, (o, l) in zip(ns, _attn_blocks(blocks, first_head)):
            rows = pl.ds(n * SUBLANES, SUBLANES)
            for p in range(PHASES):
                o_scr[2, p, rows, :] = o[SUBLANES * p:SUBLANES * (p + 1)]
                l_scr[2, p, rows, :] = l[SUBLANES * p:SUBLANES * (p + 1)]

    n_blocks = U_ROWS // SUBLANES
    for first in range(0, n_blocks, B1_IN_FLIGHT):
        branch1_blocks(range(first, min(first + B1_IN_FLIGHT, n_blocks)))

    def phase_rows(p):
        return pl.ds(p, U_ROWS, stride=PHASES)

    for p in range(PHASES):
        l3, l2, l1 = l_scr[0, p], l_scr[1, p], l_scr[2, p]
        top = jnp.maximum(jnp.maximum(l3, l2), l1)
        e3, e2, e1 = jnp.exp(l3 - top), jnp.exp(l2 - top), jnp.exp(l1 - top)
        total = e3 + e2 + e1
        o_ref[phase_rows(p), :] = (e3 * o_scr[0, p] + e2 * o_scr[1, p] + e1 * o_scr[2, p]) / total


def _attn_prompt(q, k, v):
    bsz, _, seq, _ = k.shape
    assert seq == CACHE_LEN, "prompt attention is laid out for a 2048-token prompt"
    slab = pl.BlockSpec((None, None, seq, LANES), lambda b, c: (b, c, 0, 0))
    tile = (PHASES, U_ROWS, LANES)
    return pl.pallas_call(
        _attn_prompt_body,
        grid=(bsz, HEAD_PAIRS),
        in_specs=[slab, slab, slab,
                  pl.BlockSpec((None, 2 * U_ROWS, TAB_W), lambda b, c: (c, 0, 0))],
        out_specs=slab,
        out_shape=jax.ShapeDtypeStruct((bsz, HEAD_PAIRS, seq, LANES), F32),
        scratch_shapes=[pltpu.VMEM(tile, F32)] + [pltpu.VMEM(tile, BF16)] * 4
                       + [pltpu.VMEM((seq, LANES), BF16)] * 2
                       + [pltpu.VMEM((4, seq // 4, LANES), F32),
                          pltpu.VMEM((3,) + tile, F32), pltpu.VMEM((3,) + tile, F32)],
        compiler_params=pltpu.CompilerParams(dimension_semantics=("parallel", "parallel"),
                                             vmem_limit_bytes=VMEM_LIMIT),
        name="attn_prompt",
    )(q, k, v, _prompt_bias_tables())


DEC_T = 8
QH = DEC_T * N_HEADS
B2_SPAN, B1_SPAN = 512, 128
SAMPLE_TAB_W = CACHE_LEN + B2_SPAN + B1_SPAN


def _sample_bias_tables():
    slopes = _slopes()
    i = np.repeat(np.arange(DEC_T), N_HEADS)[:, None]
    sl = np.tile(slopes, DEC_T)[:, None]

    def cache_bias(span, dil):
        t = CACHE_LEN - span + np.arange(span)[None, :]
        dist = CACHE_LEN + i - t
        valid = (dist % dil == 0) & (dist <= WINDOW_STEPS * dil)
        return np.where(valid, -sl * dist, NEG_INF)

    cache_tab = np.concatenate([cache_bias(CACHE_LEN, 16), cache_bias(B2_SPAN, 4),
                                cache_bias(B1_SPAN, 1)], axis=1)
    c = np.arange(CHUNK)[None, :]
    dn = i - c
    is_new = c < DEC_T
    n1 = np.where(is_new & (dn >= 0), -sl * dn, NEG_INF)
    n2 = np.where(is_new & ((dn == 0) | (dn == 4)), -sl * dn, NEG_INF)
    n3 = np.where(is_new & (dn == 0), 0.0, NEG_INF)
    return jnp.asarray(cache_tab, dtype=F32), jnp.asarray(np.stack([n1, n2, n3]), dtype=F32)


def _head_selector():
    h_row = np.tile(np.arange(N_HEADS), DEC_T)[:, None]
    h_col = (np.arange(D_ATTN) // HEAD_DIM)[None, :]
    return jnp.asarray((h_row == h_col).astype(np.float32))


def _attn_sample_stages(q_ref, kn_ref, vn_ref, kt_ref, vt_ref, tab_ref, tabn_ref, sel_ref, o_ref):
    sel = sel_ref[...]
    q = q_ref[...] * (HEAD_DIM ** -0.5)
    q_rows = jnp.concatenate([jnp.broadcast_to(q[i:i + 1, :], (N_HEADS, D_ATTN))
                              for i in range(DEC_T)], 0)
    qall = (q_rows * sel).astype(BF16)
    pad = jnp.zeros((CHUNK - DEC_T, D_ATTN), F32)
    kn = jnp.concatenate([kn_ref[...], pad], 0).astype(BF16)
    vn = jnp.concatenate([vn_ref[...], pad], 0).astype(BF16)

    s_all = _dot(qall, kt_ref[...].astype(BF16))
    s_new = _dot_nt(qall, kn)
    yield
    lo2, lo1 = CACHE_LEN - B2_SPAN, CACHE_LEN - B1_SPAN

    def branch(s_cache, s_fresh, v_t):
        m = jnp.maximum(jnp.max(s_cache, -1, keepdims=True), jnp.max(s_fresh, -1, keepdims=True))
        e_c, e_f = jnp.exp(s_cache - m), jnp.exp(s_fresh - m)
        den = jnp.sum(e_c, -1, keepdims=True) + jnp.sum(e_f, -1, keepdims=True)
        acc = _dot_nt(e_c.astype(BF16), v_t.astype(BF16)) + _dot(e_f.astype(BF16), vn)
        return m, den, acc

    m3, den3, acc3 = branch(s_all + tab_ref[:, :CACHE_LEN], s_new + tabn_ref[2], vt_ref[...])
    m2, den2, acc2 = branch(s_all[:, lo2:] + tab_ref[:, CACHE_LEN:CACHE_LEN + B2_SPAN],
                            s_new + tabn_ref[1], vt_ref[:, lo2:])
    m1, den1, acc1 = branch(s_all[:, lo1:] + tab_ref[:, CACHE_LEN + B2_SPAN:],
                            s_new + tabn_ref[0], vt_ref[:, lo1:])

    top = jnp.maximum(jnp.maximum(m1, m2), m3)
    sc1, sc2, sc3 = den1 * jnp.exp(m1 - top), den2 * jnp.exp(m2 - top), den3 * jnp.exp(m3 - top)
    total = sc1 + sc2 + sc3
    mixed = ((sc1 / total) * (acc1 / den1) + (sc2 / total) * (acc2 / den2)
             + (sc3 / total) * (acc3 / den3)) * sel
    for i in range(DEC_T):
        row = jnp.sum(mixed[N_HEADS * i:N_HEADS * (i + 1)], axis=0, keepdims=True)
        for c in range(HEAD_PAIRS):
            o_ref[c, i:i + 1, :] = row[:, c * LANES:(c + 1) * LANES]


def _attn_sample_body(*refs):
    for _ in _attn_sample_stages(*refs):
        pass


CACHE_KT_ARG, CACHE_VT_ARG = 3, 4
CACHE_SLOTS = 3


def _attn_sample_call(q, k_new, v_new, cache_kt, cache_vt):
    n = q.shape[0]
    assert cache_kt.shape[1:] == (D_ATTN, CACHE_LEN) and q.shape[1] == DEC_T
    tok = pl.BlockSpec((None, DEC_T, D_ATTN), lambda i: (i, 0, 0))
    cache = pl.BlockSpec((None, D_ATTN, CACHE_LEN), lambda i: (i, 0, 0))
    cache_tab, new_tab = _sample_bias_tables()
    operands = (q, k_new, v_new, cache_kt, cache_vt, cache_tab, new_tab, _head_selector())
    in_specs = [tok, tok, tok, cache, cache, _const_spec((QH, SAMPLE_TAB_W)),
                _const_spec((3, QH, CHUNK)), _const_spec((QH, D_ATTN))]
    out_spec = pl.BlockSpec((None, HEAD_PAIRS, DEC_T, LANES), lambda i: (0, 0, i, 0))
    out_shape = jax.ShapeDtypeStruct((1, HEAD_PAIRS, n * DEC_T, LANES), F32)
    return operands, in_specs, out_spec, out_shape


def _attn_sample(q, k_new, v_new, cache_kt, cache_vt):
    operands, in_specs, out_spec, out_shape = _attn_sample_call(q, k_new, v_new, cache_kt, cache_vt)
    return pl.pallas_call(
        _attn_sample_body,
        grid=(q.shape[0],),
        in_specs=in_specs,
        out_specs=out_spec,
        out_shape=out_shape,
        compiler_params=pltpu.CompilerParams(dimension_semantics=("parallel",),
                                             vmem_limit_bytes=VMEM_LIMIT),
        name="attn_sample",
    )(*operands)


HEADS_PER_GROUP = N_HEADS // SSD_GROUPS
PAD_ROWS = SUBLANES
SHORT_SEQ = SUBLANES
SEQS_PER_TILE = CHUNK // SHORT_SEQ


def _ssd_stages(packed, first_chunk, z_ref, xbc_ref, dt_ref, cp_ref, h0_ref, cw_ref, cb_ref, dtb_ref,
                alog_ref, dsk_ref, nw_ref, y_ref, hn_ref, xpad, aux):
    if packed:
        aux[0:CHUNK, :] = cp_ref[...]
        aux[CHUNK:, :] = jnp.zeros((PAD_ROWS, CONV_DIM), F32)
        xpad[0:PAD_ROWS, :] = jnp.zeros((PAD_ROWS, CONV_DIM), F32)
    else:
        @pl.when(first_chunk())
        def _init():
            xpad[0:PAD_ROWS, :] = cp_ref[...]
            aux[...] = h0_ref[...]
    yield
    xpad[PAD_ROWS:, :] = xbc_ref[...]

    step = lax.broadcasted_iota(jnp.int32, (CHUNK, CONV_DIM), 0) % SHORT_SEQ
    conv = cb_ref[...]
    for back in range(CONV_W):
        tap = CONV_W - 1 - back
        rows_back = xpad[pl.ds(PAD_ROWS - back, CHUNK), :]
        if packed and back:
            rows_back = jnp.where(step < back, aux[pl.ds(PAD_ROWS - back, CHUNK), :], rows_back)
        conv = conv + rows_back * cw_ref[tap:tap + 1, :]
    xc = _silu(conv)
    xs = xc[:, :D_SSD]

    lane = lax.broadcasted_iota(jnp.int32, (CHUNK, DT_PAD), 1)
    dt_raw = dt_ref[...] + dtb_ref[...]
    dt = jnp.maximum(dt_raw, 0.0) + jnp.log1p(jnp.exp(-jnp.abs(dt_raw)))
    dt = jnp.where(lane < N_HEADS, dt, 0.0)
    adt = dt * (-jnp.exp(alog_ref[...]))

    ri = lax.broadcasted_iota(jnp.int32, (CHUNK, CHUNK), 0)
    ci = lax.broadcasted_iota(jnp.int32, (CHUNK, CHUNK), 1)
    causal = ri >= ci
    if packed:
        causal = causal & (ri // SHORT_SEQ == ci // SHORT_SEQ)
    exact_dot = functools.partial(jnp.dot, precision=lax.Precision.HIGHEST,
                                  preferred_element_type=F32)
    cs = exact_dot(jnp.where(causal, 1.0, 0.0).astype(F32), adt)
    cs_t = cs.T
    if packed:
        pick_last = ci == (ri // SHORT_SEQ) * SHORT_SEQ + (SHORT_SEQ - 1)
        cs_end = exact_dot(jnp.where(pick_last, 1.0, 0.0).astype(F32), cs)
    else:
        cs_end = cs[CHUNK - 1:CHUNK, :]

    head_of_lane = lax.broadcasted_iota(jnp.int32, (CHUNK, GROUP_W), 1) // HEAD_DIM

    def per_head(cols):
        out = cols[HEADS_PER_GROUP - 1]
        for hl in range(HEADS_PER_GROUP - 2, -1, -1):
            out = jnp.where(head_of_lane == hl, cols[hl], out)
        return out

    y_groups = []
    for g in range(SSD_GROUPS):
        heads = range(g * HEADS_PER_GROUP, (g + 1) * HEADS_PER_GROUP)
        grp = slice(g * GROUP_W, (g + 1) * GROUP_W)
        b_g = xc[:, D_SSD + g * D_STATE:D_SSD + (g + 1) * D_STATE].astype(BF16)
        c_g = xc[:, D_SSD + (SSD_GROUPS + g) * D_STATE:
                 D_SSD + (SSD_GROUPS + g + 1) * D_STATE].astype(BF16)
        cs_cols = [cs[:, h:h + 1] for h in heads]
        xdt_g = xs[:, grp] * per_head([dt[:, h:h + 1] for h in heads])
        xdt_b = xdt_g.astype(BF16)
        gram = _dot_nt(c_g, b_g)

        y_diag = jnp.zeros((CHUNK, GROUP_W), F32)
        for hl, h in enumerate(heads):
            seg = jnp.where(causal, cs_cols[hl] - cs_t[h:h + 1, :], NEG_INF)
            weights = (gram * jnp.exp(seg)).astype(BF16)
            y_diag = y_diag + jnp.where(head_of_lane == hl, _dot(weights, xdt_b), 0.0)

        to_end = per_head([jnp.exp(cs_end[:, h:h + 1] - cs_cols[hl]) for hl, h in enumerate(heads)])
        decayed_t = (xdt_g * to_end).T
        carried = per_head([jnp.exp(col) for col in cs_cols])
        if packed:
            h_prev = h0_ref[:, grp, :]
            wide = _dot_nt(c_g, h_prev.reshape(SEQS_PER_TILE * GROUP_W, D_STATE).astype(BF16))
            y_off = jnp.concatenate(
                [wide[s * SHORT_SEQ:(s + 1) * SHORT_SEQ, s * GROUP_W:(s + 1) * GROUP_W]
                 for s in range(SEQS_PER_TILE)], 0) * carried
            seq_of_step = lax.broadcasted_iota(jnp.int32, (GROUP_W, CHUNK), 1) // SHORT_SEQ
            per_seq = jnp.concatenate([jnp.where(seq_of_step == s, decayed_t, 0.0)
                                       for s in range(SEQS_PER_TILE)], 0).astype(BF16)
            new_states = _dot(per_seq, b_g).reshape(SEQS_PER_TILE, GROUP_W, D_STATE)
            for s in range(SEQS_PER_TILE):
                row = s * SHORT_SEQ
                keep = jnp.concatenate(
                    [jnp.broadcast_to(jnp.exp(cs_end[row:row + 1, h:h + 1]), (HEAD_DIM, D_STATE))
                     for h in heads], 0)
                hn_ref[s, grp, :] = h_prev[s] * keep + new_states[s]
        else:
            h_prev = aux[grp, :]
            y_off = _dot_nt(c_g, h_prev.astype(BF16)) * carried
            keep = jnp.concatenate(
                [jnp.broadcast_to(jnp.exp(cs_end[:, h:h + 1]), (HEAD_DIM, D_STATE)) for h in heads], 0)
            aux[grp, :] = h_prev * keep + _dot(decayed_t.astype(BF16), b_g)
        y_groups.append(y_diag + y_off)

    y = jnp.concatenate(y_groups, axis=-1) + dsk_ref[...] * xs
    y = y * _silu(z_ref[...])
    normed = []
    for g in range(SSD_GROUPS):
        yg = y[:, g * GROUP_W:(g + 1) * GROUP_W]
        normed.append(yg * lax.rsqrt(jnp.mean(yg * yg, -1, keepdims=True) + LN_EPS))
    y_ref[...] = jnp.concatenate(normed, axis=-1) * nw_ref[...]

    if not packed:
        hn_ref[...] = aux[...]
        xpad[0:PAD_ROWS, :] = xpad[CHUNK:, :]


def _ssd_body(*args):
    for _ in _ssd_stages(*args):
        pass


N_SSD_INPUTS = 11
SSD_STATE_ARG = 4


def _ssd_call(z, xbc, dt, conv_prev, h0, cw, cb, dtb, alog, dsk, nw, where=None):
    n, seq, _ = z.shape
    packed = seq == SHORT_SEQ
    if packed:
        assert n % SEQS_PER_TILE == 0 and where is None
        grid = (n // SEQS_PER_TILE, 1)
        fold = lambda t: t.reshape(grid[0], CHUNK, t.shape[-1])
        z, xbc, dt, conv_prev = fold(z), fold(xbc), fold(dt), fold(conv_prev)
        history_shape, state_shape = (None, CHUNK, CONV_DIM), (SEQS_PER_TILE, D_SSD, D_STATE)
        aux = pltpu.VMEM((CHUNK + PAD_ROWS, CONV_DIM), F32)
    else:
        assert seq % CHUNK == 0
        grid = (n, seq // CHUNK)
        history_shape, state_shape = (None, PAD_ROWS, CONV_DIM), (None, D_SSD, D_STATE)
        aux = pltpu.VMEM((D_SSD, D_STATE), F32)
    where = where or (lambda s, c: (s, c))
    tile = lambda width: pl.BlockSpec((None, CHUNK, width), lambda *g: where(*g) + (0,))
    per_seq = lambda shape: pl.BlockSpec(shape, lambda *g: (where(*g)[0], 0, 0))
    history, state = per_seq(history_shape), per_seq(state_shape)
    operands = (z, xbc, dt, conv_prev, h0, cw, cb, dtb, alog, dsk, nw)
    in_specs = [tile(D_SSD), tile(CONV_DIM), tile(DT_PAD), history, state,
                _const_spec((CONV_W, CONV_DIM)), _const_spec((1, CONV_DIM)),
                _const_spec((1, DT_PAD)), _const_spec((1, DT_PAD)),
                _const_spec((1, D_SSD)), _const_spec((1, D_SSD))]
    out_specs = [tile(D_SSD), state]
    out_shape = [jax.ShapeDtypeStruct(z.shape, F32), jax.ShapeDtypeStruct((n, D_SSD, D_STATE), F32)]
    scratch = [pltpu.VMEM((PAD_ROWS + CHUNK, CONV_DIM), F32), aux]
    return packed, grid, operands, in_specs, out_specs, out_shape, scratch


def _ssd(z, *rest):
    packed, grid, operands, in_specs, out_specs, out_shape, scratch = _ssd_call(z, *rest)
    if packed:
        steps = grid[0]
        in_specs[SSD_STATE_ARG] = pl.BlockSpec(memory_space=pl.ANY)
        scratch = scratch + [pltpu.VMEM((CACHE_SLOTS, SEQS_PER_TILE, D_SSD, D_STATE), F32),
                             pltpu.SemaphoreType.DMA((CACHE_SLOTS,))]

        def body(*refs):
            *io, xpad, aux, ring, sems = refs
            step = pl.program_id(0)

            def fetch(s):
                rows = pl.ds(s * SEQS_PER_TILE, SEQS_PER_TILE)
                return pltpu.make_async_copy(io[SSD_STATE_ARG].at[rows], ring.at[s % CACHE_SLOTS],
                                             sems.at[s % CACHE_SLOTS])

            @pl.when(step == 0)
            def _prime():
                for s in range(min(CACHE_SLOTS - 1, steps)):
                    fetch(s).start()

            @pl.when(step + (CACHE_SLOTS - 1) < steps)
            def _ahead():
                fetch(step + (CACHE_SLOTS - 1)).start()

            fetch(step).wait()
            io[SSD_STATE_ARG] = ring.at[step % CACHE_SLOTS]
            _ssd_body(True, None, *io, xpad, aux)
        semantics = ("arbitrary", "arbitrary")
    else:
        body = functools.partial(_ssd_body, False, lambda: pl.program_id(1) == 0)
        semantics = ("parallel", "arbitrary")
    y, h_new = pl.pallas_call(
        body,
        grid=grid,
        in_specs=in_specs,
        out_specs=out_specs,
        out_shape=out_shape,
        scratch_shapes=scratch,
        compiler_params=pltpu.CompilerParams(dimension_semantics=semantics,
                                             vmem_limit_bytes=VMEM_LIMIT),
        name="ssd",
    )(*operands)
    return y.reshape(z.shape), h_new


def _ssd_with_sample_attn(ssd_args, attn_args):
    z = ssd_args[0]
    nchunks = z.shape[1] // CHUNK
    steps = z.shape[0] * nchunks
    assert steps == attn_args[0].shape[0]
    where = lambda i: (i // nchunks, i % nchunks)
    _, _, ssd_ops, ssd_in, ssd_out, ssd_shape, scratch = _ssd_call(*ssd_args, where=where)
    attn_ops, attn_in, attn_out, attn_shape = _attn_sample_call(*attn_args)
    caches = (CACHE_KT_ARG, CACHE_VT_ARG)
    for arg in caches:
        attn_in[arg] = pl.BlockSpec(memory_space=pl.ANY)
    ring = pltpu.VMEM((CACHE_SLOTS, D_ATTN, CACHE_LEN), F32)

    def body(*refs):
        ssd_in_refs, refs = refs[:N_SSD_INPUTS], refs[N_SSD_INPUTS:]
        attn_in_refs, refs = list(refs[:len(attn_ops)]), refs[len(attn_ops):]
        y_ref, hn_ref, o_ref, xpad, aux, kt_ring, vt_ring, sems = refs
        step = pl.program_id(0)

        def fetch(s):
            slot = s % CACHE_SLOTS
            return [pltpu.make_async_copy(attn_in_refs[arg].at[s], buf.at[slot], sems.at[j, slot])
                    for j, (arg, buf) in enumerate(zip(caches, (kt_ring, vt_ring)))]

        @pl.when(step == 0)
        def _prime():
            for s in range(CACHE_SLOTS - 1):
                for copy in fetch(s):
                    copy.start()

        @pl.when(step + (CACHE_SLOTS - 1) < steps)
        def _ahead():
            for copy in fetch(step + (CACHE_SLOTS - 1)):
                copy.start()

        for copy in fetch(step):
            copy.wait()
        for arg, buf in zip(caches, (kt_ring, vt_ring)):
            attn_in_refs[arg] = buf.at[step % CACHE_SLOTS]

        ssd = _ssd_stages(False, lambda: step % nchunks == 0, *ssd_in_refs, y_ref, hn_ref, xpad, aux)
        attn = _attn_sample_stages(*attn_in_refs, o_ref)
        for stage in (ssd, attn, ssd, attn):
            next(stage, None)

    y, h_new, attn = pl.pallas_call(
        body,
        grid=(steps,),
        in_specs=ssd_in + attn_in,
        out_specs=ssd_out + [attn_out],
        out_shape=ssd_shape + [attn_shape],
        scratch_shapes=scratch + [ring, ring, pltpu.SemaphoreType.DMA((len(caches), CACHE_SLOTS))],
        compiler_params=pltpu.CompilerParams(dimension_semantics=("arbitrary",),
                                             vmem_limit_bytes=VMEM_LIMIT),
        name="ssd_attn_sample",
    )(*ssd_ops, *attn_ops)
    return y, h_new, attn


def _row(v, width=None):
    v = v.reshape(1, -1).astype(F32)
    if width is not None and v.shape[1] < width:
        v = jnp.pad(v, ((0, 0), (0, width - v.shape[1])))
    return v


def kernel(x_prompt, x_sample, cache_k, cache_v, state_conv, state_ssm, p_prompt, p_sample,
           ln_in_g, ln_in_b, w_in, conv_w, conv_b, dt_bias, a_log, d_skip, ssd_norm_w, w_out,
           ln1_g, ln1_b, w_up, w_down, ln2_g, ln2_b, w_gate, w_ple, ln3_g, ln3_b):
    depth = w_in.shape[0]
    assert depth == 1, "single-layer step"
    alpha = (2 * depth) ** 0.25
    bsz, seq, _ = x_prompt.shape
    nd, dec_t, _ = x_sample.shape
    lyr = 0

    w_t = jnp.transpose(w_in[lyr])
    w_proj = w_t[:D_PROJ].astype(BF16)
    w_dt = jnp.pad(w_t[D_PROJ:], ((0, DT_PAD - N_HEADS), (0, 0))).astype(BF16)
    gin, bin_ = _row(ln_in_g), _row(ln_in_b)
    ssd_params = (conv_w[lyr].astype(F32), _row(conv_b[lyr]), _row(dt_bias[lyr], DT_PAD),
                  _row(a_log[lyr], DT_PAD), _row(jnp.repeat(d_skip[lyr], HEAD_DIM)),
                  _row(ssd_norm_w[lyr]))
    post_params = (gin, bin_, w_out[lyr].astype(BF16), _row(ln1_g[lyr]), _row(ln1_b[lyr]),
                   w_up[lyr].astype(BF16), w_down[lyr].astype(BF16), _row(ln2_g[lyr]),
                   _row(ln2_b[lyr]), w_gate[lyr].astype(BF16), w_ple[lyr].astype(BF16),
                   _row(ln3_g[lyr]), _row(ln3_b[lyr]))

    q, k, v, k_t, v_t, z, xbc, dt = _in_proj(x_prompt, gin, bin_, w_proj, w_dt, head_major=True, tm=1024)
    n_tok = nd * dec_t
    flat = lambda t: t.reshape(1, n_tok, t.shape[-1])
    toks = lambda t: t.reshape(nd, dec_t, t.shape[-1])
    qs, ks, vs, ks_t, vs_t, zs, xbcs, dts = _in_proj(flat(x_sample), gin, bin_, w_proj, w_dt,
                                                   head_major=False, tm=SHORT_SEQ * LANES)

    attn = _attn_prompt(q, k, v)
    transposed = lambda c: jnp.transpose(c, (0, 2, 3, 1)).reshape(nd, D_ATTN, CACHE_LEN)
    ssd_args = (z, xbc, dt, jnp.zeros((bsz, PAD_ROWS, CONV_DIM), F32),
                jnp.zeros((bsz, D_SSD, D_STATE), F32)) + ssd_params
    attn_args = (toks(qs), toks(ks), toks(vs), transposed(cache_k[lyr]), transposed(cache_v[lyr]))
    if bsz * (seq // CHUNK) == nd:
        ssd_y, ssm_p, attn_s = _ssd_with_sample_attn(ssd_args, attn_args)
    else:
        ssd_y, ssm_p = _ssd(*ssd_args)
        attn_s = _attn_sample(*attn_args)
    conv_prev = jnp.pad(state_conv[lyr].astype(F32), ((0, 0), (PAD_ROWS - (CONV_W - 1), 0), (0, 0)))
    ssd_s, ssm_s = _ssd(toks(zs), toks(xbcs), toks(dts), conv_prev,
                        state_ssm[lyr].reshape(nd, D_SSD, D_STATE).astype(F32), *ssd_params)

    y_prompt = _post(x_prompt, attn, ssd_y, p_prompt[lyr], *post_params, alpha=alpha, tm=512)
    y_sample = _post(flat(x_sample), attn_s, flat(ssd_s), flat(p_sample[lyr]), *post_params,
                     alpha=alpha, tm=256)

    from_t = lambda t: jnp.transpose(t.reshape(1, bsz, N_HEADS, HEAD_DIM, seq), (0, 1, 4, 2, 3))
    by_step = lambda t: jnp.transpose(t.reshape(1, dec_t, N_HEADS, HEAD_DIM, nd), (0, 4, 1, 2, 3))
    tail = lambda t: t[None, :, -(CONV_W - 1):, :]
    state = lambda t, n: t.reshape(1, n, N_HEADS, HEAD_DIM, D_STATE)
    return (y_prompt, y_sample.reshape(nd, dec_t, D_MODEL), from_t(k_t), from_t(v_t),
            by_step(ks_t), by_step(vs_t), tail(xbc), tail(toks(xbcs)), state(ssm_p, bsz), state(ssm_s, nd))
```

```python
import functools

import numpy as np
import jax
import jax.numpy as jnp
from jax import lax
from jax.experimental import pallas as pl
from jax.experimental.pallas import tpu as pltpu

F32 = jnp.float32
BF16 = jnp.bfloat16

D_MODEL = 1024
HEAD_DIM = 64
D_ATTN = 512
D_SSD = 512
N_HEADS = 8
SSD_GROUPS = 2
GROUP_W = D_SSD // SSD_GROUPS
D_STATE = 128
CONV_W = 4
CONV_DIM = D_SSD + 2 * SSD_GROUPS * D_STATE
CHUNK = 128
D_FF = 4096
D_PLE = 256
LN_EPS = 1e-5
WINDOW_STEPS = 128
DILATIONS = (1, 4, 16)
PHASES = DILATIONS[2]
MID_PHASES = DILATIONS[1]
CACHE_LEN = 2048
DT_PAD = 128
D_PROJ = 3 * D_ATTN + D_SSD + CONV_DIM
SUBLANES = 8
LANES = 128
VMEM_LIMIT = 56 * 1024 * 1024
NEG_INF = float("-inf")


def _slopes():
    return np.array([2.0 ** (-8.0 * (h + 1) / N_HEADS) for h in range(N_HEADS)], dtype=np.float64)


def _layer_norm(x, g, b):
    mu = jnp.mean(x, -1, keepdims=True)
    xc = x - mu
    var = jnp.mean(xc * xc, -1, keepdims=True)
    return xc * lax.rsqrt(var + LN_EPS) * g + b


def _silu(x):
    return x * (1.0 / (1.0 + jnp.exp(-x)))


def _dot(a, b):
    return jnp.dot(a, b, preferred_element_type=F32)


def _dot_nt(a, b):
    return lax.dot_general(a, b, (((1,), (1,)), ((), ())), preferred_element_type=F32)


def _const_spec(shape):
    nd = len(shape)
    return pl.BlockSpec(shape, lambda *_: (0,) * nd, pipeline_mode=pl.Buffered(1))


HEAD_PAIRS = N_HEADS // 2


def _inproj_body(head_major, x_ref, g_ref, b_ref, w_ref, wdt_ref, *out_refs):
    tm = x_ref.shape[0]
    parts = [slice(i * tm // 2, (i + 1) * tm // 2) for i in range(2)]
    h = [_layer_norm(x_ref[r, :], g_ref[...], b_ref[...]).astype(BF16) for r in parts]
    proj = lambda lo, width: [_dot_nt(v, w_ref[lo:lo + width, :]) for v in h]
    if head_major:
        q_ref, k_ref, v_ref, kt_ref, vt_ref, z_ref, xbc_ref, dt_ref = out_refs
        for idx, (ref, t_ref) in enumerate(((q_ref, None), (k_ref, kt_ref), (v_ref, vt_ref))):
            for r, res in zip(parts, proj(idx * D_ATTN, D_ATTN)):
                for c in range(HEAD_PAIRS):
                    ref[c, r, :] = res[:, c * LANES:(c + 1) * LANES]
                if t_ref is not None:
                    t_ref[:, r] = res.T
    else:
        q_ref, k_ref, v_ref, kts_ref, vts_ref, z_ref, xbc_ref, dt_ref, slabs = out_refs
        for idx, (ref, ts_ref) in enumerate(((q_ref, None), (k_ref, kts_ref), (v_ref, vts_ref))):
            for r, res in zip(parts, proj(idx * D_ATTN, D_ATTN)):
                ref[r, :] = res
                if ts_ref is not None:
                    for c in range(HEAD_PAIRS):
                        slabs[c, r, :] = res[:, c * LANES:(c + 1) * LANES]
            if ts_ref is not None:
                for t in range(SHORT_SEQ):
                    for c in range(HEAD_PAIRS):
                        by_seq = slabs[c, pl.ds(t, tm // SHORT_SEQ, stride=SHORT_SEQ), :]
                        ts_ref[t, c * LANES:(c + 1) * LANES, :] = by_seq.T
    lo = 3 * D_ATTN
    for ref, width in ((z_ref, D_SSD), (xbc_ref, CONV_DIM)):
        for r, res in zip(parts, proj(lo, width)):
            ref[r, :] = res
        lo += width
    for r, v in zip(parts, h):
        dt_ref[r, :] = _dot_nt(v, wdt_ref[...])


def _in_proj(x, g, b, w, w_dt, *, head_major, tm):
    bsz, seq, _ = x.shape
    row = lambda width: pl.BlockSpec((None, tm, width), lambda i, j: (i, j, 0))
    row_shape = lambda width: jax.ShapeDtypeStruct((bsz, seq, width), F32)
    if head_major:
        slab = pl.BlockSpec((None, HEAD_PAIRS, tm, LANES), lambda i, j: (i, 0, j, 0))
        slab_shape = jax.ShapeDtypeStruct((bsz, HEAD_PAIRS, seq, LANES), F32)
        tr = pl.BlockSpec((None, D_ATTN, tm), lambda i, j: (i, 0, j))
        tr_shape = jax.ShapeDtypeStruct((bsz, D_ATTN, seq), F32)
        qkv_specs, qkv_shapes = [slab, slab, slab, tr, tr], [slab_shape] * 3 + [tr_shape] * 2
        scratch = []
    else:
        assert tm == SHORT_SEQ * LANES, "one tile = 128 short sequences"
        by_step = pl.BlockSpec((SHORT_SEQ, D_ATTN, LANES), lambda i, j: (0, 0, i * (seq // tm) + j))
        by_step_shape = jax.ShapeDtypeStruct((SHORT_SEQ, D_ATTN, bsz * seq // SHORT_SEQ), F32)
        qkv_specs = [row(D_ATTN)] * 3 + [by_step] * 2
        qkv_shapes = [row_shape(D_ATTN)] * 3 + [by_step_shape] * 2
        scratch = [pltpu.VMEM((HEAD_PAIRS, tm, LANES), F32)]
    rest = (D_SSD, CONV_DIM, DT_PAD)
    return pl.pallas_call(
        functools.partial(_inproj_body, head_major),
        grid=(bsz, seq // tm),
        in_specs=[row(D_MODEL), _const_spec((1, D_MODEL)), _const_spec((1, D_MODEL)),
                  _const_spec((D_PROJ, D_MODEL)), _const_spec((DT_PAD, D_MODEL))],
        out_specs=qkv_specs + [row(wd) for wd in rest],
        out_shape=qkv_shapes + [row_shape(wd) for wd in rest],
        scratch_shapes=scratch,
        compiler_params=pltpu.CompilerParams(dimension_semantics=("parallel", "parallel"),
                                             vmem_limit_bytes=VMEM_LIMIT),
        name="in_proj",
    )(x, g, b, w, w_dt)


FF_CHUNK = 1024
POST_STREAMS = 2


def _post_body(alpha, x_ref, attn_ref, ssd_ref, pe_ref, gin_ref, bin_ref, wout_ref, g1_ref, b1_ref,
               wup_ref, wdown_ref, g2_ref, b2_ref, wgate_ref, wple_ref, g3_ref, b3_ref, y_ref):
    tm = x_ref.shape[0]
    parts = [slice(i * tm // POST_STREAMS, (i + 1) * tm // POST_STREAMS) for i in range(POST_STREAMS)]
    each = lambda fn, *lists: [fn(*args) for args in zip(*lists)]
    xn = [_layer_norm(x_ref[r, :], gin_ref[...], bin_ref[...]) for r in parts]
    mixed = [jnp.concatenate([attn_ref[c, r, :] for c in range(HEAD_PAIRS)] + [ssd_ref[r, :]],
                             axis=-1).astype(BF16) for r in parts]
    proj = [_dot(m, wout_ref[...]) for m in mixed]
    h = each(lambda x, p: _layer_norm(alpha * x + p, g1_ref[...], b1_ref[...]), xn, proj)
    hb = [v.astype(BF16) for v in h]
    u = [None] * POST_STREAMS
    for c in range(D_FF // FF_CHUNK):
        cols = slice(c * FF_CHUNK, (c + 1) * FF_CHUNK)
        a = [jnp.maximum(_dot(v, wup_ref[:, cols]), 0.0) for v in hb]
        part = [_dot((v * v).astype(BF16), wdown_ref[cols, :]) for v in a]
        u = part if c == 0 else each(lambda s, p: s + p, u, part)
    h = each(lambda v, w: _layer_norm(alpha * v + w, g2_ref[...], b2_ref[...]), h, u)
    gate = [1.0 / (1.0 + jnp.exp(-_dot(v.astype(BF16), wgate_ref[...]))) for v in h]
    emb = [_dot(pe_ref[r, :].astype(BF16), wple_ref[...]) for r in parts]
    for r, v, g, e in zip(parts, h, gate, emb):
        y_ref[r, :] = _layer_norm(alpha * v + g * e, g3_ref[...], b3_ref[...])


def _post(x, attn, ssd, pe, gin, bin_, wout, g1, b1, wup, wdown, g2, b2, wgate, wple, g3, b3, *,
          alpha, tm):
    bsz, seq, _ = x.shape
    row = lambda width: pl.BlockSpec((None, tm, width), lambda i, j: (i, j, 0))
    slab = pl.BlockSpec((None, HEAD_PAIRS, tm, LANES), lambda i, j: (i, 0, j, 0))
    vec = _const_spec((1, D_MODEL))
    return pl.pallas_call(
        functools.partial(_post_body, alpha),
        grid=(bsz, seq // tm),
        in_specs=[row(D_MODEL), slab, row(D_SSD), row(D_PLE), vec, vec,
                  _const_spec((D_MODEL, D_MODEL)), vec, vec,
                  _const_spec((D_MODEL, D_FF)), _const_spec((D_FF, D_MODEL)), vec, vec,
                  _const_spec((D_MODEL, D_MODEL)), _const_spec((D_PLE, D_MODEL)), vec, vec],
        out_specs=row(D_MODEL),
        out_shape=jax.ShapeDtypeStruct((bsz, seq, D_MODEL), F32),
        compiler_params=pltpu.CompilerParams(dimension_semantics=("parallel", "parallel"),
                                             vmem_limit_bytes=VMEM_LIMIT),
        name="post",
    )(x, attn, ssd, pe, gin, bin_, wout, g1, b1, wup, wdown, g2, b2, wgate, wple, g3, b3)


U_ROWS = CACHE_LEN // PHASES
B2_ROWS = U_ROWS // (PHASES // MID_PHASES)
TAB3_LO, TAB2_LO, TAB1_LO, TAB_W = 0, WINDOW_STEPS, 3 * WINDOW_STEPS, 5 * WINDOW_STEPS
WAVE = 4


def _prompt_bias_tables():
    slopes = _slopes()
    u = np.arange(U_ROWS)
    d3 = (u[:, None] - u[None, :]).astype(np.float64)
    k4, ul4 = np.meshgrid(np.arange(PHASES // MID_PHASES), np.arange(B2_ROWS), indexing="ij")
    j2 = (PHASES // MID_PHASES * ul4 + k4).reshape(-1)
    d2 = j2[:, None] - np.concatenate([j2 - WINDOW_STEPS, j2])[None, :]
    p16, ul16 = np.meshgrid(np.arange(PHASES), np.arange(SUBLANES), indexing="ij")
    t1 = (PHASES * ul16 + p16).reshape(-1)
    d1 = t1[:, None] - (np.arange(2 * WINDOW_STEPS) - WINDOW_STEPS)[None, :]
    tabs = []
    for dist, dil in zip((d3, d2, d1), reversed(DILATIONS)):
        valid = (dist >= 0) & (dist <= WINDOW_STEPS)
        per_head = [np.where(valid, -slopes[h] * dist * dil, NEG_INF) for h in range(N_HEADS)]
        tabs.append(np.stack(per_head))
    tab = np.concatenate(tabs, axis=-1)
    return jnp.asarray(tab.reshape(HEAD_PAIRS, 2 * U_ROWS, TAB_W), dtype=F32)


def _attn_blocks(blocks, sinks, first_head):
    half = U_ROWS
    state = [dict() for _ in blocks]

    def scores(i):
        q2, k, state[i]["v"], bias = blocks[i]()
        state[i]["s"] = _dot_nt(q2, k) + bias

    def top(i):
        state[i]["m"] = jnp.max(state[i]["s"], -1, keepdims=True)

    def weights(i):
        e = jnp.exp(state[i].pop("s") - state[i]["m"])
        state[i]["den"] = jnp.sum(e, -1, keepdims=True)
        state[i]["e"] = e.astype(BF16)

    def values(i):
        state[i]["pv"] = _dot(state[i].pop("e"), state[i].pop("v"))

    def finish(i):
        pv, m, den = state[i]["pv"], state[i]["m"], state[i]["den"]
        o = jnp.where(first_head, pv[:half], pv[half:])
        m2 = jnp.where(first_head, m[:half], m[half:])
        den2 = jnp.where(first_head, den[:half], den[half:])
        sinks[i](o / den2, m2 + jnp.log(den2))
        state[i] = None

    stages = (scores, top, weights, values, finish)
    groups = [range(j, min(j + WAVE, len(blocks))) for j in range(0, len(blocks), WAVE)]
    for t in range(len(groups) + len(stages) - 1):
        for g, members in enumerate(groups):
            if 0 <= t - g < len(stages):
                for i in members:
                    stages[t - g](i)


def _attn_prompt_body(q_ref, k_ref, v_ref, tab_ref, o_ref, qs, q0p, q1p, kp, vp, kn, vn, quarter,
                      o_scr, l_scr):
    lane = lax.broadcasted_iota(jnp.int32, (U_ROWS, LANES), 1)
    first_head = lane < HEAD_DIM
    scale = HEAD_DIM ** -0.5

    kn[...] = k_ref[...].astype(BF16)
    vn[...] = v_ref[...].astype(BF16)

    def split(src_ref, emit):
        for r in range(MID_PHASES):
            quarter[r] = src_ref[pl.ds(r, CACHE_LEN // MID_PHASES, stride=MID_PHASES), :]
        for r in range(MID_PHASES):
            for kk in range(PHASES // MID_PHASES):
                emit(MID_PHASES * kk + r,
                     quarter[r, pl.ds(kk, U_ROWS, stride=PHASES // MID_PHASES), :])

    def emit_q(p, tile):
        tile = tile * scale
        qs[p] = tile
        q0p[p] = jnp.where(first_head, tile, 0.0).astype(BF16)
        q1p[p] = jnp.where(first_head, 0.0, tile).astype(BF16)

    def emit_to(dst):
        def emit(p, tile):
            dst[p] = tile.astype(BF16)
        return emit

    split(q_ref, emit_q)
    split(k_ref, emit_to(kp))
    split(v_ref, emit_to(vp))

    blocks, sinks = [], []

    def block3(p):
        return (jnp.concatenate([q0p[p], q1p[p]], 0), kp[p], vp[p],
                tab_ref[:, TAB3_LO:TAB3_LO + U_ROWS])

    def sink3(p, o, l):
        o_scr[0, p] = o
        l_scr[0, p] = l

    for p in range(PHASES):
        blocks.append(functools.partial(block3, p))
        sinks.append(functools.partial(sink3, p))

    def block2(tiles, n):
        gather = lambda src, rr: [src[p, rr, :] for p in tiles]
        rows = pl.ds(B2_ROWS * n, B2_ROWS)
        q2 = jnp.concatenate(gather(q0p, rows) + gather(q1p, rows), 0)
        if n == 0:
            return (q2, jnp.concatenate(gather(kp, rows), 0), jnp.concatenate(gather(vp, rows), 0),
                    tab_ref[:, TAB2_LO + WINDOW_STEPS:TAB2_LO + 2 * WINDOW_STEPS])
        prev = pl.ds(B2_ROWS * (n - 1), B2_ROWS)
        return (q2, jnp.concatenate(gather(kp, prev) + gather(kp, rows), 0),
                jnp.concatenate(gather(vp, prev) + gather(vp, rows), 0),
                tab_ref[:, TAB2_LO:TAB2_LO + 2 * WINDOW_STEPS])

    def sink2(tiles, n, o, l):
        rows = pl.ds(B2_ROWS * n, B2_ROWS)
        for kk, p in enumerate(tiles):
            o_scr[1, p, rows, :] = o[B2_ROWS * kk:B2_ROWS * (kk + 1)]
            l_scr[1, p, rows, :] = l[B2_ROWS * kk:B2_ROWS * (kk + 1)]

    for r in range(MID_PHASES):
        tiles = [MID_PHASES * kk + r for kk in range(PHASES // MID_PHASES)]
        for n in range(U_ROWS // B2_ROWS):
            blocks.append(functools.partial(block2, tiles, n))
            sinks.append(functools.partial(sink2, tiles, n))

    def block1(n):
        rows = pl.ds(n * SUBLANES, SUBLANES)
        q = jnp.concatenate([qs[p, rows, :] for p in range(PHASES)], 0)
        q2 = jnp.concatenate([jnp.where(first_head, q, 0.0), jnp.where(first_head, 0.0, q)],
                             0).astype(BF16)
        if n == 0:
            keys = pl.ds(0, U_ROWS)
            return (q2, kn[keys, :], vn[keys, :],
                    tab_ref[:, TAB1_LO + WINDOW_STEPS:TAB1_LO + 2 * WINDOW_STEPS])
        keys = pl.ds((n - 1) * U_ROWS, 2 * U_ROWS)
        return q2, kn[keys, :], vn[keys, :], tab_ref[:, TAB1_LO:TAB1_LO + 2 * WINDOW_STEPS]

    def sink1(n, o, l):
        rows = pl.ds(n * SUBLANES, SUBLANES)
        for p in range(PHASES):
            o_scr[2, p, rows, :] = o[SUBLANES * p:SUBLANES * (p + 1)]
            l_scr[2, p, rows, :] = l[SUBLANES * p:SUBLANES * (p + 1)]

    for n in range(U_ROWS // SUBLANES):
        blocks.append(functools.partial(block1, n))
        sinks.append(functools.partial(sink1, n))

    _attn_blocks(blocks, sinks, first_head)

    def phase_rows(p):
        return pl.ds(p, U_ROWS, stride=PHASES)

    for p in range(PHASES):
        l3, l2, l1 = l_scr[0, p], l_scr[1, p], l_scr[2, p]
        top = jnp.maximum(jnp.maximum(l3, l2), l1)
        e3, e2, e1 = jnp.exp(l3 - top), jnp.exp(l2 - top), jnp.exp(l1 - top)
        total = e3 + e2 + e1
        o_ref[phase_rows(p), :] = (e3 * o_scr[0, p] + e2 * o_scr[1, p] + e1 * o_scr[2, p]) / total


def _attn_prompt(q, k, v):
    bsz, _, seq, _ = k.shape
    assert seq == CACHE_LEN, "prompt attention is laid out for a 2048-token prompt"
    slab = pl.BlockSpec((None, None, seq, LANES), lambda b, c: (b, c, 0, 0))
    tile = (PHASES, U_ROWS, LANES)
    return pl.pallas_call(
        _attn_prompt_body,
        grid=(bsz, HEAD_PAIRS),
        in_specs=[slab, slab, slab,
                  pl.BlockSpec((None, 2 * U_ROWS, TAB_W), lambda b, c: (c, 0, 0))],
        out_specs=slab,
        out_shape=jax.ShapeDtypeStruct((bsz, HEAD_PAIRS, seq, LANES), F32),
        scratch_shapes=[pltpu.VMEM(tile, F32)] + [pltpu.VMEM(tile, BF16)] * 4
                       + [pltpu.VMEM((seq, LANES), BF16)] * 2
                       + [pltpu.VMEM((4, seq // 4, LANES), F32),
                          pltpu.VMEM((3,) + tile, F32), pltpu.VMEM((3,) + tile, F32)],
        compiler_params=pltpu.CompilerParams(dimension_semantics=("parallel", "parallel"),
                                             vmem_limit_bytes=VMEM_LIMIT),
        name="attn_prompt",
    )(q, k, v, _prompt_bias_tables())


DEC_T = 8
QH = DEC_T * N_HEADS
B2_SPAN, B1_SPAN = 512, 128
SAMPLE_TAB_W = CACHE_LEN + B2_SPAN + B1_SPAN


def _sample_bias_tables():
    slopes = _slopes()
    i = np.repeat(np.arange(DEC_T), N_HEADS)[:, None]
    sl = np.tile(slopes, DEC_T)[:, None]

    def cache_bias(span, dil):
        t = CACHE_LEN - span + np.arange(span)[None, :]
        dist = CACHE_LEN + i - t
        valid = (dist % dil == 0) & (dist <= WINDOW_STEPS * dil)
        return np.where(valid, -sl * dist, NEG_INF)

    cache_tab = np.concatenate([cache_bias(CACHE_LEN, 16), cache_bias(B2_SPAN, 4),
                                cache_bias(B1_SPAN, 1)], axis=1)
    c = np.arange(CHUNK)[None, :]
    dn = i - c
    is_new = c < DEC_T
    n1 = np.where(is_new & (dn >= 0), -sl * dn, NEG_INF)
    n2 = np.where(is_new & ((dn == 0) | (dn == 4)), -sl * dn, NEG_INF)
    n3 = np.where(is_new & (dn == 0), 0.0, NEG_INF)
    return jnp.asarray(cache_tab, dtype=F32), jnp.asarray(np.stack([n1, n2, n3]), dtype=F32)


def _head_selector():
    h_row = np.tile(np.arange(N_HEADS), DEC_T)[:, None]
    h_col = (np.arange(D_ATTN) // HEAD_DIM)[None, :]
    return jnp.asarray((h_row == h_col).astype(np.float32))


def _attn_sample_stages(q_ref, kn_ref, vn_ref, kt_ref, vt_ref, tab_ref, tabn_ref, sel_ref, o_ref):
    sel = sel_ref[...]
    q = q_ref[...] * (HEAD_DIM ** -0.5)
    q_rows = jnp.concatenate([jnp.broadcast_to(q[i:i + 1, :], (N_HEADS, D_ATTN))
                              for i in range(DEC_T)], 0)
    qall = (q_rows * sel).astype(BF16)
    pad = jnp.zeros((CHUNK - DEC_T, D_ATTN), F32)
    kn = jnp.concatenate([kn_ref[...], pad], 0).astype(BF16)
    vn = jnp.concatenate([vn_ref[...], pad], 0).astype(BF16)

    s_all = _dot(qall, kt_ref[...].astype(BF16))
    s_new = _dot_nt(qall, kn)
    yield
    lo2, lo1 = CACHE_LEN - B2_SPAN, CACHE_LEN - B1_SPAN

    def branch(s_cache, s_fresh, v_t):
        m = jnp.maximum(jnp.max(s_cache, -1, keepdims=True), jnp.max(s_fresh, -1, keepdims=True))
        e_c, e_f = jnp.exp(s_cache - m), jnp.exp(s_fresh - m)
        den = jnp.sum(e_c, -1, keepdims=True) + jnp.sum(e_f, -1, keepdims=True)
        acc = _dot_nt(e_c.astype(BF16), v_t.astype(BF16)) + _dot(e_f.astype(BF16), vn)
        return m, den, acc

    m3, den3, acc3 = branch(s_all + tab_ref[:, :CACHE_LEN], s_new + tabn_ref[2], vt_ref[...])
    m2, den2, acc2 = branch(s_all[:, lo2:] + tab_ref[:, CACHE_LEN:CACHE_LEN + B2_SPAN],
                            s_new + tabn_ref[1], vt_ref[:, lo2:])
    m1, den1, acc1 = branch(s_all[:, lo1:] + tab_ref[:, CACHE_LEN + B2_SPAN:],
                            s_new + tabn_ref[0], vt_ref[:, lo1:])

    top = jnp.maximum(jnp.maximum(m1, m2), m3)
    sc1, sc2, sc3 = den1 * jnp.exp(m1 - top), den2 * jnp.exp(m2 - top), den3 * jnp.exp(m3 - top)
    total = sc1 + sc2 + sc3
    mixed = ((sc1 / total) * (acc1 / den1) + (sc2 / total) * (acc2 / den2)
             + (sc3 / total) * (acc3 / den3)) * sel
    for i in range(DEC_T):
        row = jnp.sum(mixed[N_HEADS * i:N_HEADS * (i + 1)], axis=0, keepdims=True)
        for c in range(HEAD_PAIRS):
            o_ref[c, i:i + 1, :] = row[:, c * LANES:(c + 1) * LANES]


def _attn_sample_body(*refs):
    for _ in _attn_sample_stages(*refs):
        pass


CACHE_KT_ARG, CACHE_VT_ARG = 3, 4
CACHE_SLOTS = 3


def _attn_sample_call(q, k_new, v_new, cache_kt, cache_vt):
    n = q.shape[0]
    assert cache_kt.shape[1:] == (D_ATTN, CACHE_LEN) and q.shape[1] == DEC_T
    tok = pl.BlockSpec((None, DEC_T, D_ATTN), lambda i: (i, 0, 0))
    cache = pl.BlockSpec((None, D_ATTN, CACHE_LEN), lambda i: (i, 0, 0))
    cache_tab, new_tab = _sample_bias_tables()
    operands = (q, k_new, v_new, cache_kt, cache_vt, cache_tab, new_tab, _head_selector())
    in_specs = [tok, tok, tok, cache, cache, _const_spec((QH, SAMPLE_TAB_W)),
                _const_spec((3, QH, CHUNK)), _const_spec((QH, D_ATTN))]
    out_spec = pl.BlockSpec((None, HEAD_PAIRS, DEC_T, LANES), lambda i: (0, 0, i, 0))
    out_shape = jax.ShapeDtypeStruct((1, HEAD_PAIRS, n * DEC_T, LANES), F32)
    return operands, in_specs, out_spec, out_shape


def _attn_sample(q, k_new, v_new, cache_kt, cache_vt):
    operands, in_specs, out_spec, out_shape = _attn_sample_call(q, k_new, v_new, cache_kt, cache_vt)
    return pl.pallas_call(
        _attn_sample_body,
        grid=(q.shape[0],),
        in_specs=in_specs,
        out_specs=out_spec,
        out_shape=out_shape,
        compiler_params=pltpu.CompilerParams(dimension_semantics=("parallel",),
                                             vmem_limit_bytes=VMEM_LIMIT),
        name="attn_sample",
    )(*operands)


HEADS_PER_GROUP = N_HEADS // SSD_GROUPS
PAD_ROWS = SUBLANES
SHORT_SEQ = SUBLANES
SEQS_PER_TILE = CHUNK // SHORT_SEQ


def _ssd_stages(packed, first_chunk, z_ref, xbc_ref, dt_ref, cp_ref, h0_ref, cw_ref, cb_ref, dtb_ref,
                alog_ref, dsk_ref, nw_ref, y_ref, hn_ref, xpad, aux):
    if packed:
        aux[0:CHUNK, :] = cp_ref[...]
        aux[CHUNK:, :] = jnp.zeros((PAD_ROWS, CONV_DIM), F32)
        xpad[0:PAD_ROWS, :] = jnp.zeros((PAD_ROWS, CONV_DIM), F32)
    else:
        @pl.when(first_chunk())
        def _init():
            xpad[0:PAD_ROWS, :] = cp_ref[...]
            aux[...] = h0_ref[...]
    yield
    xpad[PAD_ROWS:, :] = xbc_ref[...]

    step = lax.broadcasted_iota(jnp.int32, (CHUNK, CONV_DIM), 0) % SHORT_SEQ
    conv = cb_ref[...]
    for back in range(CONV_W):
        tap = CONV_W - 1 - back
        rows_back = xpad[pl.ds(PAD_ROWS - back, CHUNK), :]
        if packed and back:
            rows_back = jnp.where(step < back, aux[pl.ds(PAD_ROWS - back, CHUNK), :], rows_back)
        conv = conv + rows_back * cw_ref[tap:tap + 1, :]
    xc = _silu(conv)
    xs = xc[:, :D_SSD]

    lane = lax.broadcasted_iota(jnp.int32, (CHUNK, DT_PAD), 1)
    dt_raw = dt_ref[...] + dtb_ref[...]
    dt = jnp.maximum(dt_raw, 0.0) + jnp.log1p(jnp.exp(-jnp.abs(dt_raw)))
    dt = jnp.where(lane < N_HEADS, dt, 0.0)
    adt = dt * (-jnp.exp(alog_ref[...]))

    ri = lax.broadcasted_iota(jnp.int32, (CHUNK, CHUNK), 0)
    ci = lax.broadcasted_iota(jnp.int32, (CHUNK, CHUNK), 1)
    causal = ri >= ci
    if packed:
        causal = causal & (ri // SHORT_SEQ == ci // SHORT_SEQ)
    exact_dot = functools.partial(jnp.dot, precision=lax.Precision.HIGHEST,
                                  preferred_element_type=F32)
    cs = exact_dot(jnp.where(causal, 1.0, 0.0).astype(F32), adt)
    cs_t = cs.T
    if packed:
        pick_last = ci == (ri // SHORT_SEQ) * SHORT_SEQ + (SHORT_SEQ - 1)
        cs_end = exact_dot(jnp.where(pick_last, 1.0, 0.0).astype(F32), cs)
    else:
        cs_end = cs[CHUNK - 1:CHUNK, :]

    head_of_lane = lax.broadcasted_iota(jnp.int32, (CHUNK, GROUP_W), 1) // HEAD_DIM

    def per_head(cols):
        out = cols[HEADS_PER_GROUP - 1]
        for hl in range(HEADS_PER_GROUP - 2, -1, -1):
            out = jnp.where(head_of_lane == hl, cols[hl], out)
        return out

    y_groups = []
    for g in range(SSD_GROUPS):
        heads = range(g * HEADS_PER_GROUP, (g + 1) * HEADS_PER_GROUP)
        grp = slice(g * GROUP_W, (g + 1) * GROUP_W)
        b_g = xc[:, D_SSD + g * D_STATE:D_SSD + (g + 1) * D_STATE].astype(BF16)
        c_g = xc[:, D_SSD + (SSD_GROUPS + g) * D_STATE:
                 D_SSD + (SSD_GROUPS + g + 1) * D_STATE].astype(BF16)
        cs_cols = [cs[:, h:h + 1] for h in heads]
        xdt_g = xs[:, grp] * per_head([dt[:, h:h + 1] for h in heads])
        xdt_b = xdt_g.astype(BF16)
        gram = _dot_nt(c_g, b_g)

        y_diag = jnp.zeros((CHUNK, GROUP_W), F32)
        for hl, h in enumerate(heads):
            seg = jnp.where(causal, cs_cols[hl] - cs_t[h:h + 1, :], NEG_INF)
            weights = (gram * jnp.exp(seg)).astype(BF16)
            y_diag = y_diag + jnp.where(head_of_lane == hl, _dot(weights, xdt_b), 0.0)

        to_end = per_head([jnp.exp(cs_end[:, h:h + 1] - cs_cols[hl]) for hl, h in enumerate(heads)])
        decayed_t = (xdt_g * to_end).T
        carried = per_head([jnp.exp(col) for col in cs_cols])
        if packed:
            h_prev = h0_ref[:, grp, :]
            wide = _dot_nt(c_g, h_prev.reshape(SEQS_PER_TILE * GROUP_W, D_STATE).astype(BF16))
            y_off = jnp.concatenate(
                [wide[s * SHORT_SEQ:(s + 1) * SHORT_SEQ, s * GROUP_W:(s + 1) * GROUP_W]
                 for s in range(SEQS_PER_TILE)], 0) * carried
            seq_of_step = lax.broadcasted_iota(jnp.int32, (GROUP_W, CHUNK), 1) // SHORT_SEQ
            per_seq = jnp.concatenate([jnp.where(seq_of_step == s, decayed_t, 0.0)
                                       for s in range(SEQS_PER_TILE)], 0).astype(BF16)
            new_states = _dot(per_seq, b_g).reshape(SEQS_PER_TILE, GROUP_W, D_STATE)
            for s in range(SEQS_PER_TILE):
                row = s * SHORT_SEQ
                keep = jnp.concatenate(
                    [jnp.broadcast_to(jnp.exp(cs_end[row:row + 1, h:h + 1]), (HEAD_DIM, D_STATE))
                     for h in heads], 0)
                hn_ref[s, grp, :] = h_prev[s] * keep + new_states[s]
        else:
            h_prev = aux[grp, :]
            y_off = _dot_nt(c_g, h_prev.astype(BF16)) * carried
            keep = jnp.concatenate(
                [jnp.broadcast_to(jnp.exp(cs_end[:, h:h + 1]), (HEAD_DIM, D_STATE)) for h in heads], 0)
            aux[grp, :] = h_prev * keep + _dot(decayed_t.astype(BF16), b_g)
        y_groups.append(y_diag + y_off)

    y = jnp.concatenate(y_groups, axis=-1) + dsk_ref[...] * xs
    y = y * _silu(z_ref[...])
    normed = []
    for g in range(SSD_GROUPS):
        yg = y[:, g * GROUP_W:(g + 1) * GROUP_W]
        normed.append(yg * lax.rsqrt(jnp.mean(yg * yg, -1, keepdims=True) + LN_EPS))
    y_ref[...] = jnp.concatenate(normed, axis=-1) * nw_ref[...]

    if not packed:
        hn_ref[...] = aux[...]
        xpad[0:PAD_ROWS, :] = xpad[CHUNK:, :]


def _ssd_body(*args):
    for _ in _ssd_stages(*args):
        pass


N_SSD_INPUTS = 11
SSD_STATE_ARG = 4


def _ssd_call(z, xbc, dt, conv_prev, h0, cw, cb, dtb, alog, dsk, nw, where=None):
    n, seq, _ = z.shape
    packed = seq == SHORT_SEQ
    if packed:
        assert n % SEQS_PER_TILE == 0 and where is None
        grid = (n // SEQS_PER_TILE, 1)
        fold = lambda t: t.reshape(grid[0], CHUNK, t.shape[-1])
        z, xbc, dt, conv_prev = fold(z), fold(xbc), fold(dt), fold(conv_prev)
        history_shape, state_shape = (None, CHUNK, CONV_DIM), (SEQS_PER_TILE, D_SSD, D_STATE)
        aux = pltpu.VMEM((CHUNK + PAD_ROWS, CONV_DIM), F32)
    else:
        assert seq % CHUNK == 0
        grid = (n, seq // CHUNK)
        history_shape, state_shape = (None, PAD_ROWS, CONV_DIM), (None, D_SSD, D_STATE)
        aux = pltpu.VMEM((D_SSD, D_STATE), F32)
    where = where or (lambda s, c: (s, c))
    tile = lambda width: pl.BlockSpec((None, CHUNK, width), lambda *g: where(*g) + (0,))
    per_seq = lambda shape: pl.BlockSpec(shape, lambda *g: (where(*g)[0], 0, 0))
    history, state = per_seq(history_shape), per_seq(state_shape)
    operands = (z, xbc, dt, conv_prev, h0, cw, cb, dtb, alog, dsk, nw)
    in_specs = [tile(D_SSD), tile(CONV_DIM), tile(DT_PAD), history, state,
                _const_spec((CONV_W, CONV_DIM)), _const_spec((1, CONV_DIM)),
                _const_spec((1, DT_PAD)), _const_spec((1, DT_PAD)),
                _const_spec((1, D_SSD)), _const_spec((1, D_SSD))]
    out_specs = [tile(D_SSD), state]
    out_shape = [jax.ShapeDtypeStruct(z.shape, F32), jax.ShapeDtypeStruct((n, D_SSD, D_STATE), F32)]
    scratch = [pltpu.VMEM((PAD_ROWS + CHUNK, CONV_DIM), F32), aux]
    return packed, grid, operands, in_specs, out_specs, out_shape, scratch


def _ssd(z, *rest):
    packed, grid, operands, in_specs, out_specs, out_shape, scratch = _ssd_call(z, *rest)
    if packed:
        steps = grid[0]
        in_specs[SSD_STATE_ARG] = pl.BlockSpec(memory_space=pl.ANY)
        scratch = scratch + [pltpu.VMEM((CACHE_SLOTS, SEQS_PER_TILE, D_SSD, D_STATE), F32),
                             pltpu.SemaphoreType.DMA((CACHE_SLOTS,))]

        def body(*refs):
            *io, xpad, aux, ring, sems = refs
            step = pl.program_id(0)

            def fetch(s):
                rows = pl.ds(s * SEQS_PER_TILE, SEQS_PER_TILE)
                return pltpu.make_async_copy(io[SSD_STATE_ARG].at[rows], ring.at[s % CACHE_SLOTS],
                                             sems.at[s % CACHE_SLOTS])

            @pl.when(step == 0)
            def _prime():
                for s in range(min(CACHE_SLOTS - 1, steps)):
                    fetch(s).start()

            @pl.when(step + (CACHE_SLOTS - 1) < steps)
            def _ahead():
                fetch(step + (CACHE_SLOTS - 1)).start()

            fetch(step).wait()
            io[SSD_STATE_ARG] = ring.at[step % CACHE_SLOTS]
            _ssd_body(True, None, *io, xpad, aux)
        semantics = ("arbitrary", "arbitrary")
    else:
        body = functools.partial(_ssd_body, False, lambda: pl.program_id(1) == 0)
        semantics = ("parallel", "arbitrary")
    y, h_new = pl.pallas_call(
        body,
        grid=grid,
        in_specs=in_specs,
        out_specs=out_specs,
        out_shape=out_shape,
        scratch_shapes=scratch,
        compiler_params=pltpu.CompilerParams(dimension_semantics=semantics,
                                             vmem_limit_bytes=VMEM_LIMIT),
        name="ssd",
    )(*operands)
    return y.reshape(z.shape), h_new


def _ssd_with_sample_attn(ssd_args, attn_args):
    z = ssd_args[0]
    nchunks = z.shape[1] // CHUNK
    steps = z.shape[0] * nchunks
    assert steps == attn_args[0].shape[0]
    where = lambda i: (i // nchunks, i % nchunks)
    _, _, ssd_ops, ssd_in, ssd_out, ssd_shape, scratch = _ssd_call(*ssd_args, where=where)
    attn_ops, attn_in, attn_out, attn_shape = _attn_sample_call(*attn_args)
    caches = (CACHE_KT_ARG, CACHE_VT_ARG)
    for arg in caches:
        attn_in[arg] = pl.BlockSpec(memory_space=pl.ANY)
    ring = pltpu.VMEM((CACHE_SLOTS, D_ATTN, CACHE_LEN), F32)

    def body(*refs):
        ssd_in_refs, refs = refs[:N_SSD_INPUTS], refs[N_SSD_INPUTS:]
        attn_in_refs, refs = list(refs[:len(attn_ops)]), refs[len(attn_ops):]
        y_ref, hn_ref, o_ref, xpad, aux, kt_ring, vt_ring, sems = refs
        step = pl.program_id(0)

        def fetch(s):
            slot = s % CACHE_SLOTS
            return [pltpu.make_async_copy(attn_in_refs[arg].at[s], buf.at[slot], sems.at[j, slot])
                    for j, (arg, buf) in enumerate(zip(caches, (kt_ring, vt_ring)))]

        @pl.when(step == 0)
        def _prime():
            for s in range(CACHE_SLOTS - 1):
                for copy in fetch(s):
                    copy.start()

        @pl.when(step + (CACHE_SLOTS - 1) < steps)
        def _ahead():
            for copy in fetch(step + (CACHE_SLOTS - 1)):
                copy.start()

        for copy in fetch(step):
            copy.wait()
        for arg, buf in zip(caches, (kt_ring, vt_ring)):
            attn_in_refs[arg] = buf.at[step % CACHE_SLOTS]

        ssd = _ssd_stages(False, lambda: step % nchunks == 0, *ssd_in_refs, y_ref, hn_ref, xpad, aux)
        attn = _attn_sample_stages(*attn_in_refs, o_ref)
        for stage in (ssd, attn, ssd, attn):
            next(stage, None)

    y, h_new, attn = pl.pallas_call(
        body,
        grid=(steps,),
        in_specs=ssd_in + attn_in,
        out_specs=ssd_out + [attn_out],
        out_shape=ssd_shape + [attn_shape],
        scratch_shapes=scratch + [ring, ring, pltpu.SemaphoreType.DMA((len(caches), CACHE_SLOTS))],
        compiler_params=pltpu.CompilerParams(dimension_semantics=("arbitrary",),
                                             vmem_limit_bytes=VMEM_LIMIT),
        name="ssd_attn_sample",
    )(*ssd_ops, *attn_ops)
    return y, h_new, attn


def _row(v, width=None):
    v = v.reshape(1, -1).astype(F32)
    if width is not None and v.shape[1] < width:
        v = jnp.pad(v, ((0, 0), (0, width - v.shape[1])))
    return v


def kernel(x_prompt, x_sample, cache_k, cache_v, state_conv, state_ssm, p_prompt, p_sample,
           ln_in_g, ln_in_b, w_in, conv_w, conv_b, dt_bias, a_log, d_skip, ssd_norm_w, w_out,
           ln1_g, ln1_b, w_up, w_down, ln2_g, ln2_b, w_gate, w_ple, ln3_g, ln3_b):
    depth = w_in.shape[0]
    assert depth == 1, "single-layer step"
    alpha = (2 * depth) ** 0.25
    bsz, seq, _ = x_prompt.shape
    nd, dec_t, _ = x_sample.shape
    lyr = 0

    w_t = jnp.transpose(w_in[lyr])
    w_proj = w_t[:D_PROJ].astype(BF16)
    w_dt = jnp.pad(w_t[D_PROJ:], ((0, DT_PAD - N_HEADS), (0, 0))).astype(BF16)
    gin, bin_ = _row(ln_in_g), _row(ln_in_b)
    ssd_params = (conv_w[lyr].astype(F32), _row(conv_b[lyr]), _row(dt_bias[lyr], DT_PAD),
                  _row(a_log[lyr], DT_PAD), _row(jnp.repeat(d_skip[lyr], HEAD_DIM)),
                  _row(ssd_norm_w[lyr]))
    post_params = (gin, bin_, w_out[lyr].astype(BF16), _row(ln1_g[lyr]), _row(ln1_b[lyr]),
                   w_up[lyr].astype(BF16), w_down[lyr].astype(BF16), _row(ln2_g[lyr]),
                   _row(ln2_b[lyr]), w_gate[lyr].astype(BF16), w_ple[lyr].astype(BF16),
                   _row(ln3_g[lyr]), _row(ln3_b[lyr]))

    q, k, v, k_t, v_t, z, xbc, dt = _in_proj(x_prompt, gin, bin_, w_proj, w_dt, head_major=True, tm=1024)
    n_tok = nd * dec_t
    flat = lambda t: t.reshape(1, n_tok, t.shape[-1])
    toks = lambda t: t.reshape(nd, dec_t, t.shape[-1])
    qs, ks, vs, ks_t, vs_t, zs, xbcs, dts = _in_proj(flat(x_sample), gin, bin_, w_proj, w_dt,
                                                   head_major=False, tm=SHORT_SEQ * LANES)

    attn = _attn_prompt(q, k, v)
    transposed = lambda c: jnp.transpose(c, (0, 2, 3, 1)).reshape(nd, D_ATTN, CACHE_LEN)
    ssd_args = (z, xbc, dt, jnp.zeros((bsz, PAD_ROWS, CONV_DIM), F32),
                jnp.zeros((bsz, D_SSD, D_STATE), F32)) + ssd_params
    attn_args = (toks(qs), toks(ks), toks(vs), transposed(cache_k[lyr]), transposed(cache_v[lyr]))
    if bsz * (seq // CHUNK) == nd:
        ssd_y, ssm_p, attn_s = _ssd_with_sample_attn(ssd_args, attn_args)
    else:
        ssd_y, ssm_p = _ssd(*ssd_args)
        attn_s = _attn_sample(*attn_args)
    conv_prev = jnp.pad(state_conv[lyr].astype(F32), ((0, 0), (PAD_ROWS - (CONV_W - 1), 0), (0, 0)))
    ssd_s, ssm_s = _ssd(toks(zs), toks(xbcs), toks(dts), conv_prev,
                        state_ssm[lyr].reshape(nd, D_SSD, D_STATE).astype(F32), *ssd_params)

    y_prompt = _post(x_prompt, attn, ssd_y, p_prompt[lyr], *post_params, alpha=alpha, tm=512)
    y_sample = _post(flat(x_sample), attn_s, flat(ssd_s), flat(p_sample[lyr]), *post_params,
                     alpha=alpha, tm=256)

    from_t = lambda t: jnp.transpose(t.reshape(1, bsz, N_HEADS, HEAD_DIM, seq), (0, 1, 4, 2, 3))
    by_step = lambda t: jnp.transpose(t.reshape(1, dec_t, N_HEADS, HEAD_DIM, nd), (0, 4, 1, 2, 3))
    tail = lambda t: t[None, :, -(CONV_W - 1):, :]
    state = lambda t, n: t.reshape(1, n, N_HEADS, HEAD_DIM, D_STATE)
    return (y_prompt, y_sample.reshape(nd, dec_t, D_MODEL), from_t(k_t), from_t(v_t),
            by_step(ks_t), by_step(vs_t), tail(xbc), tail(toks(xbcs)), state(ssm_p, bsz), state(ssm_s, nd))
```
